```python
import jax, jax.numpy as jnp
from jax import lax
import numpy as np

D_MODEL = 2048
BATCH = 8
SEQ = 2048
DEPTH = 1

N_META = 16
POOL_WIDTH = D_MODEL // 2
POOL_WINDOWS = (2, 4, 8, 16)
N_POOL_GROUPS = len(POOL_WINDOWS)
POOL_GROUP_DIM = POOL_WIDTH // N_POOL_GROUPS
CONV_WIDTH = D_MODEL // 2
CONV_KERNEL = 31
D_FF = 4 * D_MODEL
IN_COLS = POOL_WIDTH + 2 * CONV_WIDTH + 2 * D_MODEL
RMS_EPS = 1e-6
LN_EPS = 1e-5

kernel_name = "hybrid_pool_conformer_gated_block"


def rms_norm(x, g):
    xf = x.astype(jnp.float32)
    y = xf * lax.rsqrt(jnp.mean(xf * xf, axis=-1, keepdims=True) + RMS_EPS)
    return (y * g.astype(jnp.float32)).astype(x.dtype)


def layer_norm(x, g, b):
    xf = x.astype(jnp.float32)
    mu = jnp.mean(xf, axis=-1, keepdims=True)
    var = jnp.mean(jnp.square(xf - mu), axis=-1, keepdims=True)
    y = (xf - mu) * lax.rsqrt(var + LN_EPS)
    return (y * g.astype(jnp.float32) + b.astype(jnp.float32)).astype(x.dtype)


def causal_multiscale_pool(z, w_grp, scale):
    B, L, _ = z.shape
    zf = z.astype(jnp.float32).reshape(B, L, N_POOL_GROUPS, POOL_GROUP_DIM)
    cs = jnp.cumsum(zf, axis=1)
    pos = jnp.arange(L)
    means = []
    for g, w in enumerate(POOL_WINDOWS):
        csg = cs[:, :, g]
        lag = jnp.pad(csg[:, : L - w], ((0, 0), (w, 0), (0, 0)))
        cnt = jnp.minimum(pos + 1, w).astype(jnp.float32)[None, :, None]
        means.append((csg - lag) / cnt)
    pooled = jnp.stack(means, axis=2)
    d = (pooled - zf).astype(z.dtype)
    y = jnp.einsum('blgc,gcd->blgd', d, w_grp).reshape(B, L, POOL_WIDTH)
    return y * scale


def conformer_conv(v, gate, w_dw, b_dw, ln_g, ln_b):
    a = v * jax.nn.sigmoid(gate)
    c = lax.conv_general_dilated(
        a, w_dw[:, None, :], window_strides=(1,),
        padding=[(CONV_KERNEL - 1, 0)],
        dimension_numbers=('NWC', 'WIO', 'NWC'),
        feature_group_count=CONV_WIDTH) + b_dw
    return jax.nn.silu(layer_norm(c, ln_g, ln_b))


def _fwd_setup_inputs(seed: int = 0) -> dict:
    key = jax.random.key(seed)
    ks = jax.random.split(key, 24)
    f32 = jnp.float32
    n = lambda k, shape, s: jax.random.normal(k, shape, f32) * s
    gain = lambda k, shape: 1.0 + 0.05 * jax.random.normal(k, shape, f32)
    return {
        "x": jax.random.normal(ks[0], (BATCH, SEQ, D_MODEL), f32),
        "meta": n(ks[1], (N_META, D_MODEL), 1.0),
        "g_pre_mix": gain(ks[2], (DEPTH, D_MODEL)),
        "w_in": n(ks[3], (DEPTH, D_MODEL, IN_COLS), D_MODEL ** -0.5),
        "w_pool_grp": n(ks[4], (DEPTH, N_POOL_GROUPS, POOL_GROUP_DIM, POOL_GROUP_DIM), POOL_GROUP_DIM ** -0.5),
        "pool_scale": gain(ks[5], (DEPTH, POOL_WIDTH)),
        "w_pool_out": n(ks[6], (DEPTH, POOL_WIDTH, D_MODEL), POOL_WIDTH ** -0.5),
        "w_dw": n(ks[7], (DEPTH, CONV_KERNEL, CONV_WIDTH), CONV_KERNEL ** -0.5),
        "b_dw": n(ks[8], (DEPTH, CONV_WIDTH), 0.02),
        "conv_ln_g": gain(ks[9], (DEPTH, CONV_WIDTH)),
        "conv_ln_b": n(ks[10], (DEPTH, CONV_WIDTH), 0.02),
        "w_conv_out": n(ks[11], (DEPTH, CONV_WIDTH, D_MODEL), CONV_WIDTH ** -0.5),
        "w_o": n(ks[12], (DEPTH, D_MODEL, D_MODEL), D_MODEL ** -0.5),
        "g_post_mix": gain(ks[13], (DEPTH, D_MODEL)),
        "g_pre_mlp": gain(ks[14], (DEPTH, D_MODEL)),
        "w_up": n(ks[15], (DEPTH, D_MODEL, D_FF), D_MODEL ** -0.5),
        "w_down": n(ks[16], (DEPTH, D_FF, D_MODEL), D_FF ** -0.5),
        "g_post_mlp": gain(ks[17], (DEPTH, D_MODEL)),
    }


def _fwd_reference(x, meta, g_pre_mix, w_in, w_pool_grp, pool_scale, w_pool_out,
              w_dw, b_dw, conv_ln_g, conv_ln_b, w_conv_out, w_o, g_post_mix,
              g_pre_mlp, w_up, w_down, g_post_mlp):
    B = x.shape[0]
    meta_b = jnp.broadcast_to(meta[None].astype(x.dtype), (B, N_META, D_MODEL))
    h = jnp.concatenate([meta_b, x], axis=1)
    splits = np.cumsum([POOL_WIDTH, CONV_WIDTH, CONV_WIDTH, D_MODEL]).tolist()
    for l in range(DEPTH):
        u = rms_norm(h, g_pre_mix[l])
        proj = u @ w_in[l]
        z_pool, v_conv, g_conv, gate_a, gate_b = jnp.split(proj, splits, axis=-1)
        y_a = causal_multiscale_pool(z_pool, w_pool_grp[l], pool_scale[l]) @ w_pool_out[l]
        y_b = conformer_conv(v_conv, g_conv, w_dw[l], b_dw[l],
                             conv_ln_g[l], conv_ln_b[l]) @ w_conv_out[l]
        m = jax.nn.sigmoid(gate_a) * y_a + jax.nn.sigmoid(gate_b) * y_b
        h = h + rms_norm(m @ w_o[l], g_post_mix[l])
        u = rms_norm(h, g_pre_mlp[l])
        f = jnp.square(jax.nn.relu(u @ w_up[l])) @ w_down[l]
        h = h + rms_norm(f, g_post_mlp[l])
    return h[:, N_META:]


import jax as _jax
import jax.numpy as _jnp

TWIN_FORMAT = 'train_step'
FWD_PARAMS = ['x', 'meta', 'g_pre_mix', 'w_in', 'w_pool_grp', 'pool_scale', 'w_pool_out', 'w_dw', 'b_dw', 'conv_ln_g', 'conv_ln_b', 'w_conv_out', 'w_o', 'g_post_mix', 'g_pre_mlp', 'w_up', 'w_down', 'g_post_mlp']
TWIN_WEIGHTS = ['meta', 'g_pre_mix', 'w_in', 'w_pool_grp', 'pool_scale', 'w_pool_out', 'w_dw', 'b_dw', 'conv_ln_g', 'conv_ln_b', 'w_conv_out', 'w_o', 'g_post_mix', 'g_pre_mlp', 'w_up', 'w_down', 'g_post_mlp']
TWIN_DIFF_INPUT = 'x'
TWIN_INPUTS = ['x', 'meta', 'g_pre_mix', 'w_in', 'w_pool_grp', 'pool_scale', 'w_pool_out', 'w_dw', 'b_dw', 'conv_ln_g', 'conv_ln_b', 'w_conv_out', 'w_o', 'g_post_mix', 'g_pre_mlp', 'w_up', 'w_down', 'g_post_mlp', 'loss_target', 'm_meta', 'm_g_pre_mix', 'm_w_in', 'm_w_pool_grp', 'm_pool_scale', 'm_w_pool_out', 'm_w_dw', 'm_b_dw', 'm_conv_ln_g', 'm_conv_ln_b', 'm_w_conv_out', 'm_w_o', 'm_g_post_mix', 'm_g_pre_mlp', 'm_w_up', 'm_w_down', 'm_g_post_mlp', 'v_meta', 'v_g_pre_mix', 'v_w_in', 'v_w_pool_grp', 'v_pool_scale', 'v_w_pool_out', 'v_w_dw', 'v_b_dw', 'v_conv_ln_g', 'v_conv_ln_b', 'v_w_conv_out', 'v_w_o', 'v_g_post_mix', 'v_g_pre_mlp', 'v_w_up', 'v_w_down', 'v_g_post_mlp']
TWIN_OUTPUTS = ['loss', 'grad_x', 'grad_meta', 'grad_g_pre_mix', 'grad_w_in', 'grad_w_pool_grp', 'grad_pool_scale', 'grad_w_pool_out', 'grad_w_dw', 'grad_b_dw', 'grad_conv_ln_g', 'grad_conv_ln_b', 'grad_w_conv_out', 'grad_w_o', 'grad_g_post_mix', 'grad_g_pre_mlp', 'grad_w_up', 'grad_w_down', 'grad_g_post_mlp', 'delta_meta', 'delta_g_pre_mix', 'delta_w_in', 'delta_w_pool_grp', 'delta_pool_scale', 'delta_w_pool_out', 'delta_w_dw', 'delta_b_dw', 'delta_conv_ln_g', 'delta_conv_ln_b', 'delta_w_conv_out', 'delta_w_o', 'delta_g_post_mix', 'delta_g_pre_mlp', 'delta_w_up', 'delta_w_down', 'delta_g_post_mlp', 'new_m_meta', 'new_m_g_pre_mix', 'new_m_w_in', 'new_m_w_pool_grp', 'new_m_pool_scale', 'new_m_w_pool_out', 'new_m_w_dw', 'new_m_b_dw', 'new_m_conv_ln_g', 'new_m_conv_ln_b', 'new_m_w_conv_out', 'new_m_w_o', 'new_m_g_post_mix', 'new_m_g_pre_mlp', 'new_m_w_up', 'new_m_w_down', 'new_m_g_post_mlp', 'new_v_meta', 'new_v_g_pre_mix', 'new_v_w_in', 'new_v_w_pool_grp', 'new_v_pool_scale', 'new_v_w_pool_out', 'new_v_w_dw', 'new_v_b_dw', 'new_v_conv_ln_g', 'new_v_conv_ln_b', 'new_v_w_conv_out', 'new_v_w_o', 'new_v_g_post_mix', 'new_v_g_pre_mlp', 'new_v_w_up', 'new_v_w_down', 'new_v_g_post_mlp']
TWIN_LEAF_KINDS = {'loss': 'loss', 'grad_x': 'grad_x', 'grad_meta': 'grad_w', 'grad_g_pre_mix': 'grad_w', 'grad_w_in': 'grad_w', 'grad_w_pool_grp': 'grad_w', 'grad_pool_scale': 'grad_w', 'grad_w_pool_out': 'grad_w', 'grad_w_dw': 'grad_w', 'grad_b_dw': 'grad_w', 'grad_conv_ln_g': 'grad_w', 'grad_conv_ln_b': 'grad_w', 'grad_w_conv_out': 'grad_w', 'grad_w_o': 'grad_w', 'grad_g_post_mix': 'grad_w', 'grad_g_pre_mlp': 'grad_w', 'grad_w_up': 'grad_w', 'grad_w_down': 'grad_w', 'grad_g_post_mlp': 'grad_w', 'delta_meta': 'delta_w', 'delta_g_pre_mix': 'delta_w', 'delta_w_in': 'delta_w', 'delta_w_pool_grp': 'delta_w', 'delta_pool_scale': 'delta_w', 'delta_w_pool_out': 'delta_w', 'delta_w_dw': 'delta_w', 'delta_b_dw': 'delta_w', 'delta_conv_ln_g': 'delta_w', 'delta_conv_ln_b': 'delta_w', 'delta_w_conv_out': 'delta_w', 'delta_w_o': 'delta_w', 'delta_g_post_mix': 'delta_w', 'delta_g_pre_mlp': 'delta_w', 'delta_w_up': 'delta_w', 'delta_w_down': 'delta_w', 'delta_g_post_mlp': 'delta_w', 'new_m_meta': 'new_m', 'new_m_g_pre_mix': 'new_m', 'new_m_w_in': 'new_m', 'new_m_w_pool_grp': 'new_m', 'new_m_pool_scale': 'new_m', 'new_m_w_pool_out': 'new_m', 'new_m_w_dw': 'new_m', 'new_m_b_dw': 'new_m', 'new_m_conv_ln_g': 'new_m', 'new_m_conv_ln_b': 'new_m', 'new_m_w_conv_out': 'new_m', 'new_m_w_o': 'new_m', 'new_m_g_post_mix': 'new_m', 'new_m_g_pre_mlp': 'new_m', 'new_m_w_up': 'new_m', 'new_m_w_down': 'new_m', 'new_m_g_post_mlp': 'new_m', 'new_v_meta': 'new_v', 'new_v_g_pre_mix': 'new_v', 'new_v_w_in': 'new_v', 'new_v_w_pool_grp': 'new_v', 'new_v_pool_scale': 'new_v', 'new_v_w_pool_out': 'new_v', 'new_v_w_dw': 'new_v', 'new_v_b_dw': 'new_v', 'new_v_conv_ln_g': 'new_v', 'new_v_conv_ln_b': 'new_v', 'new_v_w_conv_out': 'new_v', 'new_v_w_o': 'new_v', 'new_v_g_post_mix': 'new_v', 'new_v_g_pre_mlp': 'new_v', 'new_v_w_up': 'new_v', 'new_v_w_down': 'new_v', 'new_v_g_post_mlp': 'new_v'}


def _forward(args):
    return _fwd_reference(*[args[k] for k in FWD_PARAMS])


def _output_shape():
    out = _jax.eval_shape(lambda: _forward(_fwd_setup_inputs(0)))
    return out.shape, out.dtype

N_MICROBATCH = 1
ADAM_LR = 0.001
ADAM_B1 = 0.9
ADAM_B2 = 0.999
ADAM_EPS = 1e-08
ADAM_WD = 0.01
ADAM_STEP = 10
PER_EXAMPLE_BATCH_AXIS = {'x': 0, 'loss_target': 0}
SHARED_INPUTS = []
_WEIGHT_DTYPES = {'meta': _jnp.float32, 'g_pre_mix': _jnp.float32, 'w_in': _jnp.float32, 'w_pool_grp': _jnp.float32, 'pool_scale': _jnp.float32, 'w_pool_out': _jnp.float32, 'w_dw': _jnp.float32, 'b_dw': _jnp.float32, 'conv_ln_g': _jnp.float32, 'conv_ln_b': _jnp.float32, 'w_conv_out': _jnp.float32, 'w_o': _jnp.float32, 'g_post_mix': _jnp.float32, 'g_pre_mlp': _jnp.float32, 'w_up': _jnp.float32, 'w_down': _jnp.float32, 'g_post_mlp': _jnp.float32}
MOMENT_SCALE = {'meta': 5.833265e-03, 'g_pre_mix': 2.038320e-01, 'w_in': 1.026096e-01, 'w_pool_grp': 2.505700e-01, 'pool_scale': 2.816727e-01, 'w_pool_out': 1.931721e-01, 'w_dw': 2.333391e-01, 'b_dw': 3.852624e+00, 'conv_ln_g': 1.462962e+00, 'conv_ln_b': 2.094736e+00, 'w_conv_out': 6.232173e-01, 'w_o': 6.230320e-01, 'g_post_mix': 8.072948e+00, 'g_pre_mlp': 2.449609e-01, 'w_up': 1.208049e-01, 'w_down': 6.386936e-01, 'g_post_mlp': 8.295762e+00}


def _to_microbatches(a, axis):
    t = _jnp.moveaxis(a, axis, 0)
    t = t.reshape((N_MICROBATCH, t.shape[0] // N_MICROBATCH) + t.shape[1:])
    return _jnp.moveaxis(t, 1, axis + 1)


def setup_inputs(seed: int = 0) -> dict:
    inp = _fwd_setup_inputs(seed)
    key = _jax.random.fold_in(_jax.random.key(seed), 7919)
    shape, _ = _output_shape()
    out = dict(inp)
    out["loss_target"] = _jax.random.normal(_jax.random.fold_in(key, 0), shape, _jnp.float32)
    for i, name in enumerate(TWIN_WEIGHTS):
        w = inp[name].astype(_jnp.float32)
        if MOMENT_SCALE is None:
            s = _jnp.sqrt(_jnp.mean(_jnp.square(w)) + 1e-30)
        else:
            s = MOMENT_SCALE[name]
        km, kv = _jax.random.split(_jax.random.fold_in(key, i + 1))
        out[name] = w
        out["m_" + name] = s * _jax.random.normal(km, w.shape, _jnp.float32)
        out["v_" + name] = (s * s) * _jax.random.uniform(kv, w.shape, _jnp.float32, 0.5, 1.5)
    if N_MICROBATCH > 1:
        for name, axis in PER_EXAMPLE_BATCH_AXIS.items():
            out[name] = _to_microbatches(out[name], axis)
    return {'x': out['x'], 'meta': out['meta'], 'g_pre_mix': out['g_pre_mix'], 'w_in': out['w_in'], 'w_pool_grp': out['w_pool_grp'], 'pool_scale': out['pool_scale'], 'w_pool_out': out['w_pool_out'], 'w_dw': out['w_dw'], 'b_dw': out['b_dw'], 'conv_ln_g': out['conv_ln_g'], 'conv_ln_b': out['conv_ln_b'], 'w_conv_out': out['w_conv_out'], 'w_o': out['w_o'], 'g_post_mix': out['g_post_mix'], 'g_pre_mlp': out['g_pre_mlp'], 'w_up': out['w_up'], 'w_down': out['w_down'], 'g_post_mlp': out['g_post_mlp'], 'loss_target': out['loss_target'], 'm_meta': out['m_meta'], 'm_g_pre_mix': out['m_g_pre_mix'], 'm_w_in': out['m_w_in'], 'm_w_pool_grp': out['m_w_pool_grp'], 'm_pool_scale': out['m_pool_scale'], 'm_w_pool_out': out['m_w_pool_out'], 'm_w_dw': out['m_w_dw'], 'm_b_dw': out['m_b_dw'], 'm_conv_ln_g': out['m_conv_ln_g'], 'm_conv_ln_b': out['m_conv_ln_b'], 'm_w_conv_out': out['m_w_conv_out'], 'm_w_o': out['m_w_o'], 'm_g_post_mix': out['m_g_post_mix'], 'm_g_pre_mlp': out['m_g_pre_mlp'], 'm_w_up': out['m_w_up'], 'm_w_down': out['m_w_down'], 'm_g_post_mlp': out['m_g_post_mlp'], 'v_meta': out['v_meta'], 'v_g_pre_mix': out['v_g_pre_mix'], 'v_w_in': out['v_w_in'], 'v_w_pool_grp': out['v_w_pool_grp'], 'v_pool_scale': out['v_pool_scale'], 'v_w_pool_out': out['v_w_pool_out'], 'v_w_dw': out['v_w_dw'], 'v_b_dw': out['v_b_dw'], 'v_conv_ln_g': out['v_conv_ln_g'], 'v_conv_ln_b': out['v_conv_ln_b'], 'v_w_conv_out': out['v_w_conv_out'], 'v_w_o': out['v_w_o'], 'v_g_post_mix': out['v_g_post_mix'], 'v_g_pre_mlp': out['v_g_pre_mlp'], 'v_w_up': out['v_w_up'], 'v_w_down': out['v_w_down'], 'v_g_post_mlp': out['v_g_post_mlp']}


def _loss(weights, diff, rest, loss_target):
    with _jax.named_scope("forward"):
        args = {**rest, TWIN_DIFF_INPUT: diff, **{k: w.astype(_WEIGHT_DTYPES[k]) for k, w in weights.items()}}
        y = _forward(args)
    with _jax.named_scope("loss_head"):
        err = _jnp.square(y.astype(_jnp.float32) - loss_target)
        return 0.5 * _jnp.sum(_jnp.mean(err, axis=-1)) if err.ndim else 0.5 * err


def _adamw(w, g, m, v):
    m = ADAM_B1 * m + (1.0 - ADAM_B1) * g
    v = ADAM_B2 * v + (1.0 - ADAM_B2) * _jnp.square(g)
    m_hat = m / (1.0 - ADAM_B1 ** ADAM_STEP)
    v_hat = v / (1.0 - ADAM_B2 ** ADAM_STEP)
    delta = -ADAM_LR * (m_hat / (_jnp.sqrt(v_hat) + ADAM_EPS) + ADAM_WD * w)
    return delta, m, v


def reference(x, meta, g_pre_mix, w_in, w_pool_grp, pool_scale, w_pool_out, w_dw, b_dw, conv_ln_g, conv_ln_b, w_conv_out, w_o, g_post_mix, g_pre_mlp, w_up, w_down, g_post_mlp, loss_target, m_meta, m_g_pre_mix, m_w_in, m_w_pool_grp, m_pool_scale, m_w_pool_out, m_w_dw, m_b_dw, m_conv_ln_g, m_conv_ln_b, m_w_conv_out, m_w_o, m_g_post_mix, m_g_pre_mlp, m_w_up, m_w_down, m_g_post_mlp, v_meta, v_g_pre_mix, v_w_in, v_w_pool_grp, v_pool_scale, v_w_pool_out, v_w_dw, v_b_dw, v_conv_ln_g, v_conv_ln_b, v_w_conv_out, v_w_o, v_g_post_mix, v_g_pre_mlp, v_w_up, v_w_down, v_g_post_mlp):
    given = dict(x=x, meta=meta, g_pre_mix=g_pre_mix, w_in=w_in, w_pool_grp=w_pool_grp, pool_scale=pool_scale, w_pool_out=w_pool_out, w_dw=w_dw, b_dw=b_dw, conv_ln_g=conv_ln_g, conv_ln_b=conv_ln_b, w_conv_out=w_conv_out, w_o=w_o, g_post_mix=g_post_mix, g_pre_mlp=g_pre_mlp, w_up=w_up, w_down=w_down, g_post_mlp=g_post_mlp, loss_target=loss_target, m_meta=m_meta, m_g_pre_mix=m_g_pre_mix, m_w_in=m_w_in, m_w_pool_grp=m_w_pool_grp, m_pool_scale=m_pool_scale, m_w_pool_out=m_w_pool_out, m_w_dw=m_w_dw, m_b_dw=m_b_dw, m_conv_ln_g=m_conv_ln_g, m_conv_ln_b=m_conv_ln_b, m_w_conv_out=m_w_conv_out, m_w_o=m_w_o, m_g_post_mix=m_g_post_mix, m_g_pre_mlp=m_g_pre_mlp, m_w_up=m_w_up, m_w_down=m_w_down, m_g_post_mlp=m_g_post_mlp, v_meta=v_meta, v_g_pre_mix=v_g_pre_mix, v_w_in=v_w_in, v_w_pool_grp=v_w_pool_grp, v_pool_scale=v_pool_scale, v_w_pool_out=v_w_pool_out, v_w_dw=v_w_dw, v_b_dw=v_b_dw, v_conv_ln_g=v_conv_ln_g, v_conv_ln_b=v_conv_ln_b, v_w_conv_out=v_w_conv_out, v_w_o=v_w_o, v_g_post_mix=v_g_post_mix, v_g_pre_mlp=v_g_pre_mlp, v_w_up=v_w_up, v_w_down=v_w_down, v_g_post_mlp=v_g_post_mlp)
    weights = {n: given[n] for n in TWIN_WEIGHTS}
    shared = {n: given[n] for n in SHARED_INPUTS}
    per_example = {n: given[n] for n in ['x']}
    grad_fn = _jax.value_and_grad(_loss, argnums=(0, 1))

    def one_microbatch(ex, loss_target):
        ex = dict(ex)
        diff = ex.pop(TWIN_DIFF_INPUT)
        return grad_fn(weights, diff, {**shared, **ex}, loss_target)

    if N_MICROBATCH == 1:
        loss, (grad_w, grad_x) = one_microbatch(per_example, given["loss_target"])
    else:
        def body(carry, xs):
            loss_sum, grad_sum = carry
            l_k, (gw_k, gx_k) = one_microbatch(xs[0], xs[1])
            with _jax.named_scope("update"):
                return (loss_sum + l_k, _jax.tree.map(_jnp.add, grad_sum, gw_k)), gx_k

        init = (_jnp.zeros((), _jnp.float32), _jax.tree.map(_jnp.zeros_like, weights))
        (loss, grad_w), grad_x = _jax.lax.scan(body, init, (per_example, given["loss_target"]))
    with _jax.named_scope("update"):
        delta_w, new_m, new_v = {}, {}, {}
        for n in TWIN_WEIGHTS:
            delta_w[n], new_m[n], new_v[n] = _adamw(weights[n], grad_w[n], given["m_" + n], given["v_" + n])
    return (loss, grad_x, *[grad_w[n] for n in TWIN_WEIGHTS], *[delta_w[n] for n in TWIN_WEIGHTS],
            *[new_m[n] for n in TWIN_WEIGHTS], *[new_v[n] for n in TWIN_WEIGHTS])
```

```python
import functools

import jax
import jax.numpy as jnp
from jax import lax
from jax.experimental import pallas as pl
from jax.experimental.pallas import tpu as pltpu

F32 = jnp.float32
BF16 = jnp.bfloat16

N_META = 16
PAD_ROWS = 112
TOKEN_ROW0 = PAD_ROWS + N_META
POOL_GROUPS = 4
CONV_TAPS = 31
HALO = 32
CONV_ROWS = 128
LANES = 128
RMS_EPS = 1e-6
LN_EPS = 1e-5
ADAM_LR = 0.001
ADAM_B1 = 0.9
ADAM_B2 = 0.999
ADAM_EPS = 1e-08
ADAM_WD = 0.01
ADAM_STEP = 10
VMEM_LIMIT_MB = 56

MESH = pl.DeviceIdType.MESH
NN = (((1,), (0,)), ((), ()))
NT = (((1,), (1,)), ((), ()))
TN = (((0,), (0,)), ((), ()))


def _pick(n, pref):
    if n <= pref:
        return n
    for step in (LANES, 8, 1):
        t = (pref // step) * step
        while t >= step:
            if n % t == 0:
                return t
            t -= step
    return n


def _params(n_axes, vmem_mb=VMEM_LIMIT_MB):
    return pltpu.CompilerParams(dimension_semantics=("arbitrary",) * n_axes,
                                vmem_limit_bytes=vmem_mb << 20)


def _sigmoid(x):
    return jax.nn.sigmoid(x)


def _store(val, extras, outs):
    outs[0][...] = val.astype(outs[0].dtype)


def _mm(name, grid, arrays, in_specs, out_shapes, out_specs, dims, nk, epilogue=_store, acc_shape=None):
    n_in, n_out = len(arrays), len(out_shapes)

    def body(*refs):
        extras = refs[2:n_in]
        outs = refs[n_in:n_in + n_out]
        part = lax.dot_general(refs[0][...], refs[1][...], dims, preferred_element_type=F32)
        if nk == 1:
            epilogue(part, extras, outs)
        else:
            acc = refs[n_in + n_out]
            k = pl.program_id(len(grid) - 1)

            @pl.when(k == 0)
            def _():
                acc[...] = part

            @pl.when(k > 0)
            def _():
                acc[...] += part

            @pl.when(k == nk - 1)
            def _():
                epilogue(acc[...], extras, outs)

    scratch = [pltpu.VMEM(acc_shape, F32)] if nk > 1 else []
    single = n_out == 1
    res = pl.pallas_call(
        body, name=name, grid=grid, in_specs=in_specs,
        out_specs=out_specs[0] if single else out_specs,
        out_shape=out_shapes[0] if single else out_shapes,
        scratch_shapes=scratch, compiler_params=_params(len(grid)),
    )(*arrays)
    return res


def _sds(shape, dtype):
    return jax.ShapeDtypeStruct(shape, dtype)


def _rms_scale(h):
    return lax.rsqrt(jnp.mean(h * h, axis=-1, keepdims=True) + RMS_EPS)


def _rms_bwd(du, h, g):
    r = _rms_scale(h)
    y = h * r
    dy = du * g
    dh = r * (dy - y * jnp.mean(dy * y, axis=-1, keepdims=True))
    return dh, jnp.sum(du * y, axis=0, keepdims=True)


def _row_tile(L):
    return _pick(L, 272)


def _pre_norm(h0, g):
    L, D = h0.shape
    T = _row_tile(L)

    def body(h_ref, g_ref, u_ref):
        h = h_ref[...]
        u_ref[...] = (h * _rms_scale(h) * g_ref[...]).astype(BF16)

    return pl.pallas_call(
        body, name="pre_norm", grid=(L // T,),
        in_specs=[pl.BlockSpec((T, D), lambda i: (i, 0)), pl.BlockSpec((1, D), lambda i: (0, 0))],
        out_specs=pl.BlockSpec((T, D), lambda i: (i, 0)),
        out_shape=_sds((L, D), BF16), compiler_params=_params(1),
    )(h0, g)


def _mid_norm(o, h0, g_post, g_pre):
    L, D = h0.shape
    T = _row_tile(L)

    def body(o_ref, h_ref, gp_ref, gm_ref, h1_ref, u2_ref):
        o_ = o_ref[...]
        h1 = h_ref[...] + o_ * _rms_scale(o_) * gp_ref[...]
        h1_ref[...] = h1
        u2_ref[...] = (h1 * _rms_scale(h1) * gm_ref[...]).astype(BF16)

    row = pl.BlockSpec((T, D), lambda i: (i, 0))
    vec = pl.BlockSpec((1, D), lambda i: (0, 0))
    return pl.pallas_call(
        body, name="mid_norm", grid=(L // T,),
        in_specs=[row, row, vec, vec], out_specs=[row, row],
        out_shape=[_sds((L, D), F32), _sds((L, D), BF16)], compiler_params=_params(1),
    )(o, h0, g_post, g_pre)


def _loss_head(f, h1, tgt, g_post):
    L, D = h1.shape
    T = TOKEN_ROW0
    n = L // T

    def body(f_ref, h_ref, t_ref, g_ref, dy_ref, df_ref, dg_ref, loss_ref):
        i = pl.program_id(0)
        f_ = f_ref[...]
        g = g_ref[...]
        y = h_ref[...] + f_ * _rms_scale(f_) * g
        live = (i > 0).astype(F32)
        diff = (y - t_ref[...]) * live
        part = 0.5 * jnp.sum(jnp.mean(diff * diff, axis=-1, keepdims=True), axis=0, keepdims=True)
        dy = diff * (1.0 / D)
        dy_ref[...] = dy
        df, dg = _rms_bwd(dy, f_, g)
        df_ref[...] = df.astype(BF16)

        @pl.when(i == 0)
        def _():
            dg_ref[...] = dg
            loss_ref[...] = jnp.broadcast_to(part, loss_ref.shape)

        @pl.when(i > 0)
        def _():
            dg_ref[...] += dg
            loss_ref[...] += jnp.broadcast_to(part, loss_ref.shape)

    row = pl.BlockSpec((T, D), lambda i: (i, 0))
    vec = pl.BlockSpec((1, D), lambda i: (0, 0))
    return pl.pallas_call(
        body, name="loss_head", grid=(n,),
        in_specs=[row, row, pl.BlockSpec((T, D), lambda i: (jnp.maximum(i - 1, 0), 0)), vec],
        out_specs=[row, row, vec, pl.BlockSpec((1, LANES), lambda i: (0, 0))],
        out_shape=[_sds((L, D), F32), _sds((L, D), BF16), _sds((1, D), F32), _sds((1, LANES), F32)],
        compiler_params=_params(1),
    )(f, h1, tgt, g_post)


def _mid_norm_bwd(dy, du2, h1, o, g_pre, g_post):
    L, D = h1.shape
    T = _row_tile(L)

    def body(dy_ref, du_ref, h_ref, o_ref, gm_ref, gp_ref, dh1_ref, do_ref, dgm_ref, dgp_ref):
        i = pl.program_id(0)
        dh, dgm = _rms_bwd(du_ref[...], h_ref[...], gm_ref[...])
        dh1 = dy_ref[...] + dh
        dh1_ref[...] = dh1
        do, dgp = _rms_bwd(dh1, o_ref[...], gp_ref[...])
        do_ref[...] = do.astype(BF16)

        @pl.when(i == 0)
        def _():
            dgm_ref[...] = dgm
            dgp_ref[...] = dgp

        @pl.when(i > 0)
        def _():
            dgm_ref[...] += dgm
            dgp_ref[...] += dgp

    row = pl.BlockSpec((T, D), lambda i: (i, 0))
    vec = pl.BlockSpec((1, D), lambda i: (0, 0))
    return pl.pallas_call(
        body, name="mid_norm_bwd", grid=(L // T,),
        in_specs=[row, row, row, row, vec, vec], out_specs=[row, row, vec, vec],
        out_shape=[_sds((L, D), F32), _sds((L, D), BF16), _sds((1, D), F32), _sds((1, D), F32)],
        compiler_params=_params(1),
    )(dy, du2, h1, o, g_pre, g_post)


def _pre_norm_bwd(dh1, du1, h0, g):
    L, D = h0.shape
    T = TOKEN_ROW0
    n = L // T

    def body(dh_ref, du_ref, h_ref, g_ref, gx_ref, dmeta_ref, dg_ref):
        i = pl.program_id(0)
        dh, dg = _rms_bwd(du_ref[...], h_ref[...], g_ref[...])
        dh0 = dh_ref[...] + dh
        gx_ref[...] = dh0

        @pl.when(i == 0)
        def _():
            dmeta_ref[...] = dh0[PAD_ROWS:, :]
            dg_ref[...] = dg

        @pl.when(i > 0)
        def _():
            dg_ref[...] += dg

    row = pl.BlockSpec((T, D), lambda i: (i, 0))
    vec = pl.BlockSpec((1, D), lambda i: (0, 0))
    return pl.pallas_call(
        body, name="pre_norm_bwd", grid=(n,),
        in_specs=[row, row, row, vec],
        out_specs=[pl.BlockSpec((T, D), lambda i: (jnp.maximum(i - 1, 0), 0)),
                   pl.BlockSpec((N_META, D), lambda i: (0, 0)), vec],
        out_shape=[_sds((L - T, D), F32), _sds((N_META, D), F32), _sds((1, D), F32)],
        compiler_params=_params(1),
    )(dh1, du1, h0, g)


def _window_sum(z, g, shift_sign, L):
    s = z
    for j in range(POOL_GROUPS):
        k = 1 << j
        nxt = s + pltpu.roll(s, k if shift_sign > 0 else L - k, 0)
        s = jnp.where(j <= g, nxt, s)
    return s


def _inv_count(g, L):
    t = lax.broadcasted_iota(jnp.int32, (L, 1), 0)
    w = jnp.left_shift(2, g)
    cnt = jnp.clip(t - (PAD_ROWS - 1), 1, w)
    return 1.0 / cnt.astype(F32)


def _pool_fwd(proj, w_grp, scale):
    L = proj.shape[0]
    G, GD, _ = w_grp.shape
    P = G * GD

    def body(z_ref, w_ref, sc_ref, d_ref, ya_ref):
        g = pl.program_id(0)
        z = z_ref[...]
        d = (_window_sum(z, g, +1, L) * _inv_count(g, L) - z).astype(BF16)
        d_ref[...] = d
        y = jnp.dot(d, w_ref[...], preferred_element_type=F32)
        ya_ref[...] = (y * sc_ref[...]).astype(BF16)

    col = pl.BlockSpec((L, GD), lambda g: (0, g))
    return pl.pallas_call(
        body, name="pool_fwd", grid=(G,),
        in_specs=[col, pl.BlockSpec((None, GD, GD), lambda g: (g, 0, 0)), pl.BlockSpec((1, GD), lambda g: (0, g))],
        out_specs=[col, col], out_shape=[_sds((L, P), BF16), _sds((L, P), BF16)],
        compiler_params=_params(1),
    )(proj, w_grp, scale)


def _pool_bwd(dya, d, w_grp, scale):
    L, P = dya.shape
    G, GD, _ = w_grp.shape

    def body(dya_ref, d_ref, w_ref, sc_ref, dz_ref, dw_ref, dsc_ref):
        g = pl.program_id(0)
        dya_ = dya_ref[...]
        d_ = d_ref[...]
        w = w_ref[...]
        y = jnp.dot(d_, w, preferred_element_type=F32)
        dsc_ref[...] = jnp.sum(dya_ * y, axis=0, keepdims=True)
        dy = (dya_ * sc_ref[...]).astype(BF16)
        dw_ref[...] = lax.dot_general(d_, dy, TN, preferred_element_type=F32)
        dd = lax.dot_general(dy, w, NT, preferred_element_type=F32)
        dz = _window_sum(dd * _inv_count(g, L), g, -1, L) - dd
        dz_ref[...] = dz.astype(BF16)

    col = pl.BlockSpec((L, GD), lambda g: (0, g))
    wspec = pl.BlockSpec((None, GD, GD), lambda g: (g, 0, 0))
    vec = pl.BlockSpec((1, GD), lambda g: (0, g))
    return pl.pallas_call(
        body, name="pool_bwd", grid=(G,),
        in_specs=[col, col, wspec, vec], out_specs=[col, wspec, vec],
        out_shape=[_sds((L, P), BF16), _sds((G, GD, GD), F32), _sds((1, P), F32)],
        compiler_params=_params(1),
    )(dya, d, w_grp, scale)


def _fill_rotations(rot_ref, ext):
    n = ext.shape[0]
    rot_ref[0] = ext
    for r in range(1, 8):
        rot_ref[r] = pltpu.roll(ext, n - r, 0)


def _lane_chunks(C):
    step = LANES if C % LANES == 0 else C
    return [(c0, step) for c0 in range(0, C, step)]


def _conv_specs(L, C, col_v, col_g):
    T = CONV_ROWS
    per = T // HALO
    cur_v = pl.BlockSpec((T, C), lambda i: (i, col_v))
    cur_g = pl.BlockSpec((T, C), lambda i: (i, col_g))
    prev_v = pl.BlockSpec((HALO, C), lambda i: (jnp.maximum(i * per - 1, 0), col_v))
    prev_g = pl.BlockSpec((HALO, C), lambda i: (jnp.maximum(i * per - 1, 0), col_g))
    return cur_v, cur_g, prev_v, prev_g


def _glu_ext(vc, gc, vh, gh, i):
    a_cur = vc[...] * _sigmoid(gc[...])
    a_prev = vh[...] * _sigmoid(gh[...]) * (i > 0).astype(F32)
    return jnp.concatenate([a_prev, a_cur], axis=0)


def _conv_fwd(proj, C, w_dw, b_dw, ln_g, ln_b):
    L = proj.shape[0]
    T = CONV_ROWS
    P = C

    def body(vc, gc, vh, gh, w_ref, b_ref, lg_ref, lb_ref, s_ref, c_ref, rot):
        i = pl.program_id(0)
        _fill_rotations(rot, _glu_ext(vc, gc, vh, gh, i))
        for c0, cw in _lane_chunks(C):
            acc = jnp.zeros((T, cw), F32)
            for k in range(CONV_TAPS):
                q, r = divmod(HALO - (CONV_TAPS - 1) + k, 8)
                acc = acc + w_ref[k:k + 1, c0:c0 + cw] * rot[r, 8 * q:8 * q + T, c0:c0 + cw]
            c_ref[:, c0:c0 + cw] = acc + b_ref[:, c0:c0 + cw]
        c = c_ref[...]
        mu = jnp.mean(c, axis=-1, keepdims=True)
        cen = c - mu
        var = jnp.mean(cen * cen, axis=-1, keepdims=True)
        ln = cen * lax.rsqrt(var + LN_EPS) * lg_ref[...] + lb_ref[...]
        s_ref[...] = (ln * _sigmoid(ln)).astype(BF16)

    cur_v, cur_g, prev_v, prev_g = _conv_specs(L, C, P // C, P // C + 1)
    row = pl.BlockSpec((T, C), lambda i: (i, 0))
    vec = pl.BlockSpec((1, C), lambda i: (0, 0))
    return pl.pallas_call(
        body, name="conv_fwd", grid=(L // T,),
        in_specs=[cur_v, cur_g, prev_v, prev_g, pl.BlockSpec((CONV_TAPS, C), lambda i: (0, 0)), vec, vec, vec],
        out_specs=[row, row], out_shape=[_sds((L, C), BF16), _sds((L, C), F32)],
        scratch_shapes=[pltpu.VMEM((8, T + HALO, C), F32)], compiler_params=_params(1),
    )(proj, proj, proj, proj, w_dw, b_dw, ln_g, ln_b)


def _conv_ln_bwd(ds, c, ln_g, ln_b):
    L, C = c.shape
    T = _row_tile(L)

    def body(ds_ref, c_ref, lg_ref, lb_ref, dc_ref, dlg_ref, dlb_ref, db_ref):
        i = pl.program_id(0)
        c_ = c_ref[...]
        g = lg_ref[...]
        mu = jnp.mean(c_, axis=-1, keepdims=True)
        cen = c_ - mu
        rstd = lax.rsqrt(jnp.mean(cen * cen, axis=-1, keepdims=True) + LN_EPS)
        xhat = cen * rstd
        ln = xhat * g + lb_ref[...]
        sg = _sigmoid(ln)
        dln = ds_ref[...] * (sg * (1.0 + ln * (1.0 - sg)))
        dxh = dln * g
        dc = rstd * (dxh - jnp.mean(dxh, axis=-1, keepdims=True)
                     - xhat * jnp.mean(dxh * xhat, axis=-1, keepdims=True))
        dc_ref[...] = dc
        dlg = jnp.sum(dln * xhat, axis=0, keepdims=True)
        dlb = jnp.sum(dln, axis=0, keepdims=True)
        db = jnp.sum(dc, axis=0, keepdims=True)

        @pl.when(i == 0)
        def _():
            dlg_ref[...] = dlg
            dlb_ref[...] = dlb
            db_ref[...] = db

        @pl.when(i > 0)
        def _():
            dlg_ref[...] += dlg
            dlb_ref[...] += dlb
            db_ref[...] += db

    row = pl.BlockSpec((T, C), lambda i: (i, 0))
    vec = pl.BlockSpec((1, C), lambda i: (0, 0))
    return pl.pallas_call(
        body, name="conv_ln_bwd", grid=(L // T,),
        in_specs=[row, row, vec, vec], out_specs=[row, vec, vec, vec],
        out_shape=[_sds((L, C), F32), _sds((1, C), F32), _sds((1, C), F32), _sds((1, C), F32)],
        compiler_params=_params(1),
    )(ds, c, ln_g, ln_b)


def _conv_bwd(dc, proj, C, w_dw):
    L = proj.shape[0]
    T = CONV_ROWS
    per = T // HALO
    n = L // T
    P = C
    taps_pad = 32

    def body(dcc, dcn, vc, gc, vh, gh, w_ref, dv_ref, dg_ref, dw_ref, rot_a, rot_d):
        i = pl.program_id(0)
        _fill_rotations(rot_a, _glu_ext(vc, gc, vh, gh, i))
        dc_cur = dcc[...]
        dc_next = dcn[...] * (i < n - 1).astype(F32)
        _fill_rotations(rot_d, jnp.concatenate([dc_cur, dc_next], axis=0))

        @pl.when(i == 0)
        def _():
            dw_ref[...] = jnp.zeros(dw_ref.shape, F32)

        for c0, cw in _lane_chunks(C):
            dcs = dc_cur[:, c0:c0 + cw]
            da = jnp.zeros((T, cw), F32)
            for k in range(CONV_TAPS):
                q, r = divmod(CONV_TAPS - 1 - k, 8)
                da = da + w_ref[k:k + 1, c0:c0 + cw] * rot_d[r, 8 * q:8 * q + T, c0:c0 + cw]
                q, r = divmod(HALO - (CONV_TAPS - 1) + k, 8)
                dw_ref[k:k + 1, c0:c0 + cw] += jnp.sum(dcs * rot_a[r, 8 * q:8 * q + T, c0:c0 + cw],
                                                      axis=0, keepdims=True)
            v = vc[:, c0:c0 + cw]
            sg = _sigmoid(gc[:, c0:c0 + cw])
            dv_ref[:, c0:c0 + cw] = (da * sg).astype(BF16)
            dg_ref[:, c0:c0 + cw] = (da * v * sg * (1.0 - sg)).astype(BF16)

    cur_v, cur_g, prev_v, prev_g = _conv_specs(L, C, P // C, P // C + 1)
    row = pl.BlockSpec((T, C), lambda i: (i, 0))
    nxt = pl.BlockSpec((HALO, C), lambda i: (jnp.minimum((i + 1) * per, L // HALO - 1), 0))
    wspec = pl.BlockSpec((CONV_TAPS, C), lambda i: (0, 0))
    return pl.pallas_call(
        body, name="conv_bwd", grid=(n,),
        in_specs=[row, nxt, cur_v, cur_g, prev_v, prev_g, wspec],
        out_specs=[row, row, pl.BlockSpec((taps_pad, C), lambda i: (0, 0))],
        out_shape=[_sds((L, C), BF16), _sds((L, C), BF16), _sds((taps_pad, C), F32)],
        scratch_shapes=[pltpu.VMEM((8, T + HALO, C), F32), pltpu.VMEM((8, T + HALO, C), F32)],
        compiler_params=_params(1),
    )(dc, dc, proj, proj, proj, proj, w_dw)


def _mix_fwd(ya_pre, s, wpo, wco, proj, D):
    L, P = ya_pre.shape
    Q, _, DS = wpo.shape
    bm = _pick(L, 1088)
    gate0 = (proj.shape[1] - 2 * D) // DS
    per = D // DS

    def body(a1, a2, b1, b2, ga, gb, m_ref, ya_ref, yb_ref):
        ya = jnp.dot(a1[...], b1[...], preferred_element_type=F32)
        yb = jnp.dot(a2[...], b2[...], preferred_element_type=F32)
        ya_ref[...] = ya
        yb_ref[...] = yb
        m_ref[...] = (_sigmoid(ga[...]) * ya + _sigmoid(gb[...]) * yb).astype(BF16)

    act = pl.BlockSpec((bm, P), lambda i, q: (i, 0))
    wsp = pl.BlockSpec((None, P, DS), lambda i, q: (q, 0, 0))
    out = pl.BlockSpec((bm, DS), lambda i, q: (i, q))
    return pl.pallas_call(
        body, name="mix_fwd", grid=(L // bm, Q),
        in_specs=[act, act, wsp, wsp,
                  pl.BlockSpec((bm, DS), lambda i, q: (i, gate0 + q)),
                  pl.BlockSpec((bm, DS), lambda i, q: (i, gate0 + per + q))],
        out_specs=[out, out, out],
        out_shape=[_sds((L, D), BF16), _sds((L, D), F32), _sds((L, D), F32)],
        compiler_params=_params(2),
    )(ya_pre, s, wpo, wco, proj, proj)


def _mix_bwd(do, w_o, proj, ya, yb):
    L, D = do.shape
    bm = _pick(L, 1088)
    bn = _pick(D // N_CHIPS, 512)
    gate0 = (proj.shape[1] - 2 * D) // bn
    per = D // bn

    def epilogue(dm, extras, outs):
        ga, gb, ya_ref, yb_ref = extras
        sa = _sigmoid(ga[...])
        sb = _sigmoid(gb[...])
        outs[0][...] = (dm * sa).astype(BF16)
        outs[1][...] = (dm * sb).astype(BF16)
        outs[2][...] = (dm * ya_ref[...] * sa * (1.0 - sa)).astype(BF16)
        outs[3][...] = (dm * yb_ref[...] * sb * (1.0 - sb)).astype(BF16)

    blk = pl.BlockSpec((bm, bn), lambda i, j: (i, j))
    return _mm(
        "mix_bwd", (L // bm, D // bn), [do, w_o, proj, proj, ya, yb],
        [pl.BlockSpec((bm, D), lambda i, j: (i, 0)), pl.BlockSpec((bn, D), lambda i, j: (j, 0)),
         pl.BlockSpec((bm, bn), lambda i, j: (i, gate0 + j)),
         pl.BlockSpec((bm, bn), lambda i, j: (i, gate0 + per + j)), blk, blk],
        [_sds((L, D), BF16)] * 4, [blk] * 4, NT, 1, epilogue)


def _mm_act_colw(name, a, wg, bn_pref, epilogue=_store, out_dtypes=(F32,)):
    L, K = a.shape
    Q, _, n = wg.shape
    bn = _pick(n, bn_pref)
    nj = n // bn
    out = pl.BlockSpec((L, bn), lambda q, j: (0, q * nj + j))
    return _mm(name, (Q, nj), [a, wg],
               [pl.BlockSpec((L, K), lambda q, j: (0, 0)), pl.BlockSpec((None, K, bn), lambda q, j: (q, 0, j))],
               [_sds((L, Q * n), dt) for dt in out_dtypes], [out] * len(out_dtypes), NN, 1, epilogue)


def _mm_grad_colw_t(name, g, wg, bm_pref, bn_pref):
    L = g.shape[0]
    Q, K, n = wg.shape
    bm = _pick(L, bm_pref)
    bn = _pick(K, bn_pref)
    return _mm(name, (L // bm, K // bn, Q), [g, wg],
               [pl.BlockSpec((bm, n), lambda i, j, k: (i, k)), pl.BlockSpec((None, bn, n), lambda i, j, k: (k, j, 0))],
               [_sds((L, K), F32)], [pl.BlockSpec((bm, bn), lambda i, j, k: (i, j))], NT, Q,
               acc_shape=(bm, bn))


def _mm_wgrad_colw(name, a, g, Q, bm_pref, bn_pref):
    L, K = a.shape
    n = g.shape[1] // Q
    bm = _pick(K, bm_pref)
    bn = _pick(n, bn_pref)
    nj = n // bn
    return _mm(name, (Q, K // bm, nj), [a, g],
               [pl.BlockSpec((L, bm), lambda q, i, j: (0, i)), pl.BlockSpec((L, bn), lambda q, i, j: (0, q * nj + j))],
               [_sds((Q, K, n), F32)], [pl.BlockSpec((None, bm, bn), lambda q, i, j: (q, i, j))], TN, 1)


def _mm_wgrad(name, a, g, bm_pref, bn_pref):
    L, K = a.shape
    N = g.shape[1]
    bm = _pick(K, bm_pref)
    bn = _pick(N, bn_pref)
    return _mm(name, (K // bm, N // bn), [a, g],
               [pl.BlockSpec((L, bm), lambda i, j: (0, i)), pl.BlockSpec((L, bn), lambda i, j: (0, j))],
               [_sds((K, N), F32)], [pl.BlockSpec((bm, bn), lambda i, j: (i, j))], TN, 1)


def _mm_act_roww(name, a, w, bm_pref, bn_pref, bk_pref):
    L, K = a.shape
    N = w.shape[1]
    bm, bn, bk = _pick(L, bm_pref), _pick(N, bn_pref), _pick(K, bk_pref)
    nk = K // bk
    return _mm(name, (L // bm, N // bn, nk), [a, w],
               [pl.BlockSpec((bm, bk), lambda i, j, k: (i, k)), pl.BlockSpec((bk, bn), lambda i, j, k: (k, j))],
               [_sds((L, N), F32)], [pl.BlockSpec((bm, bn), lambda i, j, k: (i, j))], NN, nk,
               acc_shape=(bm, bn))


def _up_epilogue(val, extras, outs):
    outs[0][...] = val
    r = jnp.maximum(val, 0.0)
    outs[1][...] = (r * r).astype(BF16)


def _mlp_down_bwd(df, w_down, a_up):
    L, D = df.shape
    F = w_down.shape[0]
    bm = _pick(L, 1088)
    bn = _pick(F, 1024)

    def epilogue(val, extras, outs):
        outs[0][...] = (val * (2.0 * jnp.maximum(extras[0][...], 0.0))).astype(BF16)

    blk = pl.BlockSpec((bm, bn), lambda i, j: (i, j))
    return _mm("mlp_down_bwd", (L // bm, F // bn), [df, w_down, a_up],
               [pl.BlockSpec((bm, D), lambda i, j: (i, 0)), pl.BlockSpec((bn, D), lambda i, j: (j, 0)), blk],
               [_sds((L, F), BF16)], [blk], NT, 1, epilogue)


def _block_fwd_bwd(h0, tgt, vecs, w_grp, w_dw, win_g, wpo_g, wco_g, w_o, wup_g, w_down):
    L, D = h0.shape
    P = D // 2
    C = D // 2
    Q = win_g.shape[0]

    u1 = _pre_norm(h0, vecs["g_pre_mix"])
    proj = _mm_act_colw("proj", u1, win_g, 256)
    d, ya_pre = _pool_fwd(proj, w_grp, vecs["pool_scale"])
    s, c = _conv_fwd(proj, C, w_dw, vecs["b_dw"], vecs["conv_ln_g"], vecs["conv_ln_b"])
    m, ya, yb = _mix_fwd(ya_pre, s, wpo_g, wco_g, proj, D)
    o = _mm_act_roww("attn_out", m, w_o, 1088, 1024, 2048)
    h1, u2 = _mid_norm(o, h0, vecs["g_post_mix"], vecs["g_pre_mlp"])
    a_up, fact = _mm_act_colw("mlp_up", u2, wup_g, 512, _up_epilogue, (F32, BF16))
    f = _mm_act_roww("mlp_down", fact, w_down, 1088, 1024, 2048)
    dy, df, dg_post_mlp, loss = _loss_head(f, h1, tgt, vecs["g_post_mlp"])

    g_w_down = _mm_wgrad("dw_down", fact, df, 1024, 1024)
    da_up = _mlp_down_bwd(df, w_down, a_up)
    g_w_up = _mm_wgrad_colw("dw_up", u2, da_up, Q, 1024, 1024)
    du2 = _mm_grad_colw_t("du2", da_up, wup_g, 1088, 1024)
    dh1, do, dg_pre_mlp, dg_post_mix = _mid_norm_bwd(dy, du2, h1, o, vecs["g_pre_mlp"], vecs["g_post_mix"])
    g_w_o = _mm_wgrad("dw_o", m, do, 1024, 1024)
    dya, dyb, dga, dgb = _mix_bwd(do, w_o, proj, ya, yb)
    g_wpo = _mm_wgrad_colw("dw_pool_out", ya_pre, dya, Q, 1024, 512)
    g_wco = _mm_wgrad_colw("dw_conv_out", s, dyb, Q, 1024, 512)
    dya_pre = _mm_grad_colw_t("dya_pre", dya, wpo_g, 1088, 1024)
    ds = _mm_grad_colw_t("ds", dyb, wco_g, 1088, 1024)
    dz, g_w_grp, dscale = _pool_bwd(dya_pre, d, w_grp, vecs["pool_scale"])
    dc, dln_g, dln_b, db_dw = _conv_ln_bwd(ds, c, vecs["conv_ln_g"], vecs["conv_ln_b"])
    dv, dgc, g_w_dw = _conv_bwd(dc, proj, C, w_dw)
    dproj = jnp.concatenate([dz, dv, dgc, dga, dgb], axis=1)
    g_w_in = _mm_wgrad_colw("dw_in", u1, dproj, Q, 512, 1792)
    du1 = _mm_grad_colw_t("du1", dproj, win_g, 1088, 1024)
    grad_x, dmeta, dg_pre_mix = _pre_norm_bwd(dh1, du1, h0, vecs["g_pre_mix"])

    small = dict(g_pre_mix=dg_pre_mix, pool_scale=dscale, b_dw=db_dw, conv_ln_g=dln_g, conv_ln_b=dln_b,
                 g_post_mix=dg_post_mix, g_pre_mlp=dg_pre_mlp, g_post_mlp=dg_post_mlp)
    big = dict(w_in=g_w_in, w_pool_out=g_wpo, w_conv_out=g_wco, w_o=g_w_o, w_up=g_w_up, w_down=g_w_down,
               w_pool_grp=g_w_grp)
    return loss, grad_x, dmeta, g_w_dw, small, big


ANY = pl.BlockSpec(memory_space=pl.ANY)
N_CHIPS = 4


def _place():
    x, y, c = lax.axis_index("x"), lax.axis_index("y"), lax.axis_index("c")
    return x, y, c


def _chip_at(x, y, k):
    px = 1 - x if k & 2 else x
    py = 1 - y if k & 1 else y
    return px, py


def _gather_weights(big, small):
    nb, ns = len(big), len(small)

    def body(*refs):
        b_in, s_in = refs[:nb], refs[nb:nb + ns]
        b_out, s_out = refs[nb + ns:2 * nb + ns], refs[2 * nb + ns:2 * (nb + ns)]
        send, recv, fsend, frecv, ssend, srecv, lsem = refs[2 * (nb + ns):]
        x, y, c = _place()
        p = 2 * x + y
        sibling = (x, y, 1 - c)

        def half(ref, q, which):
            h = ref.shape[1] // 2
            return ref.at[q, pl.ds(which * h, h)]

        local = []
        for n in range(nb):
            local.append(pltpu.make_async_copy(b_in[n], b_out[n].at[p], lsem.at[n]))
        for n in range(ns):
            local.append(pltpu.make_async_copy(s_in[n], s_out[n].at[p], lsem.at[nb + n]))
        for cp in local:
            cp.start()

        first = []
        for n in range(nb):
            h = b_in[n].shape[0] // 2
            for k in range(1, N_CHIPS):
                px, py = _chip_at(x, y, k)
                first.append(pltpu.make_async_remote_copy(
                    src_ref=b_in[n].at[pl.ds(c * h, h)], dst_ref=half(b_out[n], p, c),
                    send_sem=send.at[n, k - 1], recv_sem=recv.at[n, k - 1],
                    device_id=(px, py, c), device_id_type=MESH))
        for n in range(ns):
            for k in range(1, N_CHIPS):
                px, py = _chip_at(x, y, k)
                first.append(pltpu.make_async_remote_copy(
                    src_ref=s_in[n], dst_ref=s_out[n].at[p],
                    send_sem=ssend.at[n, k - 1], recv_sem=srecv.at[n, k - 1],
                    device_id=(px, py, c), device_id_type=MESH))
        for cp in first:
            cp.start()

        passed = []
        for n in range(nb):
            for k in range(1, N_CHIPS):
                px, py = _chip_at(x, y, k)
                q = 2 * px + py
                landed = half(b_out[n], q, c)
                pltpu.make_async_remote_copy(
                    src_ref=landed, dst_ref=landed, send_sem=send.at[n, k - 1], recv_sem=recv.at[n, k - 1],
                    device_id=sibling, device_id_type=MESH).wait_recv()
                fwd = pltpu.make_async_remote_copy(
                    src_ref=landed, dst_ref=landed, send_sem=fsend.at[n, k - 1], recv_sem=frecv.at[n, k - 1],
                    device_id=sibling, device_id_type=MESH)
                fwd.start()
                passed.append(fwd)
        for n in range(nb):
            for k in range(1, N_CHIPS):
                px, py = _chip_at(x, y, k)
                q = 2 * px + py
                other = half(b_out[n], q, 1 - c)
                pltpu.make_async_remote_copy(
                    src_ref=other, dst_ref=other, send_sem=fsend.at[n, k - 1], recv_sem=frecv.at[n, k - 1],
                    device_id=sibling, device_id_type=MESH).wait_recv()
        for n in range(ns):
            for k in range(1, N_CHIPS):
                px, py = _chip_at(x, y, k)
                q = 2 * px + py
                pltpu.make_async_remote_copy(
                    src_ref=s_out[n].at[q], dst_ref=s_out[n].at[q], send_sem=ssend.at[n, k - 1],
                    recv_sem=srecv.at[n, k - 1], device_id=sibling, device_id_type=MESH).wait_recv()
        for cp in first + passed:
            cp.wait_send()
        for cp in local:
            cp.wait()

    out_shape = [_sds((N_CHIPS,) + a.shape, a.dtype) for a in list(big) + list(small)]
    res = pl.pallas_call(
        body, name="gather_weights", in_specs=[ANY] * (nb + ns), out_specs=[ANY] * (nb + ns), out_shape=out_shape,
        scratch_shapes=[pltpu.SemaphoreType.DMA((nb, 3)), pltpu.SemaphoreType.DMA((nb, 3)),
                        pltpu.SemaphoreType.DMA((nb, 3)), pltpu.SemaphoreType.DMA((nb, 3)),
                        pltpu.SemaphoreType.DMA((ns, 3)), pltpu.SemaphoreType.DMA((ns, 3)),
                        pltpu.SemaphoreType.DMA((nb + ns,))],
        compiler_params=pltpu.CompilerParams(has_side_effects=True),
    )(*big, *small)
    return res[:nb], res[nb:]


def _gather_packs(pack):
    n_dev = 8

    def body(in_ref, out_ref, send, recv, lsem):
        x, y, c = _place()
        me = 4 * x + 2 * y + c
        mine = pltpu.make_async_copy(in_ref, out_ref.at[me], lsem)
        mine.start()
        sends = []
        for r in range(1, n_dev):
            peer = (1 - x if r & 4 else x, 1 - y if r & 2 else y, 1 - c if r & 1 else c)
            cp = pltpu.make_async_remote_copy(
                src_ref=in_ref, dst_ref=out_ref.at[me], send_sem=send.at[r - 1], recv_sem=recv.at[r - 1],
                device_id=peer, device_id_type=MESH)
            cp.start()
            sends.append(cp)
        for r in range(1, n_dev):
            peer = (1 - x if r & 4 else x, 1 - y if r & 2 else y, 1 - c if r & 1 else c)
            src = 4 * peer[0] + 2 * peer[1] + peer[2]
            pltpu.make_async_remote_copy(
                src_ref=in_ref, dst_ref=out_ref.at[src], send_sem=send.at[r - 1], recv_sem=recv.at[r - 1],
                device_id=peer, device_id_type=MESH).wait_recv()
        for cp in sends:
            cp.wait_send()
        mine.wait()

    return pl.pallas_call(
        body, name="gather_packs", in_specs=[ANY], out_specs=ANY, out_shape=_sds((n_dev,) + pack.shape, pack.dtype),
        scratch_shapes=[pltpu.SemaphoreType.DMA((n_dev - 1,)), pltpu.SemaphoreType.DMA((n_dev - 1,)),
                        pltpu.SemaphoreType.DMA],
        compiler_params=pltpu.CompilerParams(has_side_effects=True),
    )(pack)


def _swap_halves(grads):
    n = len(grads)

    def body(*refs):
        g_in, g_out = refs[:n], refs[n:2 * n]
        send, recv = refs[2 * n:]
        x, y, c = _place()
        cps = []
        for a in range(n):
            h = g_in[a].shape[1] // 2
            cp = pltpu.make_async_remote_copy(
                src_ref=g_in[a].at[:, pl.ds((1 - c) * h, h)], dst_ref=g_out[a],
                send_sem=send.at[a], recv_sem=recv.at[a], device_id=(x, y, 1 - c), device_id_type=MESH)
            cp.start()
            cps.append(cp)
        for cp in cps:
            cp.wait()

    out_shape = [_sds((g.shape[0], g.shape[1] // 2, g.shape[2]), g.dtype) for g in grads]
    return pl.pallas_call(
        body, name="swap_halves", in_specs=[ANY] * n, out_specs=[ANY] * n, out_shape=out_shape,
        scratch_shapes=[pltpu.SemaphoreType.DMA((n,)), pltpu.SemaphoreType.DMA((n,))],
        compiler_params=pltpu.CompilerParams(has_side_effects=True),
    )(*grads)


def _scatter_partials(parts):
    n = len(parts)

    def body(*refs):
        p_in, p_out = refs[:n], refs[n:2 * n]
        send, recv = refs[2 * n:]
        x, y, c = _place()
        cps = []
        for a in range(n):
            for k in range(1, N_CHIPS):
                px, py = _chip_at(x, y, k)
                cp = pltpu.make_async_remote_copy(
                    src_ref=p_in[a].at[2 * px + py], dst_ref=p_out[a].at[k - 1],
                    send_sem=send.at[a, k - 1], recv_sem=recv.at[a, k - 1],
                    device_id=(px, py, c), device_id_type=MESH)
                cp.start()
                cps.append(cp)
        for cp in cps:
            cp.wait()

    out_shape = [_sds((N_CHIPS - 1,) + p.shape[1:], p.dtype) for p in parts]
    return pl.pallas_call(
        body, name="scatter_partials", in_specs=[ANY] * n, out_specs=[ANY] * n, out_shape=out_shape,
        scratch_shapes=[pltpu.SemaphoreType.DMA((n, 3)), pltpu.SemaphoreType.DMA((n, 3))],
        compiler_params=pltpu.CompilerParams(has_side_effects=True),
    )(*parts)


def _share_halves(halves):
    n = len(halves)

    def body(*refs):
        t_in, t_out = refs[:n], refs[n:2 * n]
        send, recv, lsem = refs[2 * n:]
        x, y, c = _place()
        cps = []
        for a in range(n):
            h = t_in[a].shape[0]
            mine = t_out[a].at[pl.ds(c * h, h)]
            loc = pltpu.make_async_copy(t_in[a], mine, lsem.at[a])
            loc.start()
            cp = pltpu.make_async_remote_copy(
                src_ref=t_in[a], dst_ref=mine, send_sem=send.at[a], recv_sem=recv.at[a],
                device_id=(x, y, 1 - c), device_id_type=MESH)
            cp.start()
            cps.append((loc, cp))
        for a, (loc, cp) in enumerate(cps):
            h = t_in[a].shape[0]
            other = t_out[a].at[pl.ds((1 - c) * h, h)]
            pltpu.make_async_remote_copy(
                src_ref=t_in[a], dst_ref=other, send_sem=send.at[a], recv_sem=recv.at[a],
                device_id=(x, y, 1 - c), device_id_type=MESH).wait_recv()
            cp.wait_send()
            loc.wait()

    out_shape = [_sds((2 * t.shape[0], t.shape[1]), t.dtype) for t in halves]
    return pl.pallas_call(
        body, name="share_halves", in_specs=[ANY] * n, out_specs=[ANY] * n, out_shape=out_shape,
        scratch_shapes=[pltpu.SemaphoreType.DMA((n,)), pltpu.SemaphoreType.DMA((n,)), pltpu.SemaphoreType.DMA((n,))],
        compiler_params=pltpu.CompilerParams(has_side_effects=True),
    )(*halves)


def _elem_tile(rows, cols):
    return _pick(rows, max(8, (1 << 19) // cols // 8 * 8))


def _chip_partial(grad, recv, c_idx, p_idx):
    Q, R, C = grad.shape
    h = R // 2
    T = _elem_tile(h, C)
    nt = h // T

    def body(sc_ref, g_ref, r_ref, sb_ref, own_ref):
        q = pl.program_id(1)
        s = g_ref[...] + r_ref[...]
        sb_ref[...] = s.astype(BF16)

        @pl.when(q == sc_ref[1])
        def _():
            own_ref[...] = s

    grid_spec = pltpu.PrefetchScalarGridSpec(
        num_scalar_prefetch=1, grid=(nt, Q),
        in_specs=[pl.BlockSpec((None, T, C), lambda t, q, sc: (q, sc[0] * nt + t, 0)),
                  pl.BlockSpec((None, T, C), lambda t, q, sc: (q, t, 0))],
        out_specs=[pl.BlockSpec((None, T, C), lambda t, q, sc: (q, t, 0)),
                   pl.BlockSpec((T, C), lambda t, q, sc: (t, 0))])
    return pl.pallas_call(
        body, name="chip_partial", grid_spec=grid_spec,
        out_shape=[_sds((Q, h, C), BF16), _sds((h, C), F32)], compiler_params=_params(2),
    )(jnp.stack([c_idx, p_idx]).astype(jnp.int32), grad, recv)


def _sum_partials(own, parts):
    h, C = own.shape
    T = _elem_tile(h, C)

    def body(o_ref, p_ref, t_ref):
        t = o_ref[...]
        for k in range(N_CHIPS - 1):
            t = t + p_ref[k].astype(F32)
        t_ref[...] = t

    return pl.pallas_call(
        body, name="sum_partials", grid=(h // T,),
        in_specs=[pl.BlockSpec((T, C), lambda i: (i, 0)), pl.BlockSpec((N_CHIPS - 1, T, C), lambda i: (0, i, 0))],
        out_specs=pl.BlockSpec((T, C), lambda i: (i, 0)), out_shape=_sds((h, C), F32), compiler_params=_params(1),
    )(own, parts)


def _pack_rows(name, parts):
    width = parts[0].shape[1]
    offsets, at = [], 0
    for p in parts:
        offsets.append(at)
        at += p.shape[0]
    total = -(-at // 8) * 8

    def body(*refs):
        out = refs[-1]
        out[...] = jnp.zeros(out.shape, F32)
        for ref, o in zip(refs[:-1], offsets):
            out[o:o + ref.shape[0], :] = ref[...]

    return pl.pallas_call(body, name=name, out_shape=_sds((total, width), F32))(*parts)


def _sum_packs(packs):
    n, R, C = packs.shape

    def body(p_ref, o_ref):
        t = p_ref[0]
        for k in range(1, n):
            t = t + p_ref[k]
        o_ref[...] = t

    return pl.pallas_call(
        body, name="sum_packs", grid=(1,), in_specs=[pl.BlockSpec((n, R, C), lambda i: (0, 0, 0))],
        out_specs=pl.BlockSpec((R, C), lambda i: (0, 0)), out_shape=_sds((R, C), F32), compiler_params=_params(1),
    )(packs)


def _adamw(w, g, m, v):
    R, C = w.shape
    T = _elem_tile(R, C)

    def body(w_ref, g_ref, m_ref, v_ref, d_ref, m2_ref, v2_ref):
        g_ = g_ref[...]
        m2 = ADAM_B1 * m_ref[...] + (1.0 - ADAM_B1) * g_
        v2 = ADAM_B2 * v_ref[...] + (1.0 - ADAM_B2) * (g_ * g_)
        m_hat = m2 / (1.0 - ADAM_B1 ** ADAM_STEP)
        v_hat = v2 / (1.0 - ADAM_B2 ** ADAM_STEP)
        d_ref[...] = -ADAM_LR * (m_hat / (jnp.sqrt(v_hat) + ADAM_EPS) + ADAM_WD * w_ref[...])
        m2_ref[...] = m2
        v2_ref[...] = v2

    blk = pl.BlockSpec((T, C), lambda i: (i, 0))
    return pl.pallas_call(
        body, name="adamw", grid=(R // T,), in_specs=[blk] * 4, out_specs=[blk] * 3,
        out_shape=[_sds((R, C), F32)] * 3, compiler_params=_params(1),
    )(w, g, m, v)


BIG = ("w_in", "w_pool_out", "w_conv_out", "w_o", "w_up", "w_down", "w_pool_grp")
VECTORS = ("g_pre_mix", "pool_scale", "b_dw", "conv_ln_g", "conv_ln_b", "g_post_mix", "g_pre_mlp", "g_post_mlp")
WEIGHTS = ("meta", "g_pre_mix", "w_in", "w_pool_grp", "pool_scale", "w_pool_out", "w_dw", "b_dw", "conv_ln_g",
           "conv_ln_b", "w_conv_out", "w_o", "g_post_mix", "g_pre_mlp", "w_up", "w_down", "g_post_mlp")


def _as_rows(a, width):
    r, cols = a.shape
    return a.reshape(r * (cols // width), width)


def _step(w, m, v, x, tgt):
    S, D = x.shape
    P = D // 2
    xi, yi, ci = _place()
    chip = 2 * xi + yi

    shard2d = {k: w[k].reshape(-1, w[k].shape[-1]) for k in BIG}
    big_g, small_g = _gather_weights([shard2d[k].astype(BF16) for k in BIG], [w["w_dw"], w["meta"]])
    gathered = dict(zip(BIG, big_g))
    G = POOL_GROUPS
    GD = P // G
    GS = GD // N_CHIPS
    w_grp = gathered["w_pool_grp"].reshape(N_CHIPS, G, GS, GD).transpose(1, 0, 2, 3).reshape(G, GD, GD)
    w_dw = small_g[0].transpose(1, 0, 2).reshape(CONV_TAPS, P)
    meta = small_g[1].transpose(1, 0, 2).reshape(N_META, D)
    w_o = gathered["w_o"].reshape(D, D)
    w_down = gathered["w_down"].reshape(-1, D)

    h0 = jnp.concatenate([jnp.zeros((PAD_ROWS, D), F32), meta, x], axis=0)
    vecs = {k: w[k] for k in VECTORS}
    loss, grad_x, dmeta, g_w_dw, g_vec, g_big = _block_fwd_bwd(
        h0, tgt, vecs, w_grp, w_dw, gathered["w_in"], gathered["w_pool_out"], gathered["w_conv_out"], w_o,
        gathered["w_up"], w_down)

    g_big["w_o"] = g_big["w_o"].reshape(N_CHIPS, D // N_CHIPS, D)
    g_big["w_down"] = g_big["w_down"].reshape(N_CHIPS, -1, D)
    g_big["w_pool_grp"] = g_big["w_pool_grp"].reshape(G, N_CHIPS, GS, GD).transpose(1, 0, 2, 3).reshape(
        N_CHIPS, G * GS, GD)
    slabs = [g_big[k] for k in BIG]
    from_sibling = _swap_halves(slabs)
    partial = [_chip_partial(g, r, ci, chip) for g, r in zip(slabs, from_sibling)]
    received = _scatter_partials([pb for pb, _ in partial])
    halves = [_sum_partials(own, rc) for (_, own), rc in zip(partial, received)]
    grads = dict(zip(BIG, _share_halves(halves)))

    rows = [g_w_dw, _as_rows(dmeta, P)] + [_as_rows(g_vec[k], P) for k in VECTORS]
    rows.append(jnp.broadcast_to(loss[:, :1], (1, P)))
    total = _sum_packs(_gather_packs(_pack_rows("pack_grads", rows)))
    at = 0
    taps_pad = g_w_dw.shape[0]
    g_dw_full = total[at:at + CONV_TAPS]
    at += taps_pad
    g_meta_full = total[at:at + 2 * N_META].reshape(N_META, D)
    at += 2 * N_META
    for k in VECTORS:
        n = w[k].shape[-1] // P
        grads[k] = total[at:at + n].reshape(1, n * P)
        at += n
    loss_total = total[at, 0]
    grads["w_dw"] = lax.dynamic_slice_in_dim(g_dw_full, chip * (P // N_CHIPS), P // N_CHIPS, axis=1)
    grads["meta"] = lax.dynamic_slice_in_dim(g_meta_full, chip * (D // N_CHIPS), D // N_CHIPS, axis=1)

    delta, new_m, new_v = {}, {}, {}
    for k in BIG:
        delta[k], new_m[k], new_v[k] = _adamw(shard2d[k], grads[k], m[k].reshape(shard2d[k].shape),
                                              v[k].reshape(shard2d[k].shape))
    small_names = [k for k in WEIGHTS if k not in BIG]

    def pack_small(tree):
        parts = []
        for k in small_names:
            a = tree[k].reshape(-1, tree[k].shape[-1])
            flat = a.reshape(-1)
            parts.append(jnp.pad(flat, (0, -flat.shape[0] % P)).reshape(-1, P))
        return _pack_rows("pack_small", parts)

    sd, sm, sv = _adamw(pack_small(w), pack_small(grads), pack_small(m), pack_small(v))
    at = 0
    for k in small_names:
        a = w[k].reshape(-1, w[k].shape[-1])
        n = -(-a.size // P)
        for tree, packed in ((delta, sd), (new_m, sm), (new_v, sv)):
            tree[k] = packed[at:at + n].reshape(-1)[:a.size].reshape(a.shape)
        at += n

    return loss_total, grad_x, grads, delta, new_m, new_v


def kernel(x, meta, g_pre_mix, w_in, w_pool_grp, pool_scale, w_pool_out, w_dw, b_dw, conv_ln_g, conv_ln_b, w_conv_out, w_o, g_post_mix, g_pre_mlp, w_up, w_down, g_post_mlp, loss_target, m_meta, m_g_pre_mix, m_w_in, m_w_pool_grp, m_pool_scale, m_w_pool_out, m_w_dw, m_b_dw, m_conv_ln_g, m_conv_ln_b, m_w_conv_out, m_w_o, m_g_post_mix, m_g_pre_mlp, m_w_up, m_w_down, m_g_post_mlp, v_meta, v_g_pre_mix, v_w_in, v_w_pool_grp, v_pool_scale, v_w_pool_out, v_w_dw, v_b_dw, v_conv_ln_g, v_conv_ln_b, v_w_conv_out, v_w_o, v_g_post_mix, v_g_pre_mlp, v_w_up, v_w_down, v_g_post_mlp):
    args = dict(locals())
    shapes = {k: args[k].shape for k in WEIGHTS}
    w = {k: args[k] for k in WEIGHTS}
    m = {k: args["m_" + k] for k in WEIGHTS}
    v = {k: args["v_" + k] for k in WEIGHTS}
    for tree in (w, m, v):
        tree["w_dw"] = tree["w_dw"].reshape(tree["w_dw"].shape[-2:])
    loss, grad_x, grads, delta, new_m, new_v = _step(w, m, v, x[0], loss_target[0])
    out = [loss, grad_x[None]]
    for tree in (grads, delta, new_m, new_v):
        out += [tree[k].reshape(shapes[k]) for k in WEIGHTS]
    return tuple(out)
```

```python
import functools

import jax
import jax.numpy as jnp
from jax import lax
from jax.experimental import pallas as pl
from jax.experimental.pallas import tpu as pltpu

F32 = jnp.float32
BF16 = jnp.bfloat16

N_META = 16
PAD_ROWS = 112
TOKEN_ROW0 = PAD_ROWS + N_META
POOL_GROUPS = 4
CONV_TAPS = 31
HALO = 32
CONV_ROWS = 128
LANES = 128
RMS_EPS = 1e-6
LN_EPS = 1e-5
ADAM_LR = 0.001
ADAM_B1 = 0.9
ADAM_B2 = 0.999
ADAM_EPS = 1e-08
ADAM_WD = 0.01
ADAM_STEP = 10
VMEM_LIMIT_MB = 56

MESH = pl.DeviceIdType.MESH
NN = (((1,), (0,)), ((), ()))
NT = (((1,), (1,)), ((), ()))
TN = (((0,), (0,)), ((), ()))


def _pick(n, pref):
    if n <= pref:
        return n
    if n % pref == 0:
        return pref
    for step in (LANES, 8, 1):
        t = (pref // step) * step
        while t >= step:
            if n % t == 0:
                return t
            t -= step
    return n


def _params(n_axes, vmem_mb=VMEM_LIMIT_MB):
    return pltpu.CompilerParams(dimension_semantics=("arbitrary",) * n_axes,
                                vmem_limit_bytes=vmem_mb << 20)


def _sigmoid(x):
    return jax.nn.sigmoid(x)


def _store(val, extras, outs):
    outs[0][...] = val.astype(outs[0].dtype)


def _mm(name, grid, arrays, in_specs, out_shapes, out_specs, dims, nk, epilogue=_store, acc_shape=None):
    n_in, n_out = len(arrays), len(out_shapes)

    def body(*refs):
        extras = refs[2:n_in]
        outs = refs[n_in:n_in + n_out]
        part = lax.dot_general(refs[0][...], refs[1][...], dims, preferred_element_type=F32)
        if nk == 1:
            epilogue(part, extras, outs)
        else:
            acc = refs[n_in + n_out]
            k = pl.program_id(len(grid) - 1)

            @pl.when(k == 0)
            def _():
                acc[...] = part

            @pl.when(k > 0)
            def _():
                acc[...] += part

            @pl.when(k == nk - 1)
            def _():
                epilogue(acc[...], extras, outs)

    scratch = [pltpu.VMEM(acc_shape, F32)] if nk > 1 else []
    single = n_out == 1
    res = pl.pallas_call(
        body, name=name, grid=grid, in_specs=in_specs,
        out_specs=out_specs[0] if single else out_specs,
        out_shape=out_shapes[0] if single else out_shapes,
        scratch_shapes=scratch, compiler_params=_params(len(grid)),
    )(*arrays)
    return res


def _sds(shape, dtype):
    return jax.ShapeDtypeStruct(shape, dtype)


def _rms_scale(h):
    return lax.rsqrt(jnp.mean(h * h, axis=-1, keepdims=True) + RMS_EPS)


def _rms_bwd(du, h, g):
    r = _rms_scale(h)
    y = h * r
    dy = du * g
    dh = r * (dy - y * jnp.mean(dy * y, axis=-1, keepdims=True))
    return dh, jnp.sum(du * y, axis=0, keepdims=True)


def _row_tile(L):
    return _pick(L, 272)


def _pre_norm(h0, g):
    L, D = h0.shape
    T = _row_tile(L)

    def body(h_ref, g_ref, u_ref):
        h = h_ref[...]
        u_ref[...] = (h * _rms_scale(h) * g_ref[...]).astype(BF16)

    return pl.pallas_call(
        body, name="pre_norm", grid=(L // T,),
        in_specs=[pl.BlockSpec((T, D), lambda i: (i, 0)), pl.BlockSpec((1, D), lambda i: (0, 0))],
        out_specs=pl.BlockSpec((T, D), lambda i: (i, 0)),
        out_shape=_sds((L, D), BF16), compiler_params=_params(1),
    )(h0, g)


def _mid_norm(o, h0, g_post, g_pre):
    L, D = h0.shape
    T = _row_tile(L)

    def body(o_ref, h_ref, gp_ref, gm_ref, h1_ref, u2_ref):
        o_ = o_ref[...]
        h1 = h_ref[...] + o_ * _rms_scale(o_) * gp_ref[...]
        h1_ref[...] = h1
        u2_ref[...] = (h1 * _rms_scale(h1) * gm_ref[...]).astype(BF16)

    row = pl.BlockSpec((T, D), lambda i: (i, 0))
    vec = pl.BlockSpec((1, D), lambda i: (0, 0))
    return pl.pallas_call(
        body, name="mid_norm", grid=(L // T,),
        in_specs=[row, row, vec, vec], out_specs=[row, row],
        out_shape=[_sds((L, D), F32), _sds((L, D), BF16)], compiler_params=_params(1),
    )(o, h0, g_post, g_pre)


def _loss_head(f, h1, tgt, g_post):
    L, D = h1.shape
    T = TOKEN_ROW0
    n = L // T

    def body(f_ref, h_ref, t_ref, g_ref, dy_ref, df_ref, dg_ref, loss_ref):
        i = pl.program_id(0)
        f_ = f_ref[...]
        g = g_ref[...]
        y = h_ref[...] + f_ * _rms_scale(f_) * g
        live = (i > 0).astype(F32)
        diff = (y - t_ref[...]) * live
        part = 0.5 * jnp.sum(jnp.mean(diff * diff, axis=-1, keepdims=True), axis=0, keepdims=True)
        dy = diff * (1.0 / D)
        dy_ref[...] = dy
        df, dg = _rms_bwd(dy, f_, g)
        df_ref[...] = df.astype(BF16)

        @pl.when(i == 0)
        def _():
            dg_ref[...] = dg
            loss_ref[...] = jnp.broadcast_to(part, loss_ref.shape)

        @pl.when(i > 0)
        def _():
            dg_ref[...] += dg
            loss_ref[...] += jnp.broadcast_to(part, loss_ref.shape)

    row = pl.BlockSpec((T, D), lambda i: (i, 0))
    vec = pl.BlockSpec((1, D), lambda i: (0, 0))
    return pl.pallas_call(
        body, name="loss_head", grid=(n,),
        in_specs=[row, row, pl.BlockSpec((T, D), lambda i: (jnp.maximum(i - 1, 0), 0)), vec],
        out_specs=[row, row, vec, pl.BlockSpec((1, LANES), lambda i: (0, 0))],
        out_shape=[_sds((L, D), F32), _sds((L, D), BF16), _sds((1, D), F32), _sds((1, LANES), F32)],
        compiler_params=_params(1),
    )(f, h1, tgt, g_post)


def _mid_norm_bwd(dy, du2, h1, o, g_pre, g_post):
    L, D = h1.shape
    T = _row_tile(L)

    def body(dy_ref, du_ref, h_ref, o_ref, gm_ref, gp_ref, dh1_ref, do_ref, dgm_ref, dgp_ref):
        i = pl.program_id(0)
        dh, dgm = _rms_bwd(du_ref[...], h_ref[...], gm_ref[...])
        dh1 = dy_ref[...] + dh
        dh1_ref[...] = dh1
        do, dgp = _rms_bwd(dh1, o_ref[...], gp_ref[...])
        do_ref[...] = do.astype(BF16)

        @pl.when(i == 0)
        def _():
            dgm_ref[...] = dgm
            dgp_ref[...] = dgp

        @pl.when(i > 0)
        def _():
            dgm_ref[...] += dgm
            dgp_ref[...] += dgp

    row = pl.BlockSpec((T, D), lambda i: (i, 0))
    vec = pl.BlockSpec((1, D), lambda i: (0, 0))
    return pl.pallas_call(
        body, name="mid_norm_bwd", grid=(L // T,),
        in_specs=[row, row, row, row, vec, vec], out_specs=[row, row, vec, vec],
        out_shape=[_sds((L, D), F32), _sds((L, D), BF16), _sds((1, D), F32), _sds((1, D), F32)],
        compiler_params=_params(1),
    )(dy, du2, h1, o, g_pre, g_post)


def _pre_norm_bwd(dh1, du1, h0, g):
    L, D = h0.shape
    T = TOKEN_ROW0
    n = L // T

    def body(dh_ref, du_ref, h_ref, g_ref, gx_ref, dmeta_ref, dg_ref):
        i = pl.program_id(0)
        dh, dg = _rms_bwd(du_ref[...], h_ref[...], g_ref[...])
        dh0 = dh_ref[...] + dh
        gx_ref[...] = dh0

        @pl.when(i == 0)
        def _():
            dmeta_ref[...] = dh0[PAD_ROWS:, :]
            dg_ref[...] = dg

        @pl.when(i > 0)
        def _():
            dg_ref[...] += dg

    row = pl.BlockSpec((T, D), lambda i: (i, 0))
    vec = pl.BlockSpec((1, D), lambda i: (0, 0))
    return pl.pallas_call(
        body, name="pre_norm_bwd", grid=(n,),
        in_specs=[row, row, row, vec],
        out_specs=[pl.BlockSpec((T, D), lambda i: (jnp.maximum(i - 1, 0), 0)),
                   pl.BlockSpec((N_META, D), lambda i: (0, 0)), vec],
        out_shape=[_sds((L - T, D), F32), _sds((N_META, D), F32), _sds((1, D), F32)],
        compiler_params=_params(1),
    )(dh1, du1, h0, g)


def _window_sum(z, g, shift_sign, L):
    s = z
    for j in range(POOL_GROUPS):
        k = 1 << j
        nxt = s + pltpu.roll(s, k if shift_sign > 0 else L - k, 0)
        s = jnp.where(j <= g, nxt, s)
    return s


def _inv_count(g, L):
    t = lax.broadcasted_iota(jnp.int32, (L, 1), 0)
    w = jnp.left_shift(2, g)
    cnt = jnp.clip(t - (PAD_ROWS - 1), 1, w)
    return 1.0 / cnt.astype(F32)


def _pool_fwd(proj, w_grp, scale):
    L = proj.shape[0]
    G, GD, _ = w_grp.shape
    P = G * GD

    def body(z_ref, w_ref, sc_ref, d_ref, ya_ref):
        g = pl.program_id(0)
        z = z_ref[...]
        d = (_window_sum(z, g, +1, L) * _inv_count(g, L) - z).astype(BF16)
        d_ref[...] = d
        y = jnp.dot(d, w_ref[...], preferred_element_type=F32)
        ya_ref[...] = (y * sc_ref[...]).astype(BF16)

    col = pl.BlockSpec((L, GD), lambda g: (0, g))
    return pl.pallas_call(
        body, name="pool_fwd", grid=(G,),
        in_specs=[col, pl.BlockSpec((None, GD, GD), lambda g: (g, 0, 0)), pl.BlockSpec((1, GD), lambda g: (0, g))],
        out_specs=[col, col], out_shape=[_sds((L, P), BF16), _sds((L, P), BF16)],
        compiler_params=_params(1),
    )(proj, w_grp, scale)


def _pool_bwd(dya, d, w_grp, scale):
    L, P = dya.shape
    G, GD, _ = w_grp.shape

    def body(dya_ref, d_ref, w_ref, sc_ref, dz_ref, dw_ref, dsc_ref):
        g = pl.program_id(0)
        dya_ = dya_ref[...]
        d_ = d_ref[...]
        w = w_ref[...]
        y = jnp.dot(d_, w, preferred_element_type=F32)
        dsc_ref[...] = jnp.sum(dya_ * y, axis=0, keepdims=True)
        dy = (dya_ * sc_ref[...]).astype(BF16)
        dw_ref[...] = lax.dot_general(d_, dy, TN, preferred_element_type=F32)
        dd = lax.dot_general(dy, w, NT, preferred_element_type=F32)
        dz = _window_sum(dd * _inv_count(g, L), g, -1, L) - dd
        dz_ref[...] = dz.astype(BF16)

    col = pl.BlockSpec((L, GD), lambda g: (0, g))
    wspec = pl.BlockSpec((None, GD, GD), lambda g: (g, 0, 0))
    vec = pl.BlockSpec((1, GD), lambda g: (0, g))
    return pl.pallas_call(
        body, name="pool_bwd", grid=(G,),
        in_specs=[col, col, wspec, vec], out_specs=[col, wspec, vec],
        out_shape=[_sds((L, P), BF16), _sds((G, GD, GD), F32), _sds((1, P), F32)],
        compiler_params=_params(1),
    )(dya, d, w_grp, scale)


def _fill_rotations(rot_ref, ext):
    n = ext.shape[0]
    rot_ref[0] = ext
    for r in range(1, 8):
        rot_ref[r] = pltpu.roll(ext, n - r, 0)


def _lane_chunks(C):
    step = LANES if C % LANES == 0 else C
    return [(c0, step) for c0 in range(0, C, step)]


def _conv_specs(L, C, col_v, col_g):
    T = CONV_ROWS
    per = T // HALO
    cur_v = pl.BlockSpec((T, C), lambda i: (i, col_v))
    cur_g = pl.BlockSpec((T, C), lambda i: (i, col_g))
    prev_v = pl.BlockSpec((HALO, C), lambda i: (jnp.maximum(i * per - 1, 0), col_v))
    prev_g = pl.BlockSpec((HALO, C), lambda i: (jnp.maximum(i * per - 1, 0), col_g))
    return cur_v, cur_g, prev_v, prev_g


def _glu_ext(vc, gc, vh, gh, i):
    a_cur = vc[...] * _sigmoid(gc[...])
    a_prev = vh[...] * _sigmoid(gh[...]) * (i > 0).astype(F32)
    return jnp.concatenate([a_prev, a_cur], axis=0)


def _conv_fwd(proj, C, w_dw, b_dw, ln_g, ln_b):
    L = proj.shape[0]
    T = CONV_ROWS
    P = C

    def body(vc, gc, vh, gh, w_ref, b_ref, lg_ref, lb_ref, s_ref, c_ref, rot):
        i = pl.program_id(0)
        _fill_rotations(rot, _glu_ext(vc, gc, vh, gh, i))
        for c0, cw in _lane_chunks(C):
            acc = jnp.zeros((T, cw), F32)
            for k in range(CONV_TAPS):
                q, r = divmod(HALO - (CONV_TAPS - 1) + k, 8)
                acc = acc + w_ref[k:k + 1, c0:c0 + cw] * rot[r, 8 * q:8 * q + T, c0:c0 + cw]
            c_ref[:, c0:c0 + cw] = acc + b_ref[:, c0:c0 + cw]
        c = c_ref[...]
        mu = jnp.mean(c, axis=-1, keepdims=True)
        cen = c - mu
        var = jnp.mean(cen * cen, axis=-1, keepdims=True)
        ln = cen * lax.rsqrt(var + LN_EPS) * lg_ref[...] + lb_ref[...]
        s_ref[...] = (ln * _sigmoid(ln)).astype(BF16)

    cur_v, cur_g, prev_v, prev_g = _conv_specs(L, C, P // C, P // C + 1)
    row = pl.BlockSpec((T, C), lambda i: (i, 0))
    vec = pl.BlockSpec((1, C), lambda i: (0, 0))
    return pl.pallas_call(
        body, name="conv_fwd", grid=(L // T,),
        in_specs=[cur_v, cur_g, prev_v, prev_g, pl.BlockSpec((CONV_TAPS, C), lambda i: (0, 0)), vec, vec, vec],
        out_specs=[row, row], out_shape=[_sds((L, C), BF16), _sds((L, C), F32)],
        scratch_shapes=[pltpu.VMEM((8, T + HALO, C), F32)], compiler_params=_params(1),
    )(proj, proj, proj, proj, w_dw, b_dw, ln_g, ln_b)


def _conv_ln_bwd(ds, c, ln_g, ln_b):
    L, C = c.shape
    T = _row_tile(L)

    def body(ds_ref, c_ref, lg_ref, lb_ref, dc_ref, dlg_ref, dlb_ref, db_ref):
        i = pl.program_id(0)
        c_ = c_ref[...]
        g = lg_ref[...]
        mu = jnp.mean(c_, axis=-1, keepdims=True)
        cen = c_ - mu
        rstd = lax.rsqrt(jnp.mean(cen * cen, axis=-1, keepdims=True) + LN_EPS)
        xhat = cen * rstd
        ln = xhat * g + lb_ref[...]
        sg = _sigmoid(ln)
        dln = ds_ref[...] * (sg * (1.0 + ln * (1.0 - sg)))
        dxh = dln * g
        dc = rstd * (dxh - jnp.mean(dxh, axis=-1, keepdims=True)
                     - xhat * jnp.mean(dxh * xhat, axis=-1, keepdims=True))
        dc_ref[...] = dc
        dlg = jnp.sum(dln * xhat, axis=0, keepdims=True)
        dlb = jnp.sum(dln, axis=0, keepdims=True)
        db = jnp.sum(dc, axis=0, keepdims=True)

        @pl.when(i == 0)
        def _():
            dlg_ref[...] = dlg
            dlb_ref[...] = dlb
            db_ref[...] = db

        @pl.when(i > 0)
        def _():
            dlg_ref[...] += dlg
            dlb_ref[...] += dlb
            db_ref[...] += db

    row = pl.BlockSpec((T, C), lambda i: (i, 0))
    vec = pl.BlockSpec((1, C), lambda i: (0, 0))
    return pl.pallas_call(
        body, name="conv_ln_bwd", grid=(L // T,),
        in_specs=[row, row, vec, vec], out_specs=[row, vec, vec, vec],
        out_shape=[_sds((L, C), F32), _sds((1, C), F32), _sds((1, C), F32), _sds((1, C), F32)],
        compiler_params=_params(1),
    )(ds, c, ln_g, ln_b)


def _conv_bwd(dc, proj, C, w_dw):
    L = proj.shape[0]
    T = CONV_ROWS
    per = T // HALO
    n = L // T
    P = C
    taps_pad = 32

    def body(dcc, dcn, vc, gc, vh, gh, w_ref, dv_ref, dg_ref, dw_ref, rot_a, rot_d):
        i = pl.program_id(0)
        _fill_rotations(rot_a, _glu_ext(vc, gc, vh, gh, i))
        dc_cur = dcc[...]
        dc_next = dcn[...] * (i < n - 1).astype(F32)
        _fill_rotations(rot_d, jnp.concatenate([dc_cur, dc_next], axis=0))

        @pl.when(i == 0)
        def _():
            dw_ref[...] = jnp.zeros(dw_ref.shape, F32)

        for c0, cw in _lane_chunks(C):
            dcs = dc_cur[:, c0:c0 + cw]
            da = jnp.zeros((T, cw), F32)
            for k in range(CONV_TAPS):
                q, r = divmod(CONV_TAPS - 1 - k, 8)
                da = da + w_ref[k:k + 1, c0:c0 + cw] * rot_d[r, 8 * q:8 * q + T, c0:c0 + cw]
                q, r = divmod(HALO - (CONV_TAPS - 1) + k, 8)
                dw_ref[k:k + 1, c0:c0 + cw] += jnp.sum(dcs * rot_a[r, 8 * q:8 * q + T, c0:c0 + cw],
                                                      axis=0, keepdims=True)
            v = vc[:, c0:c0 + cw]
            sg = _sigmoid(gc[:, c0:c0 + cw])
            dv_ref[:, c0:c0 + cw] = (da * sg).astype(BF16)
            dg_ref[:, c0:c0 + cw] = (da * v * sg * (1.0 - sg)).astype(BF16)

    cur_v, cur_g, prev_v, prev_g = _conv_specs(L, C, P // C, P // C + 1)
    row = pl.BlockSpec((T, C), lambda i: (i, 0))
    nxt = pl.BlockSpec((HALO, C), lambda i: (jnp.minimum((i + 1) * per, L // HALO - 1), 0))
    wspec = pl.BlockSpec((CONV_TAPS, C), lambda i: (0, 0))
    return pl.pallas_call(
        body, name="conv_bwd", grid=(n,),
        in_specs=[row, nxt, cur_v, cur_g, prev_v, prev_g, wspec],
        out_specs=[row, row, pl.BlockSpec((taps_pad, C), lambda i: (0, 0))],
        out_shape=[_sds((L, C), BF16), _sds((L, C), BF16), _sds((taps_pad, C), F32)],
        scratch_shapes=[pltpu.VMEM((8, T + HALO, C), F32), pltpu.VMEM((8, T + HALO, C), F32)],
        compiler_params=_params(1),
    )(dc, dc, proj, proj, proj, proj, w_dw)


def _mix_fwd(ya_pre, s, wpo, wco, proj, D):
    L, P = ya_pre.shape
    Q, _, DS = wpo.shape
    bm = _pick(L, 1088)
    gate0 = (proj.shape[1] - 2 * D) // DS
    per = D // DS

    def body(a1, a2, b1, b2, ga, gb, m_ref, ya_ref, yb_ref):
        ya = jnp.dot(a1[...], b1[...], preferred_element_type=F32)
        yb = jnp.dot(a2[...], b2[...], preferred_element_type=F32)
        ya_ref[...] = ya
        yb_ref[...] = yb
        m_ref[...] = (_sigmoid(ga[...]) * ya + _sigmoid(gb[...]) * yb).astype(BF16)

    act = pl.BlockSpec((bm, P), lambda i, q: (i, 0))
    wsp = pl.BlockSpec((None, P, DS), lambda i, q: (q, 0, 0))
    out = pl.BlockSpec((bm, DS), lambda i, q: (i, q))
    return pl.pallas_call(
        body, name="mix_fwd", grid=(L // bm, Q),
        in_specs=[act, act, wsp, wsp,
                  pl.BlockSpec((bm, DS), lambda i, q: (i, gate0 + q)),
                  pl.BlockSpec((bm, DS), lambda i, q: (i, gate0 + per + q))],
        out_specs=[out, out, out],
        out_shape=[_sds((L, D), BF16), _sds((L, D), F32), _sds((L, D), F32)],
        compiler_params=_params(2),
    )(ya_pre, s, wpo, wco, proj, proj)


def _mix_bwd(do, w_o, proj, ya, yb):
    L, D = do.shape
    bm = _pick(L, 544)
    bn = _pick(D // N_CHIPS, 512)
    gate0 = (proj.shape[1] - 2 * D) // bn
    per = D // bn

    def epilogue(dm, extras, outs):
        ga, gb, ya_ref, yb_ref = extras
        sa = _sigmoid(ga[...])
        sb = _sigmoid(gb[...])
        outs[0][...] = (dm * sa).astype(BF16)
        outs[1][...] = (dm * sb).astype(BF16)
        outs[2][...] = (dm * ya_ref[...] * sa * (1.0 - sa)).astype(BF16)
        outs[3][...] = (dm * yb_ref[...] * sb * (1.0 - sb)).astype(BF16)

    blk = pl.BlockSpec((bm, bn), lambda i, j: (i, j))
    return _mm(
        "mix_bwd", (L // bm, D // bn), [do, w_o, proj, proj, ya, yb],
        [pl.BlockSpec((bm, D), lambda i, j: (i, 0)), pl.BlockSpec((bn, D), lambda i, j: (j, 0)),
         pl.BlockSpec((bm, bn), lambda i, j: (i, gate0 + j)),
         pl.BlockSpec((bm, bn), lambda i, j: (i, gate0 + per + j)), blk, blk],
        [_sds((L, D), BF16)] * 4, [blk] * 4, NT, 1, epilogue)


def _mm_act_colw(name, a, wg, bn_pref, epilogue=_store, out_dtypes=(F32,)):
    L, K = a.shape
    Q, _, n = wg.shape
    bn = _pick(n, bn_pref)
    nj = n // bn
    out = pl.BlockSpec((L, bn), lambda q, j: (0, q * nj + j))
    return _mm(name, (Q, nj), [a, wg],
               [pl.BlockSpec((L, K), lambda q, j: (0, 0)), pl.BlockSpec((None, K, bn), lambda q, j: (q, 0, j))],
               [_sds((L, Q * n), dt) for dt in out_dtypes], [out] * len(out_dtypes), NN, 1, epilogue)


def _mm_grad_colw_t(name, g, wg, bm_pref, bn_pref):
    L = g.shape[0]
    Q, K, n = wg.shape
    bm = _pick(L, bm_pref)
    bn = _pick(K, bn_pref)
    return _mm(name, (L // bm, K // bn, Q), [g, wg],
               [pl.BlockSpec((bm, n), lambda i, j, k: (i, k)), pl.BlockSpec((None, bn, n), lambda i, j, k: (k, j, 0))],
               [_sds((L, K), F32)], [pl.BlockSpec((bm, bn), lambda i, j, k: (i, j))], NT, Q,
               acc_shape=(bm, bn))


def _mm_wgrad_colw(name, a, g, Q, bm_pref, bn_pref):
    L, K = a.shape
    n = g.shape[1] // Q
    bm = _pick(K, bm_pref)
    bn = _pick(n, bn_pref)
    nj = n // bn
    return _mm(name, (Q, K // bm, nj), [a, g],
               [pl.BlockSpec((L, bm), lambda q, i, j: (0, i)), pl.BlockSpec((L, bn), lambda q, i, j: (0, q * nj + j))],
               [_sds((Q, K, n), F32)], [pl.BlockSpec((None, bm, bn), lambda q, i, j: (q, i, j))], TN, 1)


def _mm_wgrad(name, a, g, bm_pref, bn_pref):
    L, K = a.shape
    N = g.shape[1]
    bm = _pick(K, bm_pref)
    bn = _pick(N, bn_pref)
    return _mm(name, (K // bm, N // bn), [a, g],
               [pl.BlockSpec((L, bm), lambda i, j: (0, i)), pl.BlockSpec((L, bn), lambda i, j: (0, j))],
               [_sds((K, N), F32)], [pl.BlockSpec((bm, bn), lambda i, j: (i, j))], TN, 1)


def _mm_act_roww(name, a, w, bm_pref, bn_pref, bk_pref):
    L, K = a.shape
    N = w.shape[1]
    bm, bn, bk = _pick(L, bm_pref), _pick(N, bn_pref), _pick(K, bk_pref)
    nk = K // bk
    return _mm(name, (L // bm, N // bn, nk), [a, w],
               [pl.BlockSpec((bm, bk), lambda i, j, k: (i, k)), pl.BlockSpec((bk, bn), lambda i, j, k: (k, j))],
               [_sds((L, N), F32)], [pl.BlockSpec((bm, bn), lambda i, j, k: (i, j))], NN, nk,
               acc_shape=(bm, bn))


def _up_epilogue(val, extras, outs):
    outs[0][...] = val
    r = jnp.maximum(val, 0.0)
    outs[1][...] = (r * r).astype(BF16)


def _mlp_down_bwd(df, w_down, a_up):
    L, D = df.shape
    F = w_down.shape[0]
    bm = _pick(L, 1088)
    bn = _pick(F, 1024)

    def epilogue(val, extras, outs):
        outs[0][...] = (val * (2.0 * jnp.maximum(extras[0][...], 0.0))).astype(BF16)

    blk = pl.BlockSpec((bm, bn), lambda i, j: (i, j))
    return _mm("mlp_down_bwd", (L // bm, F // bn), [df, w_down, a_up],
               [pl.BlockSpec((bm, D), lambda i, j: (i, 0)), pl.BlockSpec((bn, D), lambda i, j: (j, 0)), blk],
               [_sds((L, F), BF16)], [blk], NT, 1, epilogue)


def _block_fwd_bwd(h0, tgt, vecs, w_grp, w_dw, win_g, wpo_g, wco_g, w_o, wup_g, w_down):
    L, D = h0.shape
    P = D // 2
    C = D // 2
    Q = win_g.shape[0]

    u1 = _pre_norm(h0, vecs["g_pre_mix"])
    proj = _mm_act_colw("proj", u1, win_g, 256)
    d, ya_pre = _pool_fwd(proj, w_grp, vecs["pool_scale"])
    s, c = _conv_fwd(proj, C, w_dw, vecs["b_dw"], vecs["conv_ln_g"], vecs["conv_ln_b"])
    m, ya, yb = _mix_fwd(ya_pre, s, wpo_g, wco_g, proj, D)
    o = _mm_act_roww("attn_out", m, w_o, 1088, 1024, 2048)
    h1, u2 = _mid_norm(o, h0, vecs["g_post_mix"], vecs["g_pre_mlp"])
    a_up, fact = _mm_act_colw("mlp_up", u2, wup_g, 512, _up_epilogue, (F32, BF16))
    f = _mm_act_roww("mlp_down", fact, w_down, 1088, 1024, 2048)
    dy, df, dg_post_mlp, loss = _loss_head(f, h1, tgt, vecs["g_post_mlp"])

    g_w_down = _mm_wgrad("dw_down", fact, df, 1024, 1024)
    da_up = _mlp_down_bwd(df, w_down, a_up)
    g_w_up = _mm_wgrad_colw("dw_up", u2, da_up, Q, 1024, 1024)
    du2 = _mm_grad_colw_t("du2", da_up, wup_g, 1088, 1024)
    dh1, do, dg_pre_mlp, dg_post_mix = _mid_norm_bwd(dy, du2, h1, o, vecs["g_pre_mlp"], vecs["g_post_mix"])
    g_w_o = _mm_wgrad("dw_o", m, do, 1024, 1024)
    dya, dyb, dga, dgb = _mix_bwd(do, w_o, proj, ya, yb)
    g_wpo = _mm_wgrad_colw("dw_pool_out", ya_pre, dya, Q, 1024, 512)
    g_wco = _mm_wgrad_colw("dw_conv_out", s, dyb, Q, 1024, 512)
    dya_pre = _mm_grad_colw_t("dya_pre", dya, wpo_g, 1088, 1024)
    ds = _mm_grad_colw_t("ds", dyb, wco_g, 1088, 1024)
    dz, g_w_grp, dscale = _pool_bwd(dya_pre, d, w_grp, vecs["pool_scale"])
    dc, dln_g, dln_b, db_dw = _conv_ln_bwd(ds, c, vecs["conv_ln_g"], vecs["conv_ln_b"])
    dv, dgc, g_w_dw = _conv_bwd(dc, proj, C, w_dw)
    dproj = jnp.concatenate([dz, dv, dgc, dga, dgb], axis=1)
    g_w_in = _mm_wgrad_colw("dw_in", u1, dproj, Q, 512, 1792)
    du1 = _mm_grad_colw_t("du1", dproj, win_g, 1088, 1024)
    grad_x, dmeta, dg_pre_mix = _pre_norm_bwd(dh1, du1, h0, vecs["g_pre_mix"])

    small = dict(g_pre_mix=dg_pre_mix, pool_scale=dscale, b_dw=db_dw, conv_ln_g=dln_g, conv_ln_b=dln_b,
                 g_post_mix=dg_post_mix, g_pre_mlp=dg_pre_mlp, g_post_mlp=dg_post_mlp)
    big = dict(w_in=g_w_in, w_pool_out=g_wpo, w_conv_out=g_wco, w_o=g_w_o, w_up=g_w_up, w_down=g_w_down,
               w_pool_grp=g_w_grp)
    return loss, grad_x, dmeta, g_w_dw, small, big


ANY = pl.BlockSpec(memory_space=pl.ANY)
N_CHIPS = 4


def _place():
    x, y, c = lax.axis_index("x"), lax.axis_index("y"), lax.axis_index("c")
    return x, y, c


def _chip_at(x, y, k):
    px = 1 - x if k & 2 else x
    py = 1 - y if k & 1 else y
    return px, py


def _cast_into_slab(w2d, chip):
    R, C = w2d.shape
    T = _elem_tile(R, C)

    def body(p_ref, w_ref, o_ref):
        o_ref[...] = w_ref[...].astype(BF16)

    grid_spec = pltpu.PrefetchScalarGridSpec(
        num_scalar_prefetch=1, grid=(R // T,),
        in_specs=[pl.BlockSpec((T, C), lambda i, p: (i, 0))],
        out_specs=pl.BlockSpec((None, T, C), lambda i, p: (p[0], i, 0)))
    return pl.pallas_call(
        body, name="cast_into_slab", grid_spec=grid_spec, out_shape=_sds((N_CHIPS, R, C), BF16),
        compiler_params=_params(1),
    )(jnp.reshape(chip, (1,)).astype(jnp.int32), w2d)


def _gather_weights(big, small):
    nb, ns = len(big), len(small)

    def body(*refs):
        s_in = refs[nb:nb + ns]
        b_out, s_out = refs[nb + ns:2 * nb + ns], refs[2 * nb + ns:2 * (nb + ns)]
        send, recv, fsend, frecv, ssend, srecv, lsem = refs[2 * (nb + ns):]
        x, y, c = _place()
        p = 2 * x + y
        sibling = (x, y, 1 - c)

        def half(ref, q, which):
            h = ref.shape[1] // 2
            return ref.at[q, pl.ds(which * h, h)]

        first = []
        for n in range(nb):
            for k in range(1, N_CHIPS):
                px, py = _chip_at(x, y, k)
                first.append(pltpu.make_async_remote_copy(
                    src_ref=half(b_out[n], p, c), dst_ref=half(b_out[n], p, c),
                    send_sem=send.at[n, k - 1], recv_sem=recv.at[n, k - 1],
                    device_id=(px, py, c), device_id_type=MESH))
        for n in range(ns):
            for k in range(1, N_CHIPS):
                px, py = _chip_at(x, y, k)
                first.append(pltpu.make_async_remote_copy(
                    src_ref=s_in[n], dst_ref=s_out[n].at[p],
                    send_sem=ssend.at[n, k - 1], recv_sem=srecv.at[n, k - 1],
                    device_id=(px, py, c), device_id_type=MESH))
        for cp in first:
            cp.start()
        local = [pltpu.make_async_copy(s_in[n], s_out[n].at[p], lsem.at[n]) for n in range(ns)]
        for cp in local:
            cp.start()

        passed = []
        for n in range(nb):
            for k in range(1, N_CHIPS):
                px, py = _chip_at(x, y, k)
                q = 2 * px + py
                landed = half(b_out[n], q, c)
                pltpu.make_async_remote_copy(
                    src_ref=landed, dst_ref=landed, send_sem=send.at[n, k - 1], recv_sem=recv.at[n, k - 1],
                    device_id=sibling, device_id_type=MESH).wait_recv()
                fwd = pltpu.make_async_remote_copy(
                    src_ref=landed, dst_ref=landed, send_sem=fsend.at[n, k - 1], recv_sem=frecv.at[n, k - 1],
                    device_id=sibling, device_id_type=MESH)
                fwd.start()
                passed.append(fwd)
        for n in range(nb):
            for k in range(1, N_CHIPS):
                px, py = _chip_at(x, y, k)
                q = 2 * px + py
                other = half(b_out[n], q, 1 - c)
                pltpu.make_async_remote_copy(
                    src_ref=other, dst_ref=other, send_sem=fsend.at[n, k - 1], recv_sem=frecv.at[n, k - 1],
                    device_id=sibling, device_id_type=MESH).wait_recv()
        for n in range(ns):
            for k in range(1, N_CHIPS):
                px, py = _chip_at(x, y, k)
                q = 2 * px + py
                pltpu.make_async_remote_copy(
                    src_ref=s_out[n].at[q], dst_ref=s_out[n].at[q], send_sem=ssend.at[n, k - 1],
                    recv_sem=srecv.at[n, k - 1], device_id=sibling, device_id_type=MESH).wait_recv()
        for cp in first + passed:
            cp.wait_send()
        for cp in local:
            cp.wait()

    out_shape = [_sds(a.shape, a.dtype) for a in big] + [_sds((N_CHIPS,) + a.shape, a.dtype) for a in small]
    res = pl.pallas_call(
        body, name="gather_weights", in_specs=[ANY] * (nb + ns), out_specs=[ANY] * (nb + ns), out_shape=out_shape,
        input_output_aliases={n: n for n in range(nb)},
        scratch_shapes=[pltpu.SemaphoreType.DMA((nb, 3)), pltpu.SemaphoreType.DMA((nb, 3)),
                        pltpu.SemaphoreType.DMA((nb, 3)), pltpu.SemaphoreType.DMA((nb, 3)),
                        pltpu.SemaphoreType.DMA((ns, 3)), pltpu.SemaphoreType.DMA((ns, 3)),
                        pltpu.SemaphoreType.DMA((ns,))],
        compiler_params=pltpu.CompilerParams(has_side_effects=True),
    )(*big, *small)
    return res[:nb], res[nb:]


def _gather_packs(pack):
    n_dev = 8

    def body(in_ref, out_ref, send, recv, lsem):
        x, y, c = _place()
        me = 4 * x + 2 * y + c
        sends = []
        for r in range(1, n_dev):
            peer = (1 - x if r & 4 else x, 1 - y if r & 2 else y, 1 - c if r & 1 else c)
            cp = pltpu.make_async_remote_copy(
                src_ref=in_ref, dst_ref=out_ref.at[me], send_sem=send.at[r - 1], recv_sem=recv.at[r - 1],
                device_id=peer, device_id_type=MESH)
            cp.start()
            sends.append(cp)
        mine = pltpu.make_async_copy(in_ref, out_ref.at[me], lsem)
        mine.start()
        for r in range(1, n_dev):
            peer = (1 - x if r & 4 else x, 1 - y if r & 2 else y, 1 - c if r & 1 else c)
            src = 4 * peer[0] + 2 * peer[1] + peer[2]
            pltpu.make_async_remote_copy(
                src_ref=in_ref, dst_ref=out_ref.at[src], send_sem=send.at[r - 1], recv_sem=recv.at[r - 1],
                device_id=peer, device_id_type=MESH).wait_recv()
        for cp in sends:
            cp.wait_send()
        mine.wait()

    return pl.pallas_call(
        body, name="gather_packs", in_specs=[ANY], out_specs=ANY, out_shape=_sds((n_dev,) + pack.shape, pack.dtype),
        scratch_shapes=[pltpu.SemaphoreType.DMA((n_dev - 1,)), pltpu.SemaphoreType.DMA((n_dev - 1,)),
                        pltpu.SemaphoreType.DMA],
        compiler_params=pltpu.CompilerParams(has_side_effects=True),
    )(pack)


def _swap_halves(grads):
    n = len(grads)

    def body(*refs):
        g_in, g_out = refs[:n], refs[n:2 * n]
        send, recv = refs[2 * n:]
        x, y, c = _place()
        cps = []
        for a in range(n):
            h = g_in[a].shape[1] // 2
            cp = pltpu.make_async_remote_copy(
                src_ref=g_in[a].at[:, pl.ds((1 - c) * h, h)], dst_ref=g_out[a],
                send_sem=send.at[a], recv_sem=recv.at[a], device_id=(x, y, 1 - c), device_id_type=MESH)
            cp.start()
            cps.append(cp)
        for cp in cps:
            cp.wait()

    out_shape = [_sds((g.shape[0], g.shape[1] // 2, g.shape[2]), g.dtype) for g in grads]
    return pl.pallas_call(
        body, name="swap_halves", in_specs=[ANY] * n, out_specs=[ANY] * n, out_shape=out_shape,
        scratch_shapes=[pltpu.SemaphoreType.DMA((n,)), pltpu.SemaphoreType.DMA((n,))],
        compiler_params=pltpu.CompilerParams(has_side_effects=True),
    )(*grads)


def _scatter_partials(parts):
    n = len(parts)

    def body(*refs):
        p_in, p_out = refs[:n], refs[n:2 * n]
        send, recv = refs[2 * n:]
        x, y, c = _place()
        cps = []
        for a in range(n):
            for k in range(1, N_CHIPS):
                px, py = _chip_at(x, y, k)
                cp = pltpu.make_async_remote_copy(
                    src_ref=p_in[a].at[2 * px + py], dst_ref=p_out[a].at[k - 1],
                    send_sem=send.at[a, k - 1], recv_sem=recv.at[a, k - 1],
                    device_id=(px, py, c), device_id_type=MESH)
                cp.start()
                cps.append(cp)
        for cp in cps:
            cp.wait()

    out_shape = [_sds((N_CHIPS - 1,) + p.shape[1:], p.dtype) for p in parts]
    return pl.pallas_call(
        body, name="scatter_partials", in_specs=[ANY] * n, out_specs=[ANY] * n, out_shape=out_shape,
        scratch_shapes=[pltpu.SemaphoreType.DMA((n, 3)), pltpu.SemaphoreType.DMA((n, 3))],
        compiler_params=pltpu.CompilerParams(has_side_effects=True),
    )(*parts)


def _share_halves(grads):
    n = len(grads)

    def body(*refs):
        t = refs[n:2 * n]
        send, recv = refs[2 * n:]
        x, y, c = _place()
        cps = []
        for a in range(n):
            h = t[a].shape[0] // 2
            mine = t[a].at[pl.ds(c * h, h)]
            cp = pltpu.make_async_remote_copy(
                src_ref=mine, dst_ref=mine, send_sem=send.at[a], recv_sem=recv.at[a],
                device_id=(x, y, 1 - c), device_id_type=MESH)
            cp.start()
            cps.append(cp)
        for a, cp in enumerate(cps):
            h = t[a].shape[0] // 2
            other = t[a].at[pl.ds((1 - c) * h, h)]
            pltpu.make_async_remote_copy(
                src_ref=other, dst_ref=other, send_sem=send.at[a], recv_sem=recv.at[a],
                device_id=(x, y, 1 - c), device_id_type=MESH).wait_recv()
            cp.wait_send()

    return pl.pallas_call(
        body, name="share_halves", in_specs=[ANY] * n, out_specs=[ANY] * n,
        out_shape=[_sds(g.shape, g.dtype) for g in grads], input_output_aliases={a: a for a in range(n)},
        scratch_shapes=[pltpu.SemaphoreType.DMA((n,)), pltpu.SemaphoreType.DMA((n,))],
        compiler_params=pltpu.CompilerParams(has_side_effects=True),
    )(*grads)


def _elem_tile(rows, cols):
    return _pick(rows, max(8, (1 << 19) // cols // 8 * 8))


def _chip_partial(grad, recv, c_idx, p_idx):
    Q, R, C = grad.shape
    h = R // 2
    T = _elem_tile(h, C)
    nt = h // T

    def body(sc_ref, g_ref, r_ref, sb_ref, own_ref):
        q = pl.program_id(1)
        s = g_ref[...] + r_ref[...]
        sb_ref[...] = s.astype(BF16)

        @pl.when(q == sc_ref[1])
        def _():
            own_ref[...] = s

    grid_spec = pltpu.PrefetchScalarGridSpec(
        num_scalar_prefetch=1, grid=(nt, Q),
        in_specs=[pl.BlockSpec((None, T, C), lambda t, q, sc: (q, sc[0] * nt + t, 0)),
                  pl.BlockSpec((None, T, C), lambda t, q, sc: (q, t, 0))],
        out_specs=[pl.BlockSpec((None, T, C), lambda t, q, sc: (q, t, 0)),
                   pl.BlockSpec((T, C), lambda t, q, sc: (t, 0))])
    return pl.pallas_call(
        body, name="chip_partial", grid_spec=grid_spec,
        out_shape=[_sds((Q, h, C), BF16), _sds((h, C), F32)], compiler_params=_params(2),
    )(jnp.stack([c_idx, p_idx]).astype(jnp.int32), grad, recv)


def _sum_partials(own, parts, c_idx):
    h, C = own.shape
    T = _elem_tile(h, C)
    nt = h // T

    def body(c_ref, o_ref, p_ref, t_ref):
        t = o_ref[...]
        for k in range(N_CHIPS - 1):
            t = t + p_ref[k].astype(F32)
        t_ref[...] = t

    grid_spec = pltpu.PrefetchScalarGridSpec(
        num_scalar_prefetch=1, grid=(nt,),
        in_specs=[pl.BlockSpec((T, C), lambda i, c: (i, 0)),
                  pl.BlockSpec((N_CHIPS - 1, T, C), lambda i, c: (0, i, 0))],
        out_specs=pl.BlockSpec((T, C), lambda i, c: (c[0] * nt + i, 0)))
    return pl.pallas_call(
        body, name="sum_partials", grid_spec=grid_spec, out_shape=_sds((2 * h, C), F32), compiler_params=_params(1),
    )(jnp.reshape(c_idx, (1,)).astype(jnp.int32), own, parts)


def _pack_rows(name, parts):
    width = parts[0].shape[1]
    offsets, at = [], 0
    for p in parts:
        offsets.append(at)
        at += p.shape[0]
    total = -(-at // 8) * 8

    def body(*refs):
        out = refs[-1]
        out[...] = jnp.zeros(out.shape, F32)
        for ref, o in zip(refs[:-1], offsets):
            out[o:o + ref.shape[0], :] = ref[...]

    return pl.pallas_call(body, name=name, out_shape=_sds((total, width), F32))(*parts)


def _sum_packs(packs):
    n, R, C = packs.shape

    def body(p_ref, o_ref):
        t = p_ref[0]
        for k in range(1, n):
            t = t + p_ref[k]
        o_ref[...] = t

    return pl.pallas_call(
        body, name="sum_packs", grid=(1,), in_specs=[pl.BlockSpec((n, R, C), lambda i: (0, 0, 0))],
        out_specs=pl.BlockSpec((R, C), lambda i: (0, 0)), out_shape=_sds((R, C), F32), compiler_params=_params(1),
    )(packs)


def _adamw(w, g, m, v):
    R, C = w.shape
    T = _elem_tile(R, C)

    def body(w_ref, g_ref, m_ref, v_ref, d_ref, m2_ref, v2_ref):
        g_ = g_ref[...]
        m2 = ADAM_B1 * m_ref[...] + (1.0 - ADAM_B1) * g_
        v2 = ADAM_B2 * v_ref[...] + (1.0 - ADAM_B2) * (g_ * g_)
        m_hat = m2 / (1.0 - ADAM_B1 ** ADAM_STEP)
        v_hat = v2 / (1.0 - ADAM_B2 ** ADAM_STEP)
        d_ref[...] = -ADAM_LR * (m_hat / (jnp.sqrt(v_hat) + ADAM_EPS) + ADAM_WD * w_ref[...])
        m2_ref[...] = m2
        v2_ref[...] = v2

    blk = pl.BlockSpec((T, C), lambda i: (i, 0))
    return pl.pallas_call(
        body, name="adamw", grid=(R // T,), in_specs=[blk] * 4, out_specs=[blk] * 3,
        out_shape=[_sds((R, C), F32)] * 3, compiler_params=_params(1),
    )(w, g, m, v)


BIG = ("w_in", "w_pool_out", "w_conv_out", "w_o", "w_up", "w_down", "w_pool_grp")
VECTORS = ("g_pre_mix", "pool_scale", "b_dw", "conv_ln_g", "conv_ln_b", "g_post_mix", "g_pre_mlp", "g_post_mlp")
WEIGHTS = ("meta", "g_pre_mix", "w_in", "w_pool_grp", "pool_scale", "w_pool_out", "w_dw", "b_dw", "conv_ln_g",
           "conv_ln_b", "w_conv_out", "w_o", "g_post_mix", "g_pre_mlp", "w_up", "w_down", "g_post_mlp")


def _as_rows(a, width):
    r, cols = a.shape
    return a.reshape(r * (cols // width), width)


def _step(w, m, v, x, tgt):
    S, D = x.shape
    P = D // 2
    xi, yi, ci = _place()
    chip = 2 * xi + yi

    shard2d = {k: w[k].reshape(-1, w[k].shape[-1]) for k in BIG}
    big_g, small_g = _gather_weights([_cast_into_slab(shard2d[k], chip) for k in BIG], [w["w_dw"], w["meta"]])
    gathered = dict(zip(BIG, big_g))
    G = POOL_GROUPS
    GD = P // G
    GS = GD // N_CHIPS
    w_grp = gathered["w_pool_grp"].reshape(N_CHIPS, G, GS, GD).transpose(1, 0, 2, 3).reshape(G, GD, GD)
    w_dw = small_g[0].transpose(1, 0, 2).reshape(CONV_TAPS, P)
    meta = small_g[1].transpose(1, 0, 2).reshape(N_META, D)
    w_o = gathered["w_o"].reshape(D, D)
    w_down = gathered["w_down"].reshape(-1, D)

    h0 = jnp.concatenate([jnp.zeros((PAD_ROWS, D), F32), meta, x], axis=0)
    vecs = {k: w[k] for k in VECTORS}
    loss, grad_x, dmeta, g_w_dw, g_vec, g_big = _block_fwd_bwd(
        h0, tgt, vecs, w_grp, w_dw, gathered["w_in"], gathered["w_pool_out"], gathered["w_conv_out"], w_o,
        gathered["w_up"], w_down)

    g_big["w_o"] = g_big["w_o"].reshape(N_CHIPS, D // N_CHIPS, D)
    g_big["w_down"] = g_big["w_down"].reshape(N_CHIPS, -1, D)
    g_big["w_pool_grp"] = g_big["w_pool_grp"].reshape(G, N_CHIPS, GS, GD).transpose(1, 0, 2, 3).reshape(
        N_CHIPS, G * GS, GD)
    slabs = [g_big[k] for k in BIG]
    from_sibling = _swap_halves(slabs)
    partial = [_chip_partial(g, r, ci, chip) for g, r in zip(slabs, from_sibling)]
    received = _scatter_partials([pb for pb, _ in partial])
    halves = [_sum_partials(own, rc, ci) for (_, own), rc in zip(partial, received)]
    grads = dict(zip(BIG, _share_halves(halves)))

    rows = [g_w_dw, _as_rows(dmeta, P)] + [_as_rows(g_vec[k], P) for k in VECTORS]
    rows.append(jnp.broadcast_to(loss[:, :1], (1, P)))
    total = _sum_packs(_gather_packs(_pack_rows("pack_grads", rows)))
    at = 0
    taps_pad = g_w_dw.shape[0]
    g_dw_full = total[at:at + CONV_TAPS]
    at += taps_pad
    g_meta_full = total[at:at + 2 * N_META].reshape(N_META, D)
    at += 2 * N_META
    for k in VECTORS:
        n = w[k].shape[-1] // P
        grads[k] = total[at:at + n].reshape(1, n * P)
        at += n
    loss_total = total[at, 0]
    grads["w_dw"] = lax.dynamic_slice_in_dim(g_dw_full, chip * (P // N_CHIPS), P // N_CHIPS, axis=1)
    grads["meta"] = lax.dynamic_slice_in_dim(g_meta_full, chip * (D // N_CHIPS), D // N_CHIPS, axis=1)

    delta, new_m, new_v = {}, {}, {}
    for k in BIG:
        delta[k], new_m[k], new_v[k] = _adamw(shard2d[k], grads[k], m[k].reshape(shard2d[k].shape),
                                              v[k].reshape(shard2d[k].shape))
    small_names = [k for k in WEIGHTS if k not in BIG]

    def pack_small(tree):
        parts = []
        for k in small_names:
            a = tree[k].reshape(-1, tree[k].shape[-1])
            flat = a.reshape(-1)
            parts.append(jnp.pad(flat, (0, -flat.shape[0] % P)).reshape(-1, P))
        return _pack_rows("pack_small", parts)

    sd, sm, sv = _adamw(pack_small(w), pack_small(grads), pack_small(m), pack_small(v))
    at = 0
    for k in small_names:
        a = w[k].reshape(-1, w[k].shape[-1])
        n = -(-a.size // P)
        for tree, packed in ((delta, sd), (new_m, sm), (new_v, sv)):
            tree[k] = packed[at:at + n].reshape(-1)[:a.size].reshape(a.shape)
        at += n

    return loss_total, grad_x, grads, delta, new_m, new_v


def kernel(x, meta, g_pre_mix, w_in, w_pool_grp, pool_scale, w_pool_out, w_dw, b_dw, conv_ln_g, conv_ln_b, w_conv_out, w_o, g_post_mix, g_pre_mlp, w_up, w_down, g_post_mlp, loss_target, m_meta, m_g_pre_mix, m_w_in, m_w_pool_grp, m_pool_scale, m_w_pool_out, m_w_dw, m_b_dw, m_conv_ln_g, m_conv_ln_b, m_w_conv_out, m_w_o, m_g_post_mix, m_g_pre_mlp, m_w_up, m_w_down, m_g_post_mlp, v_meta, v_g_pre_mix, v_w_in, v_w_pool_grp, v_pool_scale, v_w_pool_out, v_w_dw, v_b_dw, v_conv_ln_g, v_conv_ln_b, v_w_conv_out, v_w_o, v_g_post_mix, v_g_pre_mlp, v_w_up, v_w_down, v_g_post_mlp):
    args = dict(locals())
    shapes = {k: args[k].shape for k in WEIGHTS}
    w = {k: args[k] for k in WEIGHTS}
    m = {k: args["m_" + k] for k in WEIGHTS}
    v = {k: args["v_" + k] for k in WEIGHTS}
    for tree in (w, m, v):
        tree["w_dw"] = tree["w_dw"].reshape(tree["w_dw"].shape[-2:])
    loss, grad_x, grads, delta, new_m, new_v = _step(w, m, v, x[0], loss_target[0])
    out = [loss, grad_x[None]]
    for tree in (grads, delta, new_m, new_v):
        out += [tree[k].reshape(shapes[k]) for k in WEIGHTS]
    return tuple(out)
```

```python
import jax
import jax.numpy as jnp
from jax import lax
from jax.experimental import pallas as pl
from jax.experimental.pallas import tpu as pltpu

F32 = jnp.float32
BF16 = jnp.bfloat16

N_META = 16
PAD_ROWS = 112
TOKEN_ROW0 = PAD_ROWS + N_META
POOL_GROUPS = 4
CONV_TAPS = 31
HALO = 32
CONV_ROWS = 128
LANES = 128
RMS_EPS = 1e-6
LN_EPS = 1e-5
ADAM_LR = 0.001
ADAM_B1 = 0.9
ADAM_B2 = 0.999
ADAM_EPS = 1e-08
ADAM_WD = 0.01
ADAM_STEP = 10
VMEM_LIMIT_MB = 56

MESH = pl.DeviceIdType.MESH
NN = (((1,), (0,)), ((), ()))
NT = (((1,), (1,)), ((), ()))
TN = (((0,), (0,)), ((), ()))


def _pick(n, pref):
    if n <= pref:
        return n
    if n % pref == 0:
        return pref
    for step in (LANES, 8, 1):
        t = (pref // step) * step
        while t >= step:
            if n % t == 0:
                return t
            t -= step
    return n


def _params(n_axes, vmem_mb=VMEM_LIMIT_MB):
    return pltpu.CompilerParams(dimension_semantics=("arbitrary",) * n_axes,
                                vmem_limit_bytes=vmem_mb << 20)


def _sigmoid(x):
    return jax.nn.sigmoid(x)


_CHAIN = {"after": None}


def _call(body, args, *, in_specs, out_specs, out_shape, grid=(), scalars=None, mark=0, **kw):
    after = _CHAIN["after"]
    n = len(args)
    lead = 0 if scalars is None else 1
    specs = list(in_specs)
    operands = list(args)
    fn = body
    if after is not None:
        def fn(*refs):
            body(*refs[:lead + n], *refs[lead + n + 1:])
        specs.append(pl.BlockSpec(memory_space=pl.ANY))
        operands.append(after)
    if scalars is None:
        if grid:
            kw["grid"] = grid
        res = pl.pallas_call(fn, in_specs=specs, out_specs=out_specs, out_shape=out_shape, **kw)(*operands)
    else:
        grid_spec = pltpu.PrefetchScalarGridSpec(num_scalar_prefetch=1, grid=grid, in_specs=specs, out_specs=out_specs)
        res = pl.pallas_call(fn, grid_spec=grid_spec, out_shape=out_shape, **kw)(scalars, *operands)
    outs = res if isinstance(res, (list, tuple)) else [res]
    _CHAIN["after"] = outs[mark]
    return res


def _store(val, extras, outs):
    outs[0][...] = val.astype(outs[0].dtype)


def _mm(name, grid, arrays, in_specs, out_shapes, out_specs, dims, nk, epilogue=_store, acc_shape=None):
    n_in, n_out = len(arrays), len(out_shapes)

    def body(*refs):
        extras = refs[2:n_in]
        outs = refs[n_in:n_in + n_out]
        part = lax.dot_general(refs[0][...], refs[1][...], dims, preferred_element_type=F32)
        if nk == 1:
            epilogue(part, extras, outs)
        else:
            acc = refs[n_in + n_out]
            k = pl.program_id(len(grid) - 1)

            @pl.when(k == 0)
            def _():
                acc[...] = part

            @pl.when(k > 0)
            def _():
                acc[...] += part

            @pl.when(k == nk - 1)
            def _():
                epilogue(acc[...], extras, outs)

    scratch = [pltpu.VMEM(acc_shape, F32)] if nk > 1 else []
    single = n_out == 1
    return _call(
        body, arrays, name=name, grid=grid, in_specs=in_specs,
        out_specs=out_specs[0] if single else out_specs,
        out_shape=out_shapes[0] if single else out_shapes,
        scratch_shapes=scratch, compiler_params=_params(len(grid)))


def _sds(shape, dtype):
    return jax.ShapeDtypeStruct(shape, dtype)


def _rms_scale(h):
    return lax.rsqrt(jnp.mean(h * h, axis=-1, keepdims=True) + RMS_EPS)


def _rms_bwd(du, h, g):
    r = _rms_scale(h)
    y = h * r
    dy = du * g
    dh = r * (dy - y * jnp.mean(dy * y, axis=-1, keepdims=True))
    return dh, jnp.sum(du * y, axis=0, keepdims=True)


def _row_tile(L):
    return _pick(L, 272)


def _pre_norm(h0, g):
    L, D = h0.shape
    T = _row_tile(L)

    def body(h_ref, g_ref, u_ref):
        h = h_ref[...]
        u_ref[...] = (h * _rms_scale(h) * g_ref[...]).astype(BF16)

    return _call(
        body, (h0, g), name="pre_norm", grid=(L // T,),
        in_specs=[pl.BlockSpec((T, D), lambda i: (i, 0)), pl.BlockSpec((1, D), lambda i: (0, 0))],
        out_specs=pl.BlockSpec((T, D), lambda i: (i, 0)),
        out_shape=_sds((L, D), BF16), compiler_params=_params(1))


def _mid_norm(o, h0, g_post, g_pre):
    L, D = h0.shape
    T = _row_tile(L)

    def body(o_ref, h_ref, gp_ref, gm_ref, h1_ref, u2_ref):
        o_ = o_ref[...]
        h1 = h_ref[...] + o_ * _rms_scale(o_) * gp_ref[...]
        h1_ref[...] = h1
        u2_ref[...] = (h1 * _rms_scale(h1) * gm_ref[...]).astype(BF16)

    row = pl.BlockSpec((T, D), lambda i: (i, 0))
    vec = pl.BlockSpec((1, D), lambda i: (0, 0))
    return _call(
        body, (o, h0, g_post, g_pre), name="mid_norm", grid=(L // T,),
        in_specs=[row, row, vec, vec], out_specs=[row, row],
        out_shape=[_sds((L, D), F32), _sds((L, D), BF16)], compiler_params=_params(1))


def _loss_head(f, h1, tgt, g_post):
    L, D = h1.shape
    T = TOKEN_ROW0
    n = L // T

    def body(f_ref, h_ref, t_ref, g_ref, dy_ref, df_ref, dg_ref, loss_ref):
        i = pl.program_id(0)
        f_ = f_ref[...]
        g = g_ref[...]
        y = h_ref[...] + f_ * _rms_scale(f_) * g
        live = (i > 0).astype(F32)
        diff = (y - t_ref[...]) * live
        part = 0.5 * jnp.sum(jnp.mean(diff * diff, axis=-1, keepdims=True), axis=0, keepdims=True)
        dy = diff * (1.0 / D)
        dy_ref[...] = dy
        df, dg = _rms_bwd(dy, f_, g)
        df_ref[...] = df.astype(BF16)

        @pl.when(i == 0)
        def _():
            dg_ref[...] = dg
            loss_ref[...] = jnp.broadcast_to(part, loss_ref.shape)

        @pl.when(i > 0)
        def _():
            dg_ref[...] += dg
            loss_ref[...] += jnp.broadcast_to(part, loss_ref.shape)

    row = pl.BlockSpec((T, D), lambda i: (i, 0))
    vec = pl.BlockSpec((1, D), lambda i: (0, 0))
    return _call(
        body, (f, h1, tgt, g_post), name="loss_head", grid=(n,),
        in_specs=[row, row, pl.BlockSpec((T, D), lambda i: (jnp.maximum(i - 1, 0), 0)), vec],
        out_specs=[row, row, vec, pl.BlockSpec((1, LANES), lambda i: (0, 0))],
        out_shape=[_sds((L, D), F32), _sds((L, D), BF16), _sds((1, D), F32), _sds((1, LANES), F32)],
        compiler_params=_params(1))


def _mid_norm_bwd(dy, du2, h1, o, g_pre, g_post):
    L, D = h1.shape
    T = _row_tile(L)

    def body(dy_ref, du_ref, h_ref, o_ref, gm_ref, gp_ref, dh1_ref, do_ref, dgm_ref, dgp_ref):
        i = pl.program_id(0)
        dh, dgm = _rms_bwd(du_ref[...], h_ref[...], gm_ref[...])
        dh1 = dy_ref[...] + dh
        dh1_ref[...] = dh1
        do, dgp = _rms_bwd(dh1, o_ref[...], gp_ref[...])
        do_ref[...] = do.astype(BF16)

        @pl.when(i == 0)
        def _():
            dgm_ref[...] = dgm
            dgp_ref[...] = dgp

        @pl.when(i > 0)
        def _():
            dgm_ref[...] += dgm
            dgp_ref[...] += dgp

    row = pl.BlockSpec((T, D), lambda i: (i, 0))
    vec = pl.BlockSpec((1, D), lambda i: (0, 0))
    return _call(
        body, (dy, du2, h1, o, g_pre, g_post), name="mid_norm_bwd", grid=(L // T,),
        in_specs=[row, row, row, row, vec, vec], out_specs=[row, row, vec, vec],
        out_shape=[_sds((L, D), F32), _sds((L, D), BF16), _sds((1, D), F32), _sds((1, D), F32)],
        compiler_params=_params(1))


def _pre_norm_bwd(dh1, du1, h0, g):
    L, D = h0.shape
    T = TOKEN_ROW0
    n = L // T

    def body(dh_ref, du_ref, h_ref, g_ref, gx_ref, dmeta_ref, dg_ref):
        i = pl.program_id(0)
        dh, dg = _rms_bwd(du_ref[...], h_ref[...], g_ref[...])
        dh0 = dh_ref[...] + dh
        gx_ref[...] = dh0

        @pl.when(i == 0)
        def _():
            dmeta_ref[...] = dh0[PAD_ROWS:, :]
            dg_ref[...] = dg

        @pl.when(i > 0)
        def _():
            dg_ref[...] += dg

    row = pl.BlockSpec((T, D), lambda i: (i, 0))
    vec = pl.BlockSpec((1, D), lambda i: (0, 0))
    return _call(
        body, (dh1, du1, h0, g), name="pre_norm_bwd", grid=(n,),
        in_specs=[row, row, row, vec],
        out_specs=[pl.BlockSpec((T, D), lambda i: (jnp.maximum(i - 1, 0), 0)),
                   pl.BlockSpec((N_META, D), lambda i: (0, 0)), vec],
        out_shape=[_sds((L - T, D), F32), _sds((N_META, D), F32), _sds((1, D), F32)],
        compiler_params=_params(1))


def _window_sum(z, g, shift_sign, L):
    s = z
    for j in range(POOL_GROUPS):
        k = 1 << j
        nxt = s + pltpu.roll(s, k if shift_sign > 0 else L - k, 0)
        s = jnp.where(j <= g, nxt, s)
    return s


def _inv_count(g, L):
    t = lax.broadcasted_iota(jnp.int32, (L, 1), 0)
    w = jnp.left_shift(2, g)
    cnt = jnp.clip(t - (PAD_ROWS - 1), 1, w)
    return 1.0 / cnt.astype(F32)


def _pool_fwd(proj, w_grp, scale):
    L = proj.shape[0]
    G, GD, _ = w_grp.shape
    P = G * GD

    def body(z_ref, w_ref, sc_ref, d_ref, ya_ref):
        g = pl.program_id(0)
        z = z_ref[...]
        d = (_window_sum(z, g, +1, L) * _inv_count(g, L) - z).astype(BF16)
        d_ref[...] = d
        y = jnp.dot(d, w_ref[...], preferred_element_type=F32)
        ya_ref[...] = (y * sc_ref[...]).astype(BF16)

    col = pl.BlockSpec((L, GD), lambda g: (0, g))
    return _call(
        body, (proj, w_grp, scale), name="pool_fwd", grid=(G,),
        in_specs=[col, pl.BlockSpec((None, GD, GD), lambda g: (g, 0, 0)), pl.BlockSpec((1, GD), lambda g: (0, g))],
        out_specs=[col, col], out_shape=[_sds((L, P), BF16), _sds((L, P), BF16)],
        compiler_params=_params(1))


def _pool_bwd(dya, d, w_grp, scale):
    L, P = dya.shape
    G, GD, _ = w_grp.shape

    def body(dya_ref, d_ref, w_ref, sc_ref, dz_ref, dw_ref, dsc_ref):
        g = pl.program_id(0)
        dya_ = dya_ref[...]
        d_ = d_ref[...]
        w = w_ref[...]
        y = jnp.dot(d_, w, preferred_element_type=F32)
        dsc_ref[...] = jnp.sum(dya_ * y, axis=0, keepdims=True)
        dy = (dya_ * sc_ref[...]).astype(BF16)
        dw_ref[...] = lax.dot_general(d_, dy, TN, preferred_element_type=F32)
        dd = lax.dot_general(dy, w, NT, preferred_element_type=F32)
        dz = _window_sum(dd * _inv_count(g, L), g, -1, L) - dd
        dz_ref[...] = dz.astype(BF16)

    col = pl.BlockSpec((L, GD), lambda g: (0, g))
    wspec = pl.BlockSpec((None, GD, GD), lambda g: (g, 0, 0))
    vec = pl.BlockSpec((1, GD), lambda g: (0, g))
    return _call(
        body, (dya, d, w_grp, scale), name="pool_bwd", grid=(G,),
        in_specs=[col, col, wspec, vec], out_specs=[col, wspec, vec],
        out_shape=[_sds((L, P), BF16), _sds((G, GD, GD), F32), _sds((1, P), F32)],
        compiler_params=_params(1))


def _fill_rotations(rot_ref, ext):
    n = ext.shape[0]
    rot_ref[0] = ext
    for r in range(1, 8):
        rot_ref[r] = pltpu.roll(ext, n - r, 0)


def _lane_chunks(C):
    step = LANES if C % LANES == 0 else C
    return [(c0, step) for c0 in range(0, C, step)]


def _conv_specs(L, C, col_v, col_g):
    T = CONV_ROWS
    per = T // HALO
    cur_v = pl.BlockSpec((T, C), lambda i: (i, col_v))
    cur_g = pl.BlockSpec((T, C), lambda i: (i, col_g))
    prev_v = pl.BlockSpec((HALO, C), lambda i: (jnp.maximum(i * per - 1, 0), col_v))
    prev_g = pl.BlockSpec((HALO, C), lambda i: (jnp.maximum(i * per - 1, 0), col_g))
    return cur_v, cur_g, prev_v, prev_g


def _glu_ext(vc, gc, vh, gh, i):
    a_cur = vc[...] * _sigmoid(gc[...])
    a_prev = vh[...] * _sigmoid(gh[...]) * (i > 0).astype(F32)
    return jnp.concatenate([a_prev, a_cur], axis=0)


def _conv_fwd(proj, C, w_dw, b_dw, ln_g, ln_b):
    L = proj.shape[0]
    T = CONV_ROWS
    P = C

    def body(vc, gc, vh, gh, w_ref, b_ref, lg_ref, lb_ref, s_ref, c_ref, rot):
        i = pl.program_id(0)
        _fill_rotations(rot, _glu_ext(vc, gc, vh, gh, i))
        for c0, cw in _lane_chunks(C):
            acc = jnp.zeros((T, cw), F32)
            for k in range(CONV_TAPS):
                q, r = divmod(HALO - (CONV_TAPS - 1) + k, 8)
                acc = acc + w_ref[k:k + 1, c0:c0 + cw] * rot[r, 8 * q:8 * q + T, c0:c0 + cw]
            c_ref[:, c0:c0 + cw] = acc + b_ref[:, c0:c0 + cw]
        c = c_ref[...]
        mu = jnp.mean(c, axis=-1, keepdims=True)
        cen = c - mu
        var = jnp.mean(cen * cen, axis=-1, keepdims=True)
        ln = cen * lax.rsqrt(var + LN_EPS) * lg_ref[...] + lb_ref[...]
        s_ref[...] = (ln * _sigmoid(ln)).astype(BF16)

    cur_v, cur_g, prev_v, prev_g = _conv_specs(L, C, P // C, P // C + 1)
    row = pl.BlockSpec((T, C), lambda i: (i, 0))
    vec = pl.BlockSpec((1, C), lambda i: (0, 0))
    return _call(
        body, (proj, proj, proj, proj, w_dw, b_dw, ln_g, ln_b), name="conv_fwd", grid=(L // T,),
        in_specs=[cur_v, cur_g, prev_v, prev_g, pl.BlockSpec((CONV_TAPS, C), lambda i: (0, 0)), vec, vec, vec],
        out_specs=[row, row], out_shape=[_sds((L, C), BF16), _sds((L, C), F32)],
        scratch_shapes=[pltpu.VMEM((8, T + HALO, C), F32)], compiler_params=_params(1))


def _conv_ln_bwd(ds, c, ln_g, ln_b):
    L, C = c.shape
    T = _row_tile(L)

    def body(ds_ref, c_ref, lg_ref, lb_ref, dc_ref, dlg_ref, dlb_ref, db_ref):
        i = pl.program_id(0)
        c_ = c_ref[...]
        g = lg_ref[...]
        mu = jnp.mean(c_, axis=-1, keepdims=True)
        cen = c_ - mu
        rstd = lax.rsqrt(jnp.mean(cen * cen, axis=-1, keepdims=True) + LN_EPS)
        xhat = cen * rstd
        ln = xhat * g + lb_ref[...]
        sg = _sigmoid(ln)
        dln = ds_ref[...] * (sg * (1.0 + ln * (1.0 - sg)))
        dxh = dln * g
        dc = rstd * (dxh - jnp.mean(dxh, axis=-1, keepdims=True)
                     - xhat * jnp.mean(dxh * xhat, axis=-1, keepdims=True))
        dc_ref[...] = dc
        dlg = jnp.sum(dln * xhat, axis=0, keepdims=True)
        dlb = jnp.sum(dln, axis=0, keepdims=True)
        db = jnp.sum(dc, axis=0, keepdims=True)

        @pl.when(i == 0)
        def _():
            dlg_ref[...] = dlg
            dlb_ref[...] = dlb
            db_ref[...] = db

        @pl.when(i > 0)
        def _():
            dlg_ref[...] += dlg
            dlb_ref[...] += dlb
            db_ref[...] += db

    row = pl.BlockSpec((T, C), lambda i: (i, 0))
    vec = pl.BlockSpec((1, C), lambda i: (0, 0))
    return _call(
        body, (ds, c, ln_g, ln_b), name="conv_ln_bwd", grid=(L // T,),
        in_specs=[row, row, vec, vec], out_specs=[row, vec, vec, vec],
        out_shape=[_sds((L, C), F32), _sds((1, C), F32), _sds((1, C), F32), _sds((1, C), F32)],
        compiler_params=_params(1))


def _conv_bwd(dc, proj, C, w_dw):
    L = proj.shape[0]
    T = CONV_ROWS
    per = T // HALO
    n = L // T
    P = C
    taps_pad = 32

    def body(dcc, dcn, vc, gc, vh, gh, w_ref, dv_ref, dg_ref, dw_ref, rot_a, rot_d):
        i = pl.program_id(0)
        _fill_rotations(rot_a, _glu_ext(vc, gc, vh, gh, i))
        dc_cur = dcc[...]
        dc_next = dcn[...] * (i < n - 1).astype(F32)
        _fill_rotations(rot_d, jnp.concatenate([dc_cur, dc_next], axis=0))

        @pl.when(i == 0)
        def _():
            dw_ref[...] = jnp.zeros(dw_ref.shape, F32)

        for c0, cw in _lane_chunks(C):
            dcs = dc_cur[:, c0:c0 + cw]
            da = jnp.zeros((T, cw), F32)
            for k in range(CONV_TAPS):
                q, r = divmod(CONV_TAPS - 1 - k, 8)
                da = da + w_ref[k:k + 1, c0:c0 + cw] * rot_d[r, 8 * q:8 * q + T, c0:c0 + cw]
                q, r = divmod(HALO - (CONV_TAPS - 1) + k, 8)
                dw_ref[k:k + 1, c0:c0 + cw] += jnp.sum(dcs * rot_a[r, 8 * q:8 * q + T, c0:c0 + cw],
                                                      axis=0, keepdims=True)
            v = vc[:, c0:c0 + cw]
            sg = _sigmoid(gc[:, c0:c0 + cw])
            dv_ref[:, c0:c0 + cw] = (da * sg).astype(BF16)
            dg_ref[:, c0:c0 + cw] = (da * v * sg * (1.0 - sg)).astype(BF16)

    cur_v, cur_g, prev_v, prev_g = _conv_specs(L, C, P // C, P // C + 1)
    row = pl.BlockSpec((T, C), lambda i: (i, 0))
    nxt = pl.BlockSpec((HALO, C), lambda i: (jnp.minimum((i + 1) * per, L // HALO - 1), 0))
    wspec = pl.BlockSpec((CONV_TAPS, C), lambda i: (0, 0))
    return _call(
        body, (dc, dc, proj, proj, proj, proj, w_dw), name="conv_bwd", grid=(n,),
        in_specs=[row, nxt, cur_v, cur_g, prev_v, prev_g, wspec],
        out_specs=[row, row, pl.BlockSpec((taps_pad, C), lambda i: (0, 0))],
        out_shape=[_sds((L, C), BF16), _sds((L, C), BF16), _sds((taps_pad, C), F32)],
        scratch_shapes=[pltpu.VMEM((8, T + HALO, C), F32), pltpu.VMEM((8, T + HALO, C), F32)],
        compiler_params=_params(1))


def _mix_fwd(ya_pre, s, wpo, wco, proj, D):
    L, P = ya_pre.shape
    Q, _, DS = wpo.shape
    bm = _pick(L, 1088)
    gate0 = (proj.shape[1] - 2 * D) // DS
    per = D // DS

    def body(a1, a2, b1, b2, ga, gb, m_ref, ya_ref, yb_ref):
        ya = jnp.dot(a1[...], b1[...], preferred_element_type=F32)
        yb = jnp.dot(a2[...], b2[...], preferred_element_type=F32)
        ya_ref[...] = ya
        yb_ref[...] = yb
        m_ref[...] = (_sigmoid(ga[...]) * ya + _sigmoid(gb[...]) * yb).astype(BF16)

    act = pl.BlockSpec((bm, P), lambda i, q: (i, 0))
    wsp = pl.BlockSpec((None, P, DS), lambda i, q: (q, 0, 0))
    out = pl.BlockSpec((bm, DS), lambda i, q: (i, q))
    return _call(
        body, (ya_pre, s, wpo, wco, proj, proj), name="mix_fwd", grid=(L // bm, Q),
        in_specs=[act, act, wsp, wsp,
                  pl.BlockSpec((bm, DS), lambda i, q: (i, gate0 + q)),
                  pl.BlockSpec((bm, DS), lambda i, q: (i, gate0 + per + q))],
        out_specs=[out, out, out],
        out_shape=[_sds((L, D), BF16), _sds((L, D), F32), _sds((L, D), F32)],
        compiler_params=_params(2))


def _mix_bwd(do, w_o, proj, ya, yb):
    L, D = do.shape
    bm = _pick(L, 544)
    bn = _pick(D // N_CHIPS, 512)
    gate0 = (proj.shape[1] - 2 * D) // bn
    per = D // bn

    def epilogue(dm, extras, outs):
        ga, gb, ya_ref, yb_ref = extras
        sa = _sigmoid(ga[...])
        sb = _sigmoid(gb[...])
        outs[0][...] = (dm * sa).astype(BF16)
        outs[1][...] = (dm * sb).astype(BF16)
        outs[2][...] = (dm * ya_ref[...] * sa * (1.0 - sa)).astype(BF16)
        outs[3][...] = (dm * yb_ref[...] * sb * (1.0 - sb)).astype(BF16)

    blk = pl.BlockSpec((bm, bn), lambda i, j: (i, j))
    return _mm(
        "mix_bwd", (L // bm, D // bn), [do, w_o, proj, proj, ya, yb],
        [pl.BlockSpec((bm, D), lambda i, j: (i, 0)), pl.BlockSpec((bn, D), lambda i, j: (j, 0)),
         pl.BlockSpec((bm, bn), lambda i, j: (i, gate0 + j)),
         pl.BlockSpec((bm, bn), lambda i, j: (i, gate0 + per + j)), blk, blk],
        [_sds((L, D), BF16)] * 4, [blk] * 4, NT, 1, epilogue)


def _mm_act_colw(name, a, wg, bn_pref, epilogue=_store, out_dtypes=(F32,)):
    L, K = a.shape
    Q, _, n = wg.shape
    bn = _pick(n, bn_pref)
    nj = n // bn
    out = pl.BlockSpec((L, bn), lambda q, j: (0, q * nj + j))
    return _mm(name, (Q, nj), [a, wg],
               [pl.BlockSpec((L, K), lambda q, j: (0, 0)), pl.BlockSpec((None, K, bn), lambda q, j: (q, 0, j))],
               [_sds((L, Q * n), dt) for dt in out_dtypes], [out] * len(out_dtypes), NN, 1, epilogue)


def _mm_grad_colw_t(name, g, wg, bm_pref, bn_pref):
    L = g.shape[0]
    Q, K, n = wg.shape
    bm = _pick(L, bm_pref)
    bn = _pick(K, bn_pref)
    return _mm(name, (L // bm, K // bn, Q), [g, wg],
               [pl.BlockSpec((bm, n), lambda i, j, k: (i, k)), pl.BlockSpec((None, bn, n), lambda i, j, k: (k, j, 0))],
               [_sds((L, K), F32)], [pl.BlockSpec((bm, bn), lambda i, j, k: (i, j))], NT, Q,
               acc_shape=(bm, bn))


def _mm_wgrad_colw(name, a, g, Q, bm_pref, bn_pref):
    L, K = a.shape
    n = g.shape[1] // Q
    bm = _pick(K, bm_pref)
    bn = _pick(n, bn_pref)
    nj = n // bn
    return _mm(name, (Q, K // bm, nj), [a, g],
               [pl.BlockSpec((L, bm), lambda q, i, j: (0, i)), pl.BlockSpec((L, bn), lambda q, i, j: (0, q * nj + j))],
               [_sds((Q, K, n), F32)], [pl.BlockSpec((None, bm, bn), lambda q, i, j: (q, i, j))], TN, 1)


def _mm_wgrad(name, a, g, bm_pref, bn_pref):
    L, K = a.shape
    N = g.shape[1]
    bm = _pick(K, bm_pref)
    bn = _pick(N, bn_pref)
    return _mm(name, (K // bm, N // bn), [a, g],
               [pl.BlockSpec((L, bm), lambda i, j: (0, i)), pl.BlockSpec((L, bn), lambda i, j: (0, j))],
               [_sds((K, N), F32)], [pl.BlockSpec((bm, bn), lambda i, j: (i, j))], TN, 1)


def _mm_act_roww(name, a, w, bm_pref, bn_pref, bk_pref):
    L, K = a.shape
    N = w.shape[1]
    bm, bn, bk = _pick(L, bm_pref), _pick(N, bn_pref), _pick(K, bk_pref)
    nk = K // bk
    return _mm(name, (L // bm, N // bn, nk), [a, w],
               [pl.BlockSpec((bm, bk), lambda i, j, k: (i, k)), pl.BlockSpec((bk, bn), lambda i, j, k: (k, j))],
               [_sds((L, N), F32)], [pl.BlockSpec((bm, bn), lambda i, j, k: (i, j))], NN, nk,
               acc_shape=(bm, bn))


def _up_epilogue(val, extras, outs):
    outs[0][...] = val
    r = jnp.maximum(val, 0.0)
    outs[1][...] = (r * r).astype(BF16)


def _mlp_down_bwd(df, w_down, a_up):
    L, D = df.shape
    F = w_down.shape[0]
    bm = _pick(L, 1088)
    bn = _pick(F, 1024)

    def epilogue(val, extras, outs):
        outs[0][...] = (val * (2.0 * jnp.maximum(extras[0][...], 0.0))).astype(BF16)

    blk = pl.BlockSpec((bm, bn), lambda i, j: (i, j))
    return _mm("mlp_down_bwd", (L // bm, F // bn), [df, w_down, a_up],
               [pl.BlockSpec((bm, D), lambda i, j: (i, 0)), pl.BlockSpec((bn, D), lambda i, j: (j, 0)), blk],
               [_sds((L, F), BF16)], [blk], NT, 1, epilogue)


ANY = pl.BlockSpec(memory_space=pl.ANY)
HBM = pl.BlockSpec(memory_space=pltpu.HBM)
SEM = pl.BlockSpec(memory_space=pltpu.SEMAPHORE)
EFFECT = pltpu.SideEffectType.DATAFLOW_SIDE_EFFECTING
N_CHIPS = 4


def _place():
    x, y, c = lax.axis_index("x"), lax.axis_index("y"), lax.axis_index("c")
    return x, y, c


def _chip_at(x, y, k):
    px = 1 - x if k & 2 else x
    py = 1 - y if k & 1 else y
    return px, py


def _cast_into_slab(w2d, chip, dtype):
    R, C = w2d.shape
    T = _elem_tile(R, C)

    def body(p_ref, w_ref, o_ref):
        o_ref[...] = w_ref[...].astype(dtype)

    return _call(
        body, (w2d,), name="cast_into_slab", grid=(R // T,), scalars=jnp.reshape(chip, (1,)).astype(jnp.int32),
        in_specs=[pl.BlockSpec((T, C), lambda i, p: (i, 0))],
        out_specs=pl.BlockSpec((None, T, C), lambda i, p: (p[0], i, 0)),
        out_shape=_sds((N_CHIPS, R, C), dtype), compiler_params=_params(1))


TOKEN = jax.ShapeDtypeStruct((8, LANES), F32)


class _Sems:
    def __init__(self, items, shape):
        self.items, self.shape = list(items), tuple(shape)

    def pair(self, idx):
        flat = 0
        for i, n in zip(idx, self.shape):
            flat = flat * n + i
        half = len(self.items) // 2
        return self.items[flat], self.items[half + flat]


def _sem_count(shape):
    n = 1
    for s in shape:
        n *= s
    return n


def _remote(src, dst, sems, idx, device):
    send, recv = sems.pair(idx)
    return pltpu.make_async_remote_copy(src_ref=src, dst_ref=dst, send_sem=send, recv_sem=recv,
                                        device_id=device, device_id_type=MESH)


def _thru(arrays):
    return ([pltpu.with_memory_space_constraint(a, pltpu.HBM) for a in arrays],
            [pltpu.HBM(a.shape, a.dtype) for a in arrays])


def _comm_start(name, arrays, sem_shape, plan):
    na, ns = len(arrays), 2 * _sem_count(sem_shape)

    def body(*refs):
        sems, token = _Sems(refs[na:na + ns], sem_shape), refs[-1]
        for src, dst, idx, device in plan(refs[:na])[0]:
            _remote(src, dst, sems, idx, device).start()
        token[...] = jnp.zeros(token.shape, F32)

    ins, outs = _thru(arrays)
    res = _call(
        body, ins, name=name, in_specs=[HBM] * na, mark=-1,
        out_specs=[SEM] * ns + [HBM] * na + [pl.BlockSpec(memory_space=pltpu.VMEM)],
        out_shape=[pltpu.SemaphoreType.DMA(())] * ns + outs + [TOKEN],
        input_output_aliases={a: ns + a for a in range(na)},
        compiler_params=pltpu.CompilerParams(has_side_effects=EFFECT))
    return _Sems(res[:ns], sem_shape), list(res[ns:ns + na])


def _comm_wait(name, arrays, sems, plan):
    na, ns = len(arrays), len(sems.items)

    def body(*refs):
        mine_sems = _Sems(refs[na:na + ns], sems.shape)
        x, y, c = _place()
        _, mine, arrivals = plan(refs[:na])
        for dst, idx in arrivals:
            _remote(dst, dst, mine_sems, idx, (x, y, c)).wait_recv()
        for src, idx in mine:
            _remote(src, src, mine_sems, idx, (x, y, c)).wait_send()
        refs[-1][...] = jnp.zeros(refs[-1].shape, F32)

    ins, outs = _thru(arrays)
    res = _call(
        body, ins + sems.items, name=name, in_specs=[HBM] * na + [SEM] * ns, mark=-1,
        out_specs=[HBM] * na + [pl.BlockSpec(memory_space=pltpu.VMEM)], out_shape=outs + [TOKEN],
        input_output_aliases={a: a for a in range(na)},
        compiler_params=pltpu.CompilerParams(has_side_effects=EFFECT))
    return list(res[:na])


def _comm_relay(name, arrays, sems, plan, sem_shape, next_plan):
    na, ns_in, ns_out = len(arrays), len(sems.items), 2 * _sem_count(sem_shape)

    def body(*refs):
        bufs = refs[:na]
        sems_in = _Sems(refs[na:na + ns_in], sems.shape)
        sems_out = _Sems(refs[na + ns_in:na + ns_in + ns_out], sem_shape)
        x, y, c = _place()
        _, mine, arrivals = plan(bufs)
        onward = next_plan(bufs)[0]
        for i, (dst, idx) in enumerate(arrivals):
            _remote(dst, dst, sems_in, idx, (x, y, c)).wait_recv()
            if i < len(onward):
                src, to, idx2, device = onward[i]
                _remote(src, to, sems_out, idx2, device).start()
        for src, idx in mine:
            _remote(src, src, sems_in, idx, (x, y, c)).wait_send()
        refs[-1][...] = jnp.zeros(refs[-1].shape, F32)

    ins, outs = _thru(arrays)
    res = _call(
        body, ins + sems.items, name=name, in_specs=[HBM] * na + [SEM] * ns_in, mark=-1,
        out_specs=[SEM] * ns_out + [HBM] * na + [pl.BlockSpec(memory_space=pltpu.VMEM)],
        out_shape=[pltpu.SemaphoreType.DMA(())] * ns_out + outs + [TOKEN],
        input_output_aliases={a: ns_out + a for a in range(na)},
        compiler_params=pltpu.CompilerParams(has_side_effects=EFFECT))
    return _Sems(res[:ns_out], sem_shape), list(res[ns_out:ns_out + na])


def _half(ref, q, which):
    h = ref.shape[1] // 2
    return ref.at[q, pl.ds(which * h, h)]


def _gather_plan(n_halved):
    def plan(refs):
        x, y, c = _place()
        p = 2 * x + y
        starts, mine, arrivals = [], [], []
        for n, ref in enumerate(refs):
            for k in range(1, N_CHIPS):
                px, py = _chip_at(x, y, k)
                q = 2 * px + py
                out = _half(ref, p, c) if n < n_halved else ref.at[p]
                inc = _half(ref, q, c) if n < n_halved else ref.at[q]
                starts.append((out, out, (n, k - 1), (px, py, c)))
                mine.append((out, (n, k - 1)))
                arrivals.append((inc, (n, k - 1)))
        return starts, mine, arrivals
    return plan


def _hand_on_plan(n_halved):
    def plan(refs):
        x, y, c = _place()
        starts, mine, arrivals = [], [], []
        for n in range(n_halved):
            for k in range(1, N_CHIPS):
                px, py = _chip_at(x, y, k)
                q = 2 * px + py
                landed = _half(refs[n], q, c)
                starts.append((landed, landed, (n, k - 1), (x, y, 1 - c)))
                mine.append((landed, (n, k - 1)))
                arrivals.append((_half(refs[n], q, 1 - c), (n, k - 1)))
        return starts, mine, arrivals
    return plan


def _swap_plan(n):
    def plan(refs):
        x, y, c = _place()
        starts, mine, arrivals = [], [], []
        for a in range(n):
            h = refs[a].shape[1] // 2
            src = refs[a].at[:, pl.ds((1 - c) * h, h)]
            starts.append((src, refs[n + a], (a,), (x, y, 1 - c)))
            mine.append((src, (a,)))
            arrivals.append((refs[n + a], (a,)))
        return starts, mine, arrivals
    return plan


def _scatter_plan(n):
    def plan(refs):
        x, y, c = _place()
        starts, mine, arrivals = [], [], []
        for a in range(n):
            for k in range(1, N_CHIPS):
                px, py = _chip_at(x, y, k)
                src = refs[a].at[2 * px + py]
                starts.append((src, refs[n + a].at[k - 1], (a, k - 1), (px, py, c)))
                mine.append((src, (a, k - 1)))
                arrivals.append((refs[n + a].at[k - 1], (a, k - 1)))
        return starts, mine, arrivals
    return plan


def _share_plan(n):
    def plan(refs):
        x, y, c = _place()
        starts, mine, arrivals = [], [], []
        for a in range(n):
            h = refs[a].shape[0] // 2
            own = refs[a].at[pl.ds(c * h, h)]
            starts.append((own, own, (a,), (x, y, 1 - c)))
            mine.append((own, (a,)))
            arrivals.append((refs[a].at[pl.ds((1 - c) * h, h)], (a,)))
        return starts, mine, arrivals
    return plan


def _gather_packs(pack):
    n_dev = 8

    def body(in_ref, out_ref, send, recv, lsem):
        x, y, c = _place()
        me = 4 * x + 2 * y + c
        sends = []
        for r in range(1, n_dev):
            peer = (1 - x if r & 4 else x, 1 - y if r & 2 else y, 1 - c if r & 1 else c)
            cp = pltpu.make_async_remote_copy(
                src_ref=in_ref, dst_ref=out_ref.at[me], send_sem=send.at[r - 1], recv_sem=recv.at[r - 1],
                device_id=peer, device_id_type=MESH)
            cp.start()
            sends.append(cp)
        mine = pltpu.make_async_copy(in_ref, out_ref.at[me], lsem)
        mine.start()
        for r in range(1, n_dev):
            peer = (1 - x if r & 4 else x, 1 - y if r & 2 else y, 1 - c if r & 1 else c)
            src = 4 * peer[0] + 2 * peer[1] + peer[2]
            pltpu.make_async_remote_copy(
                src_ref=in_ref, dst_ref=out_ref.at[src], send_sem=send.at[r - 1], recv_sem=recv.at[r - 1],
                device_id=peer, device_id_type=MESH).wait_recv()
        for cp in sends:
            cp.wait_send()
        mine.wait()

    return _call(
        body, (pack,), name="gather_packs", in_specs=[ANY], out_specs=ANY,
        out_shape=_sds((n_dev,) + pack.shape, pack.dtype),
        scratch_shapes=[pltpu.SemaphoreType.DMA((n_dev - 1,)), pltpu.SemaphoreType.DMA((n_dev - 1,)),
                        pltpu.SemaphoreType.DMA],
        compiler_params=pltpu.CompilerParams(has_side_effects=True))


class _Reduction:
    def __init__(self, tag, slabs, c_idx, chip):
        self.tag, self.n, self.c_idx, self.chip = tag, len(slabs), c_idx, chip
        lands = [lax.empty((g.shape[0], g.shape[1] // 2, g.shape[2]), g.dtype) for g in slabs]
        self.sems = _comm_start("swap_start_" + tag, list(slabs) + lands, (self.n,), _swap_plan(self.n))

    def partial(self):
        n = self.n
        sems, bufs = self.sems
        bufs = _comm_wait("swap_wait_" + self.tag, bufs, sems, _swap_plan(n))
        both = [_chip_partial(g, r, self.c_idx, self.chip) for g, r in zip(bufs[:n], bufs[n:])]
        self.own = [o for _, o in both]
        parts = [p for p, _ in both]
        lands = [lax.empty((N_CHIPS - 1,) + p.shape[1:], p.dtype) for p in parts]
        self.sems = _comm_start("scatter_start_" + self.tag, parts + lands, (n, N_CHIPS - 1), _scatter_plan(n))

    def total(self):
        n = self.n
        sems, bufs = self.sems
        bufs = _comm_wait("scatter_wait_" + self.tag, bufs, sems, _scatter_plan(n))
        fulls = [_sum_partials(o, r, self.c_idx) for o, r in zip(self.own, bufs[n:])]
        self.sems = _comm_start("share_start_" + self.tag, fulls, (n,), _share_plan(n))

    def finish(self):
        sems, bufs = self.sems
        return _comm_wait("share_wait_" + self.tag, bufs, sems, _share_plan(self.n))


def _elem_tile(rows, cols):
    return _pick(rows, max(8, (1 << 19) // cols // 8 * 8))


def _chip_partial(grad, recv, c_idx, p_idx):
    Q, R, C = grad.shape
    h = R // 2
    T = _elem_tile(h, C)
    nt = h // T

    def body(sc_ref, g_ref, r_ref, sb_ref, own_ref):
        q = pl.program_id(1)
        s = g_ref[...] + r_ref[...]
        sb_ref[...] = s.astype(BF16)

        @pl.when(q == sc_ref[1])
        def _():
            own_ref[...] = s

    return _call(
        body, (grad, recv), name="chip_partial", grid=(nt, Q),
        scalars=jnp.stack([c_idx, p_idx]).astype(jnp.int32),
        in_specs=[pl.BlockSpec((None, T, C), lambda t, q, sc: (q, sc[0] * nt + t, 0)),
                  pl.BlockSpec((None, T, C), lambda t, q, sc: (q, t, 0))],
        out_specs=[pl.BlockSpec((None, T, C), lambda t, q, sc: (q, t, 0)),
                   pl.BlockSpec((T, C), lambda t, q, sc: (t, 0))],
        out_shape=[_sds((Q, h, C), BF16), _sds((h, C), F32)], compiler_params=_params(2))


def _sum_partials(own, parts, c_idx):
    h, C = own.shape
    T = _elem_tile(h, C)
    nt = h // T

    def body(c_ref, o_ref, p_ref, t_ref):
        t = o_ref[...]
        for k in range(N_CHIPS - 1):
            t = t + p_ref[k].astype(F32)
        t_ref[...] = t

    return _call(
        body, (own, parts), name="sum_partials", grid=(nt,), scalars=jnp.reshape(c_idx, (1,)).astype(jnp.int32),
        in_specs=[pl.BlockSpec((T, C), lambda i, c: (i, 0)),
                  pl.BlockSpec((N_CHIPS - 1, T, C), lambda i, c: (0, i, 0))],
        out_specs=pl.BlockSpec((T, C), lambda i, c: (c[0] * nt + i, 0)),
        out_shape=_sds((2 * h, C), F32), compiler_params=_params(1))


def _pack_rows(name, parts):
    width = parts[0].shape[1]
    offsets, at = [], 0
    for p in parts:
        offsets.append(at)
        at += p.shape[0]
    total = -(-at // 8) * 8

    def body(*refs):
        out = refs[-1]
        out[...] = jnp.zeros(out.shape, F32)
        for ref, o in zip(refs[:-1], offsets):
            out[o:o + ref.shape[0], :] = ref[...]

    whole = pl.BlockSpec(memory_space=pltpu.VMEM)
    return _call(body, list(parts), name=name, in_specs=[whole] * len(parts), out_specs=whole,
                 out_shape=_sds((total, width), F32))


def _sum_packs(packs):
    n, R, C = packs.shape

    def body(p_ref, o_ref):
        t = p_ref[0]
        for k in range(1, n):
            t = t + p_ref[k]
        o_ref[...] = t

    return _call(
        body, (packs,), name="sum_packs", grid=(1,), in_specs=[pl.BlockSpec((n, R, C), lambda i: (0, 0, 0))],
        out_specs=pl.BlockSpec((R, C), lambda i: (0, 0)), out_shape=_sds((R, C), F32), compiler_params=_params(1))


def _adamw(w, g, m, v):
    R, C = w.shape
    T = _elem_tile(R, C)

    def body(w_ref, g_ref, m_ref, v_ref, d_ref, m2_ref, v2_ref):
        g_ = g_ref[...]
        m2 = ADAM_B1 * m_ref[...] + (1.0 - ADAM_B1) * g_
        v2 = ADAM_B2 * v_ref[...] + (1.0 - ADAM_B2) * (g_ * g_)
        m_hat = m2 / (1.0 - ADAM_B1 ** ADAM_STEP)
        v_hat = v2 / (1.0 - ADAM_B2 ** ADAM_STEP)
        d_ref[...] = -ADAM_LR * (m_hat / (jnp.sqrt(v_hat) + ADAM_EPS) + ADAM_WD * w_ref[...])
        m2_ref[...] = m2
        v2_ref[...] = v2

    blk = pl.BlockSpec((T, C), lambda i: (i, 0))
    return _call(
        body, (w, g, m, v), name="adamw", grid=(R // T,), in_specs=[blk] * 4, out_specs=[blk] * 3,
        out_shape=[_sds((R, C), F32)] * 3, compiler_params=_params(1))


BIG = ("w_in", "w_pool_out", "w_conv_out", "w_o", "w_up", "w_down", "w_pool_grp")
VECTORS = ("g_pre_mix", "pool_scale", "b_dw", "conv_ln_g", "conv_ln_b", "g_post_mix", "g_pre_mlp", "g_post_mlp")
WEIGHTS = ("meta", "g_pre_mix", "w_in", "w_pool_grp", "pool_scale", "w_pool_out", "w_dw", "b_dw", "conv_ln_g",
           "conv_ln_b", "w_conv_out", "w_o", "g_post_mix", "g_pre_mlp", "w_up", "w_down", "g_post_mlp")


def _as_rows(a, width):
    r, cols = a.shape
    return a.reshape(r * (cols // width), width)


def _step(w, m, v, x, tgt):
    S, D = x.shape
    P = D // 2
    xi, yi, ci = _place()
    chip = 2 * xi + yi
    _CHAIN["after"] = None

    C = D // 2
    G = POOL_GROUPS
    GD = P // G
    GS = GD // N_CHIPS
    Q = N_CHIPS
    vecs = {k: w[k] for k in VECTORS}
    shard2d = {k: w[k].reshape(-1, w[k].shape[-1]) for k in BIG}
    grads, delta, new_m, new_v = {}, {}, {}, {}

    def update(names, reduced):
        for k, g in zip(names, reduced):
            grads[k] = g
            delta[k], new_m[k], new_v[k] = _adamw(shard2d[k], g, m[k].reshape(shard2d[k].shape),
                                                  v[k].reshape(shard2d[k].shape))

    groups = (("a", ("w_in", "w_pool_grp"), ("w_dw", "meta")), ("b", ("w_pool_out", "w_conv_out", "w_o"), ()),
              ("c", ("w_up",), ()), ("d", ("w_down",), ()))
    flying = {}
    for tag, halved, whole in groups:
        slabs = [_cast_into_slab(shard2d[k], chip, BF16) for k in halved]
        slabs += [_cast_into_slab(w[k], chip, F32) for k in whole]
        flying[tag] = _comm_start("gather_start_" + tag, slabs, (len(slabs), N_CHIPS - 1), _gather_plan(len(halved)))

    def landed(tag):
        _, halved, whole = next(g for g in groups if g[0] == tag)
        nh = len(halved)
        sems, bufs = flying.pop(tag)
        sems, bufs = _comm_relay("gather_relay_" + tag, bufs, sems, _gather_plan(nh),
                                 (nh, N_CHIPS - 1), _hand_on_plan(nh))
        done = _comm_wait("gather_wait_" + tag, bufs[:nh], sems, _hand_on_plan(nh))
        return dict(zip(halved + whole, done + bufs[nh:]))

    got = landed("a")
    win_g = got["w_in"]
    w_grp = got["w_pool_grp"].reshape(N_CHIPS, G, GS, GD).transpose(1, 0, 2, 3).reshape(G, GD, GD)
    w_dw = got["w_dw"].transpose(1, 0, 2).reshape(CONV_TAPS, P)
    meta = got["meta"].transpose(1, 0, 2).reshape(N_META, D)
    h0 = jnp.concatenate([jnp.zeros((PAD_ROWS, D), F32), meta, x], axis=0)
    u1 = _pre_norm(h0, vecs["g_pre_mix"])
    proj = _mm_act_colw("proj", u1, win_g, 256)
    d, ya_pre = _pool_fwd(proj, w_grp, vecs["pool_scale"])
    s, c = _conv_fwd(proj, C, w_dw, vecs["b_dw"], vecs["conv_ln_g"], vecs["conv_ln_b"])
    got = landed("b")
    wpo_g, wco_g, w_o = got["w_pool_out"], got["w_conv_out"], got["w_o"].reshape(D, D)
    mix, ya, yb = _mix_fwd(ya_pre, s, wpo_g, wco_g, proj, D)
    o = _mm_act_roww("attn_out", mix, w_o, 1088, 1024, 2048)
    h1, u2 = _mid_norm(o, h0, vecs["g_post_mix"], vecs["g_pre_mlp"])
    wup_g = landed("c")["w_up"]
    a_up, fact = _mm_act_colw("mlp_up", u2, wup_g, 512, _up_epilogue, (F32, BF16))
    w_down = landed("d")["w_down"].reshape(-1, D)
    f = _mm_act_roww("mlp_down", fact, w_down, 1088, 1024, 2048)
    dy, df, dg_post_mlp, loss = _loss_head(f, h1, tgt, vecs["g_post_mlp"])

    g_w_down = _mm_wgrad("dw_down", fact, df, 1024, 1024)
    red1 = _Reduction("1", [g_w_down.reshape(N_CHIPS, -1, D)], ci, chip)
    da_up = _mlp_down_bwd(df, w_down, a_up)
    red1.partial()
    g_w_up = _mm_wgrad_colw("dw_up", u2, da_up, Q, 1024, 1024)
    red2 = _Reduction("2", [g_w_up], ci, chip)
    du2 = _mm_grad_colw_t("du2", da_up, wup_g, 1088, 1024)
    red2.partial()
    dh1, do, dg_pre_mlp, dg_post_mix = _mid_norm_bwd(dy, du2, h1, o, vecs["g_pre_mlp"], vecs["g_post_mix"])
    g_w_o = _mm_wgrad("dw_o", mix, do, 1024, 1024)
    red1.total()
    dya, dyb, dga, dgb = _mix_bwd(do, w_o, proj, ya, yb)
    update(("w_down",), red1.finish())
    g_wpo = _mm_wgrad_colw("dw_pool_out", ya_pre, dya, Q, 1024, 512)
    g_wco = _mm_wgrad_colw("dw_conv_out", s, dyb, Q, 1024, 512)
    red3 = _Reduction("3", [g_w_o.reshape(N_CHIPS, D // N_CHIPS, D), g_wpo, g_wco], ci, chip)
    dya_pre = _mm_grad_colw_t("dya_pre", dya, wpo_g, 1088, 1024)
    ds = _mm_grad_colw_t("ds", dyb, wco_g, 1088, 1024)
    red3.partial()
    dz, g_w_grp, dscale = _pool_bwd(dya_pre, d, w_grp, vecs["pool_scale"])
    dc, dln_g, dln_b, db_dw = _conv_ln_bwd(ds, c, vecs["conv_ln_g"], vecs["conv_ln_b"])
    red2.total()
    dv, dgc, g_w_dw = _conv_bwd(dc, proj, C, w_dw)
    update(("w_up",), red2.finish())
    dproj = jnp.concatenate([dz, dv, dgc, dga, dgb], axis=1)
    g_w_in = _mm_wgrad_colw("dw_in", u1, dproj, Q, 512, 1792)
    g_w_grp = g_w_grp.reshape(G, N_CHIPS, GS, GD).transpose(1, 0, 2, 3).reshape(N_CHIPS, G * GS, GD)
    red4 = _Reduction("4", [g_w_in, g_w_grp], ci, chip)
    du1 = _mm_grad_colw_t("du1", dproj, win_g, 1088, 1024)
    red4.partial()
    red3.total()
    grad_x, dmeta, dg_pre_mix = _pre_norm_bwd(dh1, du1, h0, vecs["g_pre_mix"])
    update(("w_o", "w_pool_out", "w_conv_out"), red3.finish())
    g_vec = dict(g_pre_mix=dg_pre_mix, pool_scale=dscale, b_dw=db_dw, conv_ln_g=dln_g, conv_ln_b=dln_b,
                 g_post_mix=dg_post_mix, g_pre_mlp=dg_pre_mlp, g_post_mlp=dg_post_mlp)

    rows = [g_w_dw, _as_rows(dmeta, P)] + [_as_rows(g_vec[k], P) for k in VECTORS]
    rows.append(jnp.broadcast_to(loss[:, :1], (1, P)))
    total = _sum_packs(_gather_packs(_pack_rows("pack_grads", rows)))
    at = 0
    taps_pad = g_w_dw.shape[0]
    g_dw_full = total[at:at + CONV_TAPS]
    at += taps_pad
    g_meta_full = total[at:at + 2 * N_META].reshape(N_META, D)
    at += 2 * N_META
    for k in VECTORS:
        n = w[k].shape[-1] // P
        grads[k] = total[at:at + n].reshape(1, n * P)
        at += n
    loss_total = total[at, 0]
    grads["w_dw"] = lax.dynamic_slice_in_dim(g_dw_full, chip * (P // N_CHIPS), P // N_CHIPS, axis=1)
    grads["meta"] = lax.dynamic_slice_in_dim(g_meta_full, chip * (D // N_CHIPS), D // N_CHIPS, axis=1)

    small_names = [k for k in WEIGHTS if k not in BIG]

    def pack_small(tree):
        parts = []
        for k in small_names:
            a = tree[k].reshape(-1, tree[k].shape[-1])
            flat = a.reshape(-1)
            parts.append(jnp.pad(flat, (0, -flat.shape[0] % P)).reshape(-1, P))
        return _pack_rows("pack_small", parts)

    sd, sm, sv = _adamw(pack_small(w), pack_small(grads), pack_small(m), pack_small(v))
    at = 0
    for k in small_names:
        a = w[k].reshape(-1, w[k].shape[-1])
        n = -(-a.size // P)
        for tree, packed in ((delta, sd), (new_m, sm), (new_v, sv)):
            tree[k] = packed[at:at + n].reshape(-1)[:a.size].reshape(a.shape)
        at += n

    red4.total()
    update(("w_in", "w_pool_grp"), red4.finish())
    return loss_total, grad_x, grads, delta, new_m, new_v


def kernel(x, meta, g_pre_mix, w_in, w_pool_grp, pool_scale, w_pool_out, w_dw, b_dw, conv_ln_g, conv_ln_b, w_conv_out, w_o, g_post_mix, g_pre_mlp, w_up, w_down, g_post_mlp, loss_target, m_meta, m_g_pre_mix, m_w_in, m_w_pool_grp, m_pool_scale, m_w_pool_out, m_w_dw, m_b_dw, m_conv_ln_g, m_conv_ln_b, m_w_conv_out, m_w_o, m_g_post_mix, m_g_pre_mlp, m_w_up, m_w_down, m_g_post_mlp, v_meta, v_g_pre_mix, v_w_in, v_w_pool_grp, v_pool_scale, v_w_pool_out, v_w_dw, v_b_dw, v_conv_ln_g, v_conv_ln_b, v_w_conv_out, v_w_o, v_g_post_mix, v_g_pre_mlp, v_w_up, v_w_down, v_g_post_mlp):
    args = dict(locals())
    shapes = {k: args[k].shape for k in WEIGHTS}
    w = {k: args[k] for k in WEIGHTS}
    m = {k: args["m_" + k] for k in WEIGHTS}
    v = {k: args["v_" + k] for k in WEIGHTS}
    for tree in (w, m, v):
        tree["w_dw"] = tree["w_dw"].reshape(tree["w_dw"].shape[-2:])
    loss, grad_x, grads, delta, new_m, new_v = _step(w, m, v, x[0], loss_target[0])
    out = [loss, grad_x[None]]
    for tree in (grads, delta, new_m, new_v):
        out += [tree[k].reshape(shapes[k]) for k in WEIGHTS]
    return tuple(out)
```

```python
import jax
import jax.numpy as jnp
from jax import lax
from jax.experimental import pallas as pl
from jax.experimental.pallas import tpu as pltpu

F32 = jnp.float32
BF16 = jnp.bfloat16

N_META = 16
PAD_ROWS = 112
TOKEN_ROW0 = PAD_ROWS + N_META
POOL_GROUPS = 4
CONV_TAPS = 31
HALO = 32
CONV_ROWS = 128
LANES = 128
RMS_EPS = 1e-6
LN_EPS = 1e-5
ADAM_LR = 0.001
ADAM_B1 = 0.9
ADAM_B2 = 0.999
ADAM_EPS = 1e-08
ADAM_WD = 0.01
ADAM_STEP = 10
VMEM_LIMIT_MB = 56

MESH = pl.DeviceIdType.MESH
NN = (((1,), (0,)), ((), ()))
NT = (((1,), (1,)), ((), ()))
TN = (((0,), (0,)), ((), ()))


def _pick(n, pref):
    if n <= pref:
        return n
    if n % pref == 0:
        return pref
    for step in (LANES, 8, 1):
        t = (pref // step) * step
        while t >= step:
            if n % t == 0:
                return t
            t -= step
    return n


def _params(n_axes, vmem_mb=VMEM_LIMIT_MB):
    return pltpu.CompilerParams(dimension_semantics=("arbitrary",) * n_axes,
                                vmem_limit_bytes=vmem_mb << 20)


def _sigmoid(x):
    return jax.nn.sigmoid(x)


_CHAIN = {"after": None}


def _call(body, args, *, in_specs, out_specs, out_shape, grid=(), scalars=None, mark=0, follows=True, **kw):
    after = _CHAIN["after"] if follows else None
    n = len(args)
    lead = 0 if scalars is None else 1
    specs = list(in_specs)
    operands = list(args)
    fn = body
    if after is not None:
        def fn(*refs):
            body(*refs[:lead + n], *refs[lead + n + 1:])
        specs.append(pl.BlockSpec(memory_space=pl.ANY))
        operands.append(after)
    if scalars is None:
        if grid:
            kw["grid"] = grid
        res = pl.pallas_call(fn, in_specs=specs, out_specs=out_specs, out_shape=out_shape, **kw)(*operands)
    else:
        grid_spec = pltpu.PrefetchScalarGridSpec(num_scalar_prefetch=1, grid=grid, in_specs=specs, out_specs=out_specs)
        res = pl.pallas_call(fn, grid_spec=grid_spec, out_shape=out_shape, **kw)(scalars, *operands)
    outs = res if isinstance(res, (list, tuple)) else [res]
    _CHAIN["after"] = outs[mark]
    return res


def _store(val, extras, outs):
    outs[0][...] = val.astype(outs[0].dtype)


def _mm(name, grid, arrays, in_specs, out_shapes, out_specs, dims, nk, epilogue=_store, acc_shape=None):
    n_in, n_out = len(arrays), len(out_shapes)

    def body(*refs):
        extras = refs[2:n_in]
        outs = refs[n_in:n_in + n_out]
        part = lax.dot_general(refs[0][...], refs[1][...], dims, preferred_element_type=F32)
        if nk == 1:
            epilogue(part, extras, outs)
        else:
            acc = refs[n_in + n_out]
            k = pl.program_id(len(grid) - 1)

            @pl.when(k == 0)
            def _():
                acc[...] = part

            @pl.when(k > 0)
            def _():
                acc[...] += part

            @pl.when(k == nk - 1)
            def _():
                epilogue(acc[...], extras, outs)

    scratch = [pltpu.VMEM(acc_shape, F32)] if nk > 1 else []
    single = n_out == 1
    return _call(
        body, arrays, name=name, grid=grid, in_specs=in_specs,
        out_specs=out_specs[0] if single else out_specs,
        out_shape=out_shapes[0] if single else out_shapes,
        scratch_shapes=scratch, compiler_params=_params(len(grid)))


def _sds(shape, dtype):
    return jax.ShapeDtypeStruct(shape, dtype)


def _rms_scale(h):
    return lax.rsqrt(jnp.mean(h * h, axis=-1, keepdims=True) + RMS_EPS)


def _rms_bwd(du, h, g):
    r = _rms_scale(h)
    y = h * r
    dy = du * g
    dh = r * (dy - y * jnp.mean(dy * y, axis=-1, keepdims=True))
    return dh, jnp.sum(du * y, axis=0, keepdims=True)


def _row_tile(L):
    return _pick(L, 272)


def _pre_norm(h0, g):
    L, D = h0.shape
    T = _row_tile(L)

    def body(h_ref, g_ref, u_ref):
        h = h_ref[...]
        u_ref[...] = (h * _rms_scale(h) * g_ref[...]).astype(BF16)

    return _call(
        body, (h0, g), name="pre_norm", grid=(L // T,),
        in_specs=[pl.BlockSpec((T, D), lambda i: (i, 0)), pl.BlockSpec((1, D), lambda i: (0, 0))],
        out_specs=pl.BlockSpec((T, D), lambda i: (i, 0)),
        out_shape=_sds((L, D), BF16), compiler_params=_params(1))


def _mid_norm(o, h0, g_post, g_pre):
    L, D = h0.shape
    T = _row_tile(L)

    def body(o_ref, h_ref, gp_ref, gm_ref, h1_ref, u2_ref):
        o_ = o_ref[...]
        h1 = h_ref[...] + o_ * _rms_scale(o_) * gp_ref[...]
        h1_ref[...] = h1
        u2_ref[...] = (h1 * _rms_scale(h1) * gm_ref[...]).astype(BF16)

    row = pl.BlockSpec((T, D), lambda i: (i, 0))
    vec = pl.BlockSpec((1, D), lambda i: (0, 0))
    return _call(
        body, (o, h0, g_post, g_pre), name="mid_norm", grid=(L // T,),
        in_specs=[row, row, vec, vec], out_specs=[row, row],
        out_shape=[_sds((L, D), F32), _sds((L, D), BF16)], compiler_params=_params(1))


def _loss_head(f, h1, tgt, g_post):
    L, D = h1.shape
    T = TOKEN_ROW0
    n = L // T

    def body(f_ref, h_ref, t_ref, g_ref, dy_ref, df_ref, dg_ref, loss_ref):
        i = pl.program_id(0)
        f_ = f_ref[...]
        g = g_ref[...]
        y = h_ref[...] + f_ * _rms_scale(f_) * g
        live = (i > 0).astype(F32)
        diff = (y - t_ref[...]) * live
        part = 0.5 * jnp.sum(jnp.mean(diff * diff, axis=-1, keepdims=True), axis=0, keepdims=True)
        dy = diff * (1.0 / D)
        dy_ref[...] = dy
        df, dg = _rms_bwd(dy, f_, g)
        df_ref[...] = df.astype(BF16)

        @pl.when(i == 0)
        def _():
            dg_ref[...] = dg
            loss_ref[...] = jnp.broadcast_to(part, loss_ref.shape)

        @pl.when(i > 0)
        def _():
            dg_ref[...] += dg
            loss_ref[...] += jnp.broadcast_to(part, loss_ref.shape)

    row = pl.BlockSpec((T, D), lambda i: (i, 0))
    vec = pl.BlockSpec((1, D), lambda i: (0, 0))
    return _call(
        body, (f, h1, tgt, g_post), name="loss_head", grid=(n,),
        in_specs=[row, row, pl.BlockSpec((T, D), lambda i: (jnp.maximum(i - 1, 0), 0)), vec],
        out_specs=[row, row, vec, pl.BlockSpec((1, LANES), lambda i: (0, 0))],
        out_shape=[_sds((L, D), F32), _sds((L, D), BF16), _sds((1, D), F32), _sds((1, LANES), F32)],
        compiler_params=_params(1))


def _mid_norm_bwd(dy, du2, h1, o, g_pre, g_post):
    L, D = h1.shape
    T = _row_tile(L)

    def body(dy_ref, du_ref, h_ref, o_ref, gm_ref, gp_ref, dh1_ref, do_ref, dgm_ref, dgp_ref):
        i = pl.program_id(0)
        dh, dgm = _rms_bwd(du_ref[...], h_ref[...], gm_ref[...])
        dh1 = dy_ref[...] + dh
        dh1_ref[...] = dh1
        do, dgp = _rms_bwd(dh1, o_ref[...], gp_ref[...])
        do_ref[...] = do.astype(BF16)

        @pl.when(i == 0)
        def _():
            dgm_ref[...] = dgm
            dgp_ref[...] = dgp

        @pl.when(i > 0)
        def _():
            dgm_ref[...] += dgm
            dgp_ref[...] += dgp

    row = pl.BlockSpec((T, D), lambda i: (i, 0))
    vec = pl.BlockSpec((1, D), lambda i: (0, 0))
    return _call(
        body, (dy, du2, h1, o, g_pre, g_post), name="mid_norm_bwd", grid=(L // T,),
        in_specs=[row, row, row, row, vec, vec], out_specs=[row, row, vec, vec],
        out_shape=[_sds((L, D), F32), _sds((L, D), BF16), _sds((1, D), F32), _sds((1, D), F32)],
        compiler_params=_params(1))


def _pre_norm_bwd(dh1, du1, h0, g):
    L, D = h0.shape
    T = TOKEN_ROW0
    n = L // T

    def body(dh_ref, du_ref, h_ref, g_ref, gx_ref, dmeta_ref, dg_ref):
        i = pl.program_id(0)
        dh, dg = _rms_bwd(du_ref[...], h_ref[...], g_ref[...])
        dh0 = dh_ref[...] + dh
        gx_ref[...] = dh0

        @pl.when(i == 0)
        def _():
            dmeta_ref[...] = dh0[PAD_ROWS:, :]
            dg_ref[...] = dg

        @pl.when(i > 0)
        def _():
            dg_ref[...] += dg

    row = pl.BlockSpec((T, D), lambda i: (i, 0))
    vec = pl.BlockSpec((1, D), lambda i: (0, 0))
    return _call(
        body, (dh1, du1, h0, g), name="pre_norm_bwd", grid=(n,),
        in_specs=[row, row, row, vec],
        out_specs=[pl.BlockSpec((T, D), lambda i: (jnp.maximum(i - 1, 0), 0)),
                   pl.BlockSpec((N_META, D), lambda i: (0, 0)), vec],
        out_shape=[_sds((L - T, D), F32), _sds((N_META, D), F32), _sds((1, D), F32)],
        compiler_params=_params(1))


def _window_sum(z, g, shift_sign, L):
    s = z
    for j in range(POOL_GROUPS):
        k = 1 << j
        nxt = s + pltpu.roll(s, k if shift_sign > 0 else L - k, 0)
        s = jnp.where(j <= g, nxt, s)
    return s


def _inv_count(g, L):
    t = lax.broadcasted_iota(jnp.int32, (L, 1), 0)
    w = jnp.left_shift(2, g)
    cnt = jnp.clip(t - (PAD_ROWS - 1), 1, w)
    return 1.0 / cnt.astype(F32)


def _pool_fwd(proj, w_grp, scale):
    L = proj.shape[0]
    G, GD, _ = w_grp.shape
    P = G * GD

    def body(z_ref, w_ref, sc_ref, d_ref, ya_ref):
        g = pl.program_id(0)
        z = z_ref[...]
        d = (_window_sum(z, g, +1, L) * _inv_count(g, L) - z).astype(BF16)
        d_ref[...] = d
        y = jnp.dot(d, w_ref[...], preferred_element_type=F32)
        ya_ref[...] = (y * sc_ref[...]).astype(BF16)

    col = pl.BlockSpec((L, GD), lambda g: (0, g))
    return _call(
        body, (proj, w_grp, scale), name="pool_fwd", grid=(G,),
        in_specs=[col, pl.BlockSpec((None, GD, GD), lambda g: (g, 0, 0)), pl.BlockSpec((1, GD), lambda g: (0, g))],
        out_specs=[col, col], out_shape=[_sds((L, P), BF16), _sds((L, P), BF16)],
        compiler_params=_params(1))


def _pool_bwd(dya, d, w_grp, scale):
    L, P = dya.shape
    G, GD, _ = w_grp.shape

    def body(dya_ref, d_ref, w_ref, sc_ref, dz_ref, dw_ref, dsc_ref):
        g = pl.program_id(0)
        dya_ = dya_ref[...]
        d_ = d_ref[...]
        w = w_ref[...]
        y = jnp.dot(d_, w, preferred_element_type=F32)
        dsc_ref[...] = jnp.sum(dya_ * y, axis=0, keepdims=True)
        dy = (dya_ * sc_ref[...]).astype(BF16)
        dw_ref[...] = lax.dot_general(d_, dy, TN, preferred_element_type=F32)
        dd = lax.dot_general(dy, w, NT, preferred_element_type=F32)
        dz = _window_sum(dd * _inv_count(g, L), g, -1, L) - dd
        dz_ref[...] = dz.astype(BF16)

    col = pl.BlockSpec((L, GD), lambda g: (0, g))
    wspec = pl.BlockSpec((None, GD, GD), lambda g: (g, 0, 0))
    vec = pl.BlockSpec((1, GD), lambda g: (0, g))
    return _call(
        body, (dya, d, w_grp, scale), name="pool_bwd", grid=(G,),
        in_specs=[col, col, wspec, vec], out_specs=[col, wspec, vec],
        out_shape=[_sds((L, P), BF16), _sds((G, GD, GD), F32), _sds((1, P), F32)],
        compiler_params=_params(1))


def _fill_rotations(rot_ref, ext):
    n = ext.shape[0]
    rot_ref[0] = ext
    for r in range(1, 8):
        rot_ref[r] = pltpu.roll(ext, n - r, 0)


def _lane_chunks(C):
    step = LANES if C % LANES == 0 else C
    return [(c0, step) for c0 in range(0, C, step)]


def _conv_specs(L, C, col_v, col_g):
    T = CONV_ROWS
    per = T // HALO
    cur_v = pl.BlockSpec((T, C), lambda i: (i, col_v))
    cur_g = pl.BlockSpec((T, C), lambda i: (i, col_g))
    prev_v = pl.BlockSpec((HALO, C), lambda i: (jnp.maximum(i * per - 1, 0), col_v))
    prev_g = pl.BlockSpec((HALO, C), lambda i: (jnp.maximum(i * per - 1, 0), col_g))
    return cur_v, cur_g, prev_v, prev_g


def _glu_ext(vc, gc, vh, gh, i):
    a_cur = vc[...] * _sigmoid(gc[...])
    a_prev = vh[...] * _sigmoid(gh[...]) * (i > 0).astype(F32)
    return jnp.concatenate([a_prev, a_cur], axis=0)


def _conv_fwd(proj, C, w_dw, b_dw, ln_g, ln_b):
    L = proj.shape[0]
    T = CONV_ROWS
    P = C

    def body(vc, gc, vh, gh, w_ref, b_ref, lg_ref, lb_ref, s_ref, c_ref, rot):
        i = pl.program_id(0)
        _fill_rotations(rot, _glu_ext(vc, gc, vh, gh, i))
        for c0, cw in _lane_chunks(C):
            acc = jnp.zeros((T, cw), F32)
            for k in range(CONV_TAPS):
                q, r = divmod(HALO - (CONV_TAPS - 1) + k, 8)
                acc = acc + w_ref[k:k + 1, c0:c0 + cw] * rot[r, 8 * q:8 * q + T, c0:c0 + cw]
            c_ref[:, c0:c0 + cw] = acc + b_ref[:, c0:c0 + cw]
        c = c_ref[...]
        mu = jnp.mean(c, axis=-1, keepdims=True)
        cen = c - mu
        var = jnp.mean(cen * cen, axis=-1, keepdims=True)
        ln = cen * lax.rsqrt(var + LN_EPS) * lg_ref[...] + lb_ref[...]
        s_ref[...] = (ln * _sigmoid(ln)).astype(BF16)

    cur_v, cur_g, prev_v, prev_g = _conv_specs(L, C, P // C, P // C + 1)
    row = pl.BlockSpec((T, C), lambda i: (i, 0))
    vec = pl.BlockSpec((1, C), lambda i: (0, 0))
    return _call(
        body, (proj, proj, proj, proj, w_dw, b_dw, ln_g, ln_b), name="conv_fwd", grid=(L // T,),
        in_specs=[cur_v, cur_g, prev_v, prev_g, pl.BlockSpec((CONV_TAPS, C), lambda i: (0, 0)), vec, vec, vec],
        out_specs=[row, row], out_shape=[_sds((L, C), BF16), _sds((L, C), F32)],
        scratch_shapes=[pltpu.VMEM((8, T + HALO, C), F32)], compiler_params=_params(1))


def _conv_ln_bwd(ds, c, ln_g, ln_b):
    L, C = c.shape
    T = _row_tile(L)

    def body(ds_ref, c_ref, lg_ref, lb_ref, dc_ref, dlg_ref, dlb_ref, db_ref):
        i = pl.program_id(0)
        c_ = c_ref[...]
        g = lg_ref[...]
        mu = jnp.mean(c_, axis=-1, keepdims=True)
        cen = c_ - mu
        rstd = lax.rsqrt(jnp.mean(cen * cen, axis=-1, keepdims=True) + LN_EPS)
        xhat = cen * rstd
        ln = xhat * g + lb_ref[...]
        sg = _sigmoid(ln)
        dln = ds_ref[...] * (sg * (1.0 + ln * (1.0 - sg)))
        dxh = dln * g
        dc = rstd * (dxh - jnp.mean(dxh, axis=-1, keepdims=True)
                     - xhat * jnp.mean(dxh * xhat, axis=-1, keepdims=True))
        dc_ref[...] = dc
        dlg = jnp.sum(dln * xhat, axis=0, keepdims=True)
        dlb = jnp.sum(dln, axis=0, keepdims=True)
        db = jnp.sum(dc, axis=0, keepdims=True)

        @pl.when(i == 0)
        def _():
            dlg_ref[...] = dlg
            dlb_ref[...] = dlb
            db_ref[...] = db

        @pl.when(i > 0)
        def _():
            dlg_ref[...] += dlg
            dlb_ref[...] += dlb
            db_ref[...] += db

    row = pl.BlockSpec((T, C), lambda i: (i, 0))
    vec = pl.BlockSpec((1, C), lambda i: (0, 0))
    return _call(
        body, (ds, c, ln_g, ln_b), name="conv_ln_bwd", grid=(L // T,),
        in_specs=[row, row, vec, vec], out_specs=[row, vec, vec, vec],
        out_shape=[_sds((L, C), F32), _sds((1, C), F32), _sds((1, C), F32), _sds((1, C), F32)],
        compiler_params=_params(1))


def _conv_bwd(dc, proj, C, w_dw):
    L = proj.shape[0]
    T = CONV_ROWS
    per = T // HALO
    n = L // T
    P = C
    taps_pad = 32

    def body(dcc, dcn, vc, gc, w_ref, dv_ref, dg_ref, dw_ref, rot_d, dw_acc):
        i = pl.program_id(0)
        dc_next = dcn[...] * (i < n - 1).astype(F32)
        _fill_rotations(rot_d, jnp.concatenate([dcc[...], dc_next], axis=0))

        @pl.when(i == 0)
        def _():
            dw_acc[...] = jnp.zeros(dw_acc.shape, F32)

        for c0, cw in _lane_chunks(C):
            v = vc[:, c0:c0 + cw]
            sg = _sigmoid(gc[:, c0:c0 + cw])
            a = v * sg
            da = jnp.zeros((T, cw), F32)
            for k in range(CONV_TAPS):
                q, r = divmod(CONV_TAPS - 1 - k, 8)
                slab = rot_d[r, 8 * q:8 * q + T, c0:c0 + cw]
                da = da + w_ref[k:k + 1, c0:c0 + cw] * slab
                dw_acc[k, :, c0:c0 + cw] += jnp.sum((a * slab).reshape(T // 8, 8, cw), axis=0)
            dv_ref[:, c0:c0 + cw] = (da * sg).astype(BF16)
            dg_ref[:, c0:c0 + cw] = (da * v * sg * (1.0 - sg)).astype(BF16)

        @pl.when(i == n - 1)
        def _():
            dw_ref[...] = jnp.sum(dw_acc[...], axis=1)

    cur_v, cur_g, _, _ = _conv_specs(L, C, P // C, P // C + 1)
    row = pl.BlockSpec((T, C), lambda i: (i, 0))
    nxt = pl.BlockSpec((HALO, C), lambda i: (jnp.minimum((i + 1) * per, L // HALO - 1), 0))
    wspec = pl.BlockSpec((CONV_TAPS, C), lambda i: (0, 0))
    return _call(
        body, (dc, dc, proj, proj, w_dw), name="conv_bwd", grid=(n,),
        in_specs=[row, nxt, cur_v, cur_g, wspec],
        out_specs=[row, row, pl.BlockSpec((taps_pad, C), lambda i: (0, 0))],
        out_shape=[_sds((L, C), BF16), _sds((L, C), BF16), _sds((taps_pad, C), F32)],
        scratch_shapes=[pltpu.VMEM((8, T + HALO, C), F32), pltpu.VMEM((taps_pad, 8, C), F32)],
        compiler_params=_params(1))


def _mix_fwd(ya_pre, s, wpo, wco, proj, D):
    L, P = ya_pre.shape
    Q, _, DS = wpo.shape
    bm = _pick(L, 1088)
    gate0 = (proj.shape[1] - 2 * D) // DS
    per = D // DS

    def body(a1, a2, b1, b2, ga, gb, m_ref, ya_ref, yb_ref):
        ya = jnp.dot(a1[...], b1[...], preferred_element_type=F32)
        yb = jnp.dot(a2[...], b2[...], preferred_element_type=F32)
        ya_ref[...] = ya
        yb_ref[...] = yb
        m_ref[...] = (_sigmoid(ga[...]) * ya + _sigmoid(gb[...]) * yb).astype(BF16)

    act = pl.BlockSpec((bm, P), lambda i, q: (i, 0))
    wsp = pl.BlockSpec((None, P, DS), lambda i, q: (q, 0, 0))
    out = pl.BlockSpec((bm, DS), lambda i, q: (i, q))
    return _call(
        body, (ya_pre, s, wpo, wco, proj, proj), name="mix_fwd", grid=(L // bm, Q),
        in_specs=[act, act, wsp, wsp,
                  pl.BlockSpec((bm, DS), lambda i, q: (i, gate0 + q)),
                  pl.BlockSpec((bm, DS), lambda i, q: (i, gate0 + per + q))],
        out_specs=[out, out, out],
        out_shape=[_sds((L, D), BF16), _sds((L, D), F32), _sds((L, D), F32)],
        compiler_params=_params(2))


def _mix_bwd(do, w_o, proj, ya, yb):
    L, D = do.shape
    bm = _pick(L, 544)
    bn = _pick(D // N_CHIPS, 512)
    gate0 = (proj.shape[1] - 2 * D) // bn
    per = D // bn

    def epilogue(dm, extras, outs):
        ga, gb, ya_ref, yb_ref = extras
        sa = _sigmoid(ga[...])
        sb = _sigmoid(gb[...])
        outs[0][...] = (dm * sa).astype(BF16)
        outs[1][...] = (dm * sb).astype(BF16)
        outs[2][...] = (dm * ya_ref[...] * sa * (1.0 - sa)).astype(BF16)
        outs[3][...] = (dm * yb_ref[...] * sb * (1.0 - sb)).astype(BF16)

    blk = pl.BlockSpec((bm, bn), lambda i, j: (i, j))
    return _mm(
        "mix_bwd", (L // bm, D // bn), [do, w_o, proj, proj, ya, yb],
        [pl.BlockSpec((bm, D), lambda i, j: (i, 0)), pl.BlockSpec((bn, D), lambda i, j: (j, 0)),
         pl.BlockSpec((bm, bn), lambda i, j: (i, gate0 + j)),
         pl.BlockSpec((bm, bn), lambda i, j: (i, gate0 + per + j)), blk, blk],
        [_sds((L, D), BF16)] * 4, [blk] * 4, NT, 1, epilogue)


def _mm_act_colw(name, a, wg, bn_pref, epilogue=_store, out_dtypes=(F32,)):
    L, K = a.shape
    Q, _, n = wg.shape
    bn = _pick(n, bn_pref)
    nj = n // bn
    out = pl.BlockSpec((L, bn), lambda q, j: (0, q * nj + j))
    return _mm(name, (Q, nj), [a, wg],
               [pl.BlockSpec((L, K), lambda q, j: (0, 0)), pl.BlockSpec((None, K, bn), lambda q, j: (q, 0, j))],
               [_sds((L, Q * n), dt) for dt in out_dtypes], [out] * len(out_dtypes), NN, 1, epilogue)


def _mm_grad_colw_t(name, g, wg, bm_pref, bn_pref):
    L = g.shape[0]
    Q, K, n = wg.shape
    bm = _pick(L, bm_pref)
    bn = _pick(K, bn_pref)
    return _mm(name, (L // bm, K // bn, Q), [g, wg],
               [pl.BlockSpec((bm, n), lambda i, j, k: (i, k)), pl.BlockSpec((None, bn, n), lambda i, j, k: (k, j, 0))],
               [_sds((L, K), F32)], [pl.BlockSpec((bm, bn), lambda i, j, k: (i, j))], NT, Q,
               acc_shape=(bm, bn))


def _mm_wgrad_colw(name, a, g, Q, bm_pref, bn_pref):
    L, K = a.shape
    n = g.shape[1] // Q
    bm = _pick(K, bm_pref)
    bn = _pick(n, bn_pref)
    nj = n // bn
    return _mm(name, (Q, K // bm, nj), [a, g],
               [pl.BlockSpec((L, bm), lambda q, i, j: (0, i)), pl.BlockSpec((L, bn), lambda q, i, j: (0, q * nj + j))],
               [_sds((Q, K, n), F32)], [pl.BlockSpec((None, bm, bn), lambda q, i, j: (q, i, j))], TN, 1)


def _mm_wgrad(name, a, g, bm_pref, bn_pref):
    L, K = a.shape
    N = g.shape[1]
    bm = _pick(K, bm_pref)
    bn = _pick(N, bn_pref)
    return _mm(name, (K // bm, N // bn), [a, g],
               [pl.BlockSpec((L, bm), lambda i, j: (0, i)), pl.BlockSpec((L, bn), lambda i, j: (0, j))],
               [_sds((K, N), F32)], [pl.BlockSpec((bm, bn), lambda i, j: (i, j))], TN, 1)


def _mm_act_roww(name, a, w, bm_pref, bn_pref, bk_pref):
    L, K = a.shape
    N = w.shape[1]
    bm, bn, bk = _pick(L, bm_pref), _pick(N, bn_pref), _pick(K, bk_pref)
    nk = K // bk
    return _mm(name, (L // bm, N // bn, nk), [a, w],
               [pl.BlockSpec((bm, bk), lambda i, j, k: (i, k)), pl.BlockSpec((bk, bn), lambda i, j, k: (k, j))],
               [_sds((L, N), F32)], [pl.BlockSpec((bm, bn), lambda i, j, k: (i, j))], NN, nk,
               acc_shape=(bm, bn))


def _up_epilogue(val, extras, outs):
    outs[0][...] = val
    r = jnp.maximum(val, 0.0)
    outs[1][...] = (r * r).astype(BF16)


def _mlp_down_bwd(df, w_down, a_up):
    L, D = df.shape
    F = w_down.shape[0]
    bm = _pick(L, 1088)
    bn = _pick(F, 1024)

    def epilogue(val, extras, outs):
        outs[0][...] = (val * (2.0 * jnp.maximum(extras[0][...], 0.0))).astype(BF16)

    blk = pl.BlockSpec((bm, bn), lambda i, j: (i, j))
    return _mm("mlp_down_bwd", (L // bm, F // bn), [df, w_down, a_up],
               [pl.BlockSpec((bm, D), lambda i, j: (i, 0)), pl.BlockSpec((bn, D), lambda i, j: (j, 0)), blk],
               [_sds((L, F), BF16)], [blk], NT, 1, epilogue)


ANY = pl.BlockSpec(memory_space=pl.ANY)
HBM = pl.BlockSpec(memory_space=pltpu.HBM)
SEM = pl.BlockSpec(memory_space=pltpu.SEMAPHORE)
EFFECT = pltpu.SideEffectType.DATAFLOW_SIDE_EFFECTING
N_CHIPS = 4


def _place():
    x, y, c = lax.axis_index("x"), lax.axis_index("y"), lax.axis_index("c")
    return x, y, c


def _chip_at(x, y, k):
    px = 1 - x if k & 2 else x
    py = 1 - y if k & 1 else y
    return px, py


def _cast_into_slab(w2d, chip, dtype):
    R, C = w2d.shape
    T = _elem_tile(R, C)

    def body(p_ref, w_ref, o_ref):
        o_ref[...] = w_ref[...].astype(dtype)

    return _call(
        body, (w2d,), name="cast_into_slab", grid=(R // T,), scalars=jnp.reshape(chip, (1,)).astype(jnp.int32),
        in_specs=[pl.BlockSpec((T, C), lambda i, p: (i, 0))],
        out_specs=pl.BlockSpec((None, T, C), lambda i, p: (p[0], i, 0)),
        out_shape=_sds((N_CHIPS, R, C), dtype), compiler_params=_params(1))


TOKEN = jax.ShapeDtypeStruct((8, LANES), F32)


class _Sems:
    def __init__(self, items, shape):
        self.items, self.shape = list(items), tuple(shape)

    def pair(self, idx):
        flat = 0
        for i, n in zip(idx, self.shape):
            flat = flat * n + i
        half = len(self.items) // 2
        return self.items[flat], self.items[half + flat]


def _sem_count(shape):
    n = 1
    for s in shape:
        n *= s
    return n


def _remote(src, dst, sems, idx, device):
    send, recv = sems.pair(idx)
    return pltpu.make_async_remote_copy(src_ref=src, dst_ref=dst, send_sem=send, recv_sem=recv,
                                        device_id=device, device_id_type=MESH)


def _thru(arrays):
    return ([pltpu.with_memory_space_constraint(a, pltpu.HBM) for a in arrays],
            [pltpu.HBM(a.shape, a.dtype) for a in arrays])


def _comm_start(name, arrays, sem_shape, plan):
    na, ns = len(arrays), 2 * _sem_count(sem_shape)

    def body(*refs):
        sems, token = _Sems(refs[na:na + ns], sem_shape), refs[-1]
        for src, dst, idx, device in plan(refs[:na])[0]:
            _remote(src, dst, sems, idx, device).start()
        token[...] = jnp.zeros(token.shape, F32)

    ins, outs = _thru(arrays)
    res = _call(
        body, ins, name=name, in_specs=[HBM] * na, mark=-1, follows=False,
        out_specs=[SEM] * ns + [HBM] * na + [pl.BlockSpec(memory_space=pltpu.VMEM)],
        out_shape=[pltpu.SemaphoreType.DMA(())] * ns + outs + [TOKEN],
        input_output_aliases={a: ns + a for a in range(na)},
        compiler_params=pltpu.CompilerParams(has_side_effects=EFFECT))
    return _Sems(res[:ns], sem_shape), list(res[ns:ns + na])


def _comm_wait(name, arrays, sems, plan):
    na, ns = len(arrays), len(sems.items)

    def body(*refs):
        mine_sems = _Sems(refs[na:na + ns], sems.shape)
        x, y, c = _place()
        _, mine, arrivals = plan(refs[:na])
        for dst, idx in arrivals:
            _remote(dst, dst, mine_sems, idx, (x, y, c)).wait_recv()
        for src, idx in mine:
            _remote(src, src, mine_sems, idx, (x, y, c)).wait_send()
        refs[-1][...] = jnp.zeros(refs[-1].shape, F32)

    ins, outs = _thru(arrays)
    res = _call(
        body, ins + sems.items, name=name, in_specs=[HBM] * na + [SEM] * ns, mark=-1,
        out_specs=[HBM] * na + [pl.BlockSpec(memory_space=pltpu.VMEM)], out_shape=outs + [TOKEN],
        input_output_aliases={a: a for a in range(na)},
        compiler_params=pltpu.CompilerParams(has_side_effects=EFFECT))
    return list(res[:na])


def _comm_relay(name, arrays, sems, plan, sem_shape, next_plan):
    na, ns_in, ns_out = len(arrays), len(sems.items), 2 * _sem_count(sem_shape)

    def body(*refs):
        bufs = refs[:na]
        sems_in = _Sems(refs[na:na + ns_in], sems.shape)
        sems_out = _Sems(refs[na + ns_in:na + ns_in + ns_out], sem_shape)
        x, y, c = _place()
        _, mine, arrivals = plan(bufs)
        onward = next_plan(bufs)[0]
        for i, (dst, idx) in enumerate(arrivals):
            _remote(dst, dst, sems_in, idx, (x, y, c)).wait_recv()
            if i < len(onward):
                src, to, idx2, device = onward[i]
                _remote(src, to, sems_out, idx2, device).start()
        for src, idx in mine:
            _remote(src, src, sems_in, idx, (x, y, c)).wait_send()
        refs[-1][...] = jnp.zeros(refs[-1].shape, F32)

    ins, outs = _thru(arrays)
    res = _call(
        body, ins + sems.items, name=name, in_specs=[HBM] * na + [SEM] * ns_in, mark=-1,
        out_specs=[SEM] * ns_out + [HBM] * na + [pl.BlockSpec(memory_space=pltpu.VMEM)],
        out_shape=[pltpu.SemaphoreType.DMA(())] * ns_out + outs + [TOKEN],
        input_output_aliases={a: ns_out + a for a in range(na)},
        compiler_params=pltpu.CompilerParams(has_side_effects=EFFECT))
    return _Sems(res[:ns_out], sem_shape), list(res[ns_out:ns_out + na])


def _half(ref, q, which):
    h = ref.shape[1] // 2
    return ref.at[q, pl.ds(which * h, h)]


def _gather_plan(n_halved):
    def plan(refs):
        x, y, c = _place()
        p = 2 * x + y
        starts, mine, arrivals = [], [], []
        for n, ref in enumerate(refs):
            for k in range(1, N_CHIPS):
                px, py = _chip_at(x, y, k)
                q = 2 * px + py
                out = _half(ref, p, c) if n < n_halved else ref.at[p]
                inc = _half(ref, q, c) if n < n_halved else ref.at[q]
                starts.append((out, out, (n, k - 1), (px, py, c)))
                mine.append((out, (n, k - 1)))
                arrivals.append((inc, (n, k - 1)))
        return starts, mine, arrivals
    return plan


def _hand_on_plan(n_halved):
    def plan(refs):
        x, y, c = _place()
        starts, mine, arrivals = [], [], []
        for n in range(n_halved):
            for k in range(1, N_CHIPS):
                px, py = _chip_at(x, y, k)
                q = 2 * px + py
                landed = _half(refs[n], q, c)
                starts.append((landed, landed, (n, k - 1), (x, y, 1 - c)))
                mine.append((landed, (n, k - 1)))
                arrivals.append((_half(refs[n], q, 1 - c), (n, k - 1)))
        return starts, mine, arrivals
    return plan


def _swap_plan(n):
    def plan(refs):
        x, y, c = _place()
        starts, mine, arrivals = [], [], []
        for a in range(n):
            h = refs[a].shape[1] // 2
            src = refs[a].at[:, pl.ds((1 - c) * h, h)]
            starts.append((src, refs[n + a], (a,), (x, y, 1 - c)))
            mine.append((src, (a,)))
            arrivals.append((refs[n + a], (a,)))
        return starts, mine, arrivals
    return plan


def _scatter_plan(n):
    def plan(refs):
        x, y, c = _place()
        starts, mine, arrivals = [], [], []
        for a in range(n):
            for k in range(1, N_CHIPS):
                px, py = _chip_at(x, y, k)
                src = refs[a].at[2 * px + py]
                starts.append((src, refs[n + a].at[k - 1], (a, k - 1), (px, py, c)))
                mine.append((src, (a, k - 1)))
                arrivals.append((refs[n + a].at[k - 1], (a, k - 1)))
        return starts, mine, arrivals
    return plan


def _share_plan(n):
    def plan(refs):
        x, y, c = _place()
        starts, mine, arrivals = [], [], []
        for a in range(n):
            h = refs[a].shape[0] // 2
            own = refs[a].at[pl.ds(c * h, h)]
            starts.append((own, own, (a,), (x, y, 1 - c)))
            mine.append((own, (a,)))
            arrivals.append((refs[a].at[pl.ds((1 - c) * h, h)], (a,)))
        return starts, mine, arrivals
    return plan


def _gather_packs(pack):
    n_dev = 8

    def body(in_ref, out_ref, send, recv, lsem):
        x, y, c = _place()
        me = 4 * x + 2 * y + c
        sends = []
        for r in range(1, n_dev):
            peer = (1 - x if r & 4 else x, 1 - y if r & 2 else y, 1 - c if r & 1 else c)
            cp = pltpu.make_async_remote_copy(
                src_ref=in_ref, dst_ref=out_ref.at[me], send_sem=send.at[r - 1], recv_sem=recv.at[r - 1],
                device_id=peer, device_id_type=MESH)
            cp.start()
            sends.append(cp)
        mine = pltpu.make_async_copy(in_ref, out_ref.at[me], lsem)
        mine.start()
        for r in range(1, n_dev):
            peer = (1 - x if r & 4 else x, 1 - y if r & 2 else y, 1 - c if r & 1 else c)
            src = 4 * peer[0] + 2 * peer[1] + peer[2]
            pltpu.make_async_remote_copy(
                src_ref=in_ref, dst_ref=out_ref.at[src], send_sem=send.at[r - 1], recv_sem=recv.at[r - 1],
                device_id=peer, device_id_type=MESH).wait_recv()
        for cp in sends:
            cp.wait_send()
        mine.wait()

    return _call(
        body, (pack,), name="gather_packs", in_specs=[ANY], out_specs=ANY,
        out_shape=_sds((n_dev,) + pack.shape, pack.dtype),
        scratch_shapes=[pltpu.SemaphoreType.DMA((n_dev - 1,)), pltpu.SemaphoreType.DMA((n_dev - 1,)),
                        pltpu.SemaphoreType.DMA],
        compiler_params=pltpu.CompilerParams(has_side_effects=True))


class _Reduction:
    def __init__(self, tag, slabs, c_idx, chip):
        self.tag, self.n, self.c_idx, self.chip = tag, len(slabs), c_idx, chip
        lands = [lax.empty((g.shape[0], g.shape[1] // 2, g.shape[2]), g.dtype) for g in slabs]
        self.sems = _comm_start("swap_start_" + tag, list(slabs) + lands, (self.n,), _swap_plan(self.n))

    def partial(self):
        n = self.n
        sems, bufs = self.sems
        bufs = _comm_wait("swap_wait_" + self.tag, bufs, sems, _swap_plan(n))
        both = [_chip_partial(g, r, self.c_idx, self.chip) for g, r in zip(bufs[:n], bufs[n:])]
        self.own = [o for _, o in both]
        parts = [p for p, _ in both]
        lands = [lax.empty((N_CHIPS - 1,) + p.shape[1:], p.dtype) for p in parts]
        self.sems = _comm_start("scatter_start_" + self.tag, parts + lands, (n, N_CHIPS - 1), _scatter_plan(n))

    def total(self):
        n = self.n
        sems, bufs = self.sems
        bufs = _comm_wait("scatter_wait_" + self.tag, bufs, sems, _scatter_plan(n))
        fulls = [_sum_partials(o, r, self.c_idx) for o, r in zip(self.own, bufs[n:])]
        self.sems = _comm_start("share_start_" + self.tag, fulls, (n,), _share_plan(n))

    def finish(self):
        sems, bufs = self.sems
        return _comm_wait("share_wait_" + self.tag, bufs, sems, _share_plan(self.n))


def _elem_tile(rows, cols):
    return _pick(rows, max(8, (1 << 19) // cols // 8 * 8))


def _chip_partial(grad, recv, c_idx, p_idx):
    Q, R, C = grad.shape
    h = R // 2
    T = _elem_tile(h, C)
    nt = h // T

    def body(sc_ref, g_ref, r_ref, sb_ref, own_ref):
        q = pl.program_id(1)
        s = g_ref[...] + r_ref[...]
        sb_ref[...] = s.astype(BF16)

        @pl.when(q == sc_ref[1])
        def _():
            own_ref[...] = s

    return _call(
        body, (grad, recv), name="chip_partial", grid=(nt, Q),
        scalars=jnp.stack([c_idx, p_idx]).astype(jnp.int32),
        in_specs=[pl.BlockSpec((None, T, C), lambda t, q, sc: (q, sc[0] * nt + t, 0)),
                  pl.BlockSpec((None, T, C), lambda t, q, sc: (q, t, 0))],
        out_specs=[pl.BlockSpec((None, T, C), lambda t, q, sc: (q, t, 0)),
                   pl.BlockSpec((T, C), lambda t, q, sc: (t, 0))],
        out_shape=[_sds((Q, h, C), BF16), _sds((h, C), F32)], compiler_params=_params(2))


def _sum_partials(own, parts, c_idx):
    h, C = own.shape
    T = _elem_tile(h, C)
    nt = h // T

    def body(c_ref, o_ref, p_ref, t_ref):
        t = o_ref[...]
        for k in range(N_CHIPS - 1):
            t = t + p_ref[k].astype(F32)
        t_ref[...] = t

    return _call(
        body, (own, parts), name="sum_partials", grid=(nt,), scalars=jnp.reshape(c_idx, (1,)).astype(jnp.int32),
        in_specs=[pl.BlockSpec((T, C), lambda i, c: (i, 0)),
                  pl.BlockSpec((N_CHIPS - 1, T, C), lambda i, c: (0, i, 0))],
        out_specs=pl.BlockSpec((T, C), lambda i, c: (c[0] * nt + i, 0)),
        out_shape=_sds((2 * h, C), F32), compiler_params=_params(1))


def _pack_rows(name, parts):
    width = parts[0].shape[1]
    offsets, at = [], 0
    for p in parts:
        offsets.append(at)
        at += p.shape[0]
    total = -(-at // 8) * 8

    def body(*refs):
        out = refs[-1]
        out[...] = jnp.zeros(out.shape, F32)
        for ref, o in zip(refs[:-1], offsets):
            out[o:o + ref.shape[0], :] = ref[...]

    whole = pl.BlockSpec(memory_space=pltpu.VMEM)
    return _call(body, list(parts), name=name, in_specs=[whole] * len(parts), out_specs=whole,
                 out_shape=_sds((total, width), F32))


def _sum_packs(packs):
    n, R, C = packs.shape

    def body(p_ref, o_ref):
        t = p_ref[0]
        for k in range(1, n):
            t = t + p_ref[k]
        o_ref[...] = t

    return _call(
        body, (packs,), name="sum_packs", grid=(1,), in_specs=[pl.BlockSpec((n, R, C), lambda i: (0, 0, 0))],
        out_specs=pl.BlockSpec((R, C), lambda i: (0, 0)), out_shape=_sds((R, C), F32), compiler_params=_params(1))


def _adamw(w, g, m, v):
    R, C = w.shape
    T = _elem_tile(R, C)

    def body(w_ref, g_ref, m_ref, v_ref, d_ref, m2_ref, v2_ref):
        g_ = g_ref[...]
        m2 = ADAM_B1 * m_ref[...] + (1.0 - ADAM_B1) * g_
        v2 = ADAM_B2 * v_ref[...] + (1.0 - ADAM_B2) * (g_ * g_)
        m_hat = m2 / (1.0 - ADAM_B1 ** ADAM_STEP)
        v_hat = v2 / (1.0 - ADAM_B2 ** ADAM_STEP)
        d_ref[...] = -ADAM_LR * (m_hat / (jnp.sqrt(v_hat) + ADAM_EPS) + ADAM_WD * w_ref[...])
        m2_ref[...] = m2
        v2_ref[...] = v2

    blk = pl.BlockSpec((T, C), lambda i: (i, 0))
    return _call(
        body, (w, g, m, v), name="adamw", grid=(R // T,), in_specs=[blk] * 4, out_specs=[blk] * 3,
        out_shape=[_sds((R, C), F32)] * 3, compiler_params=_params(1))


BIG = ("w_in", "w_pool_out", "w_conv_out", "w_o", "w_up", "w_down", "w_pool_grp")
VECTORS = ("g_pre_mix", "pool_scale", "b_dw", "conv_ln_g", "conv_ln_b", "g_post_mix", "g_pre_mlp", "g_post_mlp")
WEIGHTS = ("meta", "g_pre_mix", "w_in", "w_pool_grp", "pool_scale", "w_pool_out", "w_dw", "b_dw", "conv_ln_g",
           "conv_ln_b", "w_conv_out", "w_o", "g_post_mix", "g_pre_mlp", "w_up", "w_down", "g_post_mlp")


def _as_rows(a, width):
    r, cols = a.shape
    return a.reshape(r * (cols // width), width)


def _step(w, m, v, x, tgt):
    S, D = x.shape
    P = D // 2
    xi, yi, ci = _place()
    chip = 2 * xi + yi
    _CHAIN["after"] = None

    C = D // 2
    G = POOL_GROUPS
    GD = P // G
    GS = GD // N_CHIPS
    Q = N_CHIPS
    vecs = {k: w[k] for k in VECTORS}
    shard2d = {k: w[k].reshape(-1, w[k].shape[-1]) for k in BIG}
    grads, delta, new_m, new_v = {}, {}, {}, {}

    def update(names, reduced):
        for k, g in zip(names, reduced):
            grads[k] = g
            delta[k], new_m[k], new_v[k] = _adamw(shard2d[k], g, m[k].reshape(shard2d[k].shape),
                                                  v[k].reshape(shard2d[k].shape))

    groups = (("a", ("w_in", "w_pool_grp"), ("w_dw", "meta")), ("b", ("w_pool_out", "w_conv_out", "w_o"), ()),
              ("c", ("w_up",), ()), ("d", ("w_down",), ()))
    flying = {}
    for tag, halved, whole in groups:
        slabs = [_cast_into_slab(shard2d[k], chip, BF16) for k in halved]
        slabs += [_cast_into_slab(w[k], chip, F32) for k in whole]
        flying[tag] = _comm_start("gather_start_" + tag, slabs, (len(slabs), N_CHIPS - 1), _gather_plan(len(halved)))

    def landed(tag):
        _, halved, whole = next(g for g in groups if g[0] == tag)
        nh = len(halved)
        sems, bufs = flying.pop(tag)
        sems, bufs = _comm_relay("gather_relay_" + tag, bufs, sems, _gather_plan(nh),
                                 (nh, N_CHIPS - 1), _hand_on_plan(nh))
        done = _comm_wait("gather_wait_" + tag, bufs[:nh], sems, _hand_on_plan(nh))
        return dict(zip(halved + whole, done + bufs[nh:]))

    got = landed("a")
    win_g = got["w_in"]
    w_grp = got["w_pool_grp"].reshape(N_CHIPS, G, GS, GD).transpose(1, 0, 2, 3).reshape(G, GD, GD)
    w_dw = got["w_dw"].transpose(1, 0, 2).reshape(CONV_TAPS, P)
    meta = got["meta"].transpose(1, 0, 2).reshape(N_META, D)
    h0 = jnp.concatenate([jnp.zeros((PAD_ROWS, D), F32), meta, x], axis=0)
    u1 = _pre_norm(h0, vecs["g_pre_mix"])
    proj = _mm_act_colw("proj", u1, win_g, 256)
    d, ya_pre = _pool_fwd(proj, w_grp, vecs["pool_scale"])
    s, c = _conv_fwd(proj, C, w_dw, vecs["b_dw"], vecs["conv_ln_g"], vecs["conv_ln_b"])
    got = landed("b")
    wpo_g, wco_g, w_o = got["w_pool_out"], got["w_conv_out"], got["w_o"].reshape(D, D)
    mix, ya, yb = _mix_fwd(ya_pre, s, wpo_g, wco_g, proj, D)
    o = _mm_act_roww("attn_out", mix, w_o, 1088, 1024, 2048)
    h1, u2 = _mid_norm(o, h0, vecs["g_post_mix"], vecs["g_pre_mlp"])
    wup_g = landed("c")["w_up"]
    a_up, fact = _mm_act_colw("mlp_up", u2, wup_g, 512, _up_epilogue, (F32, BF16))
    w_down = landed("d")["w_down"].reshape(-1, D)
    f = _mm_act_roww("mlp_down", fact, w_down, 1088, 1024, 2048)
    dy, df, dg_post_mlp, loss = _loss_head(f, h1, tgt, vecs["g_post_mlp"])

    g_w_down = _mm_wgrad("dw_down", fact, df, 1024, 1024)
    red1 = _Reduction("1", [g_w_down.reshape(N_CHIPS, -1, D)], ci, chip)
    da_up = _mlp_down_bwd(df, w_down, a_up)
    red1.partial()
    g_w_up = _mm_wgrad_colw("dw_up", u2, da_up, Q, 1024, 1024)
    red2 = _Reduction("2", [g_w_up], ci, chip)
    du2 = _mm_grad_colw_t("du2", da_up, wup_g, 1088, 1024)
    red2.partial()
    dh1, do, dg_pre_mlp, dg_post_mix = _mid_norm_bwd(dy, du2, h1, o, vecs["g_pre_mlp"], vecs["g_post_mix"])
    g_w_o = _mm_wgrad("dw_o", mix, do, 1024, 1024)
    red1.total()
    dya, dyb, dga, dgb = _mix_bwd(do, w_o, proj, ya, yb)
    update(("w_down",), red1.finish())
    g_wpo = _mm_wgrad_colw("dw_pool_out", ya_pre, dya, Q, 1024, 512)
    g_wco = _mm_wgrad_colw("dw_conv_out", s, dyb, Q, 1024, 512)
    red3 = _Reduction("3", [g_w_o.reshape(N_CHIPS, D // N_CHIPS, D), g_wpo, g_wco], ci, chip)
    dya_pre = _mm_grad_colw_t("dya_pre", dya, wpo_g, 1088, 1024)
    ds = _mm_grad_colw_t("ds", dyb, wco_g, 1088, 1024)
    red3.partial()
    dz, g_w_grp, dscale = _pool_bwd(dya_pre, d, w_grp, vecs["pool_scale"])
    dc, dln_g, dln_b, db_dw = _conv_ln_bwd(ds, c, vecs["conv_ln_g"], vecs["conv_ln_b"])
    red2.total()
    dv, dgc, g_w_dw = _conv_bwd(dc, proj, C, w_dw)
    update(("w_up",), red2.finish())
    dproj = jnp.concatenate([dz, dv, dgc, dga, dgb], axis=1)
    g_w_in = _mm_wgrad_colw("dw_in", u1, dproj, Q, 512, 1792)
    g_w_grp = g_w_grp.reshape(G, N_CHIPS, GS, GD).transpose(1, 0, 2, 3).reshape(N_CHIPS, G * GS, GD)
    red4 = _Reduction("4", [g_w_in, g_w_grp], ci, chip)
    du1 = _mm_grad_colw_t("du1", dproj, win_g, 1088, 1024)
    red4.partial()
    red3.total()
    grad_x, dmeta, dg_pre_mix = _pre_norm_bwd(dh1, du1, h0, vecs["g_pre_mix"])
    update(("w_o", "w_pool_out", "w_conv_out"), red3.finish())
    g_vec = dict(g_pre_mix=dg_pre_mix, pool_scale=dscale, b_dw=db_dw, conv_ln_g=dln_g, conv_ln_b=dln_b,
                 g_post_mix=dg_post_mix, g_pre_mlp=dg_pre_mlp, g_post_mlp=dg_post_mlp)

    rows = [g_w_dw, _as_rows(dmeta, P)] + [_as_rows(g_vec[k], P) for k in VECTORS]
    rows.append(jnp.broadcast_to(loss[:, :1], (1, P)))
    total = _sum_packs(_gather_packs(_pack_rows("pack_grads", rows)))
    at = 0
    taps_pad = g_w_dw.shape[0]
    g_dw_full = total[at:at + CONV_TAPS]
    at += taps_pad
    g_meta_full = total[at:at + 2 * N_META].reshape(N_META, D)
    at += 2 * N_META
    for k in VECTORS:
        n = w[k].shape[-1] // P
        grads[k] = total[at:at + n].reshape(1, n * P)
        at += n
    loss_total = total[at, 0]
    grads["w_dw"] = lax.dynamic_slice_in_dim(g_dw_full, chip * (P // N_CHIPS), P // N_CHIPS, axis=1)
    grads["meta"] = lax.dynamic_slice_in_dim(g_meta_full, chip * (D // N_CHIPS), D // N_CHIPS, axis=1)

    small_names = [k for k in WEIGHTS if k not in BIG]

    def pack_small(tree):
        parts = []
        for k in small_names:
            a = tree[k].reshape(-1, tree[k].shape[-1])
            flat = a.reshape(-1)
            parts.append(jnp.pad(flat, (0, -flat.shape[0] % P)).reshape(-1, P))
        return _pack_rows("pack_small", parts)

    sd, sm, sv = _adamw(pack_small(w), pack_small(grads), pack_small(m), pack_small(v))
    at = 0
    for k in small_names:
        a = w[k].reshape(-1, w[k].shape[-1])
        n = -(-a.size // P)
        for tree, packed in ((delta, sd), (new_m, sm), (new_v, sv)):
            tree[k] = packed[at:at + n].reshape(-1)[:a.size].reshape(a.shape)
        at += n

    red4.total()
    update(("w_in", "w_pool_grp"), red4.finish())
    return loss_total, grad_x, grads, delta, new_m, new_v


def kernel(x, meta, g_pre_mix, w_in, w_pool_grp, pool_scale, w_pool_out, w_dw, b_dw, conv_ln_g, conv_ln_b, w_conv_out, w_o, g_post_mix, g_pre_mlp, w_up, w_down, g_post_mlp, loss_target, m_meta, m_g_pre_mix, m_w_in, m_w_pool_grp, m_pool_scale, m_w_pool_out, m_w_dw, m_b_dw, m_conv_ln_g, m_conv_ln_b, m_w_conv_out, m_w_o, m_g_post_mix, m_g_pre_mlp, m_w_up, m_w_down, m_g_post_mlp, v_meta, v_g_pre_mix, v_w_in, v_w_pool_grp, v_pool_scale, v_w_pool_out, v_w_dw, v_b_dw, v_conv_ln_g, v_conv_ln_b, v_w_conv_out, v_w_o, v_g_post_mix, v_g_pre_mlp, v_w_up, v_w_down, v_g_post_mlp):
    args = dict(locals())
    shapes = {k: args[k].shape for k in WEIGHTS}
    w = {k: args[k] for k in WEIGHTS}
    m = {k: args["m_" + k] for k in WEIGHTS}
    v = {k: args["v_" + k] for k in WEIGHTS}
    for tree in (w, m, v):
        tree["w_dw"] = tree["w_dw"].reshape(tree["w_dw"].shape[-2:])
    loss, grad_x, grads, delta, new_m, new_v = _step(w, m, v, x[0], loss_target[0])
    out = [loss, grad_x[None]]
    for tree in (grads, delta, new_m, new_v):
        out += [tree[k].reshape(shapes[k]) for k in WEIGHTS]
    return tuple(out)
```

```python
import jax
import jax.numpy as jnp
from jax import lax
from jax.experimental import pallas as pl
from jax.experimental.pallas import tpu as pltpu

F32 = jnp.float32
BF16 = jnp.bfloat16

N_META = 16
PAD_ROWS = 112
TOKEN_ROW0 = PAD_ROWS + N_META
POOL_GROUPS = 4
CONV_TAPS = 31
HALO = 32
CONV_ROWS = 128
LANES = 128
RMS_EPS = 1e-6
LN_EPS = 1e-5
ADAM_LR = 0.001
ADAM_B1 = 0.9
ADAM_B2 = 0.999
ADAM_EPS = 1e-08
ADAM_WD = 0.01
ADAM_STEP = 10
VMEM_LIMIT_MB = 56

MESH = pl.DeviceIdType.MESH
NN = (((1,), (0,)), ((), ()))
NT = (((1,), (1,)), ((), ()))
TN = (((0,), (0,)), ((), ()))


def _pick(n, pref):
    if n <= pref:
        return n
    if n % pref == 0:
        return pref
    for step in (LANES, 8, 1):
        t = (pref // step) * step
        while t >= step:
            if n % t == 0:
                return t
            t -= step
    return n


def _params(n_axes, vmem_mb=VMEM_LIMIT_MB):
    return pltpu.CompilerParams(dimension_semantics=("arbitrary",) * n_axes,
                                vmem_limit_bytes=vmem_mb << 20)


def _sigmoid(x):
    return jax.nn.sigmoid(x)


_CHAIN = {"after": None}


def _call(body, args, *, in_specs, out_specs, out_shape, grid=(), scalars=None, mark=0, follows=True, **kw):
    after = _CHAIN["after"] if follows else None
    n = len(args)
    lead = 0 if scalars is None else 1
    specs = list(in_specs)
    operands = list(args)
    fn = body
    if after is not None:
        def fn(*refs):
            body(*refs[:lead + n], *refs[lead + n + 1:])
        specs.append(pl.BlockSpec(memory_space=pl.ANY))
        operands.append(after)
    if scalars is None:
        if grid:
            kw["grid"] = grid
        res = pl.pallas_call(fn, in_specs=specs, out_specs=out_specs, out_shape=out_shape, **kw)(*operands)
    else:
        grid_spec = pltpu.PrefetchScalarGridSpec(num_scalar_prefetch=1, grid=grid, in_specs=specs, out_specs=out_specs)
        res = pl.pallas_call(fn, grid_spec=grid_spec, out_shape=out_shape, **kw)(scalars, *operands)
    outs = res if isinstance(res, (list, tuple)) else [res]
    _CHAIN["after"] = outs[mark]
    return res


def _store(val, extras, outs):
    outs[0][...] = val.astype(outs[0].dtype)


def _mm(name, grid, arrays, in_specs, out_shapes, out_specs, dims, nk, epilogue=_store, acc_shape=None):
    n_in, n_out = len(arrays), len(out_shapes)

    def body(*refs):
        extras = refs[2:n_in]
        outs = refs[n_in:n_in + n_out]
        part = lax.dot_general(refs[0][...], refs[1][...], dims, preferred_element_type=F32)
        if nk == 1:
            epilogue(part, extras, outs)
        else:
            acc = refs[n_in + n_out]
            k = pl.program_id(len(grid) - 1)

            @pl.when(k == 0)
            def _():
                acc[...] = part

            @pl.when(k > 0)
            def _():
                acc[...] += part

            @pl.when(k == nk - 1)
            def _():
                epilogue(acc[...], extras, outs)

    scratch = [pltpu.VMEM(acc_shape, F32)] if nk > 1 else []
    single = n_out == 1
    return _call(
        body, arrays, name=name, grid=grid, in_specs=in_specs,
        out_specs=out_specs[0] if single else out_specs,
        out_shape=out_shapes[0] if single else out_shapes,
        scratch_shapes=scratch, compiler_params=_params(len(grid)))


def _sds(shape, dtype):
    return jax.ShapeDtypeStruct(shape, dtype)


def _rms_scale(h):
    return lax.rsqrt(jnp.mean(h * h, axis=-1, keepdims=True) + RMS_EPS)


def _rms_bwd(du, h, g):
    r = _rms_scale(h)
    y = h * r
    dy = du * g
    dh = r * (dy - y * jnp.mean(dy * y, axis=-1, keepdims=True))
    return dh, jnp.sum(du * y, axis=0, keepdims=True)


def _row_tile(L):
    return _pick(L, 272)


def _pre_norm(h0, g):
    L, D = h0.shape
    T = _row_tile(L)

    def body(h_ref, g_ref, u_ref):
        h = h_ref[...]
        u_ref[...] = (h * _rms_scale(h) * g_ref[...]).astype(BF16)

    return _call(
        body, (h0, g), name="pre_norm", grid=(L // T,),
        in_specs=[pl.BlockSpec((T, D), lambda i: (i, 0)), pl.BlockSpec((1, D), lambda i: (0, 0))],
        out_specs=pl.BlockSpec((T, D), lambda i: (i, 0)),
        out_shape=_sds((L, D), BF16), compiler_params=_params(1))


def _mid_norm(o, h0, g_post, g_pre):
    L, D = h0.shape
    T = _row_tile(L)

    def body(o_ref, h_ref, gp_ref, gm_ref, h1_ref, u2_ref):
        o_ = o_ref[...]
        h1 = h_ref[...] + o_ * _rms_scale(o_) * gp_ref[...]
        h1_ref[...] = h1
        u2_ref[...] = (h1 * _rms_scale(h1) * gm_ref[...]).astype(BF16)

    row = pl.BlockSpec((T, D), lambda i: (i, 0))
    vec = pl.BlockSpec((1, D), lambda i: (0, 0))
    return _call(
        body, (o, h0, g_post, g_pre), name="mid_norm", grid=(L // T,),
        in_specs=[row, row, vec, vec], out_specs=[row, row],
        out_shape=[_sds((L, D), F32), _sds((L, D), BF16)], compiler_params=_params(1))


def _loss_head(f, h1, tgt, g_post):
    L, D = h1.shape
    T = TOKEN_ROW0
    n = L // T

    def body(f_ref, h_ref, t_ref, g_ref, dy_ref, df_ref, dg_ref, loss_ref):
        i = pl.program_id(0)
        f_ = f_ref[...]
        g = g_ref[...]
        y = h_ref[...] + f_ * _rms_scale(f_) * g
        live = (i > 0).astype(F32)
        diff = (y - t_ref[...]) * live
        part = 0.5 * jnp.sum(jnp.mean(diff * diff, axis=-1, keepdims=True), axis=0, keepdims=True)
        dy = diff * (1.0 / D)
        dy_ref[...] = dy
        df, dg = _rms_bwd(dy, f_, g)
        df_ref[...] = df.astype(BF16)

        @pl.when(i == 0)
        def _():
            dg_ref[...] = dg
            loss_ref[...] = jnp.broadcast_to(part, loss_ref.shape)

        @pl.when(i > 0)
        def _():
            dg_ref[...] += dg
            loss_ref[...] += jnp.broadcast_to(part, loss_ref.shape)

    row = pl.BlockSpec((T, D), lambda i: (i, 0))
    vec = pl.BlockSpec((1, D), lambda i: (0, 0))
    return _call(
        body, (f, h1, tgt, g_post), name="loss_head", grid=(n,),
        in_specs=[row, row, pl.BlockSpec((T, D), lambda i: (jnp.maximum(i - 1, 0), 0)), vec],
        out_specs=[row, row, vec, pl.BlockSpec((1, LANES), lambda i: (0, 0))],
        out_shape=[_sds((L, D), F32), _sds((L, D), BF16), _sds((1, D), F32), _sds((1, LANES), F32)],
        compiler_params=_params(1))


def _mid_norm_bwd(dy, du2, h1, o, g_pre, g_post):
    L, D = h1.shape
    T = _row_tile(L)

    def body(dy_ref, du_ref, h_ref, o_ref, gm_ref, gp_ref, dh1_ref, do_ref, dgm_ref, dgp_ref):
        i = pl.program_id(0)
        dh, dgm = _rms_bwd(du_ref[...], h_ref[...], gm_ref[...])
        dh1 = dy_ref[...] + dh
        dh1_ref[...] = dh1
        do, dgp = _rms_bwd(dh1, o_ref[...], gp_ref[...])
        do_ref[...] = do.astype(BF16)

        @pl.when(i == 0)
        def _():
            dgm_ref[...] = dgm
            dgp_ref[...] = dgp

        @pl.when(i > 0)
        def _():
            dgm_ref[...] += dgm
            dgp_ref[...] += dgp

    row = pl.BlockSpec((T, D), lambda i: (i, 0))
    vec = pl.BlockSpec((1, D), lambda i: (0, 0))
    return _call(
        body, (dy, du2, h1, o, g_pre, g_post), name="mid_norm_bwd", grid=(L // T,),
        in_specs=[row, row, row, row, vec, vec], out_specs=[row, row, vec, vec],
        out_shape=[_sds((L, D), F32), _sds((L, D), BF16), _sds((1, D), F32), _sds((1, D), F32)],
        compiler_params=_params(1))


def _pre_norm_bwd(dh1, du1, h0, g):
    L, D = h0.shape
    T = TOKEN_ROW0
    n = L // T

    def body(dh_ref, du_ref, h_ref, g_ref, gx_ref, dmeta_ref, dg_ref):
        i = pl.program_id(0)
        dh, dg = _rms_bwd(du_ref[...], h_ref[...], g_ref[...])
        dh0 = dh_ref[...] + dh
        gx_ref[...] = dh0

        @pl.when(i == 0)
        def _():
            dmeta_ref[...] = dh0[PAD_ROWS:, :]
            dg_ref[...] = dg

        @pl.when(i > 0)
        def _():
            dg_ref[...] += dg

    row = pl.BlockSpec((T, D), lambda i: (i, 0))
    vec = pl.BlockSpec((1, D), lambda i: (0, 0))
    return _call(
        body, (dh1, du1, h0, g), name="pre_norm_bwd", grid=(n,),
        in_specs=[row, row, row, vec],
        out_specs=[pl.BlockSpec((T, D), lambda i: (jnp.maximum(i - 1, 0), 0)),
                   pl.BlockSpec((N_META, D), lambda i: (0, 0)), vec],
        out_shape=[_sds((L - T, D), F32), _sds((N_META, D), F32), _sds((1, D), F32)],
        compiler_params=_params(1))


def _window_sum(z, g, shift_sign, L):
    s = z
    for j in range(POOL_GROUPS):
        k = 1 << j
        nxt = s + pltpu.roll(s, k if shift_sign > 0 else L - k, 0)
        s = jnp.where(j <= g, nxt, s)
    return s


def _inv_count(g, L):
    t = lax.broadcasted_iota(jnp.int32, (L, 1), 0)
    w = jnp.left_shift(2, g)
    cnt = jnp.clip(t - (PAD_ROWS - 1), 1, w)
    return 1.0 / cnt.astype(F32)


def _pool_fwd(proj, w_grp, scale):
    L = proj.shape[0]
    G, GD, _ = w_grp.shape
    P = G * GD

    def body(z_ref, w_ref, sc_ref, d_ref, ya_ref):
        g = pl.program_id(0)
        z = z_ref[...]
        d = (_window_sum(z, g, +1, L) * _inv_count(g, L) - z).astype(BF16)
        d_ref[...] = d
        y = jnp.dot(d, w_ref[...], preferred_element_type=F32)
        ya_ref[...] = (y * sc_ref[...]).astype(BF16)

    col = pl.BlockSpec((L, GD), lambda g: (0, g))
    return _call(
        body, (proj, w_grp, scale), name="pool_fwd", grid=(G,),
        in_specs=[col, pl.BlockSpec((None, GD, GD), lambda g: (g, 0, 0)), pl.BlockSpec((1, GD), lambda g: (0, g))],
        out_specs=[col, col], out_shape=[_sds((L, P), BF16), _sds((L, P), BF16)],
        compiler_params=_params(1))


def _pool_bwd(dya, d, w_grp, scale):
    L, P = dya.shape
    G, GD, _ = w_grp.shape

    def body(dya_ref, d_ref, w_ref, sc_ref, dz_ref, dw_ref, dsc_ref):
        g = pl.program_id(0)
        dya_ = dya_ref[...]
        d_ = d_ref[...]
        w = w_ref[...]
        y = jnp.dot(d_, w, preferred_element_type=F32)
        dsc_ref[...] = jnp.sum(dya_ * y, axis=0, keepdims=True)
        dy = (dya_ * sc_ref[...]).astype(BF16)
        dw_ref[...] = lax.dot_general(d_, dy, TN, preferred_element_type=F32)
        dd = lax.dot_general(dy, w, NT, preferred_element_type=F32)
        dz = _window_sum(dd * _inv_count(g, L), g, -1, L) - dd
        dz_ref[...] = dz.astype(BF16)

    col = pl.BlockSpec((L, GD), lambda g: (0, g))
    wspec = pl.BlockSpec((None, GD, GD), lambda g: (g, 0, 0))
    vec = pl.BlockSpec((1, GD), lambda g: (0, g))
    return _call(
        body, (dya, d, w_grp, scale), name="pool_bwd", grid=(G,),
        in_specs=[col, col, wspec, vec], out_specs=[col, wspec, vec],
        out_shape=[_sds((L, P), BF16), _sds((G, GD, GD), F32), _sds((1, P), F32)],
        compiler_params=_params(1))


def _fill_rotations(rot_ref, ext):
    n = ext.shape[0]
    rot_ref[0] = ext
    for r in range(1, 8):
        rot_ref[r] = pltpu.roll(ext, n - r, 0)


def _lane_chunks(C):
    step = LANES if C % LANES == 0 else C
    return [(c0, step) for c0 in range(0, C, step)]


def _conv_specs(L, C, col_v, col_g):
    T = CONV_ROWS
    per = T // HALO
    cur_v = pl.BlockSpec((T, C), lambda i: (i, col_v))
    cur_g = pl.BlockSpec((T, C), lambda i: (i, col_g))
    prev_v = pl.BlockSpec((HALO, C), lambda i: (jnp.maximum(i * per - 1, 0), col_v))
    prev_g = pl.BlockSpec((HALO, C), lambda i: (jnp.maximum(i * per - 1, 0), col_g))
    return cur_v, cur_g, prev_v, prev_g


def _glu_ext(vc, gc, vh, gh, i):
    a_cur = vc[...] * _sigmoid(gc[...])
    a_prev = vh[...] * _sigmoid(gh[...]) * (i > 0).astype(F32)
    return jnp.concatenate([a_prev, a_cur], axis=0)


def _conv_fwd(proj, C, w_dw, b_dw, ln_g, ln_b):
    L = proj.shape[0]
    T = CONV_ROWS
    P = C

    def body(vc, gc, vh, gh, w_ref, b_ref, lg_ref, lb_ref, s_ref, c_ref, rot):
        i = pl.program_id(0)
        _fill_rotations(rot, _glu_ext(vc, gc, vh, gh, i))
        for c0, cw in _lane_chunks(C):
            acc = jnp.zeros((T, cw), F32)
            for k in range(CONV_TAPS):
                q, r = divmod(HALO - (CONV_TAPS - 1) + k, 8)
                acc = acc + w_ref[k:k + 1, c0:c0 + cw] * rot[r, 8 * q:8 * q + T, c0:c0 + cw]
            c_ref[:, c0:c0 + cw] = acc + b_ref[:, c0:c0 + cw]
        c = c_ref[...]
        mu = jnp.mean(c, axis=-1, keepdims=True)
        cen = c - mu
        var = jnp.mean(cen * cen, axis=-1, keepdims=True)
        ln = cen * lax.rsqrt(var + LN_EPS) * lg_ref[...] + lb_ref[...]
        s_ref[...] = (ln * _sigmoid(ln)).astype(BF16)

    cur_v, cur_g, prev_v, prev_g = _conv_specs(L, C, P // C, P // C + 1)
    row = pl.BlockSpec((T, C), lambda i: (i, 0))
    vec = pl.BlockSpec((1, C), lambda i: (0, 0))
    return _call(
        body, (proj, proj, proj, proj, w_dw, b_dw, ln_g, ln_b), name="conv_fwd", grid=(L // T,),
        in_specs=[cur_v, cur_g, prev_v, prev_g, pl.BlockSpec((CONV_TAPS, C), lambda i: (0, 0)), vec, vec, vec],
        out_specs=[row, row], out_shape=[_sds((L, C), BF16), _sds((L, C), F32)],
        scratch_shapes=[pltpu.VMEM((8, T + HALO, C), F32)], compiler_params=_params(1))


def _conv_ln_bwd(ds, c, ln_g, ln_b):
    L, C = c.shape
    T = _row_tile(L)

    def body(ds_ref, c_ref, lg_ref, lb_ref, dc_ref, dlg_ref, dlb_ref, db_ref):
        i = pl.program_id(0)
        c_ = c_ref[...]
        g = lg_ref[...]
        mu = jnp.mean(c_, axis=-1, keepdims=True)
        cen = c_ - mu
        rstd = lax.rsqrt(jnp.mean(cen * cen, axis=-1, keepdims=True) + LN_EPS)
        xhat = cen * rstd
        ln = xhat * g + lb_ref[...]
        sg = _sigmoid(ln)
        dln = ds_ref[...] * (sg * (1.0 + ln * (1.0 - sg)))
        dxh = dln * g
        dc = rstd * (dxh - jnp.mean(dxh, axis=-1, keepdims=True)
                     - xhat * jnp.mean(dxh * xhat, axis=-1, keepdims=True))
        dc_ref[...] = dc
        dlg = jnp.sum(dln * xhat, axis=0, keepdims=True)
        dlb = jnp.sum(dln, axis=0, keepdims=True)
        db = jnp.sum(dc, axis=0, keepdims=True)

        @pl.when(i == 0)
        def _():
            dlg_ref[...] = dlg
            dlb_ref[...] = dlb
            db_ref[...] = db

        @pl.when(i > 0)
        def _():
            dlg_ref[...] += dlg
            dlb_ref[...] += dlb
            db_ref[...] += db

    row = pl.BlockSpec((T, C), lambda i: (i, 0))
    vec = pl.BlockSpec((1, C), lambda i: (0, 0))
    return _call(
        body, (ds, c, ln_g, ln_b), name="conv_ln_bwd", grid=(L // T,),
        in_specs=[row, row, vec, vec], out_specs=[row, vec, vec, vec],
        out_shape=[_sds((L, C), F32), _sds((1, C), F32), _sds((1, C), F32), _sds((1, C), F32)],
        compiler_params=_params(1))


def _conv_bwd(dc, proj, C, w_dw):
    L = proj.shape[0]
    T = CONV_ROWS
    per = T // HALO
    n = L // T
    P = C
    taps_pad = 32

    def body(dcc, dcn, vc, gc, w_ref, dv_ref, dg_ref, dw_ref, rot_d, dw_acc):
        i = pl.program_id(0)
        dc_next = dcn[...] * (i < n - 1).astype(F32)
        _fill_rotations(rot_d, jnp.concatenate([dcc[...], dc_next], axis=0))

        @pl.when(i == 0)
        def _():
            dw_acc[...] = jnp.zeros(dw_acc.shape, F32)

        for c0, cw in _lane_chunks(C):
            v = vc[:, c0:c0 + cw]
            sg = _sigmoid(gc[:, c0:c0 + cw])
            a = v * sg
            da = jnp.zeros((T, cw), F32)
            for k in range(CONV_TAPS):
                q, r = divmod(CONV_TAPS - 1 - k, 8)
                slab = rot_d[r, 8 * q:8 * q + T, c0:c0 + cw]
                da = da + w_ref[k:k + 1, c0:c0 + cw] * slab
                dw_acc[k, :, c0:c0 + cw] += jnp.sum((a * slab).reshape(T // 8, 8, cw), axis=0)
            dv_ref[:, c0:c0 + cw] = (da * sg).astype(BF16)
            dg_ref[:, c0:c0 + cw] = (da * v * sg * (1.0 - sg)).astype(BF16)

        @pl.when(i == n - 1)
        def _():
            dw_ref[...] = jnp.sum(dw_acc[...], axis=1)

    cur_v, cur_g, _, _ = _conv_specs(L, C, P // C, P // C + 1)
    row = pl.BlockSpec((T, C), lambda i: (i, 0))
    nxt = pl.BlockSpec((HALO, C), lambda i: (jnp.minimum((i + 1) * per, L // HALO - 1), 0))
    wspec = pl.BlockSpec((CONV_TAPS, C), lambda i: (0, 0))
    return _call(
        body, (dc, dc, proj, proj, w_dw), name="conv_bwd", grid=(n,),
        in_specs=[row, nxt, cur_v, cur_g, wspec],
        out_specs=[row, row, pl.BlockSpec((taps_pad, C), lambda i: (0, 0))],
        out_shape=[_sds((L, C), BF16), _sds((L, C), BF16), _sds((taps_pad, C), F32)],
        scratch_shapes=[pltpu.VMEM((8, T + HALO, C), F32), pltpu.VMEM((taps_pad, 8, C), F32)],
        compiler_params=_params(1))


def _mix_fwd(ya_pre, s, wpo, wco, proj, D):
    L, P = ya_pre.shape
    Q, _, DS = wpo.shape
    bm = _pick(L, 1088)
    gate0 = (proj.shape[1] - 2 * D) // DS
    per = D // DS

    def body(a1, a2, b1, b2, ga, gb, m_ref, ya_ref, yb_ref):
        ya = jnp.dot(a1[...], b1[...], preferred_element_type=F32)
        yb = jnp.dot(a2[...], b2[...], preferred_element_type=F32)
        ya_ref[...] = ya
        yb_ref[...] = yb
        m_ref[...] = (_sigmoid(ga[...]) * ya + _sigmoid(gb[...]) * yb).astype(BF16)

    act = pl.BlockSpec((bm, P), lambda i, q: (i, 0))
    wsp = pl.BlockSpec((None, P, DS), lambda i, q: (q, 0, 0))
    out = pl.BlockSpec((bm, DS), lambda i, q: (i, q))
    return _call(
        body, (ya_pre, s, wpo, wco, proj, proj), name="mix_fwd", grid=(L // bm, Q),
        in_specs=[act, act, wsp, wsp,
                  pl.BlockSpec((bm, DS), lambda i, q: (i, gate0 + q)),
                  pl.BlockSpec((bm, DS), lambda i, q: (i, gate0 + per + q))],
        out_specs=[out, out, out],
        out_shape=[_sds((L, D), BF16), _sds((L, D), F32), _sds((L, D), F32)],
        compiler_params=_params(2))


def _mix_bwd(do, w_o, proj, ya, yb):
    L, D = do.shape
    bm = _pick(L, 544)
    bn = _pick(D // N_CHIPS, 512)
    gate0 = (proj.shape[1] - 2 * D) // bn
    per = D // bn

    def epilogue(dm, extras, outs):
        ga, gb, ya_ref, yb_ref = extras
        sa = _sigmoid(ga[...])
        sb = _sigmoid(gb[...])
        outs[0][...] = (dm * sa).astype(BF16)
        outs[1][...] = (dm * sb).astype(BF16)
        outs[2][...] = (dm * ya_ref[...] * sa * (1.0 - sa)).astype(BF16)
        outs[3][...] = (dm * yb_ref[...] * sb * (1.0 - sb)).astype(BF16)

    blk = pl.BlockSpec((bm, bn), lambda i, j: (i, j))
    return _mm(
        "mix_bwd", (L // bm, D // bn), [do, w_o, proj, proj, ya, yb],
        [pl.BlockSpec((bm, D), lambda i, j: (i, 0)), pl.BlockSpec((bn, D), lambda i, j: (j, 0)),
         pl.BlockSpec((bm, bn), lambda i, j: (i, gate0 + j)),
         pl.BlockSpec((bm, bn), lambda i, j: (i, gate0 + per + j)), blk, blk],
        [_sds((L, D), BF16)] * 4, [blk] * 4, NT, 1, epilogue)


def _mm_act_colw(name, a, wg, bn_pref, epilogue=_store, out_dtypes=(F32,)):
    L, K = a.shape
    Q, _, n = wg.shape
    bn = _pick(n, bn_pref)
    nj = n // bn
    out = pl.BlockSpec((L, bn), lambda q, j: (0, q * nj + j))
    return _mm(name, (Q, nj), [a, wg],
               [pl.BlockSpec((L, K), lambda q, j: (0, 0)), pl.BlockSpec((None, K, bn), lambda q, j: (q, 0, j))],
               [_sds((L, Q * n), dt) for dt in out_dtypes], [out] * len(out_dtypes), NN, 1, epilogue)


def _mm_grad_colw_t(name, g, wg, bm_pref, bn_pref):
    L = g.shape[0]
    Q, K, n = wg.shape
    bm = _pick(L, bm_pref)
    bn = _pick(K, bn_pref)
    return _mm(name, (L // bm, K // bn, Q), [g, wg],
               [pl.BlockSpec((bm, n), lambda i, j, k: (i, k)), pl.BlockSpec((None, bn, n), lambda i, j, k: (k, j, 0))],
               [_sds((L, K), F32)], [pl.BlockSpec((bm, bn), lambda i, j, k: (i, j))], NT, Q,
               acc_shape=(bm, bn))


def _mm_wgrad_colw(name, a, g, Q, bm_pref, bn_pref, rows=None):
    L, K = a.shape
    n = g.shape[1] // Q
    first, count = rows or (0, K)
    bm = _pick(count, bm_pref)
    bn = _pick(n, bn_pref)
    nj = n // bn
    i0 = first // bm
    return _mm(name, (Q, count // bm, nj), [a, g],
               [pl.BlockSpec((L, bm), lambda q, i, j: (0, i0 + i)),
                pl.BlockSpec((L, bn), lambda q, i, j: (0, q * nj + j))],
               [_sds((Q, count, n), F32)], [pl.BlockSpec((None, bm, bn), lambda q, i, j: (q, i, j))], TN, 1)


def _mm_wgrad(name, a, g, bm_pref, bn_pref):
    L, K = a.shape
    N = g.shape[1]
    bm = _pick(K, bm_pref)
    bn = _pick(N, bn_pref)
    return _mm(name, (K // bm, N // bn), [a, g],
               [pl.BlockSpec((L, bm), lambda i, j: (0, i)), pl.BlockSpec((L, bn), lambda i, j: (0, j))],
               [_sds((K, N), F32)], [pl.BlockSpec((bm, bn), lambda i, j: (i, j))], TN, 1)


def _mm_act_roww(name, a, w, bm_pref, bn_pref, bk_pref):
    L, K = a.shape
    N = w.shape[1]
    bm, bn, bk = _pick(L, bm_pref), _pick(N, bn_pref), _pick(K, bk_pref)
    nk = K // bk
    return _mm(name, (L // bm, N // bn, nk), [a, w],
               [pl.BlockSpec((bm, bk), lambda i, j, k: (i, k)), pl.BlockSpec((bk, bn), lambda i, j, k: (k, j))],
               [_sds((L, N), F32)], [pl.BlockSpec((bm, bn), lambda i, j, k: (i, j))], NN, nk,
               acc_shape=(bm, bn))


def _up_epilogue(val, extras, outs):
    outs[0][...] = val
    r = jnp.maximum(val, 0.0)
    outs[1][...] = (r * r).astype(BF16)


def _mlp_down_bwd(df, w_down, a_up):
    L, D = df.shape
    F = w_down.shape[0]
    bm = _pick(L, 1088)
    bn = _pick(F, 1024)

    def epilogue(val, extras, outs):
        outs[0][...] = (val * (2.0 * jnp.maximum(extras[0][...], 0.0))).astype(BF16)

    blk = pl.BlockSpec((bm, bn), lambda i, j: (i, j))
    return _mm("mlp_down_bwd", (L // bm, F // bn), [df, w_down, a_up],
               [pl.BlockSpec((bm, D), lambda i, j: (i, 0)), pl.BlockSpec((bn, D), lambda i, j: (j, 0)), blk],
               [_sds((L, F), BF16)], [blk], NT, 1, epilogue)


ANY = pl.BlockSpec(memory_space=pl.ANY)
HBM = pl.BlockSpec(memory_space=pltpu.HBM)
SEM = pl.BlockSpec(memory_space=pltpu.SEMAPHORE)
EFFECT = pltpu.SideEffectType.DATAFLOW_SIDE_EFFECTING
N_CHIPS = 4


def _place():
    x, y, c = lax.axis_index("x"), lax.axis_index("y"), lax.axis_index("c")
    return x, y, c


def _chip_at(x, y, k):
    px = 1 - x if k & 2 else x
    py = 1 - y if k & 1 else y
    return px, py


def _cast_into_slab(w2d, chip, dtype):
    R, C = w2d.shape
    T = _elem_tile(R, C)

    def body(p_ref, w_ref, o_ref):
        o_ref[...] = w_ref[...].astype(dtype)

    return _call(
        body, (w2d,), name="cast_into_slab", grid=(R // T,), scalars=jnp.reshape(chip, (1,)).astype(jnp.int32),
        in_specs=[pl.BlockSpec((T, C), lambda i, p: (i, 0))],
        out_specs=pl.BlockSpec((None, T, C), lambda i, p: (p[0], i, 0)),
        out_shape=_sds((N_CHIPS, R, C), dtype), compiler_params=_params(1))


TOKEN = jax.ShapeDtypeStruct((8, LANES), F32)


class _Sems:
    def __init__(self, items, shape):
        self.items, self.shape = list(items), tuple(shape)

    def pair(self, idx):
        flat = 0
        for i, n in zip(idx, self.shape):
            flat = flat * n + i
        half = len(self.items) // 2
        return self.items[flat], self.items[half + flat]


def _sem_count(shape):
    n = 1
    for s in shape:
        n *= s
    return n


def _remote(src, dst, sems, idx, device):
    send, recv = sems.pair(idx)
    return pltpu.make_async_remote_copy(src_ref=src, dst_ref=dst, send_sem=send, recv_sem=recv,
                                        device_id=device, device_id_type=MESH)


def _thru(arrays):
    return ([pltpu.with_memory_space_constraint(a, pltpu.HBM) for a in arrays],
            [pltpu.HBM(a.shape, a.dtype) for a in arrays])


def _comm_start(name, arrays, sem_shape, plan):
    na, ns = len(arrays), 2 * _sem_count(sem_shape)

    def body(*refs):
        sems, token = _Sems(refs[na:na + ns], sem_shape), refs[-1]
        for src, dst, idx, device in plan(refs[:na])[0]:
            _remote(src, dst, sems, idx, device).start()
        token[...] = jnp.zeros(token.shape, F32)

    ins, outs = _thru(arrays)
    res = _call(
        body, ins, name=name, in_specs=[HBM] * na, mark=-1, follows=False,
        out_specs=[SEM] * ns + [HBM] * na + [pl.BlockSpec(memory_space=pltpu.VMEM)],
        out_shape=[pltpu.SemaphoreType.DMA(())] * ns + outs + [TOKEN],
        input_output_aliases={a: ns + a for a in range(na)},
        compiler_params=pltpu.CompilerParams(has_side_effects=EFFECT))
    return _Sems(res[:ns], sem_shape), list(res[ns:ns + na])


def _comm_wait(name, arrays, sems, plan):
    na, ns = len(arrays), len(sems.items)

    def body(*refs):
        mine_sems = _Sems(refs[na:na + ns], sems.shape)
        x, y, c = _place()
        _, mine, arrivals = plan(refs[:na])
        for dst, idx in arrivals:
            _remote(dst, dst, mine_sems, idx, (x, y, c)).wait_recv()
        for src, idx in mine:
            _remote(src, src, mine_sems, idx, (x, y, c)).wait_send()
        refs[-1][...] = jnp.zeros(refs[-1].shape, F32)

    ins, outs = _thru(arrays)
    res = _call(
        body, ins + sems.items, name=name, in_specs=[HBM] * na + [SEM] * ns, mark=-1,
        out_specs=[HBM] * na + [pl.BlockSpec(memory_space=pltpu.VMEM)], out_shape=outs + [TOKEN],
        input_output_aliases={a: a for a in range(na)},
        compiler_params=pltpu.CompilerParams(has_side_effects=EFFECT))
    return list(res[:na])


def _comm_relay(name, arrays, sems, plan, sem_shape, next_plan):
    na, ns_in, ns_out = len(arrays), len(sems.items), 2 * _sem_count(sem_shape)

    def body(*refs):
        bufs = refs[:na]
        sems_in = _Sems(refs[na:na + ns_in], sems.shape)
        sems_out = _Sems(refs[na + ns_in:na + ns_in + ns_out], sem_shape)
        x, y, c = _place()
        _, mine, arrivals = plan(bufs)
        onward = next_plan(bufs)[0]
        for i, (dst, idx) in enumerate(arrivals):
            _remote(dst, dst, sems_in, idx, (x, y, c)).wait_recv()
            if i < len(onward):
                src, to, idx2, device = onward[i]
                _remote(src, to, sems_out, idx2, device).start()
        for src, idx in mine:
            _remote(src, src, sems_in, idx, (x, y, c)).wait_send()
        refs[-1][...] = jnp.zeros(refs[-1].shape, F32)

    ins, outs = _thru(arrays)
    res = _call(
        body, ins + sems.items, name=name, in_specs=[HBM] * na + [SEM] * ns_in, mark=-1,
        out_specs=[SEM] * ns_out + [HBM] * na + [pl.BlockSpec(memory_space=pltpu.VMEM)],
        out_shape=[pltpu.SemaphoreType.DMA(())] * ns_out + outs + [TOKEN],
        input_output_aliases={a: ns_out + a for a in range(na)},
        compiler_params=pltpu.CompilerParams(has_side_effects=EFFECT))
    return _Sems(res[:ns_out], sem_shape), list(res[ns_out:ns_out + na])


def _half(ref, q, which):
    h = ref.shape[1] // 2
    return ref.at[q, pl.ds(which * h, h)]


def _gather_plan(n_halved):
    def plan(refs):
        x, y, c = _place()
        p = 2 * x + y
        starts, mine, arrivals = [], [], []
        for n, ref in enumerate(refs):
            for k in range(1, N_CHIPS):
                px, py = _chip_at(x, y, k)
                q = 2 * px + py
                out = _half(ref, p, c) if n < n_halved else ref.at[p]
                inc = _half(ref, q, c) if n < n_halved else ref.at[q]
                starts.append((out, out, (n, k - 1), (px, py, c)))
                mine.append((out, (n, k - 1)))
                arrivals.append((inc, (n, k - 1)))
        return starts, mine, arrivals
    return plan


def _hand_on_plan(n_halved):
    def plan(refs):
        x, y, c = _place()
        starts, mine, arrivals = [], [], []
        for n in range(n_halved):
            for k in range(1, N_CHIPS):
                px, py = _chip_at(x, y, k)
                q = 2 * px + py
                landed = _half(refs[n], q, c)
                starts.append((landed, landed, (n, k - 1), (x, y, 1 - c)))
                mine.append((landed, (n, k - 1)))
                arrivals.append((_half(refs[n], q, 1 - c), (n, k - 1)))
        return starts, mine, arrivals
    return plan


def _swap_plan(n):
    def plan(refs):
        x, y, c = _place()
        starts, mine, arrivals = [], [], []
        for a in range(n):
            h = refs[a].shape[1] // 2
            src = refs[a].at[:, pl.ds((1 - c) * h, h)]
            starts.append((src, refs[n + a], (a,), (x, y, 1 - c)))
            mine.append((src, (a,)))
            arrivals.append((refs[n + a], (a,)))
        return starts, mine, arrivals
    return plan


def _scatter_plan(n):
    def plan(refs):
        x, y, c = _place()
        starts, mine, arrivals = [], [], []
        for a in range(n):
            for k in range(1, N_CHIPS):
                px, py = _chip_at(x, y, k)
                src = refs[a].at[2 * px + py]
                starts.append((src, refs[n + a].at[k - 1], (a, k - 1), (px, py, c)))
                mine.append((src, (a, k - 1)))
                arrivals.append((refs[n + a].at[k - 1], (a, k - 1)))
        return starts, mine, arrivals
    return plan


def _share_plan(n):
    def plan(refs):
        x, y, c = _place()
        starts, mine, arrivals = [], [], []
        for a in range(n):
            h = refs[a].shape[0] // 2
            own = refs[a].at[pl.ds(c * h, h)]
            starts.append((own, own, (a,), (x, y, 1 - c)))
            mine.append((own, (a,)))
            arrivals.append((refs[a].at[pl.ds((1 - c) * h, h)], (a,)))
        return starts, mine, arrivals
    return plan


def _gather_packs(pack):
    n_dev = 8

    def body(in_ref, out_ref, send, recv, lsem):
        x, y, c = _place()
        me = 4 * x + 2 * y + c
        sends = []
        for r in range(1, n_dev):
            peer = (1 - x if r & 4 else x, 1 - y if r & 2 else y, 1 - c if r & 1 else c)
            cp = pltpu.make_async_remote_copy(
                src_ref=in_ref, dst_ref=out_ref.at[me], send_sem=send.at[r - 1], recv_sem=recv.at[r - 1],
                device_id=peer, device_id_type=MESH)
            cp.start()
            sends.append(cp)
        mine = pltpu.make_async_copy(in_ref, out_ref.at[me], lsem)
        mine.start()
        for r in range(1, n_dev):
            peer = (1 - x if r & 4 else x, 1 - y if r & 2 else y, 1 - c if r & 1 else c)
            src = 4 * peer[0] + 2 * peer[1] + peer[2]
            pltpu.make_async_remote_copy(
                src_ref=in_ref, dst_ref=out_ref.at[src], send_sem=send.at[r - 1], recv_sem=recv.at[r - 1],
                device_id=peer, device_id_type=MESH).wait_recv()
        for cp in sends:
            cp.wait_send()
        mine.wait()

    return _call(
        body, (pack,), name="gather_packs", in_specs=[ANY], out_specs=ANY,
        out_shape=_sds((n_dev,) + pack.shape, pack.dtype),
        scratch_shapes=[pltpu.SemaphoreType.DMA((n_dev - 1,)), pltpu.SemaphoreType.DMA((n_dev - 1,)),
                        pltpu.SemaphoreType.DMA],
        compiler_params=pltpu.CompilerParams(has_side_effects=True))


class _Reduction:
    def __init__(self, tag, slabs, c_idx, chip):
        self.tag, self.n, self.c_idx, self.chip = tag, len(slabs), c_idx, chip
        lands = [lax.empty((g.shape[0], g.shape[1] // 2, g.shape[2]), g.dtype) for g in slabs]
        self.sems = _comm_start("swap_start_" + tag, list(slabs) + lands, (self.n,), _swap_plan(self.n))

    def partial(self):
        n = self.n
        sems, bufs = self.sems
        bufs = _comm_wait("swap_wait_" + self.tag, bufs, sems, _swap_plan(n))
        both = [_chip_partial(g, r, self.c_idx, self.chip) for g, r in zip(bufs[:n], bufs[n:])]
        self.own = [o for _, o in both]
        parts = [p for p, _ in both]
        lands = [lax.empty((N_CHIPS - 1,) + p.shape[1:], p.dtype) for p in parts]
        self.sems = _comm_start("scatter_start_" + self.tag, parts + lands, (n, N_CHIPS - 1), _scatter_plan(n))

    def total(self):
        n = self.n
        sems, bufs = self.sems
        bufs = _comm_wait("scatter_wait_" + self.tag, bufs, sems, _scatter_plan(n))
        fulls = [_sum_partials(o, r, self.c_idx) for o, r in zip(self.own, bufs[n:])]
        self.sems = _comm_start("share_start_" + self.tag, fulls, (n,), _share_plan(n))

    def finish(self):
        sems, bufs = self.sems
        return _comm_wait("share_wait_" + self.tag, bufs, sems, _share_plan(self.n))


def _elem_tile(rows, cols):
    return _pick(rows, max(8, (1 << 19) // cols // 8 * 8))


def _chip_partial(grad, recv, c_idx, p_idx):
    Q, R, C = grad.shape
    h = R // 2
    T = _elem_tile(h, C)
    nt = h // T

    def body(sc_ref, g_ref, r_ref, sb_ref, own_ref):
        q = pl.program_id(1)
        s = g_ref[...] + r_ref[...]
        sb_ref[...] = s.astype(BF16)

        @pl.when(q == sc_ref[1])
        def _():
            own_ref[...] = s

    return _call(
        body, (grad, recv), name="chip_partial", grid=(nt, Q),
        scalars=jnp.stack([c_idx, p_idx]).astype(jnp.int32),
        in_specs=[pl.BlockSpec((None, T, C), lambda t, q, sc: (q, sc[0] * nt + t, 0)),
                  pl.BlockSpec((None, T, C), lambda t, q, sc: (q, t, 0))],
        out_specs=[pl.BlockSpec((None, T, C), lambda t, q, sc: (q, t, 0)),
                   pl.BlockSpec((T, C), lambda t, q, sc: (t, 0))],
        out_shape=[_sds((Q, h, C), BF16), _sds((h, C), F32)], compiler_params=_params(2))


def _sum_partials(own, parts, c_idx):
    h, C = own.shape
    T = _elem_tile(h, C)
    nt = h // T

    def body(c_ref, o_ref, p_ref, t_ref):
        t = o_ref[...]
        for k in range(N_CHIPS - 1):
            t = t + p_ref[k].astype(F32)
        t_ref[...] = t

    return _call(
        body, (own, parts), name="sum_partials", grid=(nt,), scalars=jnp.reshape(c_idx, (1,)).astype(jnp.int32),
        in_specs=[pl.BlockSpec((T, C), lambda i, c: (i, 0)),
                  pl.BlockSpec((N_CHIPS - 1, T, C), lambda i, c: (0, i, 0))],
        out_specs=pl.BlockSpec((T, C), lambda i, c: (c[0] * nt + i, 0)),
        out_shape=_sds((2 * h, C), F32), compiler_params=_params(1))


def _pack_rows(name, parts):
    width = parts[0].shape[1]
    offsets, at = [], 0
    for p in parts:
        offsets.append(at)
        at += p.shape[0]
    total = -(-at // 8) * 8

    def body(*refs):
        out = refs[-1]
        out[...] = jnp.zeros(out.shape, F32)
        for ref, o in zip(refs[:-1], offsets):
            out[o:o + ref.shape[0], :] = ref[...]

    whole = pl.BlockSpec(memory_space=pltpu.VMEM)
    return _call(body, list(parts), name=name, in_specs=[whole] * len(parts), out_specs=whole,
                 out_shape=_sds((total, width), F32))


def _sum_packs(packs):
    n, R, C = packs.shape

    def body(p_ref, o_ref):
        t = p_ref[0]
        for k in range(1, n):
            t = t + p_ref[k]
        o_ref[...] = t

    return _call(
        body, (packs,), name="sum_packs", grid=(1,), in_specs=[pl.BlockSpec((n, R, C), lambda i: (0, 0, 0))],
        out_specs=pl.BlockSpec((R, C), lambda i: (0, 0)), out_shape=_sds((R, C), F32), compiler_params=_params(1))


def _adamw(w, g, m, v):
    R, C = w.shape
    T = _elem_tile(R, C)

    def body(w_ref, g_ref, m_ref, v_ref, d_ref, m2_ref, v2_ref):
        g_ = g_ref[...]
        m2 = ADAM_B1 * m_ref[...] + (1.0 - ADAM_B1) * g_
        v2 = ADAM_B2 * v_ref[...] + (1.0 - ADAM_B2) * (g_ * g_)
        m_hat = m2 / (1.0 - ADAM_B1 ** ADAM_STEP)
        v_hat = v2 / (1.0 - ADAM_B2 ** ADAM_STEP)
        d_ref[...] = -ADAM_LR * (m_hat / (jnp.sqrt(v_hat) + ADAM_EPS) + ADAM_WD * w_ref[...])
        m2_ref[...] = m2
        v2_ref[...] = v2

    blk = pl.BlockSpec((T, C), lambda i: (i, 0))
    return _call(
        body, (w, g, m, v), name="adamw", grid=(R // T,), in_specs=[blk] * 4, out_specs=[blk] * 3,
        out_shape=[_sds((R, C), F32)] * 3, compiler_params=_params(1))


def _adamw_rows(w, g, m, v, row0, prev=None):
    R, C = w.shape
    T = _elem_tile(g.shape[0], C)
    off = row0 // T

    def body(w_ref, g_ref, m_ref, v_ref, *rest):
        d_ref, m2_ref, v2_ref, g2_ref = rest[-4:]
        g_ = g_ref[...]
        m2 = ADAM_B1 * m_ref[...] + (1.0 - ADAM_B1) * g_
        v2 = ADAM_B2 * v_ref[...] + (1.0 - ADAM_B2) * (g_ * g_)
        m_hat = m2 / (1.0 - ADAM_B1 ** ADAM_STEP)
        v_hat = v2 / (1.0 - ADAM_B2 ** ADAM_STEP)
        d_ref[...] = -ADAM_LR * (m_hat / (jnp.sqrt(v_hat) + ADAM_EPS) + ADAM_WD * w_ref[...])
        m2_ref[...] = m2
        v2_ref[...] = v2
        g2_ref[...] = g_

    here = pl.BlockSpec((T, C), lambda i: (off + i, 0))
    piece = pl.BlockSpec((T, C), lambda i: (i, 0))
    done = tuple(prev or ())
    return _call(
        body, (w, g, m, v) + done, name="adamw_rows", grid=(g.shape[0] // T,), follows=prev is None,
        in_specs=[here, piece, here, here] + [ANY] * len(done), out_specs=[here] * 4,
        out_shape=[_sds((R, C), F32)] * 4, input_output_aliases={4 + j: j for j in range(len(done))},
        compiler_params=_params(1))


BIG = ("w_in", "w_pool_out", "w_conv_out", "w_o", "w_up", "w_down", "w_pool_grp")
VECTORS = ("g_pre_mix", "pool_scale", "b_dw", "conv_ln_g", "conv_ln_b", "g_post_mix", "g_pre_mlp", "g_post_mlp")
WEIGHTS = ("meta", "g_pre_mix", "w_in", "w_pool_grp", "pool_scale", "w_pool_out", "w_dw", "b_dw", "conv_ln_g",
           "conv_ln_b", "w_conv_out", "w_o", "g_post_mix", "g_pre_mlp", "w_up", "w_down", "g_post_mlp")


def _as_rows(a, width):
    r, cols = a.shape
    return a.reshape(r * (cols // width), width)


def _step(w, m, v, x, tgt):
    S, D = x.shape
    P = D // 2
    xi, yi, ci = _place()
    chip = 2 * xi + yi
    _CHAIN["after"] = None

    C = D // 2
    G = POOL_GROUPS
    GD = P // G
    GS = GD // N_CHIPS
    Q = N_CHIPS
    vecs = {k: w[k] for k in VECTORS}
    shard2d = {k: w[k].reshape(-1, w[k].shape[-1]) for k in BIG}
    grads, delta, new_m, new_v = {}, {}, {}, {}

    def update(names, reduced):
        for k, g in zip(names, reduced):
            grads[k] = g
            delta[k], new_m[k], new_v[k] = _adamw(shard2d[k], g, m[k].reshape(shard2d[k].shape),
                                                  v[k].reshape(shard2d[k].shape))

    groups = (("a", ("w_in", "w_pool_grp"), ("w_dw", "meta")), ("b", ("w_pool_out", "w_conv_out", "w_o"), ()),
              ("c", ("w_up",), ()), ("d", ("w_down",), ()))
    flying = {}
    for tag, halved, whole in groups:
        slabs = [_cast_into_slab(shard2d[k], chip, BF16) for k in halved]
        slabs += [_cast_into_slab(w[k], chip, F32) for k in whole]
        flying[tag] = _comm_start("gather_start_" + tag, slabs, (len(slabs), N_CHIPS - 1), _gather_plan(len(halved)))

    def landed(tag):
        _, halved, whole = next(g for g in groups if g[0] == tag)
        nh = len(halved)
        sems, bufs = flying.pop(tag)
        sems, bufs = _comm_relay("gather_relay_" + tag, bufs, sems, _gather_plan(nh),
                                 (nh, N_CHIPS - 1), _hand_on_plan(nh))
        done = _comm_wait("gather_wait_" + tag, bufs[:nh], sems, _hand_on_plan(nh))
        return dict(zip(halved + whole, done + bufs[nh:]))

    got = landed("a")
    win_g = got["w_in"]
    w_grp = got["w_pool_grp"].reshape(N_CHIPS, G, GS, GD).transpose(1, 0, 2, 3).reshape(G, GD, GD)
    w_dw = got["w_dw"].transpose(1, 0, 2).reshape(CONV_TAPS, P)
    meta = got["meta"].transpose(1, 0, 2).reshape(N_META, D)
    h0 = jnp.concatenate([jnp.zeros((PAD_ROWS, D), F32), meta, x], axis=0)
    u1 = _pre_norm(h0, vecs["g_pre_mix"])
    proj = _mm_act_colw("proj", u1, win_g, 256)
    d, ya_pre = _pool_fwd(proj, w_grp, vecs["pool_scale"])
    s, c = _conv_fwd(proj, C, w_dw, vecs["b_dw"], vecs["conv_ln_g"], vecs["conv_ln_b"])
    got = landed("b")
    wpo_g, wco_g, w_o = got["w_pool_out"], got["w_conv_out"], got["w_o"].reshape(D, D)
    mix, ya, yb = _mix_fwd(ya_pre, s, wpo_g, wco_g, proj, D)
    o = _mm_act_roww("attn_out", mix, w_o, 1088, 1024, 2048)
    h1, u2 = _mid_norm(o, h0, vecs["g_post_mix"], vecs["g_pre_mlp"])
    wup_g = landed("c")["w_up"]
    a_up, fact = _mm_act_colw("mlp_up", u2, wup_g, 512, _up_epilogue, (F32, BF16))
    w_down = landed("d")["w_down"].reshape(-1, D)
    f = _mm_act_roww("mlp_down", fact, w_down, 1088, 1024, 2048)
    dy, df, dg_post_mlp, loss = _loss_head(f, h1, tgt, vecs["g_post_mlp"])

    g_w_down = _mm_wgrad("dw_down", fact, df, 1024, 1024)
    red1 = _Reduction("1", [g_w_down.reshape(N_CHIPS, -1, D)], ci, chip)
    da_up = _mlp_down_bwd(df, w_down, a_up)
    red1.partial()
    g_w_up = _mm_wgrad_colw("dw_up", u2, da_up, Q, 1024, 1024)
    red2 = _Reduction("2", [g_w_up], ci, chip)
    du2 = _mm_grad_colw_t("du2", da_up, wup_g, 1088, 1024)
    red2.partial()
    dh1, do, dg_pre_mlp, dg_post_mix = _mid_norm_bwd(dy, du2, h1, o, vecs["g_pre_mlp"], vecs["g_post_mix"])
    g_w_o = _mm_wgrad("dw_o", mix, do, 1024, 1024)
    red1.total()
    dya, dyb, dga, dgb = _mix_bwd(do, w_o, proj, ya, yb)
    update(("w_down",), red1.finish())
    g_wpo = _mm_wgrad_colw("dw_pool_out", ya_pre, dya, Q, 1024, 512)
    g_wco = _mm_wgrad_colw("dw_conv_out", s, dyb, Q, 1024, 512)
    red3 = _Reduction("3", [g_w_o.reshape(N_CHIPS, D // N_CHIPS, D), g_wpo, g_wco], ci, chip)
    dya_pre = _mm_grad_colw_t("dya_pre", dya, wpo_g, 1088, 1024)
    ds = _mm_grad_colw_t("ds", dyb, wco_g, 1088, 1024)
    red3.partial()
    dz, g_w_grp, dscale = _pool_bwd(dya_pre, d, w_grp, vecs["pool_scale"])
    dc, dln_g, dln_b, db_dw = _conv_ln_bwd(ds, c, vecs["conv_ln_g"], vecs["conv_ln_b"])
    red2.total()
    dv, dgc, g_w_dw = _conv_bwd(dc, proj, C, w_dw)
    update(("w_up",), red2.finish())
    dproj = jnp.concatenate([dz, dv, dgc, dga, dgb], axis=1)
    half_k = D // 2
    g_w_grp = g_w_grp.reshape(G, N_CHIPS, GS, GD).transpose(1, 0, 2, 3).reshape(N_CHIPS, G * GS, GD)
    g_in_a = _mm_wgrad_colw("dw_in_a", u1, dproj, Q, 512, 1792, rows=(0, half_k))
    red4a = _Reduction("4a", [g_in_a, g_w_grp], ci, chip)
    g_in_b = _mm_wgrad_colw("dw_in_b", u1, dproj, Q, 512, 1792, rows=(half_k, half_k))
    red4a.partial()
    red4b = _Reduction("4b", [g_in_b], ci, chip)
    du1 = _mm_grad_colw_t("du1", dproj, win_g, 1088, 1024)
    red4b.partial()
    red3.total()
    grad_x, dmeta, dg_pre_mix = _pre_norm_bwd(dh1, du1, h0, vecs["g_pre_mix"])
    red4a.total()
    update(("w_o", "w_pool_out", "w_conv_out"), red3.finish())
    red_in_a, red_grp = red4a.finish()
    update(("w_pool_grp",), [red_grp])
    w_in_rows = (shard2d["w_in"], m["w_in"].reshape(shard2d["w_in"].shape), v["w_in"].reshape(shard2d["w_in"].shape))
    first_rows = _adamw_rows(w_in_rows[0], red_in_a, w_in_rows[1], w_in_rows[2], 0)
    g_vec = dict(g_pre_mix=dg_pre_mix, pool_scale=dscale, b_dw=db_dw, conv_ln_g=dln_g, conv_ln_b=dln_b,
                 g_post_mix=dg_post_mix, g_pre_mlp=dg_pre_mlp, g_post_mlp=dg_post_mlp)

    rows = [g_w_dw, _as_rows(dmeta, P)] + [_as_rows(g_vec[k], P) for k in VECTORS]
    rows.append(jnp.broadcast_to(loss[:, :1], (1, P)))
    total = _sum_packs(_gather_packs(_pack_rows("pack_grads", rows)))
    at = 0
    taps_pad = g_w_dw.shape[0]
    g_dw_full = total[at:at + CONV_TAPS]
    at += taps_pad
    g_meta_full = total[at:at + 2 * N_META].reshape(N_META, D)
    at += 2 * N_META
    for k in VECTORS:
        n = w[k].shape[-1] // P
        grads[k] = total[at:at + n].reshape(1, n * P)
        at += n
    loss_total = total[at, 0]
    grads["w_dw"] = lax.dynamic_slice_in_dim(g_dw_full, chip * (P // N_CHIPS), P // N_CHIPS, axis=1)
    grads["meta"] = lax.dynamic_slice_in_dim(g_meta_full, chip * (D // N_CHIPS), D // N_CHIPS, axis=1)

    small_names = [k for k in WEIGHTS if k not in BIG]

    def pack_small(tree):
        parts = []
        for k in small_names:
            a = tree[k].reshape(-1, tree[k].shape[-1])
            flat = a.reshape(-1)
            parts.append(jnp.pad(flat, (0, -flat.shape[0] % P)).reshape(-1, P))
        return _pack_rows("pack_small", parts)

    sd, sm, sv = _adamw(pack_small(w), pack_small(grads), pack_small(m), pack_small(v))
    at = 0
    for k in small_names:
        a = w[k].reshape(-1, w[k].shape[-1])
        n = -(-a.size // P)
        for tree, packed in ((delta, sd), (new_m, sm), (new_v, sv)):
            tree[k] = packed[at:at + n].reshape(-1)[:a.size].reshape(a.shape)
        at += n

    red4b.total()
    delta["w_in"], new_m["w_in"], new_v["w_in"], grads["w_in"] = _adamw_rows(
        w_in_rows[0], red4b.finish()[0], w_in_rows[1], w_in_rows[2], half_k, prev=first_rows)
    return loss_total, grad_x, grads, delta, new_m, new_v


def kernel(x, meta, g_pre_mix, w_in, w_pool_grp, pool_scale, w_pool_out, w_dw, b_dw, conv_ln_g, conv_ln_b, w_conv_out, w_o, g_post_mix, g_pre_mlp, w_up, w_down, g_post_mlp, loss_target, m_meta, m_g_pre_mix, m_w_in, m_w_pool_grp, m_pool_scale, m_w_pool_out, m_w_dw, m_b_dw, m_conv_ln_g, m_conv_ln_b, m_w_conv_out, m_w_o, m_g_post_mix, m_g_pre_mlp, m_w_up, m_w_down, m_g_post_mlp, v_meta, v_g_pre_mix, v_w_in, v_w_pool_grp, v_pool_scale, v_w_pool_out, v_w_dw, v_b_dw, v_conv_ln_g, v_conv_ln_b, v_w_conv_out, v_w_o, v_g_post_mix, v_g_pre_mlp, v_w_up, v_w_down, v_g_post_mlp):
    args = dict(locals())
    shapes = {k: args[k].shape for k in WEIGHTS}
    w = {k: args[k] for k in WEIGHTS}
    m = {k: args["m_" + k] for k in WEIGHTS}
    v = {k: args["v_" + k] for k in WEIGHTS}
    for tree in (w, m, v):
        tree["w_dw"] = tree["w_dw"].reshape(tree["w_dw"].shape[-2:])
    loss, grad_x, grads, delta, new_m, new_v = _step(w, m, v, x[0], loss_target[0])
    out = [loss, grad_x[None]]
    for tree in (grads, delta, new_m, new_v):
        out += [tree[k].reshape(shapes[k]) for k in WEIGHTS]
    return tuple(out)
```

```python
import jax
import jax.numpy as jnp
from jax import lax
from jax.experimental import pallas as pl
from jax.experimental.pallas import tpu as pltpu

F32 = jnp.float32
BF16 = jnp.bfloat16

N_META = 16
PAD_ROWS = 112
TOKEN_ROW0 = PAD_ROWS + N_META
POOL_GROUPS = 4
CONV_TAPS = 31
HALO = 32
CONV_ROWS = 128
LANES = 128
RMS_EPS = 1e-6
LN_EPS = 1e-5
ADAM_LR = 0.001
ADAM_B1 = 0.9
ADAM_B2 = 0.999
ADAM_EPS = 1e-08
ADAM_WD = 0.01
ADAM_STEP = 10
VMEM_LIMIT_MB = 56

MESH = pl.DeviceIdType.MESH
NN = (((1,), (0,)), ((), ()))
NT = (((1,), (1,)), ((), ()))
TN = (((0,), (0,)), ((), ()))


def _pick(n, pref):
    if n <= pref:
        return n
    if n % pref == 0:
        return pref
    for step in (LANES, 8, 1):
        t = (pref // step) * step
        while t >= step:
            if n % t == 0:
                return t
            t -= step
    return n


def _params(n_axes, vmem_mb=VMEM_LIMIT_MB):
    return pltpu.CompilerParams(dimension_semantics=("arbitrary",) * n_axes,
                                vmem_limit_bytes=vmem_mb << 20)


def _sigmoid(x):
    return jax.nn.sigmoid(x)


_CHAIN = {"after": None}


def _call(body, args, *, in_specs, out_specs, out_shape, grid=(), scalars=None, mark=0, follows=True, **kw):
    after = _CHAIN["after"] if follows else None
    n = len(args)
    lead = 0 if scalars is None else 1
    specs = list(in_specs)
    operands = list(args)
    fn = body
    if after is not None:
        def fn(*refs):
            body(*refs[:lead + n], *refs[lead + n + 1:])
        specs.append(pl.BlockSpec(memory_space=pl.ANY))
        operands.append(after)
    if scalars is None:
        if grid:
            kw["grid"] = grid
        res = pl.pallas_call(fn, in_specs=specs, out_specs=out_specs, out_shape=out_shape, **kw)(*operands)
    else:
        grid_spec = pltpu.PrefetchScalarGridSpec(num_scalar_prefetch=1, grid=grid, in_specs=specs, out_specs=out_specs)
        res = pl.pallas_call(fn, grid_spec=grid_spec, out_shape=out_shape, **kw)(scalars, *operands)
    outs = res if isinstance(res, (list, tuple)) else [res]
    _CHAIN["after"] = outs[mark]
    return res


def _store(val, extras, outs):
    outs[0][...] = val.astype(outs[0].dtype)


def _mm(name, grid, arrays, in_specs, out_shapes, out_specs, dims, nk, epilogue=_store, acc_shape=None):
    n_in, n_out = len(arrays), len(out_shapes)

    def body(*refs):
        extras = refs[2:n_in]
        outs = refs[n_in:n_in + n_out]
        part = lax.dot_general(refs[0][...], refs[1][...], dims, preferred_element_type=F32)
        if nk == 1:
            epilogue(part, extras, outs)
        else:
            acc = refs[n_in + n_out]
            k = pl.program_id(len(grid) - 1)

            @pl.when(k == 0)
            def _():
                acc[...] = part

            @pl.when(k > 0)
            def _():
                acc[...] += part

            @pl.when(k == nk - 1)
            def _():
                epilogue(acc[...], extras, outs)

    scratch = [pltpu.VMEM(acc_shape, F32)] if nk > 1 else []
    single = n_out == 1
    return _call(
        body, arrays, name=name, grid=grid, in_specs=in_specs,
        out_specs=out_specs[0] if single else out_specs,
        out_shape=out_shapes[0] if single else out_shapes,
        scratch_shapes=scratch, compiler_params=_params(len(grid)))


def _sds(shape, dtype):
    return jax.ShapeDtypeStruct(shape, dtype)


def _rms_scale(h):
    return lax.rsqrt(jnp.mean(h * h, axis=-1, keepdims=True) + RMS_EPS)


def _rms_bwd(du, h, g):
    r = _rms_scale(h)
    y = h * r
    dy = du * g
    dh = r * (dy - y * jnp.mean(dy * y, axis=-1, keepdims=True))
    return dh, jnp.sum(du * y, axis=0, keepdims=True)


def _row_tile(L):
    return _pick(L, 272)


def _pre_norm(h0, g):
    L, D = h0.shape
    T = _row_tile(L)

    def body(h_ref, g_ref, u_ref):
        h = h_ref[...]
        u_ref[...] = (h * _rms_scale(h) * g_ref[...]).astype(BF16)

    return _call(
        body, (h0, g), name="pre_norm", grid=(L // T,),
        in_specs=[pl.BlockSpec((T, D), lambda i: (i, 0)), pl.BlockSpec((1, D), lambda i: (0, 0))],
        out_specs=pl.BlockSpec((T, D), lambda i: (i, 0)),
        out_shape=_sds((L, D), BF16), compiler_params=_params(1))


def _mid_norm(o, h0, g_post, g_pre):
    L, D = h0.shape
    T = _row_tile(L)

    def body(o_ref, h_ref, gp_ref, gm_ref, h1_ref, u2_ref):
        o_ = o_ref[...]
        h1 = h_ref[...] + o_ * _rms_scale(o_) * gp_ref[...]
        h1_ref[...] = h1
        u2_ref[...] = (h1 * _rms_scale(h1) * gm_ref[...]).astype(BF16)

    row = pl.BlockSpec((T, D), lambda i: (i, 0))
    vec = pl.BlockSpec((1, D), lambda i: (0, 0))
    return _call(
        body, (o, h0, g_post, g_pre), name="mid_norm", grid=(L // T,),
        in_specs=[row, row, vec, vec], out_specs=[row, row],
        out_shape=[_sds((L, D), F32), _sds((L, D), BF16)], compiler_params=_params(1))


def _loss_head(f, h1, tgt, g_post):
    L, D = h1.shape
    T = TOKEN_ROW0
    n = L // T

    def body(f_ref, h_ref, t_ref, g_ref, dy_ref, df_ref, dg_ref, loss_ref):
        i = pl.program_id(0)
        f_ = f_ref[...]
        g = g_ref[...]
        y = h_ref[...] + f_ * _rms_scale(f_) * g
        live = (i > 0).astype(F32)
        diff = (y - t_ref[...]) * live
        part = 0.5 * jnp.sum(jnp.mean(diff * diff, axis=-1, keepdims=True), axis=0, keepdims=True)
        dy = diff * (1.0 / D)
        dy_ref[...] = dy
        df, dg = _rms_bwd(dy, f_, g)
        df_ref[...] = df.astype(BF16)

        @pl.when(i == 0)
        def _():
            dg_ref[...] = dg
            loss_ref[...] = jnp.broadcast_to(part, loss_ref.shape)

        @pl.when(i > 0)
        def _():
            dg_ref[...] += dg
            loss_ref[...] += jnp.broadcast_to(part, loss_ref.shape)

    row = pl.BlockSpec((T, D), lambda i: (i, 0))
    vec = pl.BlockSpec((1, D), lambda i: (0, 0))
    return _call(
        body, (f, h1, tgt, g_post), name="loss_head", grid=(n,),
        in_specs=[row, row, pl.BlockSpec((T, D), lambda i: (jnp.maximum(i - 1, 0), 0)), vec],
        out_specs=[row, row, vec, pl.BlockSpec((1, LANES), lambda i: (0, 0))],
        out_shape=[_sds((L, D), F32), _sds((L, D), BF16), _sds((1, D), F32), _sds((1, LANES), F32)],
        compiler_params=_params(1))


def _mid_norm_bwd(dy, du2, h1, o, g_pre, g_post):
    L, D = h1.shape
    T = _row_tile(L)

    def body(dy_ref, du_ref, h_ref, o_ref, gm_ref, gp_ref, dh1_ref, do_ref, dgm_ref, dgp_ref):
        i = pl.program_id(0)
        dh, dgm = _rms_bwd(du_ref[...], h_ref[...], gm_ref[...])
        dh1 = dy_ref[...] + dh
        dh1_ref[...] = dh1
        do, dgp = _rms_bwd(dh1, o_ref[...], gp_ref[...])
        do_ref[...] = do.astype(BF16)

        @pl.when(i == 0)
        def _():
            dgm_ref[...] = dgm
            dgp_ref[...] = dgp

        @pl.when(i > 0)
        def _():
            dgm_ref[...] += dgm
            dgp_ref[...] += dgp

    row = pl.BlockSpec((T, D), lambda i: (i, 0))
    vec = pl.BlockSpec((1, D), lambda i: (0, 0))
    return _call(
        body, (dy, du2, h1, o, g_pre, g_post), name="mid_norm_bwd", grid=(L // T,),
        in_specs=[row, row, row, row, vec, vec], out_specs=[row, row, vec, vec],
        out_shape=[_sds((L, D), F32), _sds((L, D), BF16), _sds((1, D), F32), _sds((1, D), F32)],
        compiler_params=_params(1))


def _pre_norm_bwd(dh1, du1, h0, g):
    L, D = h0.shape
    T = TOKEN_ROW0
    n = L // T

    def body(dh_ref, du_ref, h_ref, g_ref, gx_ref, dmeta_ref, dg_ref):
        i = pl.program_id(0)
        dh, dg = _rms_bwd(du_ref[...], h_ref[...], g_ref[...])
        dh0 = dh_ref[...] + dh
        gx_ref[...] = dh0

        @pl.when(i == 0)
        def _():
            dmeta_ref[...] = dh0[PAD_ROWS:, :]
            dg_ref[...] = dg

        @pl.when(i > 0)
        def _():
            dg_ref[...] += dg

    row = pl.BlockSpec((T, D), lambda i: (i, 0))
    vec = pl.BlockSpec((1, D), lambda i: (0, 0))
    return _call(
        body, (dh1, du1, h0, g), name="pre_norm_bwd", grid=(n,),
        in_specs=[row, row, row, vec],
        out_specs=[pl.BlockSpec((T, D), lambda i: (jnp.maximum(i - 1, 0), 0)),
                   pl.BlockSpec((N_META, D), lambda i: (0, 0)), vec],
        out_shape=[_sds((L - T, D), F32), _sds((N_META, D), F32), _sds((1, D), F32)],
        compiler_params=_params(1))


def _window_sum(z, g, shift_sign, L):
    s = z
    for j in range(POOL_GROUPS):
        k = 1 << j
        nxt = s + pltpu.roll(s, k if shift_sign > 0 else L - k, 0)
        s = jnp.where(j <= g, nxt, s)
    return s


def _inv_count(g, L):
    t = lax.broadcasted_iota(jnp.int32, (L, 1), 0)
    w = jnp.left_shift(2, g)
    cnt = jnp.clip(t - (PAD_ROWS - 1), 1, w)
    return 1.0 / cnt.astype(F32)


def _pool_fwd(proj, w_grp, scale):
    L = proj.shape[0]
    G, GD, _ = w_grp.shape
    P = G * GD

    def body(z_ref, w_ref, sc_ref, d_ref, ya_ref):
        g = pl.program_id(0)
        z = z_ref[...]
        d = (_window_sum(z, g, +1, L) * _inv_count(g, L) - z).astype(BF16)
        d_ref[...] = d
        y = jnp.dot(d, w_ref[...], preferred_element_type=F32)
        ya_ref[...] = (y * sc_ref[...]).astype(BF16)

    col = pl.BlockSpec((L, GD), lambda g: (0, g))
    return _call(
        body, (proj, w_grp, scale), name="pool_fwd", grid=(G,),
        in_specs=[col, pl.BlockSpec((None, GD, GD), lambda g: (g, 0, 0)), pl.BlockSpec((1, GD), lambda g: (0, g))],
        out_specs=[col, col], out_shape=[_sds((L, P), BF16), _sds((L, P), BF16)],
        compiler_params=_params(1))


def _pool_bwd(dya, d, w_grp, scale):
    L, P = dya.shape
    G, GD, _ = w_grp.shape

    def body(dya_ref, d_ref, w_ref, sc_ref, dz_ref, dw_ref, dsc_ref):
        g = pl.program_id(0)
        dya_ = dya_ref[...]
        d_ = d_ref[...]
        w = w_ref[...]
        y = jnp.dot(d_, w, preferred_element_type=F32)
        dsc_ref[...] = jnp.sum(dya_ * y, axis=0, keepdims=True)
        dy = (dya_ * sc_ref[...]).astype(BF16)
        dw_ref[...] = lax.dot_general(d_, dy, TN, preferred_element_type=F32)
        dd = lax.dot_general(dy, w, NT, preferred_element_type=F32)
        dz = _window_sum(dd * _inv_count(g, L), g, -1, L) - dd
        dz_ref[...] = dz.astype(BF16)

    col = pl.BlockSpec((L, GD), lambda g: (0, g))
    wspec = pl.BlockSpec((None, GD, GD), lambda g: (g, 0, 0))
    vec = pl.BlockSpec((1, GD), lambda g: (0, g))
    return _call(
        body, (dya, d, w_grp, scale), name="pool_bwd", grid=(G,),
        in_specs=[col, col, wspec, vec], out_specs=[col, wspec, vec],
        out_shape=[_sds((L, P), BF16), _sds((G, GD, GD), F32), _sds((1, P), F32)],
        compiler_params=_params(1))


def _fill_rotations(rot_ref, ext):
    n = ext.shape[0]
    rot_ref[0] = ext
    for r in range(1, 8):
        rot_ref[r] = pltpu.roll(ext, n - r, 0)


def _lane_chunks(C):
    step = LANES if C % LANES == 0 else C
    return [(c0, step) for c0 in range(0, C, step)]


def _conv_specs(L, C, col_v, col_g):
    T = CONV_ROWS
    per = T // HALO
    cur_v = pl.BlockSpec((T, C), lambda i: (i, col_v))
    cur_g = pl.BlockSpec((T, C), lambda i: (i, col_g))
    prev_v = pl.BlockSpec((HALO, C), lambda i: (jnp.maximum(i * per - 1, 0), col_v))
    prev_g = pl.BlockSpec((HALO, C), lambda i: (jnp.maximum(i * per - 1, 0), col_g))
    return cur_v, cur_g, prev_v, prev_g


def _glu_ext(vc, gc, vh, gh, i):
    a_cur = vc[...] * _sigmoid(gc[...])
    a_prev = vh[...] * _sigmoid(gh[...]) * (i > 0).astype(F32)
    return jnp.concatenate([a_prev, a_cur], axis=0)


def _conv_fwd(proj, C, w_dw, b_dw, ln_g, ln_b):
    L = proj.shape[0]
    T = CONV_ROWS
    P = C

    def body(vc, gc, vh, gh, w_ref, b_ref, lg_ref, lb_ref, s_ref, c_ref, rot):
        i = pl.program_id(0)
        _fill_rotations(rot, _glu_ext(vc, gc, vh, gh, i))
        for c0, cw in _lane_chunks(C):
            acc = jnp.zeros((T, cw), F32)
            for k in range(CONV_TAPS):
                q, r = divmod(HALO - (CONV_TAPS - 1) + k, 8)
                acc = acc + w_ref[k:k + 1, c0:c0 + cw] * rot[r, 8 * q:8 * q + T, c0:c0 + cw]
            c_ref[:, c0:c0 + cw] = acc + b_ref[:, c0:c0 + cw]
        c = c_ref[...]
        mu = jnp.mean(c, axis=-1, keepdims=True)
        cen = c - mu
        var = jnp.mean(cen * cen, axis=-1, keepdims=True)
        ln = cen * lax.rsqrt(var + LN_EPS) * lg_ref[...] + lb_ref[...]
        s_ref[...] = (ln * _sigmoid(ln)).astype(BF16)

    cur_v, cur_g, prev_v, prev_g = _conv_specs(L, C, P // C, P // C + 1)
    row = pl.BlockSpec((T, C), lambda i: (i, 0))
    vec = pl.BlockSpec((1, C), lambda i: (0, 0))
    return _call(
        body, (proj, proj, proj, proj, w_dw, b_dw, ln_g, ln_b), name="conv_fwd", grid=(L // T,),
        in_specs=[cur_v, cur_g, prev_v, prev_g, pl.BlockSpec((CONV_TAPS, C), lambda i: (0, 0)), vec, vec, vec],
        out_specs=[row, row], out_shape=[_sds((L, C), BF16), _sds((L, C), F32)],
        scratch_shapes=[pltpu.VMEM((8, T + HALO, C), F32)], compiler_params=_params(1))


def _conv_ln_bwd(ds, c, ln_g, ln_b):
    L, C = c.shape
    T = _row_tile(L)

    def body(ds_ref, c_ref, lg_ref, lb_ref, dc_ref, dlg_ref, dlb_ref, db_ref):
        i = pl.program_id(0)
        c_ = c_ref[...]
        g = lg_ref[...]
        mu = jnp.mean(c_, axis=-1, keepdims=True)
        cen = c_ - mu
        rstd = lax.rsqrt(jnp.mean(cen * cen, axis=-1, keepdims=True) + LN_EPS)
        xhat = cen * rstd
        ln = xhat * g + lb_ref[...]
        sg = _sigmoid(ln)
        dln = ds_ref[...] * (sg * (1.0 + ln * (1.0 - sg)))
        dxh = dln * g
        dc = rstd * (dxh - jnp.mean(dxh, axis=-1, keepdims=True)
                     - xhat * jnp.mean(dxh * xhat, axis=-1, keepdims=True))
        dc_ref[...] = dc
        dlg = jnp.sum(dln * xhat, axis=0, keepdims=True)
        dlb = jnp.sum(dln, axis=0, keepdims=True)
        db = jnp.sum(dc, axis=0, keepdims=True)

        @pl.when(i == 0)
        def _():
            dlg_ref[...] = dlg
            dlb_ref[...] = dlb
            db_ref[...] = db

        @pl.when(i > 0)
        def _():
            dlg_ref[...] += dlg
            dlb_ref[...] += dlb
            db_ref[...] += db

    row = pl.BlockSpec((T, C), lambda i: (i, 0))
    vec = pl.BlockSpec((1, C), lambda i: (0, 0))
    return _call(
        body, (ds, c, ln_g, ln_b), name="conv_ln_bwd", grid=(L // T,),
        in_specs=[row, row, vec, vec], out_specs=[row, vec, vec, vec],
        out_shape=[_sds((L, C), F32), _sds((1, C), F32), _sds((1, C), F32), _sds((1, C), F32)],
        compiler_params=_params(1))


def _conv_bwd(dc, proj, C, w_dw):
    L = proj.shape[0]
    T = CONV_ROWS
    per = T // HALO
    n = L // T
    P = C
    taps_pad = 32

    def body(dcc, dcn, vc, gc, w_ref, dv_ref, dg_ref, dw_ref, rot_d, dw_acc):
        i = pl.program_id(0)
        dc_next = dcn[...] * (i < n - 1).astype(F32)
        _fill_rotations(rot_d, jnp.concatenate([dcc[...], dc_next], axis=0))

        @pl.when(i == 0)
        def _():
            dw_acc[...] = jnp.zeros(dw_acc.shape, F32)

        for c0, cw in _lane_chunks(C):
            v = vc[:, c0:c0 + cw]
            sg = _sigmoid(gc[:, c0:c0 + cw])
            a = v * sg
            da = jnp.zeros((T, cw), F32)
            for k in range(CONV_TAPS):
                q, r = divmod(CONV_TAPS - 1 - k, 8)
                slab = rot_d[r, 8 * q:8 * q + T, c0:c0 + cw]
                da = da + w_ref[k:k + 1, c0:c0 + cw] * slab
                dw_acc[k, :, c0:c0 + cw] += jnp.sum((a * slab).reshape(T // 8, 8, cw), axis=0)
            dv_ref[:, c0:c0 + cw] = (da * sg).astype(BF16)
            dg_ref[:, c0:c0 + cw] = (da * v * sg * (1.0 - sg)).astype(BF16)

        @pl.when(i == n - 1)
        def _():
            dw_ref[...] = jnp.sum(dw_acc[...], axis=1)

    cur_v, cur_g, _, _ = _conv_specs(L, C, P // C, P // C + 1)
    row = pl.BlockSpec((T, C), lambda i: (i, 0))
    nxt = pl.BlockSpec((HALO, C), lambda i: (jnp.minimum((i + 1) * per, L // HALO - 1), 0))
    wspec = pl.BlockSpec((CONV_TAPS, C), lambda i: (0, 0))
    return _call(
        body, (dc, dc, proj, proj, w_dw), name="conv_bwd", grid=(n,),
        in_specs=[row, nxt, cur_v, cur_g, wspec],
        out_specs=[row, row, pl.BlockSpec((taps_pad, C), lambda i: (0, 0))],
        out_shape=[_sds((L, C), BF16), _sds((L, C), BF16), _sds((taps_pad, C), F32)],
        scratch_shapes=[pltpu.VMEM((8, T + HALO, C), F32), pltpu.VMEM((taps_pad, 8, C), F32)],
        compiler_params=_params(1))


def _mix_fwd(ya_pre, s, wpo, wco, proj, D):
    L, P = ya_pre.shape
    Q, _, DS = wpo.shape
    bm = _pick(L, 1088)
    gate0 = (proj.shape[1] - 2 * D) // DS
    per = D // DS

    def body(a1, a2, b1, b2, ga, gb, m_ref, ya_ref, yb_ref):
        ya = jnp.dot(a1[...], b1[...], preferred_element_type=F32)
        yb = jnp.dot(a2[...], b2[...], preferred_element_type=F32)
        ya_ref[...] = ya
        yb_ref[...] = yb
        m_ref[...] = (_sigmoid(ga[...]) * ya + _sigmoid(gb[...]) * yb).astype(BF16)

    act = pl.BlockSpec((bm, P), lambda i, q: (i, 0))
    wsp = pl.BlockSpec((None, P, DS), lambda i, q: (q, 0, 0))
    out = pl.BlockSpec((bm, DS), lambda i, q: (i, q))
    return _call(
        body, (ya_pre, s, wpo, wco, proj, proj), name="mix_fwd", grid=(L // bm, Q),
        in_specs=[act, act, wsp, wsp,
                  pl.BlockSpec((bm, DS), lambda i, q: (i, gate0 + q)),
                  pl.BlockSpec((bm, DS), lambda i, q: (i, gate0 + per + q))],
        out_specs=[out, out, out],
        out_shape=[_sds((L, D), BF16), _sds((L, D), F32), _sds((L, D), F32)],
        compiler_params=_params(2))


def _mix_bwd(do, w_o, proj, ya, yb):
    L, D = do.shape
    bm = _pick(L, 544)
    bn = _pick(D // N_CHIPS, 512)
    gate0 = (proj.shape[1] - 2 * D) // bn
    per = D // bn

    def epilogue(dm, extras, outs):
        ga, gb, ya_ref, yb_ref = extras
        sa = _sigmoid(ga[...])
        sb = _sigmoid(gb[...])
        outs[0][...] = (dm * sa).astype(BF16)
        outs[1][...] = (dm * sb).astype(BF16)
        outs[2][...] = (dm * ya_ref[...] * sa * (1.0 - sa)).astype(BF16)
        outs[3][...] = (dm * yb_ref[...] * sb * (1.0 - sb)).astype(BF16)

    blk = pl.BlockSpec((bm, bn), lambda i, j: (i, j))
    return _mm(
        "mix_bwd", (L // bm, D // bn), [do, w_o, proj, proj, ya, yb],
        [pl.BlockSpec((bm, D), lambda i, j: (i, 0)), pl.BlockSpec((bn, D), lambda i, j: (j, 0)),
         pl.BlockSpec((bm, bn), lambda i, j: (i, gate0 + j)),
         pl.BlockSpec((bm, bn), lambda i, j: (i, gate0 + per + j)), blk, blk],
        [_sds((L, D), BF16)] * 4, [blk] * 4, NT, 1, epilogue)


def _mm_act_colw(name, a, wg, bn_pref, epilogue=_store, out_dtypes=(F32,)):
    L, K = a.shape
    Q, _, n = wg.shape
    bn = _pick(n, bn_pref)
    nj = n // bn
    out = pl.BlockSpec((L, bn), lambda q, j: (0, q * nj + j))
    return _mm(name, (Q, nj), [a, wg],
               [pl.BlockSpec((L, K), lambda q, j: (0, 0)), pl.BlockSpec((None, K, bn), lambda q, j: (q, 0, j))],
               [_sds((L, Q * n), dt) for dt in out_dtypes], [out] * len(out_dtypes), NN, 1, epilogue)


def _mm_grad_colw_t(name, g, wg, bm_pref, bn_pref):
    L = g.shape[0]
    Q, K, n = wg.shape
    bm = _pick(L, bm_pref)
    bn = _pick(K, bn_pref)
    return _mm(name, (L // bm, K // bn, Q), [g, wg],
               [pl.BlockSpec((bm, n), lambda i, j, k: (i, k)), pl.BlockSpec((None, bn, n), lambda i, j, k: (k, j, 0))],
               [_sds((L, K), F32)], [pl.BlockSpec((bm, bn), lambda i, j, k: (i, j))], NT, Q,
               acc_shape=(bm, bn))


def _mm_wgrad_colw(name, a, g, Q, bm_pref, bn_pref, rows=None):
    L, K = a.shape
    n = g.shape[1] // Q
    first, count = rows or (0, K)
    bm = _pick(count, bm_pref)
    bn = _pick(n, bn_pref)
    nj = n // bn
    i0 = first // bm
    return _mm(name, (Q, count // bm, nj), [a, g],
               [pl.BlockSpec((L, bm), lambda q, i, j: (0, i0 + i)),
                pl.BlockSpec((L, bn), lambda q, i, j: (0, q * nj + j))],
               [_sds((Q, count, n), F32)], [pl.BlockSpec((None, bm, bn), lambda q, i, j: (q, i, j))], TN, 1)


def _mm_wgrad(name, a, g, bm_pref, bn_pref):
    L, K = a.shape
    N = g.shape[1]
    bm = _pick(K, bm_pref)
    bn = _pick(N, bn_pref)
    return _mm(name, (K // bm, N // bn), [a, g],
               [pl.BlockSpec((L, bm), lambda i, j: (0, i)), pl.BlockSpec((L, bn), lambda i, j: (0, j))],
               [_sds((K, N), F32)], [pl.BlockSpec((bm, bn), lambda i, j: (i, j))], TN, 1)


def _mm_act_roww(name, a, w, bm_pref, bn_pref, bk_pref):
    L, K = a.shape
    N = w.shape[1]
    bm, bn, bk = _pick(L, bm_pref), _pick(N, bn_pref), _pick(K, bk_pref)
    nk = K // bk
    return _mm(name, (L // bm, N // bn, nk), [a, w],
               [pl.BlockSpec((bm, bk), lambda i, j, k: (i, k)), pl.BlockSpec((bk, bn), lambda i, j, k: (k, j))],
               [_sds((L, N), F32)], [pl.BlockSpec((bm, bn), lambda i, j, k: (i, j))], NN, nk,
               acc_shape=(bm, bn))


def _up_epilogue(val, extras, outs):
    outs[0][...] = val
    r = jnp.maximum(val, 0.0)
    outs[1][...] = (r * r).astype(BF16)


def _mlp_down_bwd(df, w_down, a_up):
    L, D = df.shape
    F = w_down.shape[0]
    bm = _pick(L, 1088)
    bn = _pick(F, 1024)

    def epilogue(val, extras, outs):
        outs[0][...] = (val * (2.0 * jnp.maximum(extras[0][...], 0.0))).astype(BF16)

    blk = pl.BlockSpec((bm, bn), lambda i, j: (i, j))
    return _mm("mlp_down_bwd", (L // bm, F // bn), [df, w_down, a_up],
               [pl.BlockSpec((bm, D), lambda i, j: (i, 0)), pl.BlockSpec((bn, D), lambda i, j: (j, 0)), blk],
               [_sds((L, F), BF16)], [blk], NT, 1, epilogue)


ANY = pl.BlockSpec(memory_space=pl.ANY)
HBM = pl.BlockSpec(memory_space=pltpu.HBM)
SEM = pl.BlockSpec(memory_space=pltpu.SEMAPHORE)
EFFECT = pltpu.SideEffectType.DATAFLOW_SIDE_EFFECTING
N_CHIPS = 4


def _place():
    x, y, c = lax.axis_index("x"), lax.axis_index("y"), lax.axis_index("c")
    return x, y, c


def _chip_at(x, y, k):
    px = 1 - x if k & 2 else x
    py = 1 - y if k & 1 else y
    return px, py


def _cast_into_slab(w2d, chip, dtype):
    R, C = w2d.shape
    T = _elem_tile(R, C)

    def body(p_ref, w_ref, o_ref):
        o_ref[...] = w_ref[...].astype(dtype)

    return _call(
        body, (w2d,), name="cast_into_slab", grid=(R // T,), scalars=jnp.reshape(chip, (1,)).astype(jnp.int32),
        in_specs=[pl.BlockSpec((T, C), lambda i, p: (i, 0))],
        out_specs=pl.BlockSpec((None, T, C), lambda i, p: (p[0], i, 0)),
        out_shape=_sds((N_CHIPS, R, C), dtype), compiler_params=_params(1))


TOKEN = jax.ShapeDtypeStruct((8, LANES), F32)


class _Sems:
    def __init__(self, items, shape):
        self.items, self.shape = list(items), tuple(shape)

    def pair(self, idx):
        flat = 0
        for i, n in zip(idx, self.shape):
            flat = flat * n + i
        half = len(self.items) // 2
        return self.items[flat], self.items[half + flat]


def _sem_count(shape):
    n = 1
    for s in shape:
        n *= s
    return n


def _remote(src, dst, sems, idx, device):
    send, recv = sems.pair(idx)
    return pltpu.make_async_remote_copy(src_ref=src, dst_ref=dst, send_sem=send, recv_sem=recv,
                                        device_id=device, device_id_type=MESH)


def _thru(arrays):
    return ([pltpu.with_memory_space_constraint(a, pltpu.HBM) for a in arrays],
            [pltpu.HBM(a.shape, a.dtype) for a in arrays])


def _comm_start(name, arrays, sem_shape, plan):
    na, ns = len(arrays), 2 * _sem_count(sem_shape)

    def body(*refs):
        sems, token = _Sems(refs[na:na + ns], sem_shape), refs[-1]
        for src, dst, idx, device in plan(refs[:na])[0]:
            _remote(src, dst, sems, idx, device).start()
        token[...] = jnp.zeros(token.shape, F32)

    ins, outs = _thru(arrays)
    res = _call(
        body, ins, name=name, in_specs=[HBM] * na, mark=-1, follows=False,
        out_specs=[SEM] * ns + [HBM] * na + [pl.BlockSpec(memory_space=pltpu.VMEM)],
        out_shape=[pltpu.SemaphoreType.DMA(())] * ns + outs + [TOKEN],
        input_output_aliases={a: ns + a for a in range(na)},
        compiler_params=pltpu.CompilerParams(has_side_effects=EFFECT))
    return _Sems(res[:ns], sem_shape), list(res[ns:ns + na])


def _wait_plans(refs, sem_refs, waits):
    x, y, c = _place()
    at = 0
    for sems, plan in waits:
        here = _Sems(sem_refs[at:at + len(sems.items)], sems.shape)
        at += len(sems.items)
        _, mine, arrivals = plan(refs)
        for dst, idx in arrivals:
            _remote(dst, dst, here, idx, (x, y, c)).wait_recv()
        for src, idx in mine:
            _remote(src, src, here, idx, (x, y, c)).wait_send()


def _comm_wait(name, arrays, waits):
    na = len(arrays)
    sem_items = [s for sems, _ in waits for s in sems.items]
    ns = len(sem_items)

    def body(*refs):
        _wait_plans(refs[:na], refs[na:na + ns], waits)
        refs[-1][...] = jnp.zeros(refs[-1].shape, F32)

    ins, outs = _thru(arrays)
    res = _call(
        body, ins + sem_items, name=name, in_specs=[HBM] * na + [SEM] * ns, mark=-1,
        out_specs=[HBM] * na + [pl.BlockSpec(memory_space=pltpu.VMEM)], out_shape=outs + [TOKEN],
        input_output_aliases={a: a for a in range(na)},
        compiler_params=pltpu.CompilerParams(has_side_effects=EFFECT))
    return list(res[:na])


def _comm_relay(name, arrays, sems, plan, sem_shape, next_plan):
    na, ns_in, ns_out = len(arrays), len(sems.items), 2 * _sem_count(sem_shape)

    def body(*refs):
        bufs = refs[:na]
        sems_in = _Sems(refs[na:na + ns_in], sems.shape)
        sems_out = _Sems(refs[na + ns_in:na + ns_in + ns_out], sem_shape)
        x, y, c = _place()
        _, mine, arrivals = plan(bufs)
        onward = next_plan(bufs)[0]
        for dst, idx in arrivals:
            _remote(dst, dst, sems_in, idx, (x, y, c)).wait_recv()
            for src, to, idx2, device, after_idx in onward:
                if after_idx == idx:
                    _remote(src, to, sems_out, idx2, device).start()
        for src, idx in mine:
            _remote(src, src, sems_in, idx, (x, y, c)).wait_send()
        refs[-1][...] = jnp.zeros(refs[-1].shape, F32)

    ins, outs = _thru(arrays)
    res = _call(
        body, ins + sems.items, name=name, in_specs=[HBM] * na + [SEM] * ns_in, mark=-1,
        out_specs=[SEM] * ns_out + [HBM] * na + [pl.BlockSpec(memory_space=pltpu.VMEM)],
        out_shape=[pltpu.SemaphoreType.DMA(())] * ns_out + outs + [TOKEN],
        input_output_aliases={a: ns_out + a for a in range(na)},
        compiler_params=pltpu.CompilerParams(has_side_effects=EFFECT))
    return _Sems(res[:ns_out], sem_shape), list(res[ns_out:ns_out + na])


def _half(ref, q, which):
    h = ref.shape[1] // 2
    return ref.at[q, pl.ds(which * h, h)]


def _quarter(ref, q, half, which):
    h = ref.shape[1] // 2
    return ref.at[q, pl.ds(half * h + which * (h // 2), h // 2)]


def _gather_plan(n_halved):
    def plan(refs):
        x, y, c = _place()
        p = 2 * x + y
        starts, mine, arrivals = [], [], []
        for n, ref in enumerate(refs):
            for k in range(1, N_CHIPS if n >= n_halved else 3):
                px, py = _chip_at(x, y, k)
                q = 2 * px + py
                out = _half(ref, p, c) if n < n_halved else ref.at[p]
                inc = _half(ref, q, c) if n < n_halved else ref.at[q]
                starts.append((out, out, (n, k - 1), (px, py, c)))
                mine.append((out, (n, k - 1)))
                arrivals.append((inc, (n, k - 1)))
        return starts, mine, arrivals
    return plan


def _spread_plan(n_halved, part):
    def plan(refs):
        x, y, c = _place()
        p = 2 * x + y
        starts, mine, arrivals = [], [], []
        for n in range(n_halved):
            for k in (1, 2):
                px, py = _chip_at(x, y, k)
                q = 2 * px + py
                tx, ty = _chip_at(x, y, 3 - k)
                dx, dy = _chip_at(x, y, 3)
                piece = _quarter(refs[n], q, c, 2 - k)
                landed = _half(refs[n], q, c)
                starts.append((piece, piece, (0, n, k - 1), (tx, ty, c), (n, k - 1)))
                starts.append((landed, landed, (1, n, k - 1), (x, y, 1 - c), (n, k - 1)))
                if part != 1:
                    mine.append((piece, (0, n, k - 1)))
                    arrivals.append((_quarter(refs[n], 2 * dx + dy, c, k - 1), (0, n, 2 - k)))
                if part != 0:
                    mine.append((landed, (1, n, k - 1)))
                    arrivals.append((_half(refs[n], q, 1 - c), (1, n, k - 1)))
        return starts, mine, arrivals
    return plan


def _last_hand_on_plan(n_halved):
    def plan(refs):
        x, y, c = _place()
        dx, dy = _chip_at(x, y, 3)
        d = 2 * dx + dy
        starts, mine, arrivals = [], [], []
        for n in range(n_halved):
            for k in (1, 2):
                piece = _quarter(refs[n], d, c, k - 1)
                starts.append((piece, piece, (n, k - 1), (x, y, 1 - c), (0, n, 2 - k)))
                mine.append((piece, (n, k - 1)))
                arrivals.append((_quarter(refs[n], d, 1 - c, k - 1), (n, k - 1)))
        return starts, mine, arrivals
    return plan


def _swap_plan(n):
    def plan(refs):
        x, y, c = _place()
        starts, mine, arrivals = [], [], []
        for a in range(n):
            h = refs[a].shape[1] // 2
            src = refs[a].at[:, pl.ds((1 - c) * h, h)]
            starts.append((src, refs[n + a], (a,), (x, y, 1 - c)))
            mine.append((src, (a,)))
            arrivals.append((refs[n + a], (a,)))
        return starts, mine, arrivals
    return plan


def _scatter_plan(n):
    def plan(refs):
        x, y, c = _place()
        starts, mine, arrivals = [], [], []
        for a in range(n):
            for k in range(1, N_CHIPS):
                px, py = _chip_at(x, y, k)
                src = refs[a].at[2 * px + py]
                starts.append((src, refs[n + a].at[k - 1], (a, k - 1), (px, py, c)))
                mine.append((src, (a, k - 1)))
                arrivals.append((refs[n + a].at[k - 1], (a, k - 1)))
        return starts, mine, arrivals
    return plan


def _share_plan(n):
    def plan(refs):
        x, y, c = _place()
        starts, mine, arrivals = [], [], []
        for a in range(n):
            h = refs[a].shape[0] // 2
            own = refs[a].at[pl.ds(c * h, h)]
            starts.append((own, own, (a,), (x, y, 1 - c)))
            mine.append((own, (a,)))
            arrivals.append((refs[a].at[pl.ds((1 - c) * h, h)], (a,)))
        return starts, mine, arrivals
    return plan


def _gather_packs(pack):
    n_dev = 8

    def body(in_ref, out_ref, send, recv, lsem):
        x, y, c = _place()
        me = 4 * x + 2 * y + c
        sends = []
        for r in range(1, n_dev):
            peer = (1 - x if r & 4 else x, 1 - y if r & 2 else y, 1 - c if r & 1 else c)
            cp = pltpu.make_async_remote_copy(
                src_ref=in_ref, dst_ref=out_ref.at[me], send_sem=send.at[r - 1], recv_sem=recv.at[r - 1],
                device_id=peer, device_id_type=MESH)
            cp.start()
            sends.append(cp)
        mine = pltpu.make_async_copy(in_ref, out_ref.at[me], lsem)
        mine.start()
        for r in range(1, n_dev):
            peer = (1 - x if r & 4 else x, 1 - y if r & 2 else y, 1 - c if r & 1 else c)
            src = 4 * peer[0] + 2 * peer[1] + peer[2]
            pltpu.make_async_remote_copy(
                src_ref=in_ref, dst_ref=out_ref.at[src], send_sem=send.at[r - 1], recv_sem=recv.at[r - 1],
                device_id=peer, device_id_type=MESH).wait_recv()
        for cp in sends:
            cp.wait_send()
        mine.wait()

    return _call(
        body, (pack,), name="gather_packs", in_specs=[ANY], out_specs=ANY,
        out_shape=_sds((n_dev,) + pack.shape, pack.dtype),
        scratch_shapes=[pltpu.SemaphoreType.DMA((n_dev - 1,)), pltpu.SemaphoreType.DMA((n_dev - 1,)),
                        pltpu.SemaphoreType.DMA],
        compiler_params=pltpu.CompilerParams(has_side_effects=True))


class _Reduction:
    def __init__(self, tag, slabs, c_idx, chip):
        self.tag, self.n, self.c_idx, self.chip = tag, len(slabs), c_idx, chip
        lands = [lax.empty((g.shape[0], g.shape[1] // 2, g.shape[2]), g.dtype) for g in slabs]
        self.sems = _comm_start("swap_start_" + tag, list(slabs) + lands, (self.n,), _swap_plan(self.n))

    def partial(self):
        n = self.n
        sems, bufs = self.sems
        bufs = _comm_wait("swap_wait_" + self.tag, bufs, [(sems, _swap_plan(n))])
        both = [_chip_partial(g, r, self.c_idx, self.chip) for g, r in zip(bufs[:n], bufs[n:])]
        self.own = [o for _, o in both]
        parts = [p for p, _ in both]
        lands = [lax.empty((N_CHIPS - 1,) + p.shape[1:], p.dtype) for p in parts]
        self.sems = _comm_start("scatter_start_" + self.tag, parts + lands, (n, N_CHIPS - 1), _scatter_plan(n))

    def total(self):
        n = self.n
        sems, bufs = self.sems
        bufs = _comm_wait("scatter_wait_" + self.tag, bufs, [(sems, _scatter_plan(n))])
        fulls = [_sum_partials(o, r, self.c_idx) for o, r in zip(self.own, bufs[n:])]
        self.sems = _comm_start("share_start_" + self.tag, fulls, (n,), _share_plan(n))

    def finish(self):
        sems, bufs = self.sems
        return _comm_wait("share_wait_" + self.tag, bufs, [(sems, _share_plan(self.n))])


def _elem_tile(rows, cols):
    return _pick(rows, max(8, (1 << 19) // cols // 8 * 8))


def _chip_partial(grad, recv, c_idx, p_idx):
    Q, R, C = grad.shape
    h = R // 2
    T = _elem_tile(h, C)
    nt = h // T

    def body(sc_ref, g_ref, r_ref, sb_ref, own_ref):
        q = pl.program_id(1)
        s = g_ref[...] + r_ref[...]
        sb_ref[...] = s.astype(BF16)

        @pl.when(q == sc_ref[1])
        def _():
            own_ref[...] = s

    return _call(
        body, (grad, recv), name="chip_partial", grid=(nt, Q),
        scalars=jnp.stack([c_idx, p_idx]).astype(jnp.int32),
        in_specs=[pl.BlockSpec((None, T, C), lambda t, q, sc: (q, sc[0] * nt + t, 0)),
                  pl.BlockSpec((None, T, C), lambda t, q, sc: (q, t, 0))],
        out_specs=[pl.BlockSpec((None, T, C), lambda t, q, sc: (q, t, 0)),
                   pl.BlockSpec((T, C), lambda t, q, sc: (t, 0))],
        out_shape=[_sds((Q, h, C), BF16), _sds((h, C), F32)], compiler_params=_params(2))


def _sum_partials(own, parts, c_idx):
    h, C = own.shape
    T = _elem_tile(h, C)
    nt = h // T

    def body(c_ref, o_ref, p_ref, t_ref):
        t = o_ref[...]
        for k in range(N_CHIPS - 1):
            t = t + p_ref[k].astype(F32)
        t_ref[...] = t

    return _call(
        body, (own, parts), name="sum_partials", grid=(nt,), scalars=jnp.reshape(c_idx, (1,)).astype(jnp.int32),
        in_specs=[pl.BlockSpec((T, C), lambda i, c: (i, 0)),
                  pl.BlockSpec((N_CHIPS - 1, T, C), lambda i, c: (0, i, 0))],
        out_specs=pl.BlockSpec((T, C), lambda i, c: (c[0] * nt + i, 0)),
        out_shape=_sds((2 * h, C), F32), compiler_params=_params(1))


def _pack_rows(name, parts):
    width = parts[0].shape[1]
    offsets, at = [], 0
    for p in parts:
        offsets.append(at)
        at += p.shape[0]
    total = -(-at // 8) * 8

    def body(*refs):
        out = refs[-1]
        out[...] = jnp.zeros(out.shape, F32)
        for ref, o in zip(refs[:-1], offsets):
            out[o:o + ref.shape[0], :] = ref[...]

    whole = pl.BlockSpec(memory_space=pltpu.VMEM)
    return _call(body, list(parts), name=name, in_specs=[whole] * len(parts), out_specs=whole,
                 out_shape=_sds((total, width), F32))


def _sum_packs(packs):
    n, R, C = packs.shape

    def body(p_ref, o_ref):
        t = p_ref[0]
        for k in range(1, n):
            t = t + p_ref[k]
        o_ref[...] = t

    return _call(
        body, (packs,), name="sum_packs", grid=(1,), in_specs=[pl.BlockSpec((n, R, C), lambda i: (0, 0, 0))],
        out_specs=pl.BlockSpec((R, C), lambda i: (0, 0)), out_shape=_sds((R, C), F32), compiler_params=_params(1))


def _adamw(w, g, m, v):
    R, C = w.shape
    T = _elem_tile(R, C)

    def body(w_ref, g_ref, m_ref, v_ref, d_ref, m2_ref, v2_ref):
        g_ = g_ref[...]
        m2 = ADAM_B1 * m_ref[...] + (1.0 - ADAM_B1) * g_
        v2 = ADAM_B2 * v_ref[...] + (1.0 - ADAM_B2) * (g_ * g_)
        m_hat = m2 / (1.0 - ADAM_B1 ** ADAM_STEP)
        v_hat = v2 / (1.0 - ADAM_B2 ** ADAM_STEP)
        d_ref[...] = -ADAM_LR * (m_hat / (jnp.sqrt(v_hat) + ADAM_EPS) + ADAM_WD * w_ref[...])
        m2_ref[...] = m2
        v2_ref[...] = v2

    blk = pl.BlockSpec((T, C), lambda i: (i, 0))
    return _call(
        body, (w, g, m, v), name="adamw", grid=(R // T,), in_specs=[blk] * 4, out_specs=[blk] * 3,
        out_shape=[_sds((R, C), F32)] * 3, compiler_params=_params(1))


def _adamw_rows(w, g, m, v, row0, prev=None):
    R, C = w.shape
    T = _elem_tile(g.shape[0], C)
    off = row0 // T

    def body(w_ref, g_ref, m_ref, v_ref, *rest):
        d_ref, m2_ref, v2_ref, g2_ref = rest[-4:]
        g_ = g_ref[...]
        m2 = ADAM_B1 * m_ref[...] + (1.0 - ADAM_B1) * g_
        v2 = ADAM_B2 * v_ref[...] + (1.0 - ADAM_B2) * (g_ * g_)
        m_hat = m2 / (1.0 - ADAM_B1 ** ADAM_STEP)
        v_hat = v2 / (1.0 - ADAM_B2 ** ADAM_STEP)
        d_ref[...] = -ADAM_LR * (m_hat / (jnp.sqrt(v_hat) + ADAM_EPS) + ADAM_WD * w_ref[...])
        m2_ref[...] = m2
        v2_ref[...] = v2
        g2_ref[...] = g_

    here = pl.BlockSpec((T, C), lambda i: (off + i, 0))
    piece = pl.BlockSpec((T, C), lambda i: (i, 0))
    done = tuple(prev or ())
    return _call(
        body, (w, g, m, v) + done, name="adamw_rows", grid=(g.shape[0] // T,), follows=prev is None,
        in_specs=[here, piece, here, here] + [ANY] * len(done), out_specs=[here] * 4,
        out_shape=[_sds((R, C), F32)] * 4, input_output_aliases={4 + j: j for j in range(len(done))},
        compiler_params=_params(1))


BIG = ("w_in", "w_pool_out", "w_conv_out", "w_o", "w_up", "w_down", "w_pool_grp")
VECTORS = ("g_pre_mix", "pool_scale", "b_dw", "conv_ln_g", "conv_ln_b", "g_post_mix", "g_pre_mlp", "g_post_mlp")
WEIGHTS = ("meta", "g_pre_mix", "w_in", "w_pool_grp", "pool_scale", "w_pool_out", "w_dw", "b_dw", "conv_ln_g",
           "conv_ln_b", "w_conv_out", "w_o", "g_post_mix", "g_pre_mlp", "w_up", "w_down", "g_post_mlp")


def _as_rows(a, width):
    r, cols = a.shape
    return a.reshape(r * (cols // width), width)


def _step(w, m, v, x, tgt):
    S, D = x.shape
    P = D // 2
    xi, yi, ci = _place()
    chip = 2 * xi + yi
    _CHAIN["after"] = None

    C = D // 2
    G = POOL_GROUPS
    GD = P // G
    GS = GD // N_CHIPS
    Q = N_CHIPS
    vecs = {k: w[k] for k in VECTORS}
    shard2d = {k: w[k].reshape(-1, w[k].shape[-1]) for k in BIG}
    grads, delta, new_m, new_v = {}, {}, {}, {}

    def update(names, reduced):
        for k, g in zip(names, reduced):
            delta[k], new_m[k], new_v[k], grads[k] = _adamw_rows(
                shard2d[k], g, m[k].reshape(shard2d[k].shape), v[k].reshape(shard2d[k].shape), 0)

    groups = (("a", ("w_in", "w_pool_grp"), ("w_dw", "meta")), ("b", ("w_pool_out", "w_conv_out", "w_o"), ()),
              ("c", ("w_up",), ()), ("d", ("w_down",), ()))
    flying = {}
    for tag, halved, whole in groups:
        slabs = [_cast_into_slab(shard2d[k], chip, BF16) for k in halved]
        slabs += [_cast_into_slab(w[k], chip, F32) for k in whole]
        flying[tag] = _comm_start("gather_start_" + tag, slabs, (len(slabs), N_CHIPS - 1), _gather_plan(len(halved)))

    def landed(tag):
        _, halved, whole = next(g for g in groups if g[0] == tag)
        nh = len(halved)
        sems, bufs = flying.pop(tag)
        spread, bufs = _comm_relay("gather_relay_" + tag, bufs, sems, _gather_plan(nh), (2, nh, 2),
                                   _spread_plan(nh, 0))
        last, tree = _comm_relay("gather_relay2_" + tag, bufs[:nh], spread, _spread_plan(nh, 0), (nh, 2),
                                 _last_hand_on_plan(nh))
        done = _comm_wait("gather_wait_" + tag, tree,
                          [(spread, _spread_plan(nh, 1)), (last, _last_hand_on_plan(nh))])
        return dict(zip(halved + whole, done + bufs[nh:]))

    got = landed("a")
    win_g = got["w_in"]
    w_grp = got["w_pool_grp"].reshape(N_CHIPS, G, GS, GD).transpose(1, 0, 2, 3).reshape(G, GD, GD)
    w_dw = got["w_dw"].transpose(1, 0, 2).reshape(CONV_TAPS, P)
    meta = got["meta"].transpose(1, 0, 2).reshape(N_META, D)
    h0 = jnp.concatenate([jnp.zeros((PAD_ROWS, D), F32), meta, x], axis=0)
    u1 = _pre_norm(h0, vecs["g_pre_mix"])
    proj = _mm_act_colw("proj", u1, win_g, 256)
    d, ya_pre = _pool_fwd(proj, w_grp, vecs["pool_scale"])
    s, c = _conv_fwd(proj, C, w_dw, vecs["b_dw"], vecs["conv_ln_g"], vecs["conv_ln_b"])
    got = landed("b")
    wpo_g, wco_g, w_o = got["w_pool_out"], got["w_conv_out"], got["w_o"].reshape(D, D)
    mix, ya, yb = _mix_fwd(ya_pre, s, wpo_g, wco_g, proj, D)
    o = _mm_act_roww("attn_out", mix, w_o, 1088, 1024, 2048)
    h1, u2 = _mid_norm(o, h0, vecs["g_post_mix"], vecs["g_pre_mlp"])
    wup_g = landed("c")["w_up"]
    a_up, fact = _mm_act_colw("mlp_up", u2, wup_g, 512, _up_epilogue, (F32, BF16))
    w_down = landed("d")["w_down"].reshape(-1, D)
    f = _mm_act_roww("mlp_down", fact, w_down, 1088, 1024, 2048)
    dy, df, dg_post_mlp, loss = _loss_head(f, h1, tgt, vecs["g_post_mlp"])

    g_w_down = _mm_wgrad("dw_down", fact, df, 1024, 1024)
    red1 = _Reduction("1", [g_w_down.reshape(N_CHIPS, -1, D)], ci, chip)
    da_up = _mlp_down_bwd(df, w_down, a_up)
    red1.partial()
    g_w_up = _mm_wgrad_colw("dw_up", u2, da_up, Q, 1024, 1024)
    red2 = _Reduction("2", [g_w_up], ci, chip)
    du2 = _mm_grad_colw_t("du2", da_up, wup_g, 1088, 1024)
    red2.partial()
    dh1, do, dg_pre_mlp, dg_post_mix = _mid_norm_bwd(dy, du2, h1, o, vecs["g_pre_mlp"], vecs["g_post_mix"])
    g_w_o = _mm_wgrad("dw_o", mix, do, 1024, 1024)
    red1.total()
    dya, dyb, dga, dgb = _mix_bwd(do, w_o, proj, ya, yb)
    update(("w_down",), red1.finish())
    g_wpo = _mm_wgrad_colw("dw_pool_out", ya_pre, dya, Q, 1024, 512)
    g_wco = _mm_wgrad_colw("dw_conv_out", s, dyb, Q, 1024, 512)
    red3 = _Reduction("3", [g_w_o.reshape(N_CHIPS, D // N_CHIPS, D), g_wpo, g_wco], ci, chip)
    dya_pre = _mm_grad_colw_t("dya_pre", dya, wpo_g, 1088, 1024)
    ds = _mm_grad_colw_t("ds", dyb, wco_g, 1088, 1024)
    red3.partial()
    dz, g_w_grp, dscale = _pool_bwd(dya_pre, d, w_grp, vecs["pool_scale"])
    dc, dln_g, dln_b, db_dw = _conv_ln_bwd(ds, c, vecs["conv_ln_g"], vecs["conv_ln_b"])
    red2.total()
    dv, dgc, g_w_dw = _conv_bwd(dc, proj, C, w_dw)
    update(("w_up",), red2.finish())
    dproj = jnp.concatenate([dz, dv, dgc, dga, dgb], axis=1)
    half_k = D // 2
    g_w_grp = g_w_grp.reshape(G, N_CHIPS, GS, GD).transpose(1, 0, 2, 3).reshape(N_CHIPS, G * GS, GD)
    g_in_a = _mm_wgrad_colw("dw_in_a", u1, dproj, Q, 512, 1792, rows=(0, half_k))
    red4a = _Reduction("4a", [g_in_a, g_w_grp], ci, chip)
    g_in_b = _mm_wgrad_colw("dw_in_b", u1, dproj, Q, 512, 1792, rows=(half_k, half_k))
    red4a.partial()
    red4b = _Reduction("4b", [g_in_b], ci, chip)
    du1 = _mm_grad_colw_t("du1", dproj, win_g, 1088, 1024)
    red4b.partial()
    red3.total()
    grad_x, dmeta, dg_pre_mix = _pre_norm_bwd(dh1, du1, h0, vecs["g_pre_mix"])
    red4a.total()
    update(("w_o", "w_pool_out", "w_conv_out"), red3.finish())
    red_in_a, red_grp = red4a.finish()
    update(("w_pool_grp",), [red_grp])
    w_in_rows = (shard2d["w_in"], m["w_in"].reshape(shard2d["w_in"].shape), v["w_in"].reshape(shard2d["w_in"].shape))
    first_rows = _adamw_rows(w_in_rows[0], red_in_a, w_in_rows[1], w_in_rows[2], 0)
    g_vec = dict(g_pre_mix=dg_pre_mix, pool_scale=dscale, b_dw=db_dw, conv_ln_g=dln_g, conv_ln_b=dln_b,
                 g_post_mix=dg_post_mix, g_pre_mlp=dg_pre_mlp, g_post_mlp=dg_post_mlp)

    rows = [g_w_dw, _as_rows(dmeta, P)] + [_as_rows(g_vec[k], P) for k in VECTORS]
    rows.append(jnp.broadcast_to(loss[:, :1], (1, P)))
    total = _sum_packs(_gather_packs(_pack_rows("pack_grads", rows)))
    at = 0
    taps_pad = g_w_dw.shape[0]
    g_dw_full = total[at:at + CONV_TAPS]
    at += taps_pad
    g_meta_full = total[at:at + 2 * N_META].reshape(N_META, D)
    at += 2 * N_META
    for k in VECTORS:
        n = w[k].shape[-1] // P
        grads[k] = total[at:at + n].reshape(1, n * P)
        at += n
    loss_total = total[at, 0]
    grads["w_dw"] = lax.dynamic_slice_in_dim(g_dw_full, chip * (P // N_CHIPS), P // N_CHIPS, axis=1)
    grads["meta"] = lax.dynamic_slice_in_dim(g_meta_full, chip * (D // N_CHIPS), D // N_CHIPS, axis=1)

    small_names = [k for k in WEIGHTS if k not in BIG]

    def pack_small(tree):
        parts = []
        for k in small_names:
            a = tree[k].reshape(-1, tree[k].shape[-1])
            flat = a.reshape(-1)
            parts.append(jnp.pad(flat, (0, -flat.shape[0] % P)).reshape(-1, P))
        return _pack_rows("pack_small", parts)

    sd, sm, sv = _adamw(pack_small(w), pack_small(grads), pack_small(m), pack_small(v))
    at = 0
    for k in small_names:
        a = w[k].reshape(-1, w[k].shape[-1])
        n = -(-a.size // P)
        for tree, packed in ((delta, sd), (new_m, sm), (new_v, sv)):
            tree[k] = packed[at:at + n].reshape(-1)[:a.size].reshape(a.shape)
        at += n

    red4b.total()
    delta["w_in"], new_m["w_in"], new_v["w_in"], grads["w_in"] = _adamw_rows(
        w_in_rows[0], red4b.finish()[0], w_in_rows[1], w_in_rows[2], half_k, prev=first_rows)
    return loss_total, grad_x, grads, delta, new_m, new_v


def kernel(x, meta, g_pre_mix, w_in, w_pool_grp, pool_scale, w_pool_out, w_dw, b_dw, conv_ln_g, conv_ln_b, w_conv_out, w_o, g_post_mix, g_pre_mlp, w_up, w_down, g_post_mlp, loss_target, m_meta, m_g_pre_mix, m_w_in, m_w_pool_grp, m_pool_scale, m_w_pool_out, m_w_dw, m_b_dw, m_conv_ln_g, m_conv_ln_b, m_w_conv_out, m_w_o, m_g_post_mix, m_g_pre_mlp, m_w_up, m_w_down, m_g_post_mlp, v_meta, v_g_pre_mix, v_w_in, v_w_pool_grp, v_pool_scale, v_w_pool_out, v_w_dw, v_b_dw, v_conv_ln_g, v_conv_ln_b, v_w_conv_out, v_w_o, v_g_post_mix, v_g_pre_mlp, v_w_up, v_w_down, v_g_post_mlp):
    args = dict(locals())
    shapes = {k: args[k].shape for k in WEIGHTS}
    w = {k: args[k] for k in WEIGHTS}
    m = {k: args["m_" + k] for k in WEIGHTS}
    v = {k: args["v_" + k] for k in WEIGHTS}
    for tree in (w, m, v):
        tree["w_dw"] = tree["w_dw"].reshape(tree["w_dw"].shape[-2:])
    loss, grad_x, grads, delta, new_m, new_v = _step(w, m, v, x[0], loss_target[0])
    out = [loss, grad_x[None]]
    for tree in (grads, delta, new_m, new_v):
        out += [tree[k].reshape(shapes[k]) for k in WEIGHTS]
    return tuple(out)
```

```python
import jax
import jax.numpy as jnp
from jax import lax
from jax.experimental import pallas as pl
from jax.experimental.pallas import tpu as pltpu

F32 = jnp.float32
BF16 = jnp.bfloat16

N_META = 16
PAD_ROWS = 112
TOKEN_ROW0 = PAD_ROWS + N_META
POOL_GROUPS = 4
CONV_TAPS = 31
HALO = 32
CONV_ROWS = 128
LANES = 128
RMS_EPS = 1e-6
LN_EPS = 1e-5
ADAM_LR = 0.001
ADAM_B1 = 0.9
ADAM_B2 = 0.999
ADAM_EPS = 1e-08
ADAM_WD = 0.01
ADAM_STEP = 10
VMEM_LIMIT_MB = 56

MESH = pl.DeviceIdType.MESH
NN = (((1,), (0,)), ((), ()))
NT = (((1,), (1,)), ((), ()))
TN = (((0,), (0,)), ((), ()))


def _pick(n, pref):
    if n <= pref:
        return n
    if n % pref == 0:
        return pref
    for step in (LANES, 8, 1):
        t = (pref // step) * step
        while t >= step:
            if n % t == 0:
                return t
            t -= step
    return n


def _params(n_axes, vmem_mb=VMEM_LIMIT_MB):
    return pltpu.CompilerParams(dimension_semantics=("arbitrary",) * n_axes,
                                vmem_limit_bytes=vmem_mb << 20)


def _sigmoid(x):
    return jax.nn.sigmoid(x)


_CHAIN = {"after": None}


def _call(body, args, *, in_specs, out_specs, out_shape, grid=(), scalars=None, mark=0, follows=True, **kw):
    after = _CHAIN["after"] if follows else None
    n = len(args)
    lead = 0 if scalars is None else 1
    specs = list(in_specs)
    operands = list(args)
    fn = body
    if after is not None:
        def fn(*refs):
            body(*refs[:lead + n], *refs[lead + n + 1:])
        specs.append(pl.BlockSpec(memory_space=pl.ANY))
        operands.append(after)
    if scalars is None:
        if grid:
            kw["grid"] = grid
        res = pl.pallas_call(fn, in_specs=specs, out_specs=out_specs, out_shape=out_shape, **kw)(*operands)
    else:
        grid_spec = pltpu.PrefetchScalarGridSpec(num_scalar_prefetch=1, grid=grid, in_specs=specs, out_specs=out_specs)
        res = pl.pallas_call(fn, grid_spec=grid_spec, out_shape=out_shape, **kw)(scalars, *operands)
    outs = res if isinstance(res, (list, tuple)) else [res]
    _CHAIN["after"] = outs[mark]
    return res


def _store(val, extras, outs):
    outs[0][...] = val.astype(outs[0].dtype)


def _mm(name, grid, arrays, in_specs, out_shapes, out_specs, dims, nk, epilogue=_store, acc_shape=None):
    n_in, n_out = len(arrays), len(out_shapes)

    def body(*refs):
        extras = refs[2:n_in]
        outs = refs[n_in:n_in + n_out]
        part = lax.dot_general(refs[0][...], refs[1][...], dims, preferred_element_type=F32)
        if nk == 1:
            epilogue(part, extras, outs)
        else:
            acc = refs[n_in + n_out]
            k = pl.program_id(len(grid) - 1)

            @pl.when(k == 0)
            def _():
                acc[...] = part

            @pl.when(k > 0)
            def _():
                acc[...] += part

            @pl.when(k == nk - 1)
            def _():
                epilogue(acc[...], extras, outs)

    scratch = [pltpu.VMEM(acc_shape, F32)] if nk > 1 else []
    single = n_out == 1
    return _call(
        body, arrays, name=name, grid=grid, in_specs=in_specs,
        out_specs=out_specs[0] if single else out_specs,
        out_shape=out_shapes[0] if single else out_shapes,
        scratch_shapes=scratch, compiler_params=_params(len(grid)))


def _sds(shape, dtype):
    return jax.ShapeDtypeStruct(shape, dtype)


def _rms_scale(h):
    return lax.rsqrt(jnp.mean(h * h, axis=-1, keepdims=True) + RMS_EPS)


def _rms_bwd(du, h, g):
    r = _rms_scale(h)
    y = h * r
    dy = du * g
    dh = r * (dy - y * jnp.mean(dy * y, axis=-1, keepdims=True))
    return dh, jnp.sum(du * y, axis=0, keepdims=True)


def _row_tile(L):
    return _pick(L, 272)


def _pre_norm(h0, g):
    L, D = h0.shape
    T = _row_tile(L)

    def body(h_ref, g_ref, u_ref):
        h = h_ref[...]
        u_ref[...] = (h * _rms_scale(h) * g_ref[...]).astype(BF16)

    return _call(
        body, (h0, g), name="pre_norm", grid=(L // T,),
        in_specs=[pl.BlockSpec((T, D), lambda i: (i, 0)), pl.BlockSpec((1, D), lambda i: (0, 0))],
        out_specs=pl.BlockSpec((T, D), lambda i: (i, 0)),
        out_shape=_sds((L, D), BF16), compiler_params=_params(1))


def _mid_norm(o, h0, g_post, g_pre):
    L, D = h0.shape
    T = _row_tile(L)

    def body(o_ref, h_ref, gp_ref, gm_ref, h1_ref, u2_ref):
        o_ = o_ref[...]
        h1 = h_ref[...] + o_ * _rms_scale(o_) * gp_ref[...]
        h1_ref[...] = h1
        u2_ref[...] = (h1 * _rms_scale(h1) * gm_ref[...]).astype(BF16)

    row = pl.BlockSpec((T, D), lambda i: (i, 0))
    vec = pl.BlockSpec((1, D), lambda i: (0, 0))
    return _call(
        body, (o, h0, g_post, g_pre), name="mid_norm", grid=(L // T,),
        in_specs=[row, row, vec, vec], out_specs=[row, row],
        out_shape=[_sds((L, D), F32), _sds((L, D), BF16)], compiler_params=_params(1))


def _loss_head(f, h1, tgt, g_post):
    L, D = h1.shape
    T = TOKEN_ROW0
    n = L // T

    def body(f_ref, h_ref, t_ref, g_ref, dy_ref, df_ref, dg_ref, loss_ref):
        i = pl.program_id(0)
        f_ = f_ref[...]
        g = g_ref[...]
        y = h_ref[...] + f_ * _rms_scale(f_) * g
        live = (i > 0).astype(F32)
        diff = (y - t_ref[...]) * live
        part = 0.5 * jnp.sum(jnp.mean(diff * diff, axis=-1, keepdims=True), axis=0, keepdims=True)
        dy = diff * (1.0 / D)
        dy_ref[...] = dy
        df, dg = _rms_bwd(dy, f_, g)
        df_ref[...] = df.astype(BF16)

        @pl.when(i == 0)
        def _():
            dg_ref[...] = dg
            loss_ref[...] = jnp.broadcast_to(part, loss_ref.shape)

        @pl.when(i > 0)
        def _():
            dg_ref[...] += dg
            loss_ref[...] += jnp.broadcast_to(part, loss_ref.shape)

    row = pl.BlockSpec((T, D), lambda i: (i, 0))
    vec = pl.BlockSpec((1, D), lambda i: (0, 0))
    return _call(
        body, (f, h1, tgt, g_post), name="loss_head", grid=(n,),
        in_specs=[row, row, pl.BlockSpec((T, D), lambda i: (jnp.maximum(i - 1, 0), 0)), vec],
        out_specs=[row, row, vec, pl.BlockSpec((1, LANES), lambda i: (0, 0))],
        out_shape=[_sds((L, D), F32), _sds((L, D), BF16), _sds((1, D), F32), _sds((1, LANES), F32)],
        compiler_params=_params(1))


def _mid_norm_bwd(dy, du2, h1, o, g_pre, g_post):
    L, D = h1.shape
    T = _row_tile(L)

    def body(dy_ref, du_ref, h_ref, o_ref, gm_ref, gp_ref, dh1_ref, do_ref, dgm_ref, dgp_ref):
        i = pl.program_id(0)
        dh, dgm = _rms_bwd(du_ref[...], h_ref[...], gm_ref[...])
        dh1 = dy_ref[...] + dh
        dh1_ref[...] = dh1
        do, dgp = _rms_bwd(dh1, o_ref[...], gp_ref[...])
        do_ref[...] = do.astype(BF16)

        @pl.when(i == 0)
        def _():
            dgm_ref[...] = dgm
            dgp_ref[...] = dgp

        @pl.when(i > 0)
        def _():
            dgm_ref[...] += dgm
            dgp_ref[...] += dgp

    row = pl.BlockSpec((T, D), lambda i: (i, 0))
    vec = pl.BlockSpec((1, D), lambda i: (0, 0))
    return _call(
        body, (dy, du2, h1, o, g_pre, g_post), name="mid_norm_bwd", grid=(L // T,),
        in_specs=[row, row, row, row, vec, vec], out_specs=[row, row, vec, vec],
        out_shape=[_sds((L, D), F32), _sds((L, D), BF16), _sds((1, D), F32), _sds((1, D), F32)],
        compiler_params=_params(1))


def _pre_norm_bwd(dh1, du1, h0, g):
    L, D = h0.shape
    T = TOKEN_ROW0
    n = L // T

    def body(dh_ref, du_ref, h_ref, g_ref, gx_ref, dmeta_ref, dg_ref):
        i = pl.program_id(0)
        dh, dg = _rms_bwd(du_ref[...], h_ref[...], g_ref[...])
        dh0 = dh_ref[...] + dh
        gx_ref[...] = dh0

        @pl.when(i == 0)
        def _():
            dmeta_ref[...] = dh0[PAD_ROWS:, :]
            dg_ref[...] = dg

        @pl.when(i > 0)
        def _():
            dg_ref[...] += dg

    row = pl.BlockSpec((T, D), lambda i: (i, 0))
    vec = pl.BlockSpec((1, D), lambda i: (0, 0))
    return _call(
        body, (dh1, du1, h0, g), name="pre_norm_bwd", grid=(n,),
        in_specs=[row, row, row, vec],
        out_specs=[pl.BlockSpec((T, D), lambda i: (jnp.maximum(i - 1, 0), 0)),
                   pl.BlockSpec((N_META, D), lambda i: (0, 0)), vec],
        out_shape=[_sds((L - T, D), F32), _sds((N_META, D), F32), _sds((1, D), F32)],
        compiler_params=_params(1))


def _window_sum(z, g, shift_sign, L):
    s = z
    for j in range(POOL_GROUPS):
        k = 1 << j
        nxt = s + pltpu.roll(s, k if shift_sign > 0 else L - k, 0)
        s = jnp.where(j <= g, nxt, s)
    return s


def _inv_count(g, L):
    t = lax.broadcasted_iota(jnp.int32, (L, 1), 0)
    w = jnp.left_shift(2, g)
    cnt = jnp.clip(t - (PAD_ROWS - 1), 1, w)
    return 1.0 / cnt.astype(F32)


def _pool_fwd(proj, w_grp, scale):
    L = proj.shape[0]
    G, GD, _ = w_grp.shape
    P = G * GD

    def body(z_ref, w_ref, sc_ref, d_ref, ya_ref):
        g = pl.program_id(0)
        z = z_ref[...]
        d = (_window_sum(z, g, +1, L) * _inv_count(g, L) - z).astype(BF16)
        d_ref[...] = d
        y = jnp.dot(d, w_ref[...], preferred_element_type=F32)
        ya_ref[...] = (y * sc_ref[...]).astype(BF16)

    col = pl.BlockSpec((L, GD), lambda g: (0, g))
    return _call(
        body, (proj, w_grp, scale), name="pool_fwd", grid=(G,),
        in_specs=[col, pl.BlockSpec((None, GD, GD), lambda g: (g, 0, 0)), pl.BlockSpec((1, GD), lambda g: (0, g))],
        out_specs=[col, col], out_shape=[_sds((L, P), BF16), _sds((L, P), BF16)],
        compiler_params=_params(1))


def _pool_bwd(dya, d, w_grp, scale):
    L, P = dya.shape
    G, GD, _ = w_grp.shape

    def body(dya_ref, d_ref, w_ref, sc_ref, dz_ref, dw_ref, dsc_ref):
        g = pl.program_id(0)
        dya_ = dya_ref[...]
        d_ = d_ref[...]
        w = w_ref[...]
        y = jnp.dot(d_, w, preferred_element_type=F32)
        dsc_ref[...] = jnp.sum(dya_ * y, axis=0, keepdims=True)
        dy = (dya_ * sc_ref[...]).astype(BF16)
        dw_ref[...] = lax.dot_general(d_, dy, TN, preferred_element_type=F32)
        dd = lax.dot_general(dy, w, NT, preferred_element_type=F32)
        dz = _window_sum(dd * _inv_count(g, L), g, -1, L) - dd
        dz_ref[...] = dz.astype(BF16)

    col = pl.BlockSpec((L, GD), lambda g: (0, g))
    wspec = pl.BlockSpec((None, GD, GD), lambda g: (g, 0, 0))
    vec = pl.BlockSpec((1, GD), lambda g: (0, g))
    return _call(
        body, (dya, d, w_grp, scale), name="pool_bwd", grid=(G,),
        in_specs=[col, col, wspec, vec], out_specs=[col, wspec, vec],
        out_shape=[_sds((L, P), BF16), _sds((G, GD, GD), F32), _sds((1, P), F32)],
        compiler_params=_params(1))


def _fill_rotations(rot_ref, ext):
    n = ext.shape[0]
    rot_ref[0] = ext
    for r in range(1, 8):
        rot_ref[r] = pltpu.roll(ext, n - r, 0)


def _lane_chunks(C):
    step = LANES if C % LANES == 0 else C
    return [(c0, step) for c0 in range(0, C, step)]


def _conv_specs(L, C, col_v, col_g):
    T = CONV_ROWS
    per = T // HALO
    cur_v = pl.BlockSpec((T, C), lambda i: (i, col_v))
    cur_g = pl.BlockSpec((T, C), lambda i: (i, col_g))
    prev_v = pl.BlockSpec((HALO, C), lambda i: (jnp.maximum(i * per - 1, 0), col_v))
    prev_g = pl.BlockSpec((HALO, C), lambda i: (jnp.maximum(i * per - 1, 0), col_g))
    return cur_v, cur_g, prev_v, prev_g


def _glu_ext(vc, gc, vh, gh, i):
    a_cur = vc[...] * _sigmoid(gc[...])
    a_prev = vh[...] * _sigmoid(gh[...]) * (i > 0).astype(F32)
    return jnp.concatenate([a_prev, a_cur], axis=0)


def _conv_fwd(proj, C, w_dw, b_dw, ln_g, ln_b):
    L = proj.shape[0]
    T = CONV_ROWS
    P = C

    def body(vc, gc, vh, gh, w_ref, b_ref, lg_ref, lb_ref, s_ref, c_ref, rot):
        i = pl.program_id(0)
        _fill_rotations(rot, _glu_ext(vc, gc, vh, gh, i))
        for c0, cw in _lane_chunks(C):
            acc = jnp.zeros((T, cw), F32)
            for k in range(CONV_TAPS):
                q, r = divmod(HALO - (CONV_TAPS - 1) + k, 8)
                acc = acc + w_ref[k:k + 1, c0:c0 + cw] * rot[r, 8 * q:8 * q + T, c0:c0 + cw]
            c_ref[:, c0:c0 + cw] = acc + b_ref[:, c0:c0 + cw]
        c = c_ref[...]
        mu = jnp.mean(c, axis=-1, keepdims=True)
        cen = c - mu
        var = jnp.mean(cen * cen, axis=-1, keepdims=True)
        ln = cen * lax.rsqrt(var + LN_EPS) * lg_ref[...] + lb_ref[...]
        s_ref[...] = (ln * _sigmoid(ln)).astype(BF16)

    cur_v, cur_g, prev_v, prev_g = _conv_specs(L, C, P // C, P // C + 1)
    row = pl.BlockSpec((T, C), lambda i: (i, 0))
    vec = pl.BlockSpec((1, C), lambda i: (0, 0))
    return _call(
        body, (proj, proj, proj, proj, w_dw, b_dw, ln_g, ln_b), name="conv_fwd", grid=(L // T,),
        in_specs=[cur_v, cur_g, prev_v, prev_g, pl.BlockSpec((CONV_TAPS, C), lambda i: (0, 0)), vec, vec, vec],
        out_specs=[row, row], out_shape=[_sds((L, C), BF16), _sds((L, C), F32)],
        scratch_shapes=[pltpu.VMEM((8, T + HALO, C), F32)], compiler_params=_params(1))


def _conv_ln_bwd(ds, c, ln_g, ln_b):
    L, C = c.shape
    T = _row_tile(L)

    def body(ds_ref, c_ref, lg_ref, lb_ref, dc_ref, dlg_ref, dlb_ref, db_ref):
        i = pl.program_id(0)
        c_ = c_ref[...]
        g = lg_ref[...]
        mu = jnp.mean(c_, axis=-1, keepdims=True)
        cen = c_ - mu
        rstd = lax.rsqrt(jnp.mean(cen * cen, axis=-1, keepdims=True) + LN_EPS)
        xhat = cen * rstd
        ln = xhat * g + lb_ref[...]
        sg = _sigmoid(ln)
        dln = ds_ref[...] * (sg * (1.0 + ln * (1.0 - sg)))
        dxh = dln * g
        dc = rstd * (dxh - jnp.mean(dxh, axis=-1, keepdims=True)
                     - xhat * jnp.mean(dxh * xhat, axis=-1, keepdims=True))
        dc_ref[...] = dc
        dlg = jnp.sum(dln * xhat, axis=0, keepdims=True)
        dlb = jnp.sum(dln, axis=0, keepdims=True)
        db = jnp.sum(dc, axis=0, keepdims=True)

        @pl.when(i == 0)
        def _():
            dlg_ref[...] = dlg
            dlb_ref[...] = dlb
            db_ref[...] = db

        @pl.when(i > 0)
        def _():
            dlg_ref[...] += dlg
            dlb_ref[...] += dlb
            db_ref[...] += db

    row = pl.BlockSpec((T, C), lambda i: (i, 0))
    vec = pl.BlockSpec((1, C), lambda i: (0, 0))
    return _call(
        body, (ds, c, ln_g, ln_b), name="conv_ln_bwd", grid=(L // T,),
        in_specs=[row, row, vec, vec], out_specs=[row, vec, vec, vec],
        out_shape=[_sds((L, C), F32), _sds((1, C), F32), _sds((1, C), F32), _sds((1, C), F32)],
        compiler_params=_params(1))


def _conv_bwd(dc, proj, C, w_dw):
    L = proj.shape[0]
    T = CONV_ROWS
    per = T // HALO
    n = L // T
    P = C
    taps_pad = 32

    def body(dcc, dcn, vc, gc, w_ref, dv_ref, dg_ref, dw_ref, rot_d, dw_acc):
        i = pl.program_id(0)
        dc_next = dcn[...] * (i < n - 1).astype(F32)
        _fill_rotations(rot_d, jnp.concatenate([dcc[...], dc_next], axis=0))

        @pl.when(i == 0)
        def _():
            dw_acc[...] = jnp.zeros(dw_acc.shape, F32)

        for c0, cw in _lane_chunks(C):
            v = vc[:, c0:c0 + cw]
            sg = _sigmoid(gc[:, c0:c0 + cw])
            a = v * sg
            da = jnp.zeros((T, cw), F32)
            for k in range(CONV_TAPS):
                q, r = divmod(CONV_TAPS - 1 - k, 8)
                slab = rot_d[r, 8 * q:8 * q + T, c0:c0 + cw]
                da = da + w_ref[k:k + 1, c0:c0 + cw] * slab
                dw_acc[k, :, c0:c0 + cw] += jnp.sum((a * slab).reshape(T // 8, 8, cw), axis=0)
            dv_ref[:, c0:c0 + cw] = (da * sg).astype(BF16)
            dg_ref[:, c0:c0 + cw] = (da * v * sg * (1.0 - sg)).astype(BF16)

        @pl.when(i == n - 1)
        def _():
            dw_ref[...] = jnp.sum(dw_acc[...], axis=1)

    cur_v, cur_g, _, _ = _conv_specs(L, C, P // C, P // C + 1)
    row = pl.BlockSpec((T, C), lambda i: (i, 0))
    nxt = pl.BlockSpec((HALO, C), lambda i: (jnp.minimum((i + 1) * per, L // HALO - 1), 0))
    wspec = pl.BlockSpec((CONV_TAPS, C), lambda i: (0, 0))
    return _call(
        body, (dc, dc, proj, proj, w_dw), name="conv_bwd", grid=(n,),
        in_specs=[row, nxt, cur_v, cur_g, wspec],
        out_specs=[row, row, pl.BlockSpec((taps_pad, C), lambda i: (0, 0))],
        out_shape=[_sds((L, C), BF16), _sds((L, C), BF16), _sds((taps_pad, C), F32)],
        scratch_shapes=[pltpu.VMEM((8, T + HALO, C), F32), pltpu.VMEM((taps_pad, 8, C), F32)],
        compiler_params=_params(1))


def _mix_fwd(ya_pre, s, wpo, wco, proj, D):
    L, P = ya_pre.shape
    Q, _, DS = wpo.shape
    bm = _pick(L, 1088)
    gate0 = (proj.shape[1] - 2 * D) // DS
    per = D // DS

    def body(a1, a2, b1, b2, ga, gb, m_ref, ya_ref, yb_ref):
        ya = jnp.dot(a1[...], b1[...], preferred_element_type=F32)
        yb = jnp.dot(a2[...], b2[...], preferred_element_type=F32)
        ya_ref[...] = ya
        yb_ref[...] = yb
        m_ref[...] = (_sigmoid(ga[...]) * ya + _sigmoid(gb[...]) * yb).astype(BF16)

    act = pl.BlockSpec((bm, P), lambda i, q: (i, 0))
    wsp = pl.BlockSpec((None, P, DS), lambda i, q: (q, 0, 0))
    out = pl.BlockSpec((bm, DS), lambda i, q: (i, q))
    return _call(
        body, (ya_pre, s, wpo, wco, proj, proj), name="mix_fwd", grid=(L // bm, Q),
        in_specs=[act, act, wsp, wsp,
                  pl.BlockSpec((bm, DS), lambda i, q: (i, gate0 + q)),
                  pl.BlockSpec((bm, DS), lambda i, q: (i, gate0 + per + q))],
        out_specs=[out, out, out],
        out_shape=[_sds((L, D), BF16), _sds((L, D), F32), _sds((L, D), F32)],
        compiler_params=_params(2))


def _mix_bwd(do, w_o, proj, ya, yb):
    L, D = do.shape
    bm = _pick(L, 544)
    bn = _pick(D // N_CHIPS, 512)
    gate0 = (proj.shape[1] - 2 * D) // bn
    per = D // bn

    def epilogue(dm, extras, outs):
        ga, gb, ya_ref, yb_ref = extras
        sa = _sigmoid(ga[...])
        sb = _sigmoid(gb[...])
        outs[0][...] = (dm * sa).astype(BF16)
        outs[1][...] = (dm * sb).astype(BF16)
        outs[2][...] = (dm * ya_ref[...] * sa * (1.0 - sa)).astype(BF16)
        outs[3][...] = (dm * yb_ref[...] * sb * (1.0 - sb)).astype(BF16)

    blk = pl.BlockSpec((bm, bn), lambda i, j: (i, j))
    return _mm(
        "mix_bwd", (L // bm, D // bn), [do, w_o, proj, proj, ya, yb],
        [pl.BlockSpec((bm, D), lambda i, j: (i, 0)), pl.BlockSpec((bn, D), lambda i, j: (j, 0)),
         pl.BlockSpec((bm, bn), lambda i, j: (i, gate0 + j)),
         pl.BlockSpec((bm, bn), lambda i, j: (i, gate0 + per + j)), blk, blk],
        [_sds((L, D), BF16)] * 4, [blk] * 4, NT, 1, epilogue)


def _mm_act_colw(name, a, wg, bn_pref, epilogue=_store, out_dtypes=(F32,)):
    L, K = a.shape
    Q, _, n = wg.shape
    bn = _pick(n, bn_pref)
    nj = n // bn
    out = pl.BlockSpec((L, bn), lambda q, j: (0, q * nj + j))
    return _mm(name, (Q, nj), [a, wg],
               [pl.BlockSpec((L, K), lambda q, j: (0, 0)), pl.BlockSpec((None, K, bn), lambda q, j: (q, 0, j))],
               [_sds((L, Q * n), dt) for dt in out_dtypes], [out] * len(out_dtypes), NN, 1, epilogue)


def _mm_grad_colw_t(name, g, wg, bm_pref, bn_pref):
    L = g.shape[0]
    Q, K, n = wg.shape
    bm = _pick(L, bm_pref)
    bn = _pick(K, bn_pref)
    return _mm(name, (L // bm, K // bn, Q), [g, wg],
               [pl.BlockSpec((bm, n), lambda i, j, k: (i, k)), pl.BlockSpec((None, bn, n), lambda i, j, k: (k, j, 0))],
               [_sds((L, K), F32)], [pl.BlockSpec((bm, bn), lambda i, j, k: (i, j))], NT, Q,
               acc_shape=(bm, bn))


def _mm_wgrad_colw(name, a, g, Q, bm_pref, bn_pref, rows=None):
    L, K = a.shape
    n = g.shape[1] // Q
    first, count = rows or (0, K)
    bm = _pick(count, bm_pref)
    bn = _pick(n, bn_pref)
    nj = n // bn
    i0 = first // bm
    return _mm(name, (Q, count // bm, nj), [a, g],
               [pl.BlockSpec((L, bm), lambda q, i, j: (0, i0 + i)),
                pl.BlockSpec((L, bn), lambda q, i, j: (0, q * nj + j))],
               [_sds((Q, count, n), F32)], [pl.BlockSpec((None, bm, bn), lambda q, i, j: (q, i, j))], TN, 1)


def _mm_wgrad(name, a, g, bm_pref, bn_pref):
    L, K = a.shape
    N = g.shape[1]
    bm = _pick(K, bm_pref)
    bn = _pick(N, bn_pref)
    return _mm(name, (K // bm, N // bn), [a, g],
               [pl.BlockSpec((L, bm), lambda i, j: (0, i)), pl.BlockSpec((L, bn), lambda i, j: (0, j))],
               [_sds((K, N), F32)], [pl.BlockSpec((bm, bn), lambda i, j: (i, j))], TN, 1)


def _mm_act_roww(name, a, w, bm_pref, bn_pref, bk_pref):
    L, K = a.shape
    N = w.shape[1]
    bm, bn, bk = _pick(L, bm_pref), _pick(N, bn_pref), _pick(K, bk_pref)
    nk = K // bk
    return _mm(name, (L // bm, N // bn, nk), [a, w],
               [pl.BlockSpec((bm, bk), lambda i, j, k: (i, k)), pl.BlockSpec((bk, bn), lambda i, j, k: (k, j))],
               [_sds((L, N), F32)], [pl.BlockSpec((bm, bn), lambda i, j, k: (i, j))], NN, nk,
               acc_shape=(bm, bn))


def _up_epilogue(val, extras, outs):
    outs[0][...] = val
    r = jnp.maximum(val, 0.0)
    outs[1][...] = (r * r).astype(BF16)


def _mlp_down_bwd(df, w_down, a_up):
    L, D = df.shape
    F = w_down.shape[0]
    bm = _pick(L, 1088)
    bn = _pick(F, 1024)

    def epilogue(val, extras, outs):
        outs[0][...] = (val * (2.0 * jnp.maximum(extras[0][...], 0.0))).astype(BF16)

    blk = pl.BlockSpec((bm, bn), lambda i, j: (i, j))
    return _mm("mlp_down_bwd", (L // bm, F // bn), [df, w_down, a_up],
               [pl.BlockSpec((bm, D), lambda i, j: (i, 0)), pl.BlockSpec((bn, D), lambda i, j: (j, 0)), blk],
               [_sds((L, F), BF16)], [blk], NT, 1, epilogue)


ANY = pl.BlockSpec(memory_space=pl.ANY)
HBM = pl.BlockSpec(memory_space=pltpu.HBM)
SEM = pl.BlockSpec(memory_space=pltpu.SEMAPHORE)
EFFECT = pltpu.SideEffectType.DATAFLOW_SIDE_EFFECTING
N_CHIPS = 4


def _place():
    x, y, c = lax.axis_index("x"), lax.axis_index("y"), lax.axis_index("c")
    return x, y, c


def _chip_at(x, y, k):
    px = 1 - x if k & 2 else x
    py = 1 - y if k & 1 else y
    return px, py


def _cast_into_slab(w2d, chip, dtype):
    R, C = w2d.shape
    T = _elem_tile(R, C)

    def body(p_ref, w_ref, o_ref):
        o_ref[...] = w_ref[...].astype(dtype)

    return _call(
        body, (w2d,), name="cast_into_slab", grid=(R // T,), scalars=jnp.reshape(chip, (1,)).astype(jnp.int32),
        in_specs=[pl.BlockSpec((T, C), lambda i, p: (i, 0))],
        out_specs=pl.BlockSpec((None, T, C), lambda i, p: (p[0], i, 0)),
        out_shape=_sds((N_CHIPS, R, C), dtype), compiler_params=_params(1))


TOKEN = jax.ShapeDtypeStruct((8, LANES), F32)


class _Sems:
    def __init__(self, items, shape):
        self.items, self.shape = list(items), tuple(shape)

    def pair(self, idx):
        flat = 0
        for i, n in zip(idx, self.shape):
            flat = flat * n + i
        half = len(self.items) // 2
        return self.items[flat], self.items[half + flat]


def _sem_count(shape):
    n = 1
    for s in shape:
        n *= s
    return n


def _remote(src, dst, sems, idx, device):
    send, recv = sems.pair(idx)
    return pltpu.make_async_remote_copy(src_ref=src, dst_ref=dst, send_sem=send, recv_sem=recv,
                                        device_id=device, device_id_type=MESH)


def _thru(arrays):
    return ([pltpu.with_memory_space_constraint(a, pltpu.HBM) for a in arrays],
            [pltpu.HBM(a.shape, a.dtype) for a in arrays])


def _comm_start(name, arrays, sem_shape, plan, follows=False):
    na, ns = len(arrays), 2 * _sem_count(sem_shape)

    def body(*refs):
        sems, token = _Sems(refs[na:na + ns], sem_shape), refs[-1]
        for src, dst, idx, device in plan(refs[:na])[0]:
            _remote(src, dst, sems, idx, device).start()
        token[...] = jnp.zeros(token.shape, F32)

    ins, outs = _thru(arrays)
    res = _call(
        body, ins, name=name, in_specs=[HBM] * na, mark=-1, follows=follows,
        out_specs=[SEM] * ns + [HBM] * na + [pl.BlockSpec(memory_space=pltpu.VMEM)],
        out_shape=[pltpu.SemaphoreType.DMA(())] * ns + outs + [TOKEN],
        input_output_aliases={a: ns + a for a in range(na)},
        compiler_params=pltpu.CompilerParams(has_side_effects=EFFECT))
    return _Sems(res[:ns], sem_shape), list(res[ns:ns + na])


def _wait_plans(refs, sem_refs, waits):
    x, y, c = _place()
    at = 0
    for sems, plan in waits:
        here = _Sems(sem_refs[at:at + len(sems.items)], sems.shape)
        at += len(sems.items)
        _, mine, arrivals = plan(refs)
        for dst, idx in arrivals:
            _remote(dst, dst, here, idx, (x, y, c)).wait_recv()
        for src, idx in mine:
            _remote(src, src, here, idx, (x, y, c)).wait_send()


def _comm_wait(name, arrays, waits):
    na = len(arrays)
    sem_items = [s for sems, _ in waits for s in sems.items]
    ns = len(sem_items)

    def body(*refs):
        _wait_plans(refs[:na], refs[na:na + ns], waits)
        refs[-1][...] = jnp.zeros(refs[-1].shape, F32)

    ins, outs = _thru(arrays)
    res = _call(
        body, ins + sem_items, name=name, in_specs=[HBM] * na + [SEM] * ns, mark=-1,
        out_specs=[HBM] * na + [pl.BlockSpec(memory_space=pltpu.VMEM)], out_shape=outs + [TOKEN],
        input_output_aliases={a: a for a in range(na)},
        compiler_params=pltpu.CompilerParams(has_side_effects=EFFECT))
    return list(res[:na])


def _comm_relay(name, arrays, sems, plan, sem_shape, next_plan):
    na, ns_in, ns_out = len(arrays), len(sems.items), 2 * _sem_count(sem_shape)

    def body(*refs):
        bufs = refs[:na]
        sems_in = _Sems(refs[na:na + ns_in], sems.shape)
        sems_out = _Sems(refs[na + ns_in:na + ns_in + ns_out], sem_shape)
        x, y, c = _place()
        _, mine, arrivals = plan(bufs)
        onward = next_plan(bufs)[0]
        for dst, idx in arrivals:
            _remote(dst, dst, sems_in, idx, (x, y, c)).wait_recv()
            for src, to, idx2, device, after_idx in onward:
                if after_idx == idx:
                    _remote(src, to, sems_out, idx2, device).start()
        for src, idx in mine:
            _remote(src, src, sems_in, idx, (x, y, c)).wait_send()
        refs[-1][...] = jnp.zeros(refs[-1].shape, F32)

    ins, outs = _thru(arrays)
    res = _call(
        body, ins + sems.items, name=name, in_specs=[HBM] * na + [SEM] * ns_in, mark=-1,
        out_specs=[SEM] * ns_out + [HBM] * na + [pl.BlockSpec(memory_space=pltpu.VMEM)],
        out_shape=[pltpu.SemaphoreType.DMA(())] * ns_out + outs + [TOKEN],
        input_output_aliases={a: ns_out + a for a in range(na)},
        compiler_params=pltpu.CompilerParams(has_side_effects=EFFECT))
    return _Sems(res[:ns_out], sem_shape), list(res[ns_out:ns_out + na])


def _half(ref, q, which):
    h = ref.shape[1] // 2
    return ref.at[q, pl.ds(which * h, h)]


def _quarter(ref, q, half, which):
    h = ref.shape[1] // 2
    return ref.at[q, pl.ds(half * h + which * (h // 2), h // 2)]


def _gather_plan(n_halved):
    def plan(refs):
        x, y, c = _place()
        p = 2 * x + y
        starts, mine, arrivals = [], [], []
        for n, ref in enumerate(refs):
            for k in range(1, N_CHIPS if n >= n_halved else 3):
                px, py = _chip_at(x, y, k)
                q = 2 * px + py
                out = _half(ref, p, c) if n < n_halved else ref.at[p]
                inc = _half(ref, q, c) if n < n_halved else ref.at[q]
                starts.append((out, out, (n, k - 1), (px, py, c)))
                mine.append((out, (n, k - 1)))
                arrivals.append((inc, (n, k - 1)))
        return starts, mine, arrivals
    return plan


def _spread_plan(n_halved, part):
    def plan(refs):
        x, y, c = _place()
        p = 2 * x + y
        starts, mine, arrivals = [], [], []
        for n in range(n_halved):
            for k in (1, 2):
                px, py = _chip_at(x, y, k)
                q = 2 * px + py
                tx, ty = _chip_at(x, y, 3 - k)
                dx, dy = _chip_at(x, y, 3)
                piece = _quarter(refs[n], q, c, 2 - k)
                landed = _half(refs[n], q, c)
                starts.append((piece, piece, (0, n, k - 1), (tx, ty, c), (n, k - 1)))
                starts.append((landed, landed, (1, n, k - 1), (x, y, 1 - c), (n, k - 1)))
                if part != 1:
                    mine.append((piece, (0, n, k - 1)))
                    arrivals.append((_quarter(refs[n], 2 * dx + dy, c, k - 1), (0, n, 2 - k)))
                if part != 0:
                    mine.append((landed, (1, n, k - 1)))
                    arrivals.append((_half(refs[n], q, 1 - c), (1, n, k - 1)))
        return starts, mine, arrivals
    return plan


def _last_hand_on_plan(n_halved):
    def plan(refs):
        x, y, c = _place()
        dx, dy = _chip_at(x, y, 3)
        d = 2 * dx + dy
        starts, mine, arrivals = [], [], []
        for n in range(n_halved):
            for k in (1, 2):
                piece = _quarter(refs[n], d, c, k - 1)
                starts.append((piece, piece, (n, k - 1), (x, y, 1 - c), (0, n, 2 - k)))
                mine.append((piece, (n, k - 1)))
                arrivals.append((_quarter(refs[n], d, 1 - c, k - 1), (n, k - 1)))
        return starts, mine, arrivals
    return plan


def _swap_plan(n):
    def plan(refs):
        x, y, c = _place()
        starts, mine, arrivals = [], [], []
        for a in range(n):
            h = refs[a].shape[1] // 2
            src = refs[a].at[:, pl.ds((1 - c) * h, h)]
            starts.append((src, refs[n + a], (a,), (x, y, 1 - c)))
            mine.append((src, (a,)))
            arrivals.append((refs[n + a], (a,)))
        return starts, mine, arrivals
    return plan


def _scatter_plan(n):
    def plan(refs):
        x, y, c = _place()
        starts, mine, arrivals = [], [], []
        for a in range(n):
            for k in range(1, N_CHIPS):
                px, py = _chip_at(x, y, k)
                src = refs[a].at[2 * px + py]
                starts.append((src, refs[n + a].at[k - 1], (a, k - 1), (px, py, c)))
                mine.append((src, (a, k - 1)))
                arrivals.append((refs[n + a].at[k - 1], (a, k - 1)))
        return starts, mine, arrivals
    return plan


def _share_plan(n):
    def plan(refs):
        x, y, c = _place()
        starts, mine, arrivals = [], [], []
        for a in range(n):
            h = refs[a].shape[0] // 2
            own = refs[a].at[pl.ds(c * h, h)]
            starts.append((own, own, (a,), (x, y, 1 - c)))
            mine.append((own, (a,)))
            arrivals.append((refs[a].at[pl.ds((1 - c) * h, h)], (a,)))
        return starts, mine, arrivals
    return plan


def _gather_packs(pack):
    n_dev = 8

    def body(in_ref, out_ref, send, recv, lsem):
        x, y, c = _place()
        me = 4 * x + 2 * y + c
        sends = []
        for r in range(1, n_dev):
            peer = (1 - x if r & 4 else x, 1 - y if r & 2 else y, 1 - c if r & 1 else c)
            cp = pltpu.make_async_remote_copy(
                src_ref=in_ref, dst_ref=out_ref.at[me], send_sem=send.at[r - 1], recv_sem=recv.at[r - 1],
                device_id=peer, device_id_type=MESH)
            cp.start()
            sends.append(cp)
        mine = pltpu.make_async_copy(in_ref, out_ref.at[me], lsem)
        mine.start()
        for r in range(1, n_dev):
            peer = (1 - x if r & 4 else x, 1 - y if r & 2 else y, 1 - c if r & 1 else c)
            src = 4 * peer[0] + 2 * peer[1] + peer[2]
            pltpu.make_async_remote_copy(
                src_ref=in_ref, dst_ref=out_ref.at[src], send_sem=send.at[r - 1], recv_sem=recv.at[r - 1],
                device_id=peer, device_id_type=MESH).wait_recv()
        for cp in sends:
            cp.wait_send()
        mine.wait()

    return _call(
        body, (pack,), name="gather_packs", in_specs=[ANY], out_specs=ANY,
        out_shape=_sds((n_dev,) + pack.shape, pack.dtype),
        scratch_shapes=[pltpu.SemaphoreType.DMA((n_dev - 1,)), pltpu.SemaphoreType.DMA((n_dev - 1,)),
                        pltpu.SemaphoreType.DMA],
        compiler_params=pltpu.CompilerParams(has_side_effects=True))


class _Reduction:
    def __init__(self, tag, slabs, c_idx, chip):
        self.tag, self.n, self.c_idx, self.chip = tag, len(slabs), c_idx, chip
        lands = [lax.empty((g.shape[0], g.shape[1] // 2, g.shape[2]), g.dtype) for g in slabs]
        self.sems = _comm_start("swap_start_" + tag, list(slabs) + lands, (self.n,), _swap_plan(self.n))

    def partial(self):
        n = self.n
        sems, bufs = self.sems
        bufs = _comm_wait("swap_wait_" + self.tag, bufs, [(sems, _swap_plan(n))])
        both = [_chip_partial(g, r, self.c_idx, self.chip) for g, r in zip(bufs[:n], bufs[n:])]
        self.own = [o for _, o in both]
        parts = [p for p, _ in both]
        lands = [lax.empty((N_CHIPS - 1,) + p.shape[1:], p.dtype) for p in parts]
        self.sems = _comm_start("scatter_start_" + self.tag, parts + lands, (n, N_CHIPS - 1), _scatter_plan(n))

    def total(self):
        n = self.n
        sems, bufs = self.sems
        bufs = _comm_wait("scatter_wait_" + self.tag, bufs, [(sems, _scatter_plan(n))])
        fulls = [_sum_partials(o, r, self.c_idx) for o, r in zip(self.own, bufs[n:])]
        self.sems = _comm_start("share_start_" + self.tag, fulls, (n,), _share_plan(n))

    def finish(self):
        sems, bufs = self.sems
        return _comm_wait("share_wait_" + self.tag, bufs, [(sems, _share_plan(self.n))])


def _elem_tile(rows, cols):
    return _pick(rows, max(8, (1 << 19) // cols // 8 * 8))


def _chip_partial(grad, recv, c_idx, p_idx):
    Q, R, C = grad.shape
    h = R // 2
    T = _elem_tile(h, C)
    nt = h // T

    def body(sc_ref, g_ref, r_ref, sb_ref, own_ref):
        q = pl.program_id(1)
        s = g_ref[...] + r_ref[...]
        sb_ref[...] = s.astype(BF16)

        @pl.when(q == sc_ref[1])
        def _():
            own_ref[...] = s

    return _call(
        body, (grad, recv), name="chip_partial", grid=(nt, Q),
        scalars=jnp.stack([c_idx, p_idx]).astype(jnp.int32),
        in_specs=[pl.BlockSpec((None, T, C), lambda t, q, sc: (q, sc[0] * nt + t, 0)),
                  pl.BlockSpec((None, T, C), lambda t, q, sc: (q, t, 0))],
        out_specs=[pl.BlockSpec((None, T, C), lambda t, q, sc: (q, t, 0)),
                   pl.BlockSpec((T, C), lambda t, q, sc: (t, 0))],
        out_shape=[_sds((Q, h, C), BF16), _sds((h, C), F32)], compiler_params=_params(2))


def _sum_partials(own, parts, c_idx):
    h, C = own.shape
    T = _elem_tile(h, C)
    nt = h // T

    def body(c_ref, o_ref, p_ref, t_ref):
        t = o_ref[...]
        for k in range(N_CHIPS - 1):
            t = t + p_ref[k].astype(F32)
        t_ref[...] = t

    return _call(
        body, (own, parts), name="sum_partials", grid=(nt,), scalars=jnp.reshape(c_idx, (1,)).astype(jnp.int32),
        in_specs=[pl.BlockSpec((T, C), lambda i, c: (i, 0)),
                  pl.BlockSpec((N_CHIPS - 1, T, C), lambda i, c: (0, i, 0))],
        out_specs=pl.BlockSpec((T, C), lambda i, c: (c[0] * nt + i, 0)),
        out_shape=_sds((2 * h, C), F32), compiler_params=_params(1))


def _pack_rows(name, parts):
    width = parts[0].shape[1]
    offsets, at = [], 0
    for p in parts:
        offsets.append(at)
        at += p.shape[0]
    total = -(-at // 8) * 8

    def body(*refs):
        out = refs[-1]
        out[...] = jnp.zeros(out.shape, F32)
        for ref, o in zip(refs[:-1], offsets):
            out[o:o + ref.shape[0], :] = ref[...]

    whole = pl.BlockSpec(memory_space=pltpu.VMEM)
    return _call(body, list(parts), name=name, in_specs=[whole] * len(parts), out_specs=whole,
                 out_shape=_sds((total, width), F32))


def _sum_packs(packs):
    n, R, C = packs.shape

    def body(p_ref, o_ref):
        t = p_ref[0]
        for k in range(1, n):
            t = t + p_ref[k]
        o_ref[...] = t

    return _call(
        body, (packs,), name="sum_packs", grid=(1,), in_specs=[pl.BlockSpec((n, R, C), lambda i: (0, 0, 0))],
        out_specs=pl.BlockSpec((R, C), lambda i: (0, 0)), out_shape=_sds((R, C), F32), compiler_params=_params(1))


def _adamw(w, g, m, v):
    R, C = w.shape
    T = _elem_tile(R, C)

    def body(w_ref, g_ref, m_ref, v_ref, d_ref, m2_ref, v2_ref):
        g_ = g_ref[...]
        m2 = ADAM_B1 * m_ref[...] + (1.0 - ADAM_B1) * g_
        v2 = ADAM_B2 * v_ref[...] + (1.0 - ADAM_B2) * (g_ * g_)
        m_hat = m2 / (1.0 - ADAM_B1 ** ADAM_STEP)
        v_hat = v2 / (1.0 - ADAM_B2 ** ADAM_STEP)
        d_ref[...] = -ADAM_LR * (m_hat / (jnp.sqrt(v_hat) + ADAM_EPS) + ADAM_WD * w_ref[...])
        m2_ref[...] = m2
        v2_ref[...] = v2

    blk = pl.BlockSpec((T, C), lambda i: (i, 0))
    return _call(
        body, (w, g, m, v), name="adamw", grid=(R // T,), in_specs=[blk] * 4, out_specs=[blk] * 3,
        out_shape=[_sds((R, C), F32)] * 3, compiler_params=_params(1))


def _adamw_rows(w, g, m, v, row0, prev=None):
    R, C = w.shape
    T = _elem_tile(g.shape[0], C)
    off = row0 // T

    def body(w_ref, g_ref, m_ref, v_ref, *rest):
        d_ref, m2_ref, v2_ref, g2_ref = rest[-4:]
        g_ = g_ref[...]
        m2 = ADAM_B1 * m_ref[...] + (1.0 - ADAM_B1) * g_
        v2 = ADAM_B2 * v_ref[...] + (1.0 - ADAM_B2) * (g_ * g_)
        m_hat = m2 / (1.0 - ADAM_B1 ** ADAM_STEP)
        v_hat = v2 / (1.0 - ADAM_B2 ** ADAM_STEP)
        d_ref[...] = -ADAM_LR * (m_hat / (jnp.sqrt(v_hat) + ADAM_EPS) + ADAM_WD * w_ref[...])
        m2_ref[...] = m2
        v2_ref[...] = v2
        g2_ref[...] = g_

    here = pl.BlockSpec((T, C), lambda i: (off + i, 0))
    piece = pl.BlockSpec((T, C), lambda i: (i, 0))
    done = tuple(prev or ())
    return _call(
        body, (w, g, m, v) + done, name="adamw_rows", grid=(g.shape[0] // T,), follows=prev is None,
        in_specs=[here, piece, here, here] + [ANY] * len(done), out_specs=[here] * 4,
        out_shape=[_sds((R, C), F32)] * 4, input_output_aliases={4 + j: j for j in range(len(done))},
        compiler_params=_params(1))


BIG = ("w_in", "w_pool_out", "w_conv_out", "w_o", "w_up", "w_down", "w_pool_grp")
VECTORS = ("g_pre_mix", "pool_scale", "b_dw", "conv_ln_g", "conv_ln_b", "g_post_mix", "g_pre_mlp", "g_post_mlp")
WEIGHTS = ("meta", "g_pre_mix", "w_in", "w_pool_grp", "pool_scale", "w_pool_out", "w_dw", "b_dw", "conv_ln_g",
           "conv_ln_b", "w_conv_out", "w_o", "g_post_mix", "g_pre_mlp", "w_up", "w_down", "g_post_mlp")


def _as_rows(a, width):
    r, cols = a.shape
    return a.reshape(r * (cols // width), width)


def _step(w, m, v, x, tgt):
    S, D = x.shape
    P = D // 2
    xi, yi, ci = _place()
    chip = 2 * xi + yi
    _CHAIN["after"] = None

    C = D // 2
    G = POOL_GROUPS
    GD = P // G
    GS = GD // N_CHIPS
    Q = N_CHIPS
    vecs = {k: w[k] for k in VECTORS}
    shard2d = {k: w[k].reshape(-1, w[k].shape[-1]) for k in BIG}
    grads, delta, new_m, new_v = {}, {}, {}, {}

    def update(names, reduced):
        for k, g in zip(names, reduced):
            delta[k], new_m[k], new_v[k], grads[k] = _adamw_rows(
                shard2d[k], g, m[k].reshape(shard2d[k].shape), v[k].reshape(shard2d[k].shape), 0)

    groups = dict(a=(("w_in", "w_pool_grp"), ("w_dw", "meta")), b=(("w_pool_out", "w_conv_out", "w_o"), ()),
                  c=(("w_up",), ()), d=(("w_down",), ()))
    flying = {}

    def start(tag, follows):
        halved, whole = groups[tag]
        flying[tag] = _comm_start("gather_start_" + tag, flying[tag], (len(halved + whole), N_CHIPS - 1),
                                  _gather_plan(len(halved)), follows=follows)

    def spread(tag):
        nh = len(groups[tag][0])
        sems, bufs = flying[tag]
        flying[tag] = _comm_relay("gather_relay_" + tag, bufs, sems, _gather_plan(nh), (2, nh, 2),
                                  _spread_plan(nh, 0))

    def landed(tag):
        halved, whole = groups[tag]
        nh = len(halved)
        sems, bufs = flying.pop(tag)
        last, tree = _comm_relay("gather_relay2_" + tag, bufs[:nh], sems, _spread_plan(nh, 0), (nh, 2),
                                 _last_hand_on_plan(nh))
        done = _comm_wait("gather_wait_" + tag, tree,
                          [(sems, _spread_plan(nh, 1)), (last, _last_hand_on_plan(nh))])
        return dict(zip(halved + whole, done + bufs[nh:]))

    for tag, (halved, whole) in groups.items():
        flying[tag] = [_cast_into_slab(shard2d[k], chip, BF16) for k in halved]
        flying[tag] += [_cast_into_slab(w[k], chip, F32) for k in whole]
        if tag == "a":
            start(tag, False)
    spread("a")
    start("b", True)

    got = landed("a")
    win_g = got["w_in"]
    w_grp = got["w_pool_grp"].reshape(N_CHIPS, G, GS, GD).transpose(1, 0, 2, 3).reshape(G, GD, GD)
    w_dw = got["w_dw"].transpose(1, 0, 2).reshape(CONV_TAPS, P)
    meta = got["meta"].transpose(1, 0, 2).reshape(N_META, D)
    h0 = jnp.concatenate([jnp.zeros((PAD_ROWS, D), F32), meta, x], axis=0)
    u1 = _pre_norm(h0, vecs["g_pre_mix"])
    spread("b")
    start("c", True)
    start("d", True)
    proj = _mm_act_colw("proj", u1, win_g, 256)
    spread("c")
    d, ya_pre = _pool_fwd(proj, w_grp, vecs["pool_scale"])
    s, c = _conv_fwd(proj, C, w_dw, vecs["b_dw"], vecs["conv_ln_g"], vecs["conv_ln_b"])
    got = landed("b")
    spread("d")
    wpo_g, wco_g, w_o = got["w_pool_out"], got["w_conv_out"], got["w_o"].reshape(D, D)
    mix, ya, yb = _mix_fwd(ya_pre, s, wpo_g, wco_g, proj, D)
    o = _mm_act_roww("attn_out", mix, w_o, 1088, 1024, 2048)
    h1, u2 = _mid_norm(o, h0, vecs["g_post_mix"], vecs["g_pre_mlp"])
    wup_g = landed("c")["w_up"]
    a_up, fact = _mm_act_colw("mlp_up", u2, wup_g, 512, _up_epilogue, (F32, BF16))
    w_down = landed("d")["w_down"].reshape(-1, D)
    f = _mm_act_roww("mlp_down", fact, w_down, 1088, 1024, 2048)
    dy, df, dg_post_mlp, loss = _loss_head(f, h1, tgt, vecs["g_post_mlp"])

    g_w_down = _mm_wgrad("dw_down", fact, df, 1024, 1024)
    red1 = _Reduction("1", [g_w_down.reshape(N_CHIPS, -1, D)], ci, chip)
    da_up = _mlp_down_bwd(df, w_down, a_up)
    red1.partial()
    g_w_up = _mm_wgrad_colw("dw_up", u2, da_up, Q, 1024, 1024)
    red2 = _Reduction("2", [g_w_up], ci, chip)
    du2 = _mm_grad_colw_t("du2", da_up, wup_g, 1088, 1024)
    red2.partial()
    dh1, do, dg_pre_mlp, dg_post_mix = _mid_norm_bwd(dy, du2, h1, o, vecs["g_pre_mlp"], vecs["g_post_mix"])
    g_w_o = _mm_wgrad("dw_o", mix, do, 1024, 1024)
    red1.total()
    dya, dyb, dga, dgb = _mix_bwd(do, w_o, proj, ya, yb)
    update(("w_down",), red1.finish())
    g_wpo = _mm_wgrad_colw("dw_pool_out", ya_pre, dya, Q, 1024, 512)
    g_wco = _mm_wgrad_colw("dw_conv_out", s, dyb, Q, 1024, 512)
    red3 = _Reduction("3", [g_w_o.reshape(N_CHIPS, D // N_CHIPS, D), g_wpo, g_wco], ci, chip)
    dya_pre = _mm_grad_colw_t("dya_pre", dya, wpo_g, 1088, 1024)
    ds = _mm_grad_colw_t("ds", dyb, wco_g, 1088, 1024)
    red3.partial()
    dz, g_w_grp, dscale = _pool_bwd(dya_pre, d, w_grp, vecs["pool_scale"])
    dc, dln_g, dln_b, db_dw = _conv_ln_bwd(ds, c, vecs["conv_ln_g"], vecs["conv_ln_b"])
    red2.total()
    dv, dgc, g_w_dw = _conv_bwd(dc, proj, C, w_dw)
    update(("w_up",), red2.finish())
    dproj = jnp.concatenate([dz, dv, dgc, dga, dgb], axis=1)
    half_k = D // 2
    g_w_grp = g_w_grp.reshape(G, N_CHIPS, GS, GD).transpose(1, 0, 2, 3).reshape(N_CHIPS, G * GS, GD)
    g_in_a = _mm_wgrad_colw("dw_in_a", u1, dproj, Q, 512, 1792, rows=(0, half_k))
    red4a = _Reduction("4a", [g_in_a, g_w_grp], ci, chip)
    g_in_b = _mm_wgrad_colw("dw_in_b", u1, dproj, Q, 512, 1792, rows=(half_k, half_k))
    red4a.partial()
    red4b = _Reduction("4b", [g_in_b], ci, chip)
    du1 = _mm_grad_colw_t("du1", dproj, win_g, 1088, 1024)
    red4b.partial()
    red3.total()
    grad_x, dmeta, dg_pre_mix = _pre_norm_bwd(dh1, du1, h0, vecs["g_pre_mix"])
    red4a.total()
    update(("w_o", "w_pool_out", "w_conv_out"), red3.finish())
    red_in_a, red_grp = red4a.finish()
    update(("w_pool_grp",), [red_grp])
    w_in_rows = (shard2d["w_in"], m["w_in"].reshape(shard2d["w_in"].shape), v["w_in"].reshape(shard2d["w_in"].shape))
    first_rows = _adamw_rows(w_in_rows[0], red_in_a, w_in_rows[1], w_in_rows[2], 0)
    g_vec = dict(g_pre_mix=dg_pre_mix, pool_scale=dscale, b_dw=db_dw, conv_ln_g=dln_g, conv_ln_b=dln_b,
                 g_post_mix=dg_post_mix, g_pre_mlp=dg_pre_mlp, g_post_mlp=dg_post_mlp)

    rows = [g_w_dw, _as_rows(dmeta, P)] + [_as_rows(g_vec[k], P) for k in VECTORS]
    rows.append(jnp.broadcast_to(loss[:, :1], (1, P)))
    total = _sum_packs(_gather_packs(_pack_rows("pack_grads", rows)))
    at = 0
    taps_pad = g_w_dw.shape[0]
    g_dw_full = total[at:at + CONV_TAPS]
    at += taps_pad
    g_meta_full = total[at:at + 2 * N_META].reshape(N_META, D)
    at += 2 * N_META
    for k in VECTORS:
        n = w[k].shape[-1] // P
        grads[k] = total[at:at + n].reshape(1, n * P)
        at += n
    loss_total = total[at, 0]
    grads["w_dw"] = lax.dynamic_slice_in_dim(g_dw_full, chip * (P // N_CHIPS), P // N_CHIPS, axis=1)
    grads["meta"] = lax.dynamic_slice_in_dim(g_meta_full, chip * (D // N_CHIPS), D // N_CHIPS, axis=1)

    small_names = [k for k in WEIGHTS if k not in BIG]

    def pack_small(tree):
        parts = []
        for k in small_names:
            a = tree[k].reshape(-1, tree[k].shape[-1])
            flat = a.reshape(-1)
            parts.append(jnp.pad(flat, (0, -flat.shape[0] % P)).reshape(-1, P))
        return _pack_rows("pack_small", parts)

    sd, sm, sv = _adamw(pack_small(w), pack_small(grads), pack_small(m), pack_small(v))
    at = 0
    for k in small_names:
        a = w[k].reshape(-1, w[k].shape[-1])
        n = -(-a.size // P)
        for tree, packed in ((delta, sd), (new_m, sm), (new_v, sv)):
            tree[k] = packed[at:at + n].reshape(-1)[:a.size].reshape(a.shape)
        at += n

    red4b.total()
    delta["w_in"], new_m["w_in"], new_v["w_in"], grads["w_in"] = _adamw_rows(
        w_in_rows[0], red4b.finish()[0], w_in_rows[1], w_in_rows[2], half_k, prev=first_rows)
    return loss_total, grad_x, grads, delta, new_m, new_v


def kernel(x, meta, g_pre_mix, w_in, w_pool_grp, pool_scale, w_pool_out, w_dw, b_dw, conv_ln_g, conv_ln_b, w_conv_out, w_o, g_post_mix, g_pre_mlp, w_up, w_down, g_post_mlp, loss_target, m_meta, m_g_pre_mix, m_w_in, m_w_pool_grp, m_pool_scale, m_w_pool_out, m_w_dw, m_b_dw, m_conv_ln_g, m_conv_ln_b, m_w_conv_out, m_w_o, m_g_post_mix, m_g_pre_mlp, m_w_up, m_w_down, m_g_post_mlp, v_meta, v_g_pre_mix, v_w_in, v_w_pool_grp, v_pool_scale, v_w_pool_out, v_w_dw, v_b_dw, v_conv_ln_g, v_conv_ln_b, v_w_conv_out, v_w_o, v_g_post_mix, v_g_pre_mlp, v_w_up, v_w_down, v_g_post_mlp):
    args = dict(locals())
    shapes = {k: args[k].shape for k in WEIGHTS}
    w = {k: args[k] for k in WEIGHTS}
    m = {k: args["m_" + k] for k in WEIGHTS}
    v = {k: args["v_" + k] for k in WEIGHTS}
    for tree in (w, m, v):
        tree["w_dw"] = tree["w_dw"].reshape(tree["w_dw"].shape[-2:])
    loss, grad_x, grads, delta, new_m, new_v = _step(w, m, v, x[0], loss_target[0])
    out = [loss, grad_x[None]]
    for tree in (grads, delta, new_m, new_v):
        out += [tree[k].reshape(shapes[k]) for k in WEIGHTS]
    return tuple(out)
```

```python
import jax
import jax.numpy as jnp
from jax import lax
from jax.experimental import pallas as pl
from jax.experimental.pallas import tpu as pltpu

F32 = jnp.float32
BF16 = jnp.bfloat16

N_META = 16
PAD_ROWS = 112
TOKEN_ROW0 = PAD_ROWS + N_META
POOL_GROUPS = 4
CONV_TAPS = 31
HALO = 32
CONV_ROWS = 128
LANES = 128
RMS_EPS = 1e-6
LN_EPS = 1e-5
ADAM_LR = 0.001
ADAM_B1 = 0.9
ADAM_B2 = 0.999
ADAM_EPS = 1e-08
ADAM_WD = 0.01
ADAM_STEP = 10
VMEM_LIMIT_MB = 56

MESH = pl.DeviceIdType.MESH
NN = (((1,), (0,)), ((), ()))
NT = (((1,), (1,)), ((), ()))
TN = (((0,), (0,)), ((), ()))


def _pick(n, pref):
    if n <= pref:
        return n
    if n % pref == 0:
        return pref
    for step in (LANES, 8, 1):
        t = (pref // step) * step
        while t >= step:
            if n % t == 0:
                return t
            t -= step
    return n


def _params(n_axes, vmem_mb=VMEM_LIMIT_MB):
    return pltpu.CompilerParams(dimension_semantics=("arbitrary",) * n_axes,
                                vmem_limit_bytes=vmem_mb << 20)


def _sigmoid(x):
    return jax.nn.sigmoid(x)


_CHAIN = {"after": None}


def _call(body, args, *, in_specs, out_specs, out_shape, grid=(), scalars=None, mark=0, follows=True, **kw):
    after = _CHAIN["after"] if follows else None
    n = len(args)
    lead = 0 if scalars is None else 1
    specs = list(in_specs)
    operands = list(args)
    fn = body
    if after is not None:
        def fn(*refs):
            body(*refs[:lead + n], *refs[lead + n + 1:])
        specs.append(pl.BlockSpec(memory_space=pl.ANY))
        operands.append(after)
    if scalars is None:
        if grid:
            kw["grid"] = grid
        res = pl.pallas_call(fn, in_specs=specs, out_specs=out_specs, out_shape=out_shape, **kw)(*operands)
    else:
        grid_spec = pltpu.PrefetchScalarGridSpec(num_scalar_prefetch=1, grid=grid, in_specs=specs, out_specs=out_specs)
        res = pl.pallas_call(fn, grid_spec=grid_spec, out_shape=out_shape, **kw)(scalars, *operands)
    outs = res if isinstance(res, (list, tuple)) else [res]
    _CHAIN["after"] = outs[mark]
    return res


def _store(val, extras, outs):
    outs[0][...] = val.astype(outs[0].dtype)


def _mm(name, grid, arrays, in_specs, out_shapes, out_specs, dims, nk, epilogue=_store, acc_shape=None):
    n_in, n_out = len(arrays), len(out_shapes)

    def body(*refs):
        extras = refs[2:n_in]
        outs = refs[n_in:n_in + n_out]
        part = lax.dot_general(refs[0][...], refs[1][...], dims, preferred_element_type=F32)
        if nk == 1:
            epilogue(part, extras, outs)
        else:
            acc = refs[n_in + n_out]
            k = pl.program_id(len(grid) - 1)

            @pl.when(k == 0)
            def _():
                acc[...] = part

            @pl.when(k > 0)
            def _():
                acc[...] += part

            @pl.when(k == nk - 1)
            def _():
                epilogue(acc[...], extras, outs)

    scratch = [pltpu.VMEM(acc_shape, F32)] if nk > 1 else []
    single = n_out == 1
    return _call(
        body, arrays, name=name, grid=grid, in_specs=in_specs,
        out_specs=out_specs[0] if single else out_specs,
        out_shape=out_shapes[0] if single else out_shapes,
        scratch_shapes=scratch, compiler_params=_params(len(grid)))


def _sds(shape, dtype):
    return jax.ShapeDtypeStruct(shape, dtype)


def _rms_scale(h):
    return lax.rsqrt(jnp.mean(h * h, axis=-1, keepdims=True) + RMS_EPS)


def _rms_bwd(du, h, g):
    r = _rms_scale(h)
    y = h * r
    dy = du * g
    dh = r * (dy - y * jnp.mean(dy * y, axis=-1, keepdims=True))
    return dh, jnp.sum(du * y, axis=0, keepdims=True)


def _row_tile(L):
    return _pick(L, 272)


def _pre_norm(h0, g):
    L, D = h0.shape
    T = _row_tile(L)

    def body(h_ref, g_ref, u_ref):
        h = h_ref[...]
        u_ref[...] = (h * _rms_scale(h) * g_ref[...]).astype(BF16)

    return _call(
        body, (h0, g), name="pre_norm", grid=(L // T,),
        in_specs=[pl.BlockSpec((T, D), lambda i: (i, 0)), pl.BlockSpec((1, D), lambda i: (0, 0))],
        out_specs=pl.BlockSpec((T, D), lambda i: (i, 0)),
        out_shape=_sds((L, D), BF16), compiler_params=_params(1))


def _mid_norm(o, h0, g_post, g_pre):
    L, D = h0.shape
    T = _row_tile(L)

    def body(o_ref, h_ref, gp_ref, gm_ref, h1_ref, u2_ref):
        o_ = o_ref[...]
        h1 = h_ref[...] + o_ * _rms_scale(o_) * gp_ref[...]
        h1_ref[...] = h1
        u2_ref[...] = (h1 * _rms_scale(h1) * gm_ref[...]).astype(BF16)

    row = pl.BlockSpec((T, D), lambda i: (i, 0))
    vec = pl.BlockSpec((1, D), lambda i: (0, 0))
    return _call(
        body, (o, h0, g_post, g_pre), name="mid_norm", grid=(L // T,),
        in_specs=[row, row, vec, vec], out_specs=[row, row],
        out_shape=[_sds((L, D), F32), _sds((L, D), BF16)], compiler_params=_params(1))


def _loss_head(f, h1, tgt, g_post):
    L, D = h1.shape
    T = TOKEN_ROW0
    n = L // T

    def body(f_ref, h_ref, t_ref, g_ref, dy_ref, df_ref, dg_ref, loss_ref):
        i = pl.program_id(0)
        f_ = f_ref[...]
        g = g_ref[...]
        y = h_ref[...] + f_ * _rms_scale(f_) * g
        live = (i > 0).astype(F32)
        diff = (y - t_ref[...]) * live
        part = 0.5 * jnp.sum(jnp.mean(diff * diff, axis=-1, keepdims=True), axis=0, keepdims=True)
        dy = diff * (1.0 / D)
        dy_ref[...] = dy
        df, dg = _rms_bwd(dy, f_, g)
        df_ref[...] = df.astype(BF16)

        @pl.when(i == 0)
        def _():
            dg_ref[...] = dg
            loss_ref[...] = jnp.broadcast_to(part, loss_ref.shape)

        @pl.when(i > 0)
        def _():
            dg_ref[...] += dg
            loss_ref[...] += jnp.broadcast_to(part, loss_ref.shape)

    row = pl.BlockSpec((T, D), lambda i: (i, 0))
    vec = pl.BlockSpec((1, D), lambda i: (0, 0))
    return _call(
        body, (f, h1, tgt, g_post), name="loss_head", grid=(n,),
        in_specs=[row, row, pl.BlockSpec((T, D), lambda i: (jnp.maximum(i - 1, 0), 0)), vec],
        out_specs=[row, row, vec, pl.BlockSpec((1, LANES), lambda i: (0, 0))],
        out_shape=[_sds((L, D), F32), _sds((L, D), BF16), _sds((1, D), F32), _sds((1, LANES), F32)],
        compiler_params=_params(1))


def _mid_norm_bwd(dy, du2, h1, o, g_pre, g_post):
    L, D = h1.shape
    T = _row_tile(L)

    def body(dy_ref, du_ref, h_ref, o_ref, gm_ref, gp_ref, dh1_ref, do_ref, dgm_ref, dgp_ref):
        i = pl.program_id(0)
        dh, dgm = _rms_bwd(du_ref[...], h_ref[...], gm_ref[...])
        dh1 = dy_ref[...] + dh
        dh1_ref[...] = dh1
        do, dgp = _rms_bwd(dh1, o_ref[...], gp_ref[...])
        do_ref[...] = do.astype(BF16)

        @pl.when(i == 0)
        def _():
            dgm_ref[...] = dgm
            dgp_ref[...] = dgp

        @pl.when(i > 0)
        def _():
            dgm_ref[...] += dgm
            dgp_ref[...] += dgp

    row = pl.BlockSpec((T, D), lambda i: (i, 0))
    vec = pl.BlockSpec((1, D), lambda i: (0, 0))
    return _call(
        body, (dy, du2, h1, o, g_pre, g_post), name="mid_norm_bwd", grid=(L // T,),
        in_specs=[row, row, row, row, vec, vec], out_specs=[row, row, vec, vec],
        out_shape=[_sds((L, D), F32), _sds((L, D), BF16), _sds((1, D), F32), _sds((1, D), F32)],
        compiler_params=_params(1))


def _pre_norm_bwd(dh1, du1, h0, g):
    L, D = h0.shape
    T = TOKEN_ROW0
    n = L // T

    def body(dh_ref, du_ref, h_ref, g_ref, gx_ref, dmeta_ref, dg_ref):
        i = pl.program_id(0)
        dh, dg = _rms_bwd(du_ref[...], h_ref[...], g_ref[...])
        dh0 = dh_ref[...] + dh
        gx_ref[...] = dh0

        @pl.when(i == 0)
        def _():
            dmeta_ref[...] = dh0[PAD_ROWS:, :]
            dg_ref[...] = dg

        @pl.when(i > 0)
        def _():
            dg_ref[...] += dg

    row = pl.BlockSpec((T, D), lambda i: (i, 0))
    vec = pl.BlockSpec((1, D), lambda i: (0, 0))
    return _call(
        body, (dh1, du1, h0, g), name="pre_norm_bwd", grid=(n,),
        in_specs=[row, row, row, vec],
        out_specs=[pl.BlockSpec((T, D), lambda i: (jnp.maximum(i - 1, 0), 0)),
                   pl.BlockSpec((N_META, D), lambda i: (0, 0)), vec],
        out_shape=[_sds((L - T, D), F32), _sds((N_META, D), F32), _sds((1, D), F32)],
        compiler_params=_params(1))


def _window_sum(z, g, shift_sign, L):
    s = z
    for j in range(POOL_GROUPS):
        k = 1 << j
        nxt = s + pltpu.roll(s, k if shift_sign > 0 else L - k, 0)
        s = jnp.where(j <= g, nxt, s)
    return s


def _inv_count(g, L):
    t = lax.broadcasted_iota(jnp.int32, (L, 1), 0)
    w = jnp.left_shift(2, g)
    cnt = jnp.clip(t - (PAD_ROWS - 1), 1, w)
    return 1.0 / cnt.astype(F32)


def _pool_fwd(proj, w_grp, scale):
    L = proj.shape[0]
    G, GD, _ = w_grp.shape
    P = G * GD

    def body(z_ref, w_ref, sc_ref, d_ref, ya_ref):
        g = pl.program_id(0)
        z = z_ref[...]
        d = (_window_sum(z, g, +1, L) * _inv_count(g, L) - z).astype(BF16)
        d_ref[...] = d
        y = jnp.dot(d, w_ref[...], preferred_element_type=F32)
        ya_ref[...] = (y * sc_ref[...]).astype(BF16)

    col = pl.BlockSpec((L, GD), lambda g: (0, g))
    return _call(
        body, (proj, w_grp, scale), name="pool_fwd", grid=(G,),
        in_specs=[col, pl.BlockSpec((None, GD, GD), lambda g: (g, 0, 0)), pl.BlockSpec((1, GD), lambda g: (0, g))],
        out_specs=[col, col], out_shape=[_sds((L, P), BF16), _sds((L, P), BF16)],
        compiler_params=_params(1))


def _pool_bwd(dya, d, w_grp, scale):
    L, P = dya.shape
    G, GD, _ = w_grp.shape

    def body(dya_ref, d_ref, w_ref, sc_ref, dz_ref, dw_ref, dsc_ref):
        g = pl.program_id(0)
        dya_ = dya_ref[...]
        d_ = d_ref[...]
        w = w_ref[...]
        y = jnp.dot(d_, w, preferred_element_type=F32)
        dsc_ref[...] = jnp.sum(dya_ * y, axis=0, keepdims=True)
        dy = (dya_ * sc_ref[...]).astype(BF16)
        dw_ref[...] = lax.dot_general(d_, dy, TN, preferred_element_type=F32)
        dd = lax.dot_general(dy, w, NT, preferred_element_type=F32)
        dz = _window_sum(dd * _inv_count(g, L), g, -1, L) - dd
        dz_ref[...] = dz.astype(BF16)

    col = pl.BlockSpec((L, GD), lambda g: (0, g))
    wspec = pl.BlockSpec((None, GD, GD), lambda g: (g, 0, 0))
    vec = pl.BlockSpec((1, GD), lambda g: (0, g))
    return _call(
        body, (dya, d, w_grp, scale), name="pool_bwd", grid=(G,),
        in_specs=[col, col, wspec, vec], out_specs=[col, wspec, vec],
        out_shape=[_sds((L, P), BF16), _sds((G, GD, GD), F32), _sds((1, P), F32)],
        compiler_params=_params(1))


def _fill_rotations(rot_ref, ext):
    n = ext.shape[0]
    rot_ref[0] = ext
    for r in range(1, 8):
        rot_ref[r] = pltpu.roll(ext, n - r, 0)


def _lane_chunks(C):
    step = LANES if C % LANES == 0 else C
    return [(c0, step) for c0 in range(0, C, step)]


def _conv_specs(L, C, col_v, col_g):
    T = CONV_ROWS
    per = T // HALO
    cur_v = pl.BlockSpec((T, C), lambda i: (i, col_v))
    cur_g = pl.BlockSpec((T, C), lambda i: (i, col_g))
    prev_v = pl.BlockSpec((HALO, C), lambda i: (jnp.maximum(i * per - 1, 0), col_v))
    prev_g = pl.BlockSpec((HALO, C), lambda i: (jnp.maximum(i * per - 1, 0), col_g))
    return cur_v, cur_g, prev_v, prev_g


def _glu_ext(vc, gc, vh, gh, i):
    a_cur = vc[...] * _sigmoid(gc[...])
    a_prev = vh[...] * _sigmoid(gh[...]) * (i > 0).astype(F32)
    return jnp.concatenate([a_prev, a_cur], axis=0)


def _conv_fwd(proj, C, w_dw, b_dw, ln_g, ln_b):
    L = proj.shape[0]
    T = CONV_ROWS
    P = C

    def body(vc, gc, vh, gh, w_ref, b_ref, lg_ref, lb_ref, s_ref, c_ref, rot):
        i = pl.program_id(0)
        _fill_rotations(rot, _glu_ext(vc, gc, vh, gh, i))
        for c0, cw in _lane_chunks(C):
            acc = jnp.zeros((T, cw), F32)
            for k in range(CONV_TAPS):
                q, r = divmod(HALO - (CONV_TAPS - 1) + k, 8)
                acc = acc + w_ref[k:k + 1, c0:c0 + cw] * rot[r, 8 * q:8 * q + T, c0:c0 + cw]
            c_ref[:, c0:c0 + cw] = acc + b_ref[:, c0:c0 + cw]
        c = c_ref[...]
        mu = jnp.mean(c, axis=-1, keepdims=True)
        cen = c - mu
        var = jnp.mean(cen * cen, axis=-1, keepdims=True)
        ln = cen * lax.rsqrt(var + LN_EPS) * lg_ref[...] + lb_ref[...]
        s_ref[...] = (ln * _sigmoid(ln)).astype(BF16)

    cur_v, cur_g, prev_v, prev_g = _conv_specs(L, C, P // C, P // C + 1)
    row = pl.BlockSpec((T, C), lambda i: (i, 0))
    vec = pl.BlockSpec((1, C), lambda i: (0, 0))
    return _call(
        body, (proj, proj, proj, proj, w_dw, b_dw, ln_g, ln_b), name="conv_fwd", grid=(L // T,),
        in_specs=[cur_v, cur_g, prev_v, prev_g, pl.BlockSpec((CONV_TAPS, C), lambda i: (0, 0)), vec, vec, vec],
        out_specs=[row, row], out_shape=[_sds((L, C), BF16), _sds((L, C), F32)],
        scratch_shapes=[pltpu.VMEM((8, T + HALO, C), F32)], compiler_params=_params(1))


def _conv_ln_bwd(ds, c, ln_g, ln_b):
    L, C = c.shape
    T = _row_tile(L)

    def body(ds_ref, c_ref, lg_ref, lb_ref, dc_ref, dlg_ref, dlb_ref, db_ref):
        i = pl.program_id(0)
        c_ = c_ref[...]
        g = lg_ref[...]
        mu = jnp.mean(c_, axis=-1, keepdims=True)
        cen = c_ - mu
        rstd = lax.rsqrt(jnp.mean(cen * cen, axis=-1, keepdims=True) + LN_EPS)
        xhat = cen * rstd
        ln = xhat * g + lb_ref[...]
        sg = _sigmoid(ln)
        dln = ds_ref[...] * (sg * (1.0 + ln * (1.0 - sg)))
        dxh = dln * g
        dc = rstd * (dxh - jnp.mean(dxh, axis=-1, keepdims=True)
                     - xhat * jnp.mean(dxh * xhat, axis=-1, keepdims=True))
        dc_ref[...] = dc
        dlg = jnp.sum(dln * xhat, axis=0, keepdims=True)
        dlb = jnp.sum(dln, axis=0, keepdims=True)
        db = jnp.sum(dc, axis=0, keepdims=True)

        @pl.when(i == 0)
        def _():
            dlg_ref[...] = dlg
            dlb_ref[...] = dlb
            db_ref[...] = db

        @pl.when(i > 0)
        def _():
            dlg_ref[...] += dlg
            dlb_ref[...] += dlb
            db_ref[...] += db

    row = pl.BlockSpec((T, C), lambda i: (i, 0))
    vec = pl.BlockSpec((1, C), lambda i: (0, 0))
    return _call(
        body, (ds, c, ln_g, ln_b), name="conv_ln_bwd", grid=(L // T,),
        in_specs=[row, row, vec, vec], out_specs=[row, vec, vec, vec],
        out_shape=[_sds((L, C), F32), _sds((1, C), F32), _sds((1, C), F32), _sds((1, C), F32)],
        compiler_params=_params(1))


def _conv_bwd(dc, proj, C, w_dw):
    L = proj.shape[0]
    T = CONV_ROWS
    per = T // HALO
    n = L // T
    P = C
    taps_pad = 32

    def body(dcc, dcn, vc, gc, w_ref, dv_ref, dg_ref, dw_ref, rot_d, dw_acc):
        i = pl.program_id(0)
        dc_next = dcn[...] * (i < n - 1).astype(F32)
        _fill_rotations(rot_d, jnp.concatenate([dcc[...], dc_next], axis=0))

        @pl.when(i == 0)
        def _():
            dw_acc[...] = jnp.zeros(dw_acc.shape, F32)

        for c0, cw in _lane_chunks(C):
            v = vc[:, c0:c0 + cw]
            sg = _sigmoid(gc[:, c0:c0 + cw])
            a = v * sg
            da = jnp.zeros((T, cw), F32)
            for k in range(CONV_TAPS):
                q, r = divmod(CONV_TAPS - 1 - k, 8)
                slab = rot_d[r, 8 * q:8 * q + T, c0:c0 + cw]
                da = da + w_ref[k:k + 1, c0:c0 + cw] * slab
                dw_acc[k, :, c0:c0 + cw] += jnp.sum((a * slab).reshape(T // 8, 8, cw), axis=0)
            dv_ref[:, c0:c0 + cw] = (da * sg).astype(BF16)
            dg_ref[:, c0:c0 + cw] = (da * v * sg * (1.0 - sg)).astype(BF16)

        @pl.when(i == n - 1)
        def _():
            dw_ref[...] = jnp.sum(dw_acc[...], axis=1)

    cur_v, cur_g, _, _ = _conv_specs(L, C, P // C, P // C + 1)
    row = pl.BlockSpec((T, C), lambda i: (i, 0))
    nxt = pl.BlockSpec((HALO, C), lambda i: (jnp.minimum((i + 1) * per, L // HALO - 1), 0))
    wspec = pl.BlockSpec((CONV_TAPS, C), lambda i: (0, 0))
    return _call(
        body, (dc, dc, proj, proj, w_dw), name="conv_bwd", grid=(n,),
        in_specs=[row, nxt, cur_v, cur_g, wspec],
        out_specs=[row, row, pl.BlockSpec((taps_pad, C), lambda i: (0, 0))],
        out_shape=[_sds((L, C), BF16), _sds((L, C), BF16), _sds((taps_pad, C), F32)],
        scratch_shapes=[pltpu.VMEM((8, T + HALO, C), F32), pltpu.VMEM((taps_pad, 8, C), F32)],
        compiler_params=_params(1))


def _mix_fwd(ya_pre, s, wpo, wco, proj, D):
    L, P = ya_pre.shape
    Q, _, DS = wpo.shape
    bm = _pick(L, 1088)
    gate0 = (proj.shape[1] - 2 * D) // DS
    per = D // DS

    def body(a1, a2, b1, b2, ga, gb, m_ref, ya_ref, yb_ref):
        ya = jnp.dot(a1[...], b1[...], preferred_element_type=F32)
        yb = jnp.dot(a2[...], b2[...], preferred_element_type=F32)
        ya_ref[...] = ya
        yb_ref[...] = yb
        m_ref[...] = (_sigmoid(ga[...]) * ya + _sigmoid(gb[...]) * yb).astype(BF16)

    act = pl.BlockSpec((bm, P), lambda i, q: (i, 0))
    wsp = pl.BlockSpec((None, P, DS), lambda i, q: (q, 0, 0))
    out = pl.BlockSpec((bm, DS), lambda i, q: (i, q))
    return _call(
        body, (ya_pre, s, wpo, wco, proj, proj), name="mix_fwd", grid=(L // bm, Q),
        in_specs=[act, act, wsp, wsp,
                  pl.BlockSpec((bm, DS), lambda i, q: (i, gate0 + q)),
                  pl.BlockSpec((bm, DS), lambda i, q: (i, gate0 + per + q))],
        out_specs=[out, out, out],
        out_shape=[_sds((L, D), BF16), _sds((L, D), F32), _sds((L, D), F32)],
        compiler_params=_params(2))


def _mix_bwd(do, w_o, proj, ya, yb):
    L, D = do.shape
    bm = _pick(L, 544)
    bn = _pick(D // N_CHIPS, 512)
    gate0 = (proj.shape[1] - 2 * D) // bn
    per = D // bn

    def epilogue(dm, extras, outs):
        ga, gb, ya_ref, yb_ref = extras
        sa = _sigmoid(ga[...])
        sb = _sigmoid(gb[...])
        outs[0][...] = (dm * sa).astype(BF16)
        outs[1][...] = (dm * sb).astype(BF16)
        outs[2][...] = (dm * ya_ref[...] * sa * (1.0 - sa)).astype(BF16)
        outs[3][...] = (dm * yb_ref[...] * sb * (1.0 - sb)).astype(BF16)

    blk = pl.BlockSpec((bm, bn), lambda i, j: (i, j))
    return _mm(
        "mix_bwd", (L // bm, D // bn), [do, w_o, proj, proj, ya, yb],
        [pl.BlockSpec((bm, D), lambda i, j: (i, 0)), pl.BlockSpec((bn, D), lambda i, j: (j, 0)),
         pl.BlockSpec((bm, bn), lambda i, j: (i, gate0 + j)),
         pl.BlockSpec((bm, bn), lambda i, j: (i, gate0 + per + j)), blk, blk],
        [_sds((L, D), BF16)] * 4, [blk] * 4, NT, 1, epilogue)


def _mm_act_colw(name, a, wg, bn_pref, epilogue=_store, out_dtypes=(F32,)):
    L, K = a.shape
    Q, _, n = wg.shape
    bn = _pick(n, bn_pref)
    nj = n // bn
    out = pl.BlockSpec((L, bn), lambda q, j: (0, q * nj + j))
    return _mm(name, (Q, nj), [a, wg],
               [pl.BlockSpec((L, K), lambda q, j: (0, 0)), pl.BlockSpec((None, K, bn), lambda q, j: (q, 0, j))],
               [_sds((L, Q * n), dt) for dt in out_dtypes], [out] * len(out_dtypes), NN, 1, epilogue)


def _mm_grad_colw_t(name, g, wg, bm_pref, bn_pref):
    L = g.shape[0]
    Q, K, n = wg.shape
    bm = _pick(L, bm_pref)
    bn = _pick(K, bn_pref)
    return _mm(name, (L // bm, K // bn, Q), [g, wg],
               [pl.BlockSpec((bm, n), lambda i, j, k: (i, k)), pl.BlockSpec((None, bn, n), lambda i, j, k: (k, j, 0))],
               [_sds((L, K), F32)], [pl.BlockSpec((bm, bn), lambda i, j, k: (i, j))], NT, Q,
               acc_shape=(bm, bn))


def _mm_wgrad_colw(name, a, g, Q, bm_pref, bn_pref, rows=None):
    L, K = a.shape
    n = g.shape[1] // Q
    first, count = rows or (0, K)
    bm = _pick(count, bm_pref)
    bn = _pick(n, bn_pref)
    nj = n // bn
    i0 = first // bm
    return _mm(name, (Q, count // bm, nj), [a, g],
               [pl.BlockSpec((L, bm), lambda q, i, j: (0, i0 + i)),
                pl.BlockSpec((L, bn), lambda q, i, j: (0, q * nj + j))],
               [_sds((Q, count, n), F32)], [pl.BlockSpec((None, bm, bn), lambda q, i, j: (q, i, j))], TN, 1)


def _mm_wgrad(name, a, g, bm_pref, bn_pref):
    L, K = a.shape
    N = g.shape[1]
    bm = _pick(K, bm_pref)
    bn = _pick(N, bn_pref)
    return _mm(name, (K // bm, N // bn), [a, g],
               [pl.BlockSpec((L, bm), lambda i, j: (0, i)), pl.BlockSpec((L, bn), lambda i, j: (0, j))],
               [_sds((K, N), F32)], [pl.BlockSpec((bm, bn), lambda i, j: (i, j))], TN, 1)


def _mm_act_roww(name, a, w, bm_pref, bn_pref, bk_pref):
    L, K = a.shape
    N = w.shape[1]
    bm, bn, bk = _pick(L, bm_pref), _pick(N, bn_pref), _pick(K, bk_pref)
    nk = K // bk
    return _mm(name, (L // bm, N // bn, nk), [a, w],
               [pl.BlockSpec((bm, bk), lambda i, j, k: (i, k)), pl.BlockSpec((bk, bn), lambda i, j, k: (k, j))],
               [_sds((L, N), F32)], [pl.BlockSpec((bm, bn), lambda i, j, k: (i, j))], NN, nk,
               acc_shape=(bm, bn))


def _up_epilogue(val, extras, outs):
    outs[0][...] = val
    r = jnp.maximum(val, 0.0)
    outs[1][...] = (r * r).astype(BF16)


def _mlp_down_bwd(df, w_down, a_up):
    L, D = df.shape
    F = w_down.shape[0]
    bm = _pick(L, 1088)
    bn = _pick(F, 1024)

    def epilogue(val, extras, outs):
        outs[0][...] = (val * (2.0 * jnp.maximum(extras[0][...], 0.0))).astype(BF16)

    blk = pl.BlockSpec((bm, bn), lambda i, j: (i, j))
    return _mm("mlp_down_bwd", (L // bm, F // bn), [df, w_down, a_up],
               [pl.BlockSpec((bm, D), lambda i, j: (i, 0)), pl.BlockSpec((bn, D), lambda i, j: (j, 0)), blk],
               [_sds((L, F), BF16)], [blk], NT, 1, epilogue)


ANY = pl.BlockSpec(memory_space=pl.ANY)
HBM = pl.BlockSpec(memory_space=pltpu.HBM)
SEM = pl.BlockSpec(memory_space=pltpu.SEMAPHORE)
EFFECT = pltpu.SideEffectType.DATAFLOW_SIDE_EFFECTING
N_CHIPS = 4


def _place():
    x, y, c = lax.axis_index("x"), lax.axis_index("y"), lax.axis_index("c")
    return x, y, c


def _chip_at(x, y, k):
    px = 1 - x if k & 2 else x
    py = 1 - y if k & 1 else y
    return px, py


def _cast_into_slab(w2d, chip, dtype):
    R, C = w2d.shape
    T = _elem_tile(R, C)

    def body(p_ref, w_ref, o_ref):
        o_ref[...] = w_ref[...].astype(dtype)

    return _call(
        body, (w2d,), name="cast_into_slab", grid=(R // T,), scalars=jnp.reshape(chip, (1,)).astype(jnp.int32),
        in_specs=[pl.BlockSpec((T, C), lambda i, p: (i, 0))],
        out_specs=pl.BlockSpec((None, T, C), lambda i, p: (p[0], i, 0)),
        out_shape=_sds((N_CHIPS, R, C), dtype), compiler_params=_params(1))


TOKEN = jax.ShapeDtypeStruct((8, LANES), F32)


class _Sems:
    def __init__(self, items, shape):
        self.items, self.shape = list(items), tuple(shape)

    def pair(self, idx):
        flat = 0
        for i, n in zip(idx, self.shape):
            flat = flat * n + i
        half = len(self.items) // 2
        return self.items[flat], self.items[half + flat]


def _sem_count(shape):
    n = 1
    for s in shape:
        n *= s
    return n


def _remote(src, dst, sems, idx, device):
    send, recv = sems.pair(idx)
    return pltpu.make_async_remote_copy(src_ref=src, dst_ref=dst, send_sem=send, recv_sem=recv,
                                        device_id=device, device_id_type=MESH)


def _thru(arrays):
    return ([pltpu.with_memory_space_constraint(a, pltpu.HBM) for a in arrays],
            [pltpu.HBM(a.shape, a.dtype) for a in arrays])


def _comm_start(name, arrays, sem_shape, plan, follows=False):
    na, ns = len(arrays), 2 * _sem_count(sem_shape)

    def body(*refs):
        sems, token = _Sems(refs[na:na + ns], sem_shape), refs[-1]
        for src, dst, idx, device in plan(refs[:na])[0]:
            _remote(src, dst, sems, idx, device).start()
        token[...] = jnp.zeros(token.shape, F32)

    ins, outs = _thru(arrays)
    res = _call(
        body, ins, name=name, in_specs=[HBM] * na, mark=-1, follows=follows,
        out_specs=[SEM] * ns + [HBM] * na + [pl.BlockSpec(memory_space=pltpu.VMEM)],
        out_shape=[pltpu.SemaphoreType.DMA(())] * ns + outs + [TOKEN],
        input_output_aliases={a: ns + a for a in range(na)},
        compiler_params=pltpu.CompilerParams(has_side_effects=EFFECT))
    return _Sems(res[:ns], sem_shape), list(res[ns:ns + na])


def _wait_plans(refs, sem_refs, waits):
    x, y, c = _place()
    at = 0
    for sems, plan in waits:
        here = _Sems(sem_refs[at:at + len(sems.items)], sems.shape)
        at += len(sems.items)
        _, mine, arrivals = plan(refs)
        for dst, idx in arrivals:
            _remote(dst, dst, here, idx, (x, y, c)).wait_recv()
        for src, idx in mine:
            _remote(src, src, here, idx, (x, y, c)).wait_send()


def _comm_wait(name, arrays, waits):
    na = len(arrays)
    sem_items = [s for sems, _ in waits for s in sems.items]
    ns = len(sem_items)

    def body(*refs):
        _wait_plans(refs[:na], refs[na:na + ns], waits)
        refs[-1][...] = jnp.zeros(refs[-1].shape, F32)

    ins, outs = _thru(arrays)
    res = _call(
        body, ins + sem_items, name=name, in_specs=[HBM] * na + [SEM] * ns, mark=-1,
        out_specs=[HBM] * na + [pl.BlockSpec(memory_space=pltpu.VMEM)], out_shape=outs + [TOKEN],
        input_output_aliases={a: a for a in range(na)},
        compiler_params=pltpu.CompilerParams(has_side_effects=EFFECT))
    return list(res[:na])


def _comm_relay(name, arrays, sems, plan, sem_shape, next_plan):
    na, ns_in, ns_out = len(arrays), len(sems.items), 2 * _sem_count(sem_shape)

    def body(*refs):
        bufs = refs[:na]
        sems_in = _Sems(refs[na:na + ns_in], sems.shape)
        sems_out = _Sems(refs[na + ns_in:na + ns_in + ns_out], sem_shape)
        x, y, c = _place()
        _, mine, arrivals = plan(bufs)
        onward = next_plan(bufs)[0]
        for dst, idx in arrivals:
            _remote(dst, dst, sems_in, idx, (x, y, c)).wait_recv()
            for src, to, idx2, device, after_idx in onward:
                if after_idx == idx:
                    _remote(src, to, sems_out, idx2, device).start()
        for src, idx in mine:
            _remote(src, src, sems_in, idx, (x, y, c)).wait_send()
        refs[-1][...] = jnp.zeros(refs[-1].shape, F32)

    ins, outs = _thru(arrays)
    res = _call(
        body, ins + sems.items, name=name, in_specs=[HBM] * na + [SEM] * ns_in, mark=-1,
        out_specs=[SEM] * ns_out + [HBM] * na + [pl.BlockSpec(memory_space=pltpu.VMEM)],
        out_shape=[pltpu.SemaphoreType.DMA(())] * ns_out + outs + [TOKEN],
        input_output_aliases={a: ns_out + a for a in range(na)},
        compiler_params=pltpu.CompilerParams(has_side_effects=EFFECT))
    return _Sems(res[:ns_out], sem_shape), list(res[ns_out:ns_out + na])


def _half(ref, q, which):
    h = ref.shape[1] // 2
    return ref.at[q, pl.ds(which * h, h)]


def _quarter(ref, q, half, which):
    h = ref.shape[1] // 2
    return ref.at[q, pl.ds(half * h + which * (h // 2), h // 2)]


def _gather_plan(n_halved):
    def plan(refs):
        x, y, c = _place()
        p = 2 * x + y
        starts, mine, arrivals = [], [], []
        for n, ref in enumerate(refs):
            for k in range(1, N_CHIPS if n >= n_halved else 3):
                px, py = _chip_at(x, y, k)
                q = 2 * px + py
                out = _half(ref, p, c) if n < n_halved else ref.at[p]
                inc = _half(ref, q, c) if n < n_halved else ref.at[q]
                starts.append((out, out, (n, k - 1), (px, py, c)))
                mine.append((out, (n, k - 1)))
                arrivals.append((inc, (n, k - 1)))
        return starts, mine, arrivals
    return plan


def _spread_plan(n_halved, part):
    def plan(refs):
        x, y, c = _place()
        p = 2 * x + y
        starts, mine, arrivals = [], [], []
        for n in range(n_halved):
            for k in (1, 2):
                px, py = _chip_at(x, y, k)
                q = 2 * px + py
                tx, ty = _chip_at(x, y, 3 - k)
                dx, dy = _chip_at(x, y, 3)
                piece = _quarter(refs[n], q, c, 2 - k)
                landed = _half(refs[n], q, c)
                starts.append((piece, piece, (0, n, k - 1), (tx, ty, c), (n, k - 1)))
                starts.append((landed, landed, (1, n, k - 1), (x, y, 1 - c), (n, k - 1)))
                if part != 1:
                    mine.append((piece, (0, n, k - 1)))
                    arrivals.append((_quarter(refs[n], 2 * dx + dy, c, k - 1), (0, n, 2 - k)))
                if part != 0:
                    mine.append((landed, (1, n, k - 1)))
                    arrivals.append((_half(refs[n], q, 1 - c), (1, n, k - 1)))
        return starts, mine, arrivals
    return plan


def _last_hand_on_plan(n_halved):
    def plan(refs):
        x, y, c = _place()
        dx, dy = _chip_at(x, y, 3)
        d = 2 * dx + dy
        starts, mine, arrivals = [], [], []
        for n in range(n_halved):
            for k in (1, 2):
                piece = _quarter(refs[n], d, c, k - 1)
                starts.append((piece, piece, (n, k - 1), (x, y, 1 - c), (0, n, 2 - k)))
                mine.append((piece, (n, k - 1)))
                arrivals.append((_quarter(refs[n], d, 1 - c, k - 1), (n, k - 1)))
        return starts, mine, arrivals
    return plan


def _swap_plan(n):
    def plan(refs):
        x, y, c = _place()
        starts, mine, arrivals = [], [], []
        for a in range(n):
            h = refs[a].shape[1] // 2
            src = refs[a].at[:, pl.ds((1 - c) * h, h)]
            starts.append((src, refs[n + a], (a,), (x, y, 1 - c)))
            mine.append((src, (a,)))
            arrivals.append((refs[n + a], (a,)))
        return starts, mine, arrivals
    return plan


def _scatter_plan(n):
    def plan(refs):
        x, y, c = _place()
        starts, mine, arrivals = [], [], []
        for a in range(n):
            for k in range(1, N_CHIPS):
                px, py = _chip_at(x, y, k)
                src = refs[a].at[2 * px + py]
                starts.append((src, refs[n + a].at[k - 1], (a, k - 1), (px, py, c)))
                mine.append((src, (a, k - 1)))
                arrivals.append((refs[n + a].at[k - 1], (a, k - 1)))
        return starts, mine, arrivals
    return plan


def _share_plan(n):
    def plan(refs):
        x, y, c = _place()
        starts, mine, arrivals = [], [], []
        for a in range(n):
            h = refs[a].shape[0] // 2
            own = refs[a].at[pl.ds(c * h, h)]
            starts.append((own, own, (a,), (x, y, 1 - c)))
            mine.append((own, (a,)))
            arrivals.append((refs[a].at[pl.ds((1 - c) * h, h)], (a,)))
        return starts, mine, arrivals
    return plan


N_DEVICES = 8


def _packs_plan(refs):
    buf = refs[0]
    x, y, c = _place()
    me = 4 * x + 2 * y + c
    starts, mine, arrivals = [], [], []
    for r in range(1, N_DEVICES):
        peer = (1 - x if r & 4 else x, 1 - y if r & 2 else y, 1 - c if r & 1 else c)
        starts.append((buf.at[me], buf.at[me], (r - 1,), peer))
        mine.append((buf.at[me], (r - 1,)))
        arrivals.append((buf.at[4 * peer[0] + 2 * peer[1] + peer[2]], (r - 1,)))
    return starts, mine, arrivals


class _Reduction:
    def __init__(self, tag, slabs, c_idx, chip):
        self.tag, self.n, self.c_idx, self.chip = tag, len(slabs), c_idx, chip
        lands = [lax.empty((g.shape[0], g.shape[1] // 2, g.shape[2]), g.dtype) for g in slabs]
        self.sems = _comm_start("swap_start_" + tag, list(slabs) + lands, (self.n,), _swap_plan(self.n))

    def partial(self):
        n = self.n
        sems, bufs = self.sems
        bufs = _comm_wait("swap_wait_" + self.tag, bufs, [(sems, _swap_plan(n))])
        both = [_chip_partial(g, r, self.c_idx, self.chip) for g, r in zip(bufs[:n], bufs[n:])]
        self.own = [o for _, o in both]
        parts = [p for p, _ in both]
        lands = [lax.empty((N_CHIPS - 1,) + p.shape[1:], p.dtype) for p in parts]
        self.sems = _comm_start("scatter_start_" + self.tag, parts + lands, (n, N_CHIPS - 1), _scatter_plan(n))

    def total(self):
        n = self.n
        sems, bufs = self.sems
        bufs = _comm_wait("scatter_wait_" + self.tag, bufs, [(sems, _scatter_plan(n))])
        fulls = [_sum_partials(o, r, self.c_idx) for o, r in zip(self.own, bufs[n:])]
        self.sems = _comm_start("share_start_" + self.tag, fulls, (n,), _share_plan(n))

    def finish(self):
        sems, bufs = self.sems
        return _comm_wait("share_wait_" + self.tag, bufs, [(sems, _share_plan(self.n))])


def _elem_tile(rows, cols):
    return _pick(rows, max(8, (1 << 19) // cols // 8 * 8))


def _chip_partial(grad, recv, c_idx, p_idx):
    Q, R, C = grad.shape
    h = R // 2
    T = _elem_tile(h, C)
    nt = h // T

    def body(sc_ref, g_ref, r_ref, sb_ref, own_ref):
        q = pl.program_id(1)
        s = g_ref[...] + r_ref[...]
        sb_ref[...] = s.astype(BF16)

        @pl.when(q == sc_ref[1])
        def _():
            own_ref[...] = s

    return _call(
        body, (grad, recv), name="chip_partial", grid=(nt, Q),
        scalars=jnp.stack([c_idx, p_idx]).astype(jnp.int32),
        in_specs=[pl.BlockSpec((None, T, C), lambda t, q, sc: (q, sc[0] * nt + t, 0)),
                  pl.BlockSpec((None, T, C), lambda t, q, sc: (q, t, 0))],
        out_specs=[pl.BlockSpec((None, T, C), lambda t, q, sc: (q, t, 0)),
                   pl.BlockSpec((T, C), lambda t, q, sc: (t, 0))],
        out_shape=[_sds((Q, h, C), BF16), _sds((h, C), F32)], compiler_params=_params(2))


def _sum_partials(own, parts, c_idx):
    h, C = own.shape
    T = _elem_tile(h, C)
    nt = h // T

    def body(c_ref, o_ref, p_ref, t_ref):
        t = o_ref[...]
        for k in range(N_CHIPS - 1):
            t = t + p_ref[k].astype(F32)
        t_ref[...] = t

    return _call(
        body, (own, parts), name="sum_partials", grid=(nt,), scalars=jnp.reshape(c_idx, (1,)).astype(jnp.int32),
        in_specs=[pl.BlockSpec((T, C), lambda i, c: (i, 0)),
                  pl.BlockSpec((N_CHIPS - 1, T, C), lambda i, c: (0, i, 0))],
        out_specs=pl.BlockSpec((T, C), lambda i, c: (c[0] * nt + i, 0)),
        out_shape=_sds((2 * h, C), F32), compiler_params=_params(1))


def _pack_rows(name, parts, slot=None, n_slots=1):
    width = parts[0].shape[1]
    offsets, at = [], 0
    for p in parts:
        offsets.append(at)
        at += p.shape[0]
    total = -(-at // 8) * 8
    lead = 0 if slot is None else 1

    def body(*refs):
        out = refs[-1]
        out[...] = jnp.zeros(out.shape, F32)
        for ref, o in zip(refs[lead:-1], offsets):
            out[o:o + ref.shape[0], :] = ref[...]

    if slot is None:
        whole = pl.BlockSpec(memory_space=pltpu.VMEM)
        return _call(body, list(parts), name=name, in_specs=[whole] * len(parts), out_specs=whole,
                     out_shape=_sds((total, width), F32))
    return _call(body, list(parts), name=name, grid=(1,), scalars=jnp.reshape(slot, (1,)).astype(jnp.int32),
                 in_specs=[pl.BlockSpec(p.shape, lambda i, s: (0, 0)) for p in parts],
                 out_specs=pl.BlockSpec((None, total, width), lambda i, s: (s[0], 0, 0)),
                 out_shape=_sds((n_slots, total, width), F32))


def _sum_packs(packs):
    n, R, C = packs.shape

    def body(p_ref, o_ref):
        t = p_ref[0]
        for k in range(1, n):
            t = t + p_ref[k]
        o_ref[...] = t

    return _call(
        body, (packs,), name="sum_packs", grid=(1,), in_specs=[pl.BlockSpec((n, R, C), lambda i: (0, 0, 0))],
        out_specs=pl.BlockSpec((R, C), lambda i: (0, 0)), out_shape=_sds((R, C), F32), compiler_params=_params(1))


def _adamw(w, g, m, v):
    R, C = w.shape
    T = _elem_tile(R, C)

    def body(w_ref, g_ref, m_ref, v_ref, d_ref, m2_ref, v2_ref):
        g_ = g_ref[...]
        m2 = ADAM_B1 * m_ref[...] + (1.0 - ADAM_B1) * g_
        v2 = ADAM_B2 * v_ref[...] + (1.0 - ADAM_B2) * (g_ * g_)
        m_hat = m2 / (1.0 - ADAM_B1 ** ADAM_STEP)
        v_hat = v2 / (1.0 - ADAM_B2 ** ADAM_STEP)
        d_ref[...] = -ADAM_LR * (m_hat / (jnp.sqrt(v_hat) + ADAM_EPS) + ADAM_WD * w_ref[...])
        m2_ref[...] = m2
        v2_ref[...] = v2

    blk = pl.BlockSpec((T, C), lambda i: (i, 0))
    return _call(
        body, (w, g, m, v), name="adamw", grid=(R // T,), in_specs=[blk] * 4, out_specs=[blk] * 3,
        out_shape=[_sds((R, C), F32)] * 3, compiler_params=_params(1))


def _adamw_rows(w, g, m, v, row0, prev=None):
    R, C = w.shape
    T = _elem_tile(g.shape[0], C)
    off = row0 // T

    def body(w_ref, g_ref, m_ref, v_ref, *rest):
        d_ref, m2_ref, v2_ref, g2_ref = rest[-4:]
        g_ = g_ref[...]
        m2 = ADAM_B1 * m_ref[...] + (1.0 - ADAM_B1) * g_
        v2 = ADAM_B2 * v_ref[...] + (1.0 - ADAM_B2) * (g_ * g_)
        m_hat = m2 / (1.0 - ADAM_B1 ** ADAM_STEP)
        v_hat = v2 / (1.0 - ADAM_B2 ** ADAM_STEP)
        d_ref[...] = -ADAM_LR * (m_hat / (jnp.sqrt(v_hat) + ADAM_EPS) + ADAM_WD * w_ref[...])
        m2_ref[...] = m2
        v2_ref[...] = v2
        g2_ref[...] = g_

    here = pl.BlockSpec((T, C), lambda i: (off + i, 0))
    piece = pl.BlockSpec((T, C), lambda i: (i, 0))
    done = tuple(prev or ())
    return _call(
        body, (w, g, m, v) + done, name="adamw_rows", grid=(g.shape[0] // T,), follows=prev is None,
        in_specs=[here, piece, here, here] + [ANY] * len(done), out_specs=[here] * 4,
        out_shape=[_sds((R, C), F32)] * 4, input_output_aliases={4 + j: j for j in range(len(done))},
        compiler_params=_params(1))


BIG = ("w_in", "w_pool_out", "w_conv_out", "w_o", "w_up", "w_down", "w_pool_grp")
VECTORS = ("g_pre_mix", "pool_scale", "b_dw", "conv_ln_g", "conv_ln_b", "g_post_mix", "g_pre_mlp", "g_post_mlp")
WEIGHTS = ("meta", "g_pre_mix", "w_in", "w_pool_grp", "pool_scale", "w_pool_out", "w_dw", "b_dw", "conv_ln_g",
           "conv_ln_b", "w_conv_out", "w_o", "g_post_mix", "g_pre_mlp", "w_up", "w_down", "g_post_mlp")


def _as_rows(a, width):
    r, cols = a.shape
    return a.reshape(r * (cols // width), width)


def _step(w, m, v, x, tgt):
    S, D = x.shape
    P = D // 2
    xi, yi, ci = _place()
    chip = 2 * xi + yi
    _CHAIN["after"] = None

    C = D // 2
    G = POOL_GROUPS
    GD = P // G
    GS = GD // N_CHIPS
    Q = N_CHIPS
    vecs = {k: w[k] for k in VECTORS}
    shard2d = {k: w[k].reshape(-1, w[k].shape[-1]) for k in BIG}
    grads, delta, new_m, new_v = {}, {}, {}, {}

    def update(names, reduced):
        for k, g in zip(names, reduced):
            delta[k], new_m[k], new_v[k], grads[k] = _adamw_rows(
                shard2d[k], g, m[k].reshape(shard2d[k].shape), v[k].reshape(shard2d[k].shape), 0)

    groups = dict(a=(("w_in", "w_pool_grp"), ("w_dw", "meta")), b=(("w_pool_out", "w_conv_out", "w_o"), ()),
                  c=(("w_up",), ()), d=(("w_down",), ()))
    flying = {}

    def start(tag, follows):
        halved, whole = groups[tag]
        flying[tag] = _comm_start("gather_start_" + tag, flying[tag], (len(halved + whole), N_CHIPS - 1),
                                  _gather_plan(len(halved)), follows=follows)

    def spread(tag):
        nh = len(groups[tag][0])
        sems, bufs = flying[tag]
        flying[tag] = _comm_relay("gather_relay_" + tag, bufs, sems, _gather_plan(nh), (2, nh, 2),
                                  _spread_plan(nh, 0))

    def landed(tag):
        halved, whole = groups[tag]
        nh = len(halved)
        sems, bufs = flying.pop(tag)
        last, tree = _comm_relay("gather_relay2_" + tag, bufs[:nh], sems, _spread_plan(nh, 0), (nh, 2),
                                 _last_hand_on_plan(nh))
        done = _comm_wait("gather_wait_" + tag, tree,
                          [(sems, _spread_plan(nh, 1)), (last, _last_hand_on_plan(nh))])
        return dict(zip(halved + whole, done + bufs[nh:]))

    for tag, (halved, whole) in groups.items():
        flying[tag] = [_cast_into_slab(shard2d[k], chip, BF16) for k in halved]
        flying[tag] += [_cast_into_slab(w[k], chip, F32) for k in whole]
        if tag == "a":
            start(tag, False)
    spread("a")
    start("b", True)

    got = landed("a")
    win_g = got["w_in"]
    w_grp = got["w_pool_grp"].reshape(N_CHIPS, G, GS, GD).transpose(1, 0, 2, 3).reshape(G, GD, GD)
    w_dw = got["w_dw"].transpose(1, 0, 2).reshape(CONV_TAPS, P)
    meta = got["meta"].transpose(1, 0, 2).reshape(N_META, D)
    h0 = jnp.concatenate([jnp.zeros((PAD_ROWS, D), F32), meta, x], axis=0)
    u1 = _pre_norm(h0, vecs["g_pre_mix"])
    spread("b")
    start("c", True)
    start("d", True)
    proj = _mm_act_colw("proj", u1, win_g, 256)
    d, ya_pre = _pool_fwd(proj, w_grp, vecs["pool_scale"])
    s, c = _conv_fwd(proj, C, w_dw, vecs["b_dw"], vecs["conv_ln_g"], vecs["conv_ln_b"])
    spread("c")
    got = landed("b")
    wpo_g, wco_g, w_o = got["w_pool_out"], got["w_conv_out"], got["w_o"].reshape(D, D)
    mix, ya, yb = _mix_fwd(ya_pre, s, wpo_g, wco_g, proj, D)
    o = _mm_act_roww("attn_out", mix, w_o, 1088, 1024, 2048)
    spread("d")
    h1, u2 = _mid_norm(o, h0, vecs["g_post_mix"], vecs["g_pre_mlp"])
    wup_g = landed("c")["w_up"]
    a_up, fact = _mm_act_colw("mlp_up", u2, wup_g, 512, _up_epilogue, (F32, BF16))
    w_down = landed("d")["w_down"].reshape(-1, D)
    f = _mm_act_roww("mlp_down", fact, w_down, 1088, 1024, 2048)
    dy, df, dg_post_mlp, loss = _loss_head(f, h1, tgt, vecs["g_post_mlp"])

    g_w_down = _mm_wgrad("dw_down", fact, df, 1024, 1024)
    red1 = _Reduction("1", [g_w_down.reshape(N_CHIPS, -1, D)], ci, chip)
    da_up = _mlp_down_bwd(df, w_down, a_up)
    red1.partial()
    g_w_up = _mm_wgrad_colw("dw_up", u2, da_up, Q, 1024, 1024)
    red2 = _Reduction("2", [g_w_up], ci, chip)
    du2 = _mm_grad_colw_t("du2", da_up, wup_g, 1088, 1024)
    red2.partial()
    dh1, do, dg_pre_mlp, dg_post_mix = _mid_norm_bwd(dy, du2, h1, o, vecs["g_pre_mlp"], vecs["g_post_mix"])
    g_w_o = _mm_wgrad("dw_o", mix, do, 1024, 1024)
    red1.total()
    dya, dyb, dga, dgb = _mix_bwd(do, w_o, proj, ya, yb)
    update(("w_down",), red1.finish())
    g_wpo = _mm_wgrad_colw("dw_pool_out", ya_pre, dya, Q, 1024, 512)
    g_wco = _mm_wgrad_colw("dw_conv_out", s, dyb, Q, 1024, 512)
    red3 = _Reduction("3", [g_w_o.reshape(N_CHIPS, D // N_CHIPS, D), g_wpo, g_wco], ci, chip)
    dya_pre = _mm_grad_colw_t("dya_pre", dya, wpo_g, 1088, 1024)
    ds = _mm_grad_colw_t("ds", dyb, wco_g, 1088, 1024)
    red3.partial()
    dz, g_w_grp, dscale = _pool_bwd(dya_pre, d, w_grp, vecs["pool_scale"])
    dc, dln_g, dln_b, db_dw = _conv_ln_bwd(ds, c, vecs["conv_ln_g"], vecs["conv_ln_b"])
    red2.total()
    dv, dgc, g_w_dw = _conv_bwd(dc, proj, C, w_dw)
    update(("w_up",), red2.finish())
    dproj = jnp.concatenate([dz, dv, dgc, dga, dgb], axis=1)
    half_k = D // 2
    g_w_grp = g_w_grp.reshape(G, N_CHIPS, GS, GD).transpose(1, 0, 2, 3).reshape(N_CHIPS, G * GS, GD)
    g_in_a = _mm_wgrad_colw("dw_in_a", u1, dproj, Q, 512, 1792, rows=(0, half_k))
    red4a = _Reduction("4a", [g_in_a, g_w_grp], ci, chip)
    g_in_b = _mm_wgrad_colw("dw_in_b", u1, dproj, Q, 512, 1792, rows=(half_k, half_k))
    red4a.partial()
    red4b = _Reduction("4b", [g_in_b], ci, chip)
    du1 = _mm_grad_colw_t("du1", dproj, win_g, 1088, 1024)
    red4b.partial()
    red3.total()
    grad_x, dmeta, dg_pre_mix = _pre_norm_bwd(dh1, du1, h0, vecs["g_pre_mix"])

    g_vec = dict(g_pre_mix=dg_pre_mix, pool_scale=dscale, b_dw=db_dw, conv_ln_g=dln_g, conv_ln_b=dln_b,
                 g_post_mix=dg_post_mix, g_pre_mlp=dg_pre_mlp, g_post_mlp=dg_post_mlp)
    rows = [g_w_dw, _as_rows(dmeta, P)] + [_as_rows(g_vec[k], P) for k in VECTORS]
    rows.append(jnp.broadcast_to(loss[:, :1], (1, P)))
    packs = _comm_start("packs_start", [_pack_rows("pack_grads", rows, 2 * chip + ci, N_DEVICES)],
                        (N_DEVICES - 1,), _packs_plan)
    red4a.total()
    update(("w_o", "w_pool_out", "w_conv_out"), red3.finish())
    red_in_a, red_grp = red4a.finish()
    update(("w_pool_grp",), [red_grp])
    w_in_rows = (shard2d["w_in"], m["w_in"].reshape(shard2d["w_in"].shape), v["w_in"].reshape(shard2d["w_in"].shape))
    first_rows = _adamw_rows(w_in_rows[0], red_in_a, w_in_rows[1], w_in_rows[2], 0)
    total = _sum_packs(_comm_wait("packs_wait", packs[1], [(packs[0], _packs_plan)])[0])
    at = 0
    taps_pad = g_w_dw.shape[0]
    g_dw_full = total[at:at + CONV_TAPS]
    at += taps_pad
    g_meta_full = total[at:at + 2 * N_META].reshape(N_META, D)
    at += 2 * N_META
    for k in VECTORS:
        n = w[k].shape[-1] // P
        grads[k] = total[at:at + n].reshape(1, n * P)
        at += n
    loss_total = total[at, 0]
    grads["w_dw"] = lax.dynamic_slice_in_dim(g_dw_full, chip * (P // N_CHIPS), P // N_CHIPS, axis=1)
    grads["meta"] = lax.dynamic_slice_in_dim(g_meta_full, chip * (D // N_CHIPS), D // N_CHIPS, axis=1)

    small_names = [k for k in WEIGHTS if k not in BIG]

    def pack_small(tree):
        parts = []
        for k in small_names:
            a = tree[k].reshape(-1, tree[k].shape[-1])
            flat = a.reshape(-1)
            parts.append(jnp.pad(flat, (0, -flat.shape[0] % P)).reshape(-1, P))
        return _pack_rows("pack_small", parts)

    sd, sm, sv = _adamw(pack_small(w), pack_small(grads), pack_small(m), pack_small(v))
    at = 0
    for k in small_names:
        a = w[k].reshape(-1, w[k].shape[-1])
        n = -(-a.size // P)
        for tree, packed in ((delta, sd), (new_m, sm), (new_v, sv)):
            tree[k] = packed[at:at + n].reshape(-1)[:a.size].reshape(a.shape)
        at += n

    red4b.total()
    delta["w_in"], new_m["w_in"], new_v["w_in"], grads["w_in"] = _adamw_rows(
        w_in_rows[0], red4b.finish()[0], w_in_rows[1], w_in_rows[2], half_k, prev=first_rows)
    return loss_total, grad_x, grads, delta, new_m, new_v


def kernel(x, meta, g_pre_mix, w_in, w_pool_grp, pool_scale, w_pool_out, w_dw, b_dw, conv_ln_g, conv_ln_b, w_conv_out, w_o, g_post_mix, g_pre_mlp, w_up, w_down, g_post_mlp, loss_target, m_meta, m_g_pre_mix, m_w_in, m_w_pool_grp, m_pool_scale, m_w_pool_out, m_w_dw, m_b_dw, m_conv_ln_g, m_conv_ln_b, m_w_conv_out, m_w_o, m_g_post_mix, m_g_pre_mlp, m_w_up, m_w_down, m_g_post_mlp, v_meta, v_g_pre_mix, v_w_in, v_w_pool_grp, v_pool_scale, v_w_pool_out, v_w_dw, v_b_dw, v_conv_ln_g, v_conv_ln_b, v_w_conv_out, v_w_o, v_g_post_mix, v_g_pre_mlp, v_w_up, v_w_down, v_g_post_mlp):
    args = dict(locals())
    shapes = {k: args[k].shape for k in WEIGHTS}
    w = {k: args[k] for k in WEIGHTS}
    m = {k: args["m_" + k] for k in WEIGHTS}
    v = {k: args["v_" + k] for k in WEIGHTS}
    for tree in (w, m, v):
        tree["w_dw"] = tree["w_dw"].reshape(tree["w_dw"].shape[-2:])
    loss, grad_x, grads, delta, new_m, new_v = _step(w, m, v, x[0], loss_target[0])
    out = [loss, grad_x[None]]
    for tree in (grads, delta, new_m, new_v):
        out += [tree[k].reshape(shapes[k]) for k in WEIGHTS]
    return tuple(out)
```

```python
import jax
import jax.numpy as jnp
from jax import lax
from jax.experimental import pallas as pl
from jax.experimental.pallas import tpu as pltpu
from jax.experimental.pallas import tpu_sc as plsc

F32 = jnp.float32
BF16 = jnp.bfloat16

N_META = 16
PAD_ROWS = 112
TOKEN_ROW0 = PAD_ROWS + N_META
POOL_GROUPS = 4
CONV_TAPS = 31
HALO = 32
CONV_ROWS = 128
LANES = 128
RMS_EPS = 1e-6
LN_EPS = 1e-5
ADAM_LR = 0.001
ADAM_B1 = 0.9
ADAM_B2 = 0.999
ADAM_EPS = 1e-08
ADAM_WD = 0.01
ADAM_STEP = 10
VMEM_LIMIT_MB = 56

MESH = pl.DeviceIdType.MESH
NN = (((1,), (0,)), ((), ()))
NT = (((1,), (1,)), ((), ()))
TN = (((0,), (0,)), ((), ()))


def _pick(n, pref):
    if n <= pref:
        return n
    if n % pref == 0:
        return pref
    for step in (LANES, 8, 1):
        t = (pref // step) * step
        while t >= step:
            if n % t == 0:
                return t
            t -= step
    return n


def _params(n_axes, vmem_mb=VMEM_LIMIT_MB):
    return pltpu.CompilerParams(dimension_semantics=("arbitrary",) * n_axes,
                                vmem_limit_bytes=vmem_mb << 20)


def _sigmoid(x):
    return jax.nn.sigmoid(x)


_CHAIN = {"after": []}


def _call(body, args, *, in_specs, out_specs, out_shape, grid=(), scalars=None, mark=0, follows=True, made_from=None,
          **kw):
    pending = _CHAIN["after"]
    after = pending if follows else []
    n = len(args)
    lead = 0 if scalars is None else 1
    specs = list(in_specs)
    operands = list(args)
    fn = body
    if after:
        def fn(*refs):
            body(*refs[:lead + n], *refs[lead + n + len(after):])
        specs += [pl.BlockSpec(memory_space=pl.ANY)] * len(after)
        operands += after
    if scalars is None:
        if grid:
            kw["grid"] = grid
        res = pl.pallas_call(fn, in_specs=specs, out_specs=out_specs, out_shape=out_shape, **kw)(*operands)
    else:
        grid_spec = pltpu.PrefetchScalarGridSpec(num_scalar_prefetch=1, grid=grid, in_specs=specs, out_specs=out_specs)
        res = pl.pallas_call(fn, grid_spec=grid_spec, out_shape=out_shape, **kw)(scalars, *operands)
    outs = res if isinstance(res, (list, tuple)) else [res]
    consumed = args if made_from is None else made_from
    kept = [] if follows else [m for m in pending if not any(m is a for a in consumed)]
    _CHAIN["after"] = kept + [outs[mark]]
    return res


def _store(val, extras, outs):
    outs[0][...] = val.astype(outs[0].dtype)


def _mm(name, grid, arrays, in_specs, out_shapes, out_specs, dims, nk, epilogue=_store, acc_shape=None):
    n_in, n_out = len(arrays), len(out_shapes)

    def body(*refs):
        extras = refs[2:n_in]
        outs = refs[n_in:n_in + n_out]
        part = lax.dot_general(refs[0][...], refs[1][...], dims, preferred_element_type=F32)
        if nk == 1:
            epilogue(part, extras, outs)
        else:
            acc = refs[n_in + n_out]
            k = pl.program_id(len(grid) - 1)

            @pl.when(k == 0)
            def _():
                acc[...] = part

            @pl.when(k > 0)
            def _():
                acc[...] += part

            @pl.when(k == nk - 1)
            def _():
                epilogue(acc[...], extras, outs)

    scratch = [pltpu.VMEM(acc_shape, F32)] if nk > 1 else []
    single = n_out == 1
    return _call(
        body, arrays, name=name, grid=grid, in_specs=in_specs,
        out_specs=out_specs[0] if single else out_specs,
        out_shape=out_shapes[0] if single else out_shapes,
        scratch_shapes=scratch, compiler_params=_params(len(grid)))


def _sds(shape, dtype):
    return jax.ShapeDtypeStruct(shape, dtype)


def _rms_scale(h):
    return lax.rsqrt(jnp.mean(h * h, axis=-1, keepdims=True) + RMS_EPS)


def _rms_bwd(du, h, g):
    r = _rms_scale(h)
    y = h * r
    dy = du * g
    dh = r * (dy - y * jnp.mean(dy * y, axis=-1, keepdims=True))
    return dh, jnp.sum(du * y, axis=0, keepdims=True)


def _row_tile(L):
    return _pick(L, 272)


def _pre_norm(h0, g):
    L, D = h0.shape
    T = _row_tile(L)

    def body(h_ref, g_ref, u_ref):
        h = h_ref[...]
        u_ref[...] = (h * _rms_scale(h) * g_ref[...]).astype(BF16)

    return _call(
        body, (h0, g), name="pre_norm", grid=(L // T,),
        in_specs=[pl.BlockSpec((T, D), lambda i: (i, 0)), pl.BlockSpec((1, D), lambda i: (0, 0))],
        out_specs=pl.BlockSpec((T, D), lambda i: (i, 0)),
        out_shape=_sds((L, D), BF16), compiler_params=_params(1))


def _mid_norm(o, h0, g_post, g_pre):
    L, D = h0.shape
    T = _row_tile(L)

    def body(o_ref, h_ref, gp_ref, gm_ref, h1_ref, u2_ref):
        o_ = o_ref[...]
        h1 = h_ref[...] + o_ * _rms_scale(o_) * gp_ref[...]
        h1_ref[...] = h1
        u2_ref[...] = (h1 * _rms_scale(h1) * gm_ref[...]).astype(BF16)

    row = pl.BlockSpec((T, D), lambda i: (i, 0))
    vec = pl.BlockSpec((1, D), lambda i: (0, 0))
    return _call(
        body, (o, h0, g_post, g_pre), name="mid_norm", grid=(L // T,),
        in_specs=[row, row, vec, vec], out_specs=[row, row],
        out_shape=[_sds((L, D), F32), _sds((L, D), BF16)], compiler_params=_params(1))


def _loss_head(f, h1, tgt, g_post):
    L, D = h1.shape
    T = TOKEN_ROW0
    n = L // T

    def body(f_ref, h_ref, t_ref, g_ref, dy_ref, df_ref, dg_ref, loss_ref):
        i = pl.program_id(0)
        f_ = f_ref[...]
        g = g_ref[...]
        y = h_ref[...] + f_ * _rms_scale(f_) * g
        live = (i > 0).astype(F32)
        diff = (y - t_ref[...]) * live
        part = 0.5 * jnp.sum(jnp.mean(diff * diff, axis=-1, keepdims=True), axis=0, keepdims=True)
        dy = diff * (1.0 / D)
        dy_ref[...] = dy
        df, dg = _rms_bwd(dy, f_, g)
        df_ref[...] = df.astype(BF16)

        @pl.when(i == 0)
        def _():
            dg_ref[...] = dg
            loss_ref[...] = jnp.broadcast_to(part, loss_ref.shape)

        @pl.when(i > 0)
        def _():
            dg_ref[...] += dg
            loss_ref[...] += jnp.broadcast_to(part, loss_ref.shape)

    row = pl.BlockSpec((T, D), lambda i: (i, 0))
    vec = pl.BlockSpec((1, D), lambda i: (0, 0))
    return _call(
        body, (f, h1, tgt, g_post), name="loss_head", grid=(n,),
        in_specs=[row, row, pl.BlockSpec((T, D), lambda i: (jnp.maximum(i - 1, 0), 0)), vec],
        out_specs=[row, row, vec, pl.BlockSpec((1, LANES), lambda i: (0, 0))],
        out_shape=[_sds((L, D), F32), _sds((L, D), BF16), _sds((1, D), F32), _sds((1, LANES), F32)],
        compiler_params=_params(1))


def _mid_norm_bwd(dy, du2, h1, o, g_pre, g_post):
    L, D = h1.shape
    T = _row_tile(L)

    def body(dy_ref, du_ref, h_ref, o_ref, gm_ref, gp_ref, dh1_ref, do_ref, dgm_ref, dgp_ref):
        i = pl.program_id(0)
        dh, dgm = _rms_bwd(du_ref[...], h_ref[...], gm_ref[...])
        dh1 = dy_ref[...] + dh
        dh1_ref[...] = dh1
        do, dgp = _rms_bwd(dh1, o_ref[...], gp_ref[...])
        do_ref[...] = do.astype(BF16)

        @pl.when(i == 0)
        def _():
            dgm_ref[...] = dgm
            dgp_ref[...] = dgp

        @pl.when(i > 0)
        def _():
            dgm_ref[...] += dgm
            dgp_ref[...] += dgp

    row = pl.BlockSpec((T, D), lambda i: (i, 0))
    vec = pl.BlockSpec((1, D), lambda i: (0, 0))
    return _call(
        body, (dy, du2, h1, o, g_pre, g_post), name="mid_norm_bwd", grid=(L // T,),
        in_specs=[row, row, row, row, vec, vec], out_specs=[row, row, vec, vec],
        out_shape=[_sds((L, D), F32), _sds((L, D), BF16), _sds((1, D), F32), _sds((1, D), F32)],
        compiler_params=_params(1))


def _pre_norm_bwd(dh1, du1, h0, g):
    L, D = h0.shape
    T = TOKEN_ROW0
    n = L // T

    def body(dh_ref, du_ref, h_ref, g_ref, gx_ref, dmeta_ref, dg_ref):
        i = pl.program_id(0)
        dh, dg = _rms_bwd(du_ref[...], h_ref[...], g_ref[...])
        dh0 = dh_ref[...] + dh
        gx_ref[...] = dh0

        @pl.when(i == 0)
        def _():
            dmeta_ref[...] = dh0[PAD_ROWS:, :]
            dg_ref[...] = dg

        @pl.when(i > 0)
        def _():
            dg_ref[...] += dg

    row = pl.BlockSpec((T, D), lambda i: (i, 0))
    vec = pl.BlockSpec((1, D), lambda i: (0, 0))
    return _call(
        body, (dh1, du1, h0, g), name="pre_norm_bwd", grid=(n,),
        in_specs=[row, row, row, vec],
        out_specs=[pl.BlockSpec((T, D), lambda i: (jnp.maximum(i - 1, 0), 0)),
                   pl.BlockSpec((N_META, D), lambda i: (0, 0)), vec],
        out_shape=[_sds((L - T, D), F32), _sds((N_META, D), F32), _sds((1, D), F32)],
        compiler_params=_params(1))


def _window_sum(z, g, shift_sign, L):
    s = z
    for j in range(POOL_GROUPS):
        k = 1 << j
        nxt = s + pltpu.roll(s, k if shift_sign > 0 else L - k, 0)
        s = jnp.where(j <= g, nxt, s)
    return s


def _inv_count(g, L):
    t = lax.broadcasted_iota(jnp.int32, (L, 1), 0)
    w = jnp.left_shift(2, g)
    cnt = jnp.clip(t - (PAD_ROWS - 1), 1, w)
    return 1.0 / cnt.astype(F32)


def _pool_fwd(proj, w_grp, scale):
    L = proj.shape[0]
    G, GD, _ = w_grp.shape
    P = G * GD

    def body(z_ref, w_ref, sc_ref, d_ref, ya_ref):
        g = pl.program_id(0)
        z = z_ref[...]
        d = (_window_sum(z, g, +1, L) * _inv_count(g, L) - z).astype(BF16)
        d_ref[...] = d
        y = jnp.dot(d, w_ref[...], preferred_element_type=F32)
        ya_ref[...] = (y * sc_ref[...]).astype(BF16)

    col = pl.BlockSpec((L, GD), lambda g: (0, g))
    return _call(
        body, (proj, w_grp, scale), name="pool_fwd", grid=(G,),
        in_specs=[col, pl.BlockSpec((None, GD, GD), lambda g: (g, 0, 0)), pl.BlockSpec((1, GD), lambda g: (0, g))],
        out_specs=[col, col], out_shape=[_sds((L, P), BF16), _sds((L, P), BF16)],
        compiler_params=_params(1))


def _pool_bwd(dya, d, w_grp, scale):
    L, P = dya.shape
    G, GD, _ = w_grp.shape

    def body(dya_ref, d_ref, w_ref, sc_ref, dz_ref, dw_ref, dsc_ref):
        g = pl.program_id(0)
        dya_ = dya_ref[...]
        d_ = d_ref[...]
        w = w_ref[...]
        y = jnp.dot(d_, w, preferred_element_type=F32)
        dsc_ref[...] = jnp.sum(dya_ * y, axis=0, keepdims=True)
        dy = (dya_ * sc_ref[...]).astype(BF16)
        dw_ref[...] = lax.dot_general(d_, dy, TN, preferred_element_type=F32)
        dd = lax.dot_general(dy, w, NT, preferred_element_type=F32)
        dz = _window_sum(dd * _inv_count(g, L), g, -1, L) - dd
        dz_ref[...] = dz.astype(BF16)

    col = pl.BlockSpec((L, GD), lambda g: (0, g))
    wspec = pl.BlockSpec((None, GD, GD), lambda g: (g, 0, 0))
    vec = pl.BlockSpec((1, GD), lambda g: (0, g))
    return _call(
        body, (dya, d, w_grp, scale), name="pool_bwd", grid=(G,),
        in_specs=[col, col, wspec, vec], out_specs=[col, wspec, vec],
        out_shape=[_sds((L, P), BF16), _sds((G, GD, GD), F32), _sds((1, P), F32)],
        compiler_params=_params(1))


def _fill_rotations(rot_ref, ext):
    n = ext.shape[0]
    rot_ref[0] = ext
    for r in range(1, 8):
        rot_ref[r] = pltpu.roll(ext, n - r, 0)


def _lane_chunks(C):
    step = LANES if C % LANES == 0 else C
    return [(c0, step) for c0 in range(0, C, step)]


def _conv_specs(L, C, col_v, col_g):
    T = CONV_ROWS
    per = T // HALO
    cur_v = pl.BlockSpec((T, C), lambda i: (i, col_v))
    cur_g = pl.BlockSpec((T, C), lambda i: (i, col_g))
    prev_v = pl.BlockSpec((HALO, C), lambda i: (jnp.maximum(i * per - 1, 0), col_v))
    prev_g = pl.BlockSpec((HALO, C), lambda i: (jnp.maximum(i * per - 1, 0), col_g))
    return cur_v, cur_g, prev_v, prev_g


def _glu_ext(vc, gc, vh, gh, i):
    a_cur = vc[...] * _sigmoid(gc[...])
    a_prev = vh[...] * _sigmoid(gh[...]) * (i > 0).astype(F32)
    return jnp.concatenate([a_prev, a_cur], axis=0)


def _conv_fwd(proj, C, w_dw, b_dw, ln_g, ln_b):
    L = proj.shape[0]
    T = CONV_ROWS
    P = C

    def body(vc, gc, vh, gh, w_ref, b_ref, lg_ref, lb_ref, s_ref, c_ref, rot):
        i = pl.program_id(0)
        _fill_rotations(rot, _glu_ext(vc, gc, vh, gh, i))
        for c0, cw in _lane_chunks(C):
            acc = jnp.zeros((T, cw), F32)
            for k in range(CONV_TAPS):
                q, r = divmod(HALO - (CONV_TAPS - 1) + k, 8)
                acc = acc + w_ref[k:k + 1, c0:c0 + cw] * rot[r, 8 * q:8 * q + T, c0:c0 + cw]
            c_ref[:, c0:c0 + cw] = acc + b_ref[:, c0:c0 + cw]
        c = c_ref[...]
        mu = jnp.mean(c, axis=-1, keepdims=True)
        cen = c - mu
        var = jnp.mean(cen * cen, axis=-1, keepdims=True)
        ln = cen * lax.rsqrt(var + LN_EPS) * lg_ref[...] + lb_ref[...]
        s_ref[...] = (ln * _sigmoid(ln)).astype(BF16)

    cur_v, cur_g, prev_v, prev_g = _conv_specs(L, C, P // C, P // C + 1)
    row = pl.BlockSpec((T, C), lambda i: (i, 0))
    vec = pl.BlockSpec((1, C), lambda i: (0, 0))
    return _call(
        body, (proj, proj, proj, proj, w_dw, b_dw, ln_g, ln_b), name="conv_fwd", grid=(L // T,),
        in_specs=[cur_v, cur_g, prev_v, prev_g, pl.BlockSpec((CONV_TAPS, C), lambda i: (0, 0)), vec, vec, vec],
        out_specs=[row, row], out_shape=[_sds((L, C), BF16), _sds((L, C), F32)],
        scratch_shapes=[pltpu.VMEM((8, T + HALO, C), F32)], compiler_params=_params(1))


def _conv_ln_bwd(ds, c, ln_g, ln_b):
    L, C = c.shape
    T = _row_tile(L)

    def body(ds_ref, c_ref, lg_ref, lb_ref, dc_ref, dlg_ref, dlb_ref, db_ref):
        i = pl.program_id(0)
        c_ = c_ref[...]
        g = lg_ref[...]
        mu = jnp.mean(c_, axis=-1, keepdims=True)
        cen = c_ - mu
        rstd = lax.rsqrt(jnp.mean(cen * cen, axis=-1, keepdims=True) + LN_EPS)
        xhat = cen * rstd
        ln = xhat * g + lb_ref[...]
        sg = _sigmoid(ln)
        dln = ds_ref[...] * (sg * (1.0 + ln * (1.0 - sg)))
        dxh = dln * g
        dc = rstd * (dxh - jnp.mean(dxh, axis=-1, keepdims=True)
                     - xhat * jnp.mean(dxh * xhat, axis=-1, keepdims=True))
        dc_ref[...] = dc
        dlg = jnp.sum(dln * xhat, axis=0, keepdims=True)
        dlb = jnp.sum(dln, axis=0, keepdims=True)
        db = jnp.sum(dc, axis=0, keepdims=True)

        @pl.when(i == 0)
        def _():
            dlg_ref[...] = dlg
            dlb_ref[...] = dlb
            db_ref[...] = db

        @pl.when(i > 0)
        def _():
            dlg_ref[...] += dlg
            dlb_ref[...] += dlb
            db_ref[...] += db

    row = pl.BlockSpec((T, C), lambda i: (i, 0))
    vec = pl.BlockSpec((1, C), lambda i: (0, 0))
    return _call(
        body, (ds, c, ln_g, ln_b), name="conv_ln_bwd", grid=(L // T,),
        in_specs=[row, row, vec, vec], out_specs=[row, vec, vec, vec],
        out_shape=[_sds((L, C), F32), _sds((1, C), F32), _sds((1, C), F32), _sds((1, C), F32)],
        compiler_params=_params(1))


def _conv_bwd(dc, proj, C, w_dw):
    L = proj.shape[0]
    T = CONV_ROWS
    per = T // HALO
    n = L // T
    P = C
    taps_pad = 32

    def body(dcc, dcn, vc, gc, w_ref, dv_ref, dg_ref, dw_ref, rot_d, dw_acc):
        i = pl.program_id(0)
        dc_next = dcn[...] * (i < n - 1).astype(F32)
        _fill_rotations(rot_d, jnp.concatenate([dcc[...], dc_next], axis=0))

        @pl.when(i == 0)
        def _():
            dw_acc[...] = jnp.zeros(dw_acc.shape, F32)

        for c0, cw in _lane_chunks(C):
            v = vc[:, c0:c0 + cw]
            sg = _sigmoid(gc[:, c0:c0 + cw])
            a = v * sg
            da = jnp.zeros((T, cw), F32)
            for k in range(CONV_TAPS):
                q, r = divmod(CONV_TAPS - 1 - k, 8)
                slab = rot_d[r, 8 * q:8 * q + T, c0:c0 + cw]
                da = da + w_ref[k:k + 1, c0:c0 + cw] * slab
                dw_acc[k, :, c0:c0 + cw] += jnp.sum((a * slab).reshape(T // 8, 8, cw), axis=0)
            dv_ref[:, c0:c0 + cw] = (da * sg).astype(BF16)
            dg_ref[:, c0:c0 + cw] = (da * v * sg * (1.0 - sg)).astype(BF16)

        @pl.when(i == n - 1)
        def _():
            dw_ref[...] = jnp.sum(dw_acc[...], axis=1)

    cur_v, cur_g, _, _ = _conv_specs(L, C, P // C, P // C + 1)
    row = pl.BlockSpec((T, C), lambda i: (i, 0))
    nxt = pl.BlockSpec((HALO, C), lambda i: (jnp.minimum((i + 1) * per, L // HALO - 1), 0))
    wspec = pl.BlockSpec((CONV_TAPS, C), lambda i: (0, 0))
    return _call(
        body, (dc, dc, proj, proj, w_dw), name="conv_bwd", grid=(n,),
        in_specs=[row, nxt, cur_v, cur_g, wspec],
        out_specs=[row, row, pl.BlockSpec((taps_pad, C), lambda i: (0, 0))],
        out_shape=[_sds((L, C), BF16), _sds((L, C), BF16), _sds((taps_pad, C), F32)],
        scratch_shapes=[pltpu.VMEM((8, T + HALO, C), F32), pltpu.VMEM((taps_pad, 8, C), F32)],
        compiler_params=_params(1))


def _mix_fwd(ya_pre, s, wpo, wco, proj, D):
    L, P = ya_pre.shape
    Q, _, DS = wpo.shape
    bm = _pick(L, 1088)
    gate0 = (proj.shape[1] - 2 * D) // DS
    per = D // DS

    def body(a1, a2, b1, b2, ga, gb, m_ref, ya_ref, yb_ref):
        ya = jnp.dot(a1[...], b1[...], preferred_element_type=F32)
        yb = jnp.dot(a2[...], b2[...], preferred_element_type=F32)
        ya_ref[...] = ya.astype(BF16)
        yb_ref[...] = yb.astype(BF16)
        m_ref[...] = (_sigmoid(ga[...]) * ya + _sigmoid(gb[...]) * yb).astype(BF16)

    act = pl.BlockSpec((bm, P), lambda i, q: (i, 0))
    wsp = pl.BlockSpec((None, P, DS), lambda i, q: (q, 0, 0))
    out = pl.BlockSpec((bm, DS), lambda i, q: (i, q))
    return _call(
        body, (ya_pre, s, wpo, wco, proj, proj), name="mix_fwd", grid=(L // bm, Q),
        in_specs=[act, act, wsp, wsp,
                  pl.BlockSpec((bm, DS), lambda i, q: (i, gate0 + q)),
                  pl.BlockSpec((bm, DS), lambda i, q: (i, gate0 + per + q))],
        out_specs=[out, out, out],
        out_shape=[_sds((L, D), BF16), _sds((L, D), BF16), _sds((L, D), BF16)],
        compiler_params=_params(2))


def _mix_bwd(do, w_o, proj, ya, yb):
    L, D = do.shape
    bm = _pick(L, 544)
    bn = _pick(D // N_CHIPS, 512)
    gate0 = (proj.shape[1] - 2 * D) // bn
    per = D // bn

    def epilogue(dm, extras, outs):
        ga, gb, ya_ref, yb_ref = extras
        sa = _sigmoid(ga[...])
        sb = _sigmoid(gb[...])
        outs[0][...] = (dm * sa).astype(BF16)
        outs[1][...] = (dm * sb).astype(BF16)
        outs[2][...] = (dm * ya_ref[...].astype(F32) * sa * (1.0 - sa)).astype(BF16)
        outs[3][...] = (dm * yb_ref[...].astype(F32) * sb * (1.0 - sb)).astype(BF16)

    blk = pl.BlockSpec((bm, bn), lambda i, j: (i, j))
    return _mm(
        "mix_bwd", (L // bm, D // bn), [do, w_o, proj, proj, ya, yb],
        [pl.BlockSpec((bm, D), lambda i, j: (i, 0)), pl.BlockSpec((bn, D), lambda i, j: (j, 0)),
         pl.BlockSpec((bm, bn), lambda i, j: (i, gate0 + j)),
         pl.BlockSpec((bm, bn), lambda i, j: (i, gate0 + per + j)), blk, blk],
        [_sds((L, D), BF16)] * 4, [blk] * 4, NT, 1, epilogue)


def _mm_act_colw(name, a, wg, bn_pref, epilogue=_store, out_dtypes=(F32,)):
    L, K = a.shape
    Q, _, n = wg.shape
    bn = _pick(n, bn_pref)
    nj = n // bn
    out = pl.BlockSpec((L, bn), lambda q, j: (0, q * nj + j))
    return _mm(name, (Q, nj), [a, wg],
               [pl.BlockSpec((L, K), lambda q, j: (0, 0)), pl.BlockSpec((None, K, bn), lambda q, j: (q, 0, j))],
               [_sds((L, Q * n), dt) for dt in out_dtypes], [out] * len(out_dtypes), NN, 1, epilogue)


def _mm_grad_colw_t(name, g, wg, bm_pref, bn_pref):
    L = g.shape[0]
    Q, K, n = wg.shape
    bm = _pick(L, bm_pref)
    bn = _pick(K, bn_pref)
    return _mm(name, (L // bm, K // bn, Q), [g, wg],
               [pl.BlockSpec((bm, n), lambda i, j, k: (i, k)), pl.BlockSpec((None, bn, n), lambda i, j, k: (k, j, 0))],
               [_sds((L, K), F32)], [pl.BlockSpec((bm, bn), lambda i, j, k: (i, j))], NT, Q,
               acc_shape=(bm, bn))


def _mm_wgrad_colw(name, a, g, Q, bm_pref, bn_pref, rows=None):
    L, K = a.shape
    n = g.shape[1] // Q
    first, count = rows or (0, K)
    bm = _pick(count, bm_pref)
    bn = _pick(n, bn_pref)
    nj = n // bn
    i0 = first // bm
    return _mm(name, (Q, count // bm, nj), [a, g],
               [pl.BlockSpec((L, bm), lambda q, i, j: (0, i0 + i)),
                pl.BlockSpec((L, bn), lambda q, i, j: (0, q * nj + j))],
               [_sds((Q, count, n), F32)], [pl.BlockSpec((None, bm, bn), lambda q, i, j: (q, i, j))], TN, 1)


def _mm_wgrad(name, a, g, bm_pref, bn_pref):
    L, K = a.shape
    N = g.shape[1]
    bm = _pick(K, bm_pref)
    bn = _pick(N, bn_pref)
    return _mm(name, (K // bm, N // bn), [a, g],
               [pl.BlockSpec((L, bm), lambda i, j: (0, i)), pl.BlockSpec((L, bn), lambda i, j: (0, j))],
               [_sds((K, N), F32)], [pl.BlockSpec((bm, bn), lambda i, j: (i, j))], TN, 1)


def _mm_act_roww(name, a, w, bm_pref, bn_pref, bk_pref):
    L, K = a.shape
    N = w.shape[1]
    bm, bn, bk = _pick(L, bm_pref), _pick(N, bn_pref), _pick(K, bk_pref)
    nk = K // bk
    return _mm(name, (L // bm, N // bn, nk), [a, w],
               [pl.BlockSpec((bm, bk), lambda i, j, k: (i, k)), pl.BlockSpec((bk, bn), lambda i, j, k: (k, j))],
               [_sds((L, N), F32)], [pl.BlockSpec((bm, bn), lambda i, j, k: (i, j))], NN, nk,
               acc_shape=(bm, bn))


def _concat_cols(name, parts):
    L = parts[0].shape[0]
    T = _row_tile(L)
    widths = [p.shape[1] for p in parts]

    def body(*refs):
        at = 0
        for ref, wd in zip(refs[:-1], widths):
            refs[-1][:, at:at + wd] = ref[...]
            at += wd

    return _call(
        body, list(parts), name=name, grid=(L // T,),
        in_specs=[pl.BlockSpec((T, wd), lambda i: (i, 0)) for wd in widths],
        out_specs=pl.BlockSpec((T, sum(widths)), lambda i: (i, 0)),
        out_shape=_sds((L, sum(widths)), parts[0].dtype), compiler_params=_params(1))


def _up_epilogue(val, extras, outs):
    outs[0][...] = val.astype(BF16)
    r = jnp.maximum(val, 0.0)
    outs[1][...] = (r * r).astype(BF16)


def _mlp_down_bwd(df, w_down, a_up):
    L, D = df.shape
    F = w_down.shape[0]
    bm = _pick(L, 1088)
    bn = _pick(F, 1024)

    def epilogue(val, extras, outs):
        outs[0][...] = (val * (2.0 * jnp.maximum(extras[0][...].astype(F32), 0.0))).astype(BF16)

    blk = pl.BlockSpec((bm, bn), lambda i, j: (i, j))
    return _mm("mlp_down_bwd", (L // bm, F // bn), [df, w_down, a_up],
               [pl.BlockSpec((bm, D), lambda i, j: (i, 0)), pl.BlockSpec((bn, D), lambda i, j: (j, 0)), blk],
               [_sds((L, F), BF16)], [blk], NT, 1, epilogue)


ANY = pl.BlockSpec(memory_space=pl.ANY)
HBM = pl.BlockSpec(memory_space=pltpu.HBM)
SEM = pl.BlockSpec(memory_space=pltpu.SEMAPHORE)
EFFECT = pltpu.SideEffectType.DATAFLOW_SIDE_EFFECTING
N_CHIPS = 4


def _place():
    x, y, c = lax.axis_index("x"), lax.axis_index("y"), lax.axis_index("c")
    return x, y, c


def _chip_at(x, y, k):
    px = 1 - x if k & 2 else x
    py = 1 - y if k & 1 else y
    return px, py


def _cast_into_slab(w2d, chip, dtype):
    R, C = w2d.shape
    T = _elem_tile(R, C)

    def body(p_ref, w_ref, o_ref):
        o_ref[...] = w_ref[...].astype(dtype)

    return _call(
        body, (w2d,), name="cast_into_slab", grid=(R // T,), scalars=jnp.reshape(chip, (1,)).astype(jnp.int32),
        in_specs=[pl.BlockSpec((T, C), lambda i, p: (i, 0))],
        out_specs=pl.BlockSpec((None, T, C), lambda i, p: (p[0], i, 0)),
        out_shape=_sds((N_CHIPS, R, C), dtype), compiler_params=_params(1))


TOKEN = jax.ShapeDtypeStruct((8, LANES), F32)


class _Sems:
    def __init__(self, items, shape):
        self.items, self.shape = list(items), tuple(shape)

    def pair(self, idx):
        flat = 0
        for i, n in zip(idx, self.shape):
            flat = flat * n + i
        half = len(self.items) // 2
        return self.items[flat], self.items[half + flat]


def _sem_count(shape):
    n = 1
    for s in shape:
        n *= s
    return n


def _remote(src, dst, sems, idx, device):
    send, recv = sems.pair(idx)
    return pltpu.make_async_remote_copy(src_ref=src, dst_ref=dst, send_sem=send, recv_sem=recv,
                                        device_id=device, device_id_type=MESH)


def _thru(arrays):
    return ([pltpu.with_memory_space_constraint(a, pltpu.HBM) for a in arrays],
            [pltpu.HBM(a.shape, a.dtype) for a in arrays])


def _comm_start(name, arrays, sem_shape, plan, follows=False):
    na, ns = len(arrays), 2 * _sem_count(sem_shape)

    def body(*refs):
        sems, token = _Sems(refs[na:na + ns], sem_shape), refs[-1]
        for src, dst, idx, device in plan(refs[:na])[0]:
            _remote(src, dst, sems, idx, device).start()
        token[...] = jnp.zeros(token.shape, F32)

    ins, outs = _thru(arrays)
    res = _call(
        body, ins, name=name, in_specs=[HBM] * na, mark=-1, follows=follows, made_from=arrays,
        out_specs=[SEM] * ns + [HBM] * na + [pl.BlockSpec(memory_space=pltpu.VMEM)],
        out_shape=[pltpu.SemaphoreType.DMA(())] * ns + outs + [TOKEN],
        input_output_aliases={a: ns + a for a in range(na)},
        compiler_params=pltpu.CompilerParams(has_side_effects=EFFECT))
    return _Sems(res[:ns], sem_shape), list(res[ns:ns + na])


def _wait_plans(refs, sem_refs, waits):
    x, y, c = _place()
    at = 0
    for sems, plan in waits:
        here = _Sems(sem_refs[at:at + len(sems.items)], sems.shape)
        at += len(sems.items)
        _, mine, arrivals = plan(refs)
        for dst, idx in arrivals:
            _remote(dst, dst, here, idx, (x, y, c)).wait_recv()
        for src, idx in mine:
            _remote(src, src, here, idx, (x, y, c)).wait_send()


def _comm_wait(name, arrays, waits):
    na = len(arrays)
    sem_items = [s for sems, _ in waits for s in sems.items]
    ns = len(sem_items)

    def body(*refs):
        _wait_plans(refs[:na], refs[na:na + ns], waits)
        refs[-1][...] = jnp.zeros(refs[-1].shape, F32)

    ins, outs = _thru(arrays)
    res = _call(
        body, ins + sem_items, name=name, in_specs=[HBM] * na + [SEM] * ns, mark=-1,
        out_specs=[HBM] * na + [pl.BlockSpec(memory_space=pltpu.VMEM)], out_shape=outs + [TOKEN],
        input_output_aliases={a: a for a in range(na)},
        compiler_params=pltpu.CompilerParams(has_side_effects=EFFECT))
    return list(res[:na])


def _comm_relay(name, arrays, sems, plan, sem_shape, next_plan):
    na, ns_in, ns_out = len(arrays), len(sems.items), 2 * _sem_count(sem_shape)

    def body(*refs):
        bufs = refs[:na]
        sems_in = _Sems(refs[na:na + ns_in], sems.shape)
        sems_out = _Sems(refs[na + ns_in:na + ns_in + ns_out], sem_shape)
        x, y, c = _place()
        _, mine, arrivals = plan(bufs)
        onward = next_plan(bufs)[0]
        for dst, idx in arrivals:
            _remote(dst, dst, sems_in, idx, (x, y, c)).wait_recv()
            for src, to, idx2, device, after_idx in onward:
                if after_idx == idx:
                    _remote(src, to, sems_out, idx2, device).start()
        for src, idx in mine:
            _remote(src, src, sems_in, idx, (x, y, c)).wait_send()
        refs[-1][...] = jnp.zeros(refs[-1].shape, F32)

    ins, outs = _thru(arrays)
    res = _call(
        body, ins + sems.items, name=name, in_specs=[HBM] * na + [SEM] * ns_in, mark=-1,
        out_specs=[SEM] * ns_out + [HBM] * na + [pl.BlockSpec(memory_space=pltpu.VMEM)],
        out_shape=[pltpu.SemaphoreType.DMA(())] * ns_out + outs + [TOKEN],
        input_output_aliases={a: ns_out + a for a in range(na)},
        compiler_params=pltpu.CompilerParams(has_side_effects=EFFECT))
    return _Sems(res[:ns_out], sem_shape), list(res[ns_out:ns_out + na])


def _half(ref, q, which):
    h = ref.shape[1] // 2
    return ref.at[q, pl.ds(which * h, h)]


def _quarter(ref, q, half, which):
    h = ref.shape[1] // 2
    return ref.at[q, pl.ds(half * h + which * (h // 2), h // 2)]


def _gather_plan(n_halved):
    def plan(refs):
        x, y, c = _place()
        p = 2 * x + y
        starts, mine, arrivals = [], [], []
        for n, ref in enumerate(refs):
            for k in range(1, N_CHIPS if n >= n_halved else 3):
                px, py = _chip_at(x, y, k)
                q = 2 * px + py
                out = _half(ref, p, c) if n < n_halved else ref.at[p]
                inc = _half(ref, q, c) if n < n_halved else ref.at[q]
                starts.append((out, out, (n, k - 1), (px, py, c)))
                mine.append((out, (n, k - 1)))
                arrivals.append((inc, (n, k - 1)))
        return starts, mine, arrivals
    return plan


def _spread_plan(n_halved, part):
    def plan(refs):
        x, y, c = _place()
        p = 2 * x + y
        starts, mine, arrivals = [], [], []
        for n in range(n_halved):
            for k in (1, 2):
                px, py = _chip_at(x, y, k)
                q = 2 * px + py
                tx, ty = _chip_at(x, y, 3 - k)
                dx, dy = _chip_at(x, y, 3)
                piece = _quarter(refs[n], q, c, 2 - k)
                landed = _half(refs[n], q, c)
                starts.append((piece, piece, (0, n, k - 1), (tx, ty, c), (n, k - 1)))
                starts.append((landed, landed, (1, n, k - 1), (x, y, 1 - c), (n, k - 1)))
                if part != 1:
                    mine.append((piece, (0, n, k - 1)))
                    arrivals.append((_quarter(refs[n], 2 * dx + dy, c, k - 1), (0, n, 2 - k)))
                if part != 0:
                    mine.append((landed, (1, n, k - 1)))
                    arrivals.append((_half(refs[n], q, 1 - c), (1, n, k - 1)))
        return starts, mine, arrivals
    return plan


def _last_hand_on_plan(n_halved):
    def plan(refs):
        x, y, c = _place()
        dx, dy = _chip_at(x, y, 3)
        d = 2 * dx + dy
        starts, mine, arrivals = [], [], []
        for n in range(n_halved):
            for k in (1, 2):
                piece = _quarter(refs[n], d, c, k - 1)
                starts.append((piece, piece, (n, k - 1), (x, y, 1 - c), (0, n, 2 - k)))
                mine.append((piece, (n, k - 1)))
                arrivals.append((_quarter(refs[n], d, 1 - c, k - 1), (n, k - 1)))
        return starts, mine, arrivals
    return plan


def _swap_plan(n):
    def plan(refs):
        x, y, c = _place()
        starts, mine, arrivals = [], [], []
        for a in range(n):
            h = refs[a].shape[1] // 2
            src = refs[a].at[:, pl.ds((1 - c) * h, h)]
            starts.append((src, refs[n + a], (a,), (x, y, 1 - c)))
            mine.append((src, (a,)))
            arrivals.append((refs[n + a], (a,)))
        return starts, mine, arrivals
    return plan


def _scatter_plan(n):
    def plan(refs):
        x, y, c = _place()
        starts, mine, arrivals = [], [], []
        for a in range(n):
            for k in range(1, N_CHIPS):
                px, py = _chip_at(x, y, k)
                src = refs[a].at[2 * px + py]
                starts.append((src, refs[n + a].at[k - 1], (a, k - 1), (px, py, c)))
                mine.append((src, (a, k - 1)))
                arrivals.append((refs[n + a].at[k - 1], (a, k - 1)))
        return starts, mine, arrivals
    return plan


def _share_plan(n):
    def plan(refs):
        x, y, c = _place()
        starts, mine, arrivals = [], [], []
        for a in range(n):
            h = refs[a].shape[0] // 2
            own = refs[a].at[pl.ds(c * h, h)]
            starts.append((own, own, (a,), (x, y, 1 - c)))
            mine.append((own, (a,)))
            arrivals.append((refs[a].at[pl.ds((1 - c) * h, h)], (a,)))
        return starts, mine, arrivals
    return plan


N_DEVICES = 8


def _packs_plan(refs):
    buf = refs[0]
    x, y, c = _place()
    me = 4 * x + 2 * y + c
    starts, mine, arrivals = [], [], []
    for r in range(1, N_DEVICES):
        peer = (1 - x if r & 4 else x, 1 - y if r & 2 else y, 1 - c if r & 1 else c)
        starts.append((buf.at[me], buf.at[me], (r - 1,), peer))
        mine.append((buf.at[me], (r - 1,)))
        arrivals.append((buf.at[4 * peer[0] + 2 * peer[1] + peer[2]], (r - 1,)))
    return starts, mine, arrivals


class _Reduction:
    def __init__(self, tag, slabs, c_idx, chip):
        self.tag, self.n, self.c_idx, self.chip = tag, len(slabs), c_idx, chip
        lands = [lax.empty((g.shape[0], g.shape[1] // 2, g.shape[2]), g.dtype) for g in slabs]
        self.sems = _comm_start("swap_start_" + tag, list(slabs) + lands, (self.n,), _swap_plan(self.n))

    def partial(self):
        n = self.n
        sems, bufs = self.sems
        bufs = _comm_wait("swap_wait_" + self.tag, bufs, [(sems, _swap_plan(n))])
        both = [_chip_partial(g, r, self.c_idx, self.chip) for g, r in zip(bufs[:n], bufs[n:])]
        self.own = [o for _, o in both]
        parts = [p for p, _ in both]
        lands = [lax.empty((N_CHIPS - 1,) + p.shape[1:], p.dtype) for p in parts]
        self.sems = _comm_start("scatter_start_" + self.tag, parts + lands, (n, N_CHIPS - 1), _scatter_plan(n))

    def total(self):
        n = self.n
        sems, bufs = self.sems
        bufs = _comm_wait("scatter_wait_" + self.tag, bufs, [(sems, _scatter_plan(n))])
        fulls = [_sum_partials(o, r, self.c_idx) for o, r in zip(self.own, bufs[n:])]
        self.sems = _comm_start("share_start_" + self.tag, fulls, (n,), _share_plan(n))

    def finish(self):
        sems, bufs = self.sems
        return _comm_wait("share_wait_" + self.tag, bufs, [(sems, _share_plan(self.n))])


def _elem_tile(rows, cols):
    return _pick(rows, max(8, (1 << 19) // cols // 8 * 8))


def _chip_partial(grad, recv, c_idx, p_idx):
    Q, R, C = grad.shape
    h = R // 2
    T = _elem_tile(h, C)
    nt = h // T

    def body(sc_ref, g_ref, r_ref, sb_ref, own_ref):
        q = pl.program_id(1)
        s = g_ref[...] + r_ref[...]
        sb_ref[...] = s.astype(BF16)

        @pl.when(q == sc_ref[1])
        def _():
            own_ref[...] = s

    return _call(
        body, (grad, recv), name="chip_partial", grid=(nt, Q),
        scalars=jnp.stack([c_idx, p_idx]).astype(jnp.int32),
        in_specs=[pl.BlockSpec((None, T, C), lambda t, q, sc: (q, sc[0] * nt + t, 0)),
                  pl.BlockSpec((None, T, C), lambda t, q, sc: (q, t, 0))],
        out_specs=[pl.BlockSpec((None, T, C), lambda t, q, sc: (q, t, 0)),
                   pl.BlockSpec((T, C), lambda t, q, sc: (t, 0))],
        out_shape=[_sds((Q, h, C), BF16), _sds((h, C), F32)], compiler_params=_params(2))


def _sum_partials(own, parts, c_idx):
    h, C = own.shape
    T = _elem_tile(h, C)
    nt = h // T

    def body(c_ref, o_ref, p_ref, t_ref):
        t = o_ref[...]
        for k in range(N_CHIPS - 1):
            t = t + p_ref[k].astype(F32)
        t_ref[...] = t

    return _call(
        body, (own, parts), name="sum_partials", grid=(nt,), scalars=jnp.reshape(c_idx, (1,)).astype(jnp.int32),
        in_specs=[pl.BlockSpec((T, C), lambda i, c: (i, 0)),
                  pl.BlockSpec((N_CHIPS - 1, T, C), lambda i, c: (0, i, 0))],
        out_specs=pl.BlockSpec((T, C), lambda i, c: (c[0] * nt + i, 0)),
        out_shape=_sds((2 * h, C), F32), compiler_params=_params(1))


def _pack_rows(name, parts, slot=None, n_slots=1):
    width = parts[0].shape[1]
    offsets, at = [], 0
    for p in parts:
        offsets.append(at)
        at += p.shape[0]
    total = -(-at // 8) * 8
    lead = 0 if slot is None else 1

    def body(*refs):
        out = refs[-1]
        out[...] = jnp.zeros(out.shape, F32)
        for ref, o in zip(refs[lead:-1], offsets):
            out[o:o + ref.shape[0], :] = ref[...]

    if slot is None:
        whole = pl.BlockSpec(memory_space=pltpu.VMEM)
        return _call(body, list(parts), name=name, in_specs=[whole] * len(parts), out_specs=whole,
                     out_shape=_sds((total, width), F32))
    return _call(body, list(parts), name=name, grid=(1,), scalars=jnp.reshape(slot, (1,)).astype(jnp.int32),
                 in_specs=[pl.BlockSpec(p.shape, lambda i, s: (0, 0)) for p in parts],
                 out_specs=pl.BlockSpec((None, total, width), lambda i, s: (s[0], 0, 0)),
                 out_shape=_sds((n_slots, total, width), F32))


def _sum_packs(packs):
    n, R, C = packs.shape

    def body(p_ref, o_ref):
        t = p_ref[0]
        for k in range(1, n):
            t = t + p_ref[k]
        o_ref[...] = t

    return _call(
        body, (packs,), name="sum_packs", grid=(1,), in_specs=[pl.BlockSpec((n, R, C), lambda i: (0, 0, 0))],
        out_specs=pl.BlockSpec((R, C), lambda i: (0, 0)), out_shape=_sds((R, C), F32), compiler_params=_params(1))


def _adamw(w, g, m, v):
    R, C = w.shape
    T = _elem_tile(R, C)

    def body(w_ref, g_ref, m_ref, v_ref, d_ref, m2_ref, v2_ref):
        g_ = g_ref[...]
        m2 = ADAM_B1 * m_ref[...] + (1.0 - ADAM_B1) * g_
        v2 = ADAM_B2 * v_ref[...] + (1.0 - ADAM_B2) * (g_ * g_)
        m_hat = m2 / (1.0 - ADAM_B1 ** ADAM_STEP)
        v_hat = v2 / (1.0 - ADAM_B2 ** ADAM_STEP)
        d_ref[...] = -ADAM_LR * (m_hat / (jnp.sqrt(v_hat) + ADAM_EPS) + ADAM_WD * w_ref[...])
        m2_ref[...] = m2
        v2_ref[...] = v2

    blk = pl.BlockSpec((T, C), lambda i: (i, 0))
    return _call(
        body, (w, g, m, v), name="adamw", grid=(R // T,), in_specs=[blk] * 4, out_specs=[blk] * 3,
        out_shape=[_sds((R, C), F32)] * 3, compiler_params=_params(1))


def _adamw_rows(w, g, m, v, row0, prev=None):
    R, C = w.shape
    T = _elem_tile(g.shape[0], C)
    off = row0 // T

    def body(w_ref, g_ref, m_ref, v_ref, *rest):
        d_ref, m2_ref, v2_ref, g2_ref = rest[-4:]
        g_ = g_ref[...]
        m2 = ADAM_B1 * m_ref[...] + (1.0 - ADAM_B1) * g_
        v2 = ADAM_B2 * v_ref[...] + (1.0 - ADAM_B2) * (g_ * g_)
        m_hat = m2 / (1.0 - ADAM_B1 ** ADAM_STEP)
        v_hat = v2 / (1.0 - ADAM_B2 ** ADAM_STEP)
        d_ref[...] = -ADAM_LR * (m_hat / (jnp.sqrt(v_hat) + ADAM_EPS) + ADAM_WD * w_ref[...])
        m2_ref[...] = m2
        v2_ref[...] = v2
        g2_ref[...] = g_

    here = pl.BlockSpec((T, C), lambda i: (off + i, 0))
    piece = pl.BlockSpec((T, C), lambda i: (i, 0))
    done = tuple(prev or ())
    return _call(
        body, (w, g, m, v) + done, name="adamw_rows", grid=(g.shape[0] // T,), follows=prev is None,
        in_specs=[here, piece, here, here] + [ANY] * len(done), out_specs=[here] * 4,
        out_shape=[_sds((R, C), F32)] * 4, input_output_aliases={4 + j: j for j in range(len(done))},
        compiler_params=_params(1))


SC_TILES = 32
SC_ROWS = 8
SC_LANES = 16


def _adamw_sc(w, g, m, v):
    R, C = w.shape
    per_tile = R // SC_TILES
    assert per_tile % SC_ROWS == 0 and C % SC_LANES == 0

    def body(w_hbm, g_hbm, m_hbm, v_hbm, d_out, m_out, v_out, g_out, wb, gb, mb, vb):
        tile = lax.axis_index("sc_tile") * 2 + lax.axis_index("sc_core")

        @pl.loop(0, per_tile // SC_ROWS)
        def _(chunk):
            here = pl.ds(tile * per_tile + chunk * SC_ROWS, SC_ROWS)
            pltpu.sync_copy(w_hbm.at[here], wb)
            pltpu.sync_copy(g_hbm.at[here], gb)
            pltpu.sync_copy(m_hbm.at[here], mb)
            pltpu.sync_copy(v_hbm.at[here], vb)
            for r in range(SC_ROWS):
                @pl.loop(0, C, step=SC_LANES)
                def _(j):
                    at = pl.ds(j, SC_LANES)
                    g_ = gb[r, at]
                    m2 = ADAM_B1 * mb[r, at] + (1.0 - ADAM_B1) * g_
                    v2 = ADAM_B2 * vb[r, at] + (1.0 - ADAM_B2) * (g_ * g_)
                    m_hat = m2 / (1.0 - ADAM_B1 ** ADAM_STEP)
                    v_hat = v2 / (1.0 - ADAM_B2 ** ADAM_STEP)
                    wb[r, at] = -ADAM_LR * (m_hat / (jnp.sqrt(v_hat) + ADAM_EPS) + ADAM_WD * wb[r, at])
                    mb[r, at] = m2
                    vb[r, at] = v2
            pltpu.sync_copy(wb, d_out.at[here])
            pltpu.sync_copy(mb, m_out.at[here])
            pltpu.sync_copy(vb, v_out.at[here])
            pltpu.sync_copy(gb, g_out.at[here])

    buf = pltpu.VMEM((SC_ROWS, C), F32)
    return pl.kernel(
        body, name="adamw_sc", out_type=(_sds((R, C), F32),) * 4,
        mesh=plsc.VectorSubcoreMesh(core_axis_name="sc_core", subcore_axis_name="sc_tile"),
        scratch_types=[buf, buf, buf, buf])(w, g, m, v)


BIG = ("w_in", "w_pool_out", "w_conv_out", "w_o", "w_up", "w_down", "w_pool_grp")
VECTORS = ("g_pre_mix", "pool_scale", "b_dw", "conv_ln_g", "conv_ln_b", "g_post_mix", "g_pre_mlp", "g_post_mlp")
WEIGHTS = ("meta", "g_pre_mix", "w_in", "w_pool_grp", "pool_scale", "w_pool_out", "w_dw", "b_dw", "conv_ln_g",
           "conv_ln_b", "w_conv_out", "w_o", "g_post_mix", "g_pre_mlp", "w_up", "w_down", "g_post_mlp")


def _as_rows(a, width):
    r, cols = a.shape
    return a.reshape(r * (cols // width), width)


def _step(w, m, v, x, tgt):
    S, D = x.shape
    P = D // 2
    xi, yi, ci = _place()
    chip = 2 * xi + yi
    _CHAIN["after"] = []

    C = D // 2
    G = POOL_GROUPS
    GD = P // G
    GS = GD // N_CHIPS
    Q = N_CHIPS
    vecs = {k: w[k] for k in VECTORS}
    shard2d = {k: w[k].reshape(-1, w[k].shape[-1]) for k in BIG}
    grads, delta, new_m, new_v = {}, {}, {}, {}

    def update(names, reduced, on_sparsecore=False):
        for k, g in zip(names, reduced):
            args = (shard2d[k], g, m[k].reshape(shard2d[k].shape), v[k].reshape(shard2d[k].shape))
            delta[k], new_m[k], new_v[k], grads[k] = _adamw_sc(*args) if on_sparsecore else _adamw_rows(*args, 0)

    groups = dict(a=(("w_in", "w_pool_grp"), ("w_dw", "meta")), b=(("w_pool_out", "w_conv_out", "w_o"), ()),
                  c=(("w_up",), ()), d=(("w_down",), ()))
    flying = {}

    def start(tag, follows):
        halved, whole = groups[tag]
        flying[tag] = _comm_start("gather_start_" + tag, flying[tag], (len(halved + whole), N_CHIPS - 1),
                                  _gather_plan(len(halved)), follows=follows)

    def spread(tag):
        nh = len(groups[tag][0])
        sems, bufs = flying[tag]
        flying[tag] = _comm_relay("gather_relay_" + tag, bufs, sems, _gather_plan(nh), (2, nh, 2),
                                  _spread_plan(nh, 0))

    def landed(tag):
        halved, whole = groups[tag]
        nh = len(halved)
        sems, bufs = flying.pop(tag)
        last, tree = _comm_relay("gather_relay2_" + tag, bufs[:nh], sems, _spread_plan(nh, 0), (nh, 2),
                                 _last_hand_on_plan(nh))
        done = _comm_wait("gather_wait_" + tag, tree,
                          [(sems, _spread_plan(nh, 1)), (last, _last_hand_on_plan(nh))])
        return dict(zip(halved + whole, done + bufs[nh:]))

    for tag, (halved, whole) in groups.items():
        flying[tag] = [_cast_into_slab(shard2d[k], chip, BF16) for k in halved]
        flying[tag] += [_cast_into_slab(w[k], chip, F32) for k in whole]
        if tag == "a":
            start(tag, False)
    spread("a")
    start("b", True)

    got = landed("a")
    win_g = got["w_in"]
    w_grp = got["w_pool_grp"].reshape(N_CHIPS, G, GS, GD).transpose(1, 0, 2, 3).reshape(G, GD, GD)
    w_dw = got["w_dw"].transpose(1, 0, 2).reshape(CONV_TAPS, P)
    meta = got["meta"].transpose(1, 0, 2).reshape(N_META, D)
    h0 = jnp.concatenate([jnp.zeros((PAD_ROWS, D), F32), meta, x], axis=0)
    u1 = _pre_norm(h0, vecs["g_pre_mix"])
    spread("b")
    start("c", True)
    start("d", True)
    proj = _mm_act_colw("proj", u1, win_g, 256)
    d, ya_pre = _pool_fwd(proj, w_grp, vecs["pool_scale"])
    s, c = _conv_fwd(proj, C, w_dw, vecs["b_dw"], vecs["conv_ln_g"], vecs["conv_ln_b"])
    spread("c")
    got = landed("b")
    wpo_g, wco_g, w_o = got["w_pool_out"], got["w_conv_out"], got["w_o"].reshape(D, D)
    mix, ya, yb = _mix_fwd(ya_pre, s, wpo_g, wco_g, proj, D)
    o = _mm_act_roww("attn_out", mix, w_o, 1088, 1024, 2048)
    spread("d")
    h1, u2 = _mid_norm(o, h0, vecs["g_post_mix"], vecs["g_pre_mlp"])
    wup_g = landed("c")["w_up"]
    a_up, fact = _mm_act_colw("mlp_up", u2, wup_g, 512, _up_epilogue, (BF16, BF16))
    w_down = landed("d")["w_down"].reshape(-1, D)
    f = _mm_act_roww("mlp_down", fact, w_down, 1088, 1024, 2048)
    dy, df, dg_post_mlp, loss = _loss_head(f, h1, tgt, vecs["g_post_mlp"])

    g_w_down = _mm_wgrad("dw_down", fact, df, 1024, 1024).reshape(N_CHIPS, -1, D)
    _CHAIN["after"] = [g_w_down]
    red1 = _Reduction("1", [g_w_down], ci, chip)
    da_up = _mlp_down_bwd(df, w_down, a_up)
    red1.partial()
    g_w_up = _mm_wgrad_colw("dw_up", u2, da_up, Q, 1024, 1024)
    red2 = _Reduction("2", [g_w_up], ci, chip)
    du2 = _mm_grad_colw_t("du2", da_up, wup_g, 1088, 1024)
    red2.partial()
    dh1, do, dg_pre_mlp, dg_post_mix = _mid_norm_bwd(dy, du2, h1, o, vecs["g_pre_mlp"], vecs["g_post_mix"])
    g_w_o = _mm_wgrad("dw_o", mix, do, 1024, 1024)
    red1.total()
    dya, dyb, dga, dgb = _mix_bwd(do, w_o, proj, ya, yb)
    update(("w_down",), red1.finish(), on_sparsecore=True)
    g_wpo = _mm_wgrad_colw("dw_pool_out", ya_pre, dya, Q, 1024, 512)
    g_wco = _mm_wgrad_colw("dw_conv_out", s, dyb, Q, 1024, 512)
    red3 = _Reduction("3", [g_w_o.reshape(N_CHIPS, D // N_CHIPS, D), g_wpo, g_wco], ci, chip)
    dya_pre = _mm_grad_colw_t("dya_pre", dya, wpo_g, 1088, 1024)
    ds = _mm_grad_colw_t("ds", dyb, wco_g, 1088, 1024)
    red3.partial()
    dz, g_w_grp, dscale = _pool_bwd(dya_pre, d, w_grp, vecs["pool_scale"])
    dc, dln_g, dln_b, db_dw = _conv_ln_bwd(ds, c, vecs["conv_ln_g"], vecs["conv_ln_b"])
    red2.total()
    dv, dgc, g_w_dw = _conv_bwd(dc, proj, C, w_dw)
    update(("w_up",), red2.finish(), on_sparsecore=True)
    dproj = _concat_cols("dproj", [dz, dv, dgc, dga, dgb])
    half_k = D // 2
    g_w_grp = g_w_grp.reshape(G, N_CHIPS, GS, GD).transpose(1, 0, 2, 3).reshape(N_CHIPS, G * GS, GD)
    g_in_a = _mm_wgrad_colw("dw_in_a", u1, dproj, Q, 512, 1792, rows=(0, half_k))
    red4a = _Reduction("4a", [g_in_a, g_w_grp], ci, chip)
    g_in_b = _mm_wgrad_colw("dw_in_b", u1, dproj, Q, 512, 1792, rows=(half_k, half_k))
    red4a.partial()
    red4b = _Reduction("4b", [g_in_b], ci, chip)
    du1 = _mm_grad_colw_t("du1", dproj, win_g, 1088, 1024)
    red4b.partial()
    red3.total()
    grad_x, dmeta, dg_pre_mix = _pre_norm_bwd(dh1, du1, h0, vecs["g_pre_mix"])

    g_vec = dict(g_pre_mix=dg_pre_mix, pool_scale=dscale, b_dw=db_dw, conv_ln_g=dln_g, conv_ln_b=dln_b,
                 g_post_mix=dg_post_mix, g_pre_mlp=dg_pre_mlp, g_post_mlp=dg_post_mlp)
    rows = [g_w_dw, _as_rows(dmeta, P)] + [_as_rows(g_vec[k], P) for k in VECTORS]
    rows.append(jnp.broadcast_to(loss[:, :1], (1, P)))
    packs = _comm_start("packs_start", [_pack_rows("pack_grads", rows, 2 * chip + ci, N_DEVICES)],
                        (N_DEVICES - 1,), _packs_plan)
    red4a.total()
    update(("w_o", "w_pool_out", "w_conv_out"), red3.finish())
    red_in_a, red_grp = red4a.finish()
    update(("w_pool_grp",), [red_grp])
    w_in_rows = (shard2d["w_in"], m["w_in"].reshape(shard2d["w_in"].shape), v["w_in"].reshape(shard2d["w_in"].shape))
    first_rows = _adamw_rows(w_in_rows[0], red_in_a, w_in_rows[1], w_in_rows[2], 0)
    total = _sum_packs(_comm_wait("packs_wait", packs[1], [(packs[0], _packs_plan)])[0])
    at = 0
    taps_pad = g_w_dw.shape[0]
    g_dw_full = total[at:at + CONV_TAPS]
    at += taps_pad
    g_meta_full = total[at:at + 2 * N_META].reshape(N_META, D)
    at += 2 * N_META
    for k in VECTORS:
        n = w[k].shape[-1] // P
        grads[k] = total[at:at + n].reshape(1, n * P)
        at += n
    loss_total = total[at, 0]
    grads["w_dw"] = lax.dynamic_slice_in_dim(g_dw_full, chip * (P // N_CHIPS), P // N_CHIPS, axis=1)
    grads["meta"] = lax.dynamic_slice_in_dim(g_meta_full, chip * (D // N_CHIPS), D // N_CHIPS, axis=1)

    small_names = [k for k in WEIGHTS if k not in BIG]

    def pack_small(tree):
        parts = []
        for k in small_names:
            a = tree[k].reshape(-1, tree[k].shape[-1])
            flat = a.reshape(-1)
            parts.append(jnp.pad(flat, (0, -flat.shape[0] % P)).reshape(-1, P))
        return _pack_rows("pack_small", parts)

    sd, sm, sv = _adamw(pack_small(w), pack_small(grads), pack_small(m), pack_small(v))
    at = 0
    for k in small_names:
        a = w[k].reshape(-1, w[k].shape[-1])
        n = -(-a.size // P)
        for tree, packed in ((delta, sd), (new_m, sm), (new_v, sv)):
            tree[k] = packed[at:at + n].reshape(-1)[:a.size].reshape(a.shape)
        at += n

    red4b.total()
    delta["w_in"], new_m["w_in"], new_v["w_in"], grads["w_in"] = _adamw_rows(
        w_in_rows[0], red4b.finish()[0], w_in_rows[1], w_in_rows[2], half_k, prev=first_rows)
    return loss_total, grad_x, grads, delta, new_m, new_v


def kernel(x, meta, g_pre_mix, w_in, w_pool_grp, pool_scale, w_pool_out, w_dw, b_dw, conv_ln_g, conv_ln_b, w_conv_out, w_o, g_post_mix, g_pre_mlp, w_up, w_down, g_post_mlp, loss_target, m_meta, m_g_pre_mix, m_w_in, m_w_pool_grp, m_pool_scale, m_w_pool_out, m_w_dw, m_b_dw, m_conv_ln_g, m_conv_ln_b, m_w_conv_out, m_w_o, m_g_post_mix, m_g_pre_mlp, m_w_up, m_w_down, m_g_post_mlp, v_meta, v_g_pre_mix, v_w_in, v_w_pool_grp, v_pool_scale, v_w_pool_out, v_w_dw, v_b_dw, v_conv_ln_g, v_conv_ln_b, v_w_conv_out, v_w_o, v_g_post_mix, v_g_pre_mlp, v_w_up, v_w_down, v_g_post_mlp):
    args = dict(locals())
    shapes = {k: args[k].shape for k in WEIGHTS}
    w = {k: args[k] for k in WEIGHTS}
    m = {k: args["m_" + k] for k in WEIGHTS}
    v = {k: args["v_" + k] for k in WEIGHTS}
    for tree in (w, m, v):
        tree["w_dw"] = tree["w_dw"].reshape(tree["w_dw"].shape[-2:])
    loss, grad_x, grads, delta, new_m, new_v = _step(w, m, v, x[0], loss_target[0])
    out = [loss, grad_x[None]]
    for tree in (grads, delta, new_m, new_v):
        out += [tree[k].reshape(shapes[k]) for k in WEIGHTS]
    return tuple(out)
```

```python
import jax
import jax.numpy as jnp
from jax import lax
from jax.experimental import pallas as pl
from jax.experimental.pallas import tpu as pltpu
from jax.experimental.pallas import tpu_sc as plsc

F32 = jnp.float32
BF16 = jnp.bfloat16

N_META = 16
PAD_ROWS = 112
TOKEN_ROW0 = PAD_ROWS + N_META
POOL_GROUPS = 4
CONV_TAPS = 31
HALO = 32
CONV_ROWS = 128
LANES = 128
RMS_EPS = 1e-6
LN_EPS = 1e-5
ADAM_LR = 0.001
ADAM_B1 = 0.9
ADAM_B2 = 0.999
ADAM_EPS = 1e-08
ADAM_WD = 0.01
ADAM_STEP = 10
VMEM_LIMIT_MB = 56

MESH = pl.DeviceIdType.MESH
NN = (((1,), (0,)), ((), ()))
NT = (((1,), (1,)), ((), ()))
TN = (((0,), (0,)), ((), ()))


def _pick(n, pref):
    if n <= pref:
        return n
    if n % pref == 0:
        return pref
    for step in (LANES, 8, 1):
        t = (pref // step) * step
        while t >= step:
            if n % t == 0:
                return t
            t -= step
    return n


def _params(n_axes, vmem_mb=VMEM_LIMIT_MB):
    return pltpu.CompilerParams(dimension_semantics=("arbitrary",) * n_axes,
                                vmem_limit_bytes=vmem_mb << 20)


def _sigmoid(x):
    return jax.nn.sigmoid(x)


_CHAIN = {"after": []}


def _call(body, args, *, in_specs, out_specs, out_shape, grid=(), scalars=None, mark=0, follows=True, made_from=None,
          **kw):
    pending = _CHAIN["after"]
    after = pending if follows else []
    n = len(args)
    lead = 0 if scalars is None else 1
    specs = list(in_specs)
    operands = list(args)
    fn = body
    if after:
        def fn(*refs):
            body(*refs[:lead + n], *refs[lead + n + len(after):])
        specs += [pl.BlockSpec(memory_space=pl.ANY)] * len(after)
        operands += after
    if scalars is None:
        if grid:
            kw["grid"] = grid
        res = pl.pallas_call(fn, in_specs=specs, out_specs=out_specs, out_shape=out_shape, **kw)(*operands)
    else:
        grid_spec = pltpu.PrefetchScalarGridSpec(num_scalar_prefetch=1, grid=grid, in_specs=specs, out_specs=out_specs)
        res = pl.pallas_call(fn, grid_spec=grid_spec, out_shape=out_shape, **kw)(scalars, *operands)
    outs = res if isinstance(res, (list, tuple)) else [res]
    consumed = args if made_from is None else made_from
    kept = [] if follows else [m for m in pending if not any(m is a for a in consumed)]
    _CHAIN["after"] = kept + [outs[mark]]
    return res


def _store(val, extras, outs):
    outs[0][...] = val.astype(outs[0].dtype)


def _mm(name, grid, arrays, in_specs, out_shapes, out_specs, dims, nk, epilogue=_store, acc_shape=None):
    n_in, n_out = len(arrays), len(out_shapes)

    def body(*refs):
        extras = refs[2:n_in]
        outs = refs[n_in:n_in + n_out]
        part = lax.dot_general(refs[0][...], refs[1][...], dims, preferred_element_type=F32)
        if nk == 1:
            epilogue(part, extras, outs)
        else:
            acc = refs[n_in + n_out]
            k = pl.program_id(len(grid) - 1)

            @pl.when(k == 0)
            def _():
                acc[...] = part

            @pl.when(k > 0)
            def _():
                acc[...] += part

            @pl.when(k == nk - 1)
            def _():
                epilogue(acc[...], extras, outs)

    scratch = [pltpu.VMEM(acc_shape, F32)] if nk > 1 else []
    single = n_out == 1
    return _call(
        body, arrays, name=name, grid=grid, in_specs=in_specs,
        out_specs=out_specs[0] if single else out_specs,
        out_shape=out_shapes[0] if single else out_shapes,
        scratch_shapes=scratch, compiler_params=_params(len(grid)))


def _sds(shape, dtype):
    return jax.ShapeDtypeStruct(shape, dtype)


def _rms_scale(h):
    return lax.rsqrt(jnp.mean(h * h, axis=-1, keepdims=True) + RMS_EPS)


def _rms_bwd(du, h, g):
    r = _rms_scale(h)
    y = h * r
    dy = du * g
    dh = r * (dy - y * jnp.mean(dy * y, axis=-1, keepdims=True))
    return dh, jnp.sum(du * y, axis=0, keepdims=True)


def _row_tile(L):
    return _pick(L, 272)


def _pre_norm_tokens(x, g):
    S, D = x.shape
    T = TOKEN_ROW0
    L = S + T

    def body(x_ref, g_ref, h_ref, u_ref):
        h = x_ref[...]
        h_ref[...] = h
        u_ref[...] = (h * _rms_scale(h) * g_ref[...]).astype(BF16)

    below = pl.BlockSpec((T, D), lambda i: (i + 1, 0))
    return _call(
        body, (x, g), name="pre_norm_tokens", grid=(S // T,),
        in_specs=[pl.BlockSpec((T, D), lambda i: (i, 0)), pl.BlockSpec((1, D), lambda i: (0, 0))],
        out_specs=[below, below], out_shape=[_sds((L, D), F32), _sds((L, D), BF16)], compiler_params=_params(1))


def _pre_norm_meta(meta, g, h0, u1):
    L, D = h0.shape
    T = TOKEN_ROW0

    def body(m_ref, g_ref, h_in, u_in, h_ref, u_ref):
        h_ref[...] = jnp.zeros(h_ref.shape, F32)
        h_ref[PAD_ROWS:, :] = m_ref[...]
        h = h_ref[...]
        u_ref[...] = (h * _rms_scale(h) * g_ref[...]).astype(BF16)

    first = pl.BlockSpec((T, D), lambda i: (0, 0))
    return _call(
        body, (meta, g, h0, u1), name="pre_norm_meta", grid=(1,),
        in_specs=[pl.BlockSpec((N_META, D), lambda i: (0, 0)), pl.BlockSpec((1, D), lambda i: (0, 0)), ANY, ANY],
        out_specs=[first, first], out_shape=[_sds((L, D), F32), _sds((L, D), BF16)],
        input_output_aliases={2: 0, 3: 1}, compiler_params=_params(1))


def _mid_norm(o, h0, g_post, g_pre):
    L, D = h0.shape
    T = _row_tile(L)

    def body(o_ref, h_ref, gp_ref, gm_ref, h1_ref, u2_ref):
        o_ = o_ref[...]
        h1 = h_ref[...] + o_ * _rms_scale(o_) * gp_ref[...]
        h1_ref[...] = h1
        u2_ref[...] = (h1 * _rms_scale(h1) * gm_ref[...]).astype(BF16)

    row = pl.BlockSpec((T, D), lambda i: (i, 0))
    vec = pl.BlockSpec((1, D), lambda i: (0, 0))
    return _call(
        body, (o, h0, g_post, g_pre), name="mid_norm", grid=(L // T,),
        in_specs=[row, row, vec, vec], out_specs=[row, row],
        out_shape=[_sds((L, D), F32), _sds((L, D), BF16)], compiler_params=_params(1))


def _loss_head(f, h1, tgt, g_post):
    L, D = h1.shape
    T = TOKEN_ROW0
    n = L // T

    def body(f_ref, h_ref, t_ref, g_ref, dy_ref, df_ref, dg_ref, loss_ref):
        i = pl.program_id(0)
        f_ = f_ref[...]
        g = g_ref[...]
        y = h_ref[...] + f_ * _rms_scale(f_) * g
        live = (i > 0).astype(F32)
        diff = (y - t_ref[...]) * live
        part = 0.5 * jnp.sum(jnp.mean(diff * diff, axis=-1, keepdims=True), axis=0, keepdims=True)
        dy = diff * (1.0 / D)
        dy_ref[...] = dy
        df, dg = _rms_bwd(dy, f_, g)
        df_ref[...] = df.astype(BF16)

        @pl.when(i == 0)
        def _():
            dg_ref[...] = dg
            loss_ref[...] = jnp.broadcast_to(part, loss_ref.shape)

        @pl.when(i > 0)
        def _():
            dg_ref[...] += dg
            loss_ref[...] += jnp.broadcast_to(part, loss_ref.shape)

    row = pl.BlockSpec((T, D), lambda i: (i, 0))
    vec = pl.BlockSpec((1, D), lambda i: (0, 0))
    return _call(
        body, (f, h1, tgt, g_post), name="loss_head", grid=(n,),
        in_specs=[row, row, pl.BlockSpec((T, D), lambda i: (jnp.maximum(i - 1, 0), 0)), vec],
        out_specs=[row, row, vec, pl.BlockSpec((1, LANES), lambda i: (0, 0))],
        out_shape=[_sds((L, D), F32), _sds((L, D), BF16), _sds((1, D), F32), _sds((1, LANES), F32)],
        compiler_params=_params(1))


def _mid_norm_bwd(dy, du2, h1, o, g_pre, g_post):
    L, D = h1.shape
    T = _row_tile(L)

    def body(dy_ref, du_ref, h_ref, o_ref, gm_ref, gp_ref, dh1_ref, do_ref, dgm_ref, dgp_ref):
        i = pl.program_id(0)
        dh, dgm = _rms_bwd(du_ref[...], h_ref[...], gm_ref[...])
        dh1 = dy_ref[...] + dh
        dh1_ref[...] = dh1
        do, dgp = _rms_bwd(dh1, o_ref[...], gp_ref[...])
        do_ref[...] = do.astype(BF16)

        @pl.when(i == 0)
        def _():
            dgm_ref[...] = dgm
            dgp_ref[...] = dgp

        @pl.when(i > 0)
        def _():
            dgm_ref[...] += dgm
            dgp_ref[...] += dgp

    row = pl.BlockSpec((T, D), lambda i: (i, 0))
    vec = pl.BlockSpec((1, D), lambda i: (0, 0))
    return _call(
        body, (dy, du2, h1, o, g_pre, g_post), name="mid_norm_bwd", grid=(L // T,),
        in_specs=[row, row, row, row, vec, vec], out_specs=[row, row, vec, vec],
        out_shape=[_sds((L, D), F32), _sds((L, D), BF16), _sds((1, D), F32), _sds((1, D), F32)],
        compiler_params=_params(1))


def _pre_norm_bwd(dh1, du1, h0, g):
    L, D = h0.shape
    T = TOKEN_ROW0
    n = L // T

    def body(dh_ref, du_ref, h_ref, g_ref, gx_ref, dmeta_ref, dg_ref):
        i = pl.program_id(0)
        dh, dg = _rms_bwd(du_ref[...], h_ref[...], g_ref[...])
        dh0 = dh_ref[...] + dh
        gx_ref[...] = dh0

        @pl.when(i == 0)
        def _():
            dmeta_ref[...] = dh0[PAD_ROWS:, :]
            dg_ref[...] = dg

        @pl.when(i > 0)
        def _():
            dg_ref[...] += dg

    row = pl.BlockSpec((T, D), lambda i: (i, 0))
    vec = pl.BlockSpec((1, D), lambda i: (0, 0))
    return _call(
        body, (dh1, du1, h0, g), name="pre_norm_bwd", grid=(n,),
        in_specs=[row, row, row, vec],
        out_specs=[pl.BlockSpec((T, D), lambda i: (jnp.maximum(i - 1, 0), 0)),
                   pl.BlockSpec((N_META, D), lambda i: (0, 0)), vec],
        out_shape=[_sds((L - T, D), F32), _sds((N_META, D), F32), _sds((1, D), F32)],
        compiler_params=_params(1))


def _window_sum(z, g, shift_sign, L):
    s = z
    for j in range(POOL_GROUPS):
        k = 1 << j
        nxt = s + pltpu.roll(s, k if shift_sign > 0 else L - k, 0)
        s = jnp.where(j <= g, nxt, s)
    return s


def _inv_count(g, L):
    t = lax.broadcasted_iota(jnp.int32, (L, 1), 0)
    w = jnp.left_shift(2, g)
    cnt = jnp.clip(t - (PAD_ROWS - 1), 1, w)
    return 1.0 / cnt.astype(F32)


def _pool_fwd(proj, w_grp, scale):
    L = proj.shape[0]
    G, GD, _ = w_grp.shape
    P = G * GD

    def body(z_ref, w_ref, sc_ref, d_ref, ya_ref):
        g = pl.program_id(0)
        z = z_ref[...]
        d = (_window_sum(z, g, +1, L) * _inv_count(g, L) - z).astype(BF16)
        d_ref[...] = d
        y = jnp.dot(d, w_ref[...], preferred_element_type=F32)
        ya_ref[...] = (y * sc_ref[...]).astype(BF16)

    col = pl.BlockSpec((L, GD), lambda g: (0, g))
    return _call(
        body, (proj, w_grp, scale), name="pool_fwd", grid=(G,),
        in_specs=[col, pl.BlockSpec((None, GD, GD), lambda g: (g, 0, 0)), pl.BlockSpec((1, GD), lambda g: (0, g))],
        out_specs=[col, col], out_shape=[_sds((L, P), BF16), _sds((L, P), BF16)],
        compiler_params=_params(1))


def _pool_bwd(dya, d, w_grp, scale):
    L, P = dya.shape
    G, GD, _ = w_grp.shape

    def body(dya_ref, d_ref, w_ref, sc_ref, dz_ref, dw_ref, dsc_ref):
        g = pl.program_id(0)
        dya_ = dya_ref[...]
        d_ = d_ref[...]
        w = w_ref[...]
        y = jnp.dot(d_, w, preferred_element_type=F32)
        dsc_ref[...] = jnp.sum(dya_ * y, axis=0, keepdims=True)
        dy = (dya_ * sc_ref[...]).astype(BF16)
        dw_ref[...] = lax.dot_general(d_, dy, TN, preferred_element_type=F32)
        dd = lax.dot_general(dy, w, NT, preferred_element_type=F32)
        dz = _window_sum(dd * _inv_count(g, L), g, -1, L) - dd
        dz_ref[...] = dz.astype(BF16)

    col = pl.BlockSpec((L, GD), lambda g: (0, g))
    wspec = pl.BlockSpec((None, GD, GD), lambda g: (g, 0, 0))
    vec = pl.BlockSpec((1, GD), lambda g: (0, g))
    return _call(
        body, (dya, d, w_grp, scale), name="pool_bwd", grid=(G,),
        in_specs=[col, col, wspec, vec], out_specs=[col, wspec, vec],
        out_shape=[_sds((L, P), BF16), _sds((G, GD, GD), F32), _sds((1, P), F32)],
        compiler_params=_params(1))


def _fill_rotations(rot_ref, ext):
    n = ext.shape[0]
    rot_ref[0] = ext
    for r in range(1, 8):
        rot_ref[r] = pltpu.roll(ext, n - r, 0)


def _lane_chunks(C):
    step = LANES if C % LANES == 0 else C
    return [(c0, step) for c0 in range(0, C, step)]


def _conv_specs(L, C, col_v, col_g):
    T = CONV_ROWS
    per = T // HALO
    cur_v = pl.BlockSpec((T, C), lambda i: (i, col_v))
    cur_g = pl.BlockSpec((T, C), lambda i: (i, col_g))
    prev_v = pl.BlockSpec((HALO, C), lambda i: (jnp.maximum(i * per - 1, 0), col_v))
    prev_g = pl.BlockSpec((HALO, C), lambda i: (jnp.maximum(i * per - 1, 0), col_g))
    return cur_v, cur_g, prev_v, prev_g


def _glu_ext(vc, gc, vh, gh, i):
    a_cur = vc[...] * _sigmoid(gc[...])
    a_prev = vh[...] * _sigmoid(gh[...]) * (i > 0).astype(F32)
    return jnp.concatenate([a_prev, a_cur], axis=0)


def _conv_fwd(proj, C, w_dw, b_dw, ln_g, ln_b):
    L = proj.shape[0]
    T = CONV_ROWS
    P = C

    def body(vc, gc, vh, gh, w_ref, b_ref, lg_ref, lb_ref, s_ref, c_ref, rot):
        i = pl.program_id(0)
        _fill_rotations(rot, _glu_ext(vc, gc, vh, gh, i))
        for c0, cw in _lane_chunks(C):
            acc = jnp.zeros((T, cw), F32)
            for k in range(CONV_TAPS):
                q, r = divmod(HALO - (CONV_TAPS - 1) + k, 8)
                acc = acc + w_ref[k:k + 1, c0:c0 + cw] * rot[r, 8 * q:8 * q + T, c0:c0 + cw]
            c_ref[:, c0:c0 + cw] = acc + b_ref[:, c0:c0 + cw]
        c = c_ref[...]
        mu = jnp.mean(c, axis=-1, keepdims=True)
        cen = c - mu
        var = jnp.mean(cen * cen, axis=-1, keepdims=True)
        ln = cen * lax.rsqrt(var + LN_EPS) * lg_ref[...] + lb_ref[...]
        s_ref[...] = (ln * _sigmoid(ln)).astype(BF16)

    cur_v, cur_g, prev_v, prev_g = _conv_specs(L, C, P // C, P // C + 1)
    row = pl.BlockSpec((T, C), lambda i: (i, 0))
    vec = pl.BlockSpec((1, C), lambda i: (0, 0))
    return _call(
        body, (proj, proj, proj, proj, w_dw, b_dw, ln_g, ln_b), name="conv_fwd", grid=(L // T,),
        in_specs=[cur_v, cur_g, prev_v, prev_g, pl.BlockSpec((CONV_TAPS, C), lambda i: (0, 0)), vec, vec, vec],
        out_specs=[row, row], out_shape=[_sds((L, C), BF16), _sds((L, C), F32)],
        scratch_shapes=[pltpu.VMEM((8, T + HALO, C), F32)], compiler_params=_params(1))


def _conv_ln_bwd(ds, c, ln_g, ln_b):
    L, C = c.shape
    T = _row_tile(L)

    def body(ds_ref, c_ref, lg_ref, lb_ref, dc_ref, dlg_ref, dlb_ref, db_ref):
        i = pl.program_id(0)
        c_ = c_ref[...]
        g = lg_ref[...]
        mu = jnp.mean(c_, axis=-1, keepdims=True)
        cen = c_ - mu
        rstd = lax.rsqrt(jnp.mean(cen * cen, axis=-1, keepdims=True) + LN_EPS)
        xhat = cen * rstd
        ln = xhat * g + lb_ref[...]
        sg = _sigmoid(ln)
        dln = ds_ref[...] * (sg * (1.0 + ln * (1.0 - sg)))
        dxh = dln * g
        dc = rstd * (dxh - jnp.mean(dxh, axis=-1, keepdims=True)
                     - xhat * jnp.mean(dxh * xhat, axis=-1, keepdims=True))
        dc_ref[...] = dc
        dlg = jnp.sum(dln * xhat, axis=0, keepdims=True)
        dlb = jnp.sum(dln, axis=0, keepdims=True)
        db = jnp.sum(dc, axis=0, keepdims=True)

        @pl.when(i == 0)
        def _():
            dlg_ref[...] = dlg
            dlb_ref[...] = dlb
            db_ref[...] = db

        @pl.when(i > 0)
        def _():
            dlg_ref[...] += dlg
            dlb_ref[...] += dlb
            db_ref[...] += db

    row = pl.BlockSpec((T, C), lambda i: (i, 0))
    vec = pl.BlockSpec((1, C), lambda i: (0, 0))
    return _call(
        body, (ds, c, ln_g, ln_b), name="conv_ln_bwd", grid=(L // T,),
        in_specs=[row, row, vec, vec], out_specs=[row, vec, vec, vec],
        out_shape=[_sds((L, C), F32), _sds((1, C), F32), _sds((1, C), F32), _sds((1, C), F32)],
        compiler_params=_params(1))


def _conv_bwd(dc, proj, C, w_dw):
    L = proj.shape[0]
    T = CONV_ROWS
    per = T // HALO
    n = L // T
    P = C
    taps_pad = 32

    def body(dcc, dcn, vc, gc, w_ref, dv_ref, dg_ref, dw_ref, rot_d, dw_acc):
        i = pl.program_id(0)
        dc_next = dcn[...] * (i < n - 1).astype(F32)
        _fill_rotations(rot_d, jnp.concatenate([dcc[...], dc_next], axis=0))

        @pl.when(i == 0)
        def _():
            dw_acc[...] = jnp.zeros(dw_acc.shape, F32)

        for c0, cw in _lane_chunks(C):
            v = vc[:, c0:c0 + cw]
            sg = _sigmoid(gc[:, c0:c0 + cw])
            a = v * sg
            da = jnp.zeros((T, cw), F32)
            for k in range(CONV_TAPS):
                q, r = divmod(CONV_TAPS - 1 - k, 8)
                slab = rot_d[r, 8 * q:8 * q + T, c0:c0 + cw]
                da = da + w_ref[k:k + 1, c0:c0 + cw] * slab
                dw_acc[k, :, c0:c0 + cw] += jnp.sum((a * slab).reshape(T // 8, 8, cw), axis=0)
            dv_ref[:, c0:c0 + cw] = (da * sg).astype(BF16)
            dg_ref[:, c0:c0 + cw] = (da * v * sg * (1.0 - sg)).astype(BF16)

        @pl.when(i == n - 1)
        def _():
            dw_ref[...] = jnp.sum(dw_acc[...], axis=1)

    cur_v, cur_g, _, _ = _conv_specs(L, C, P // C, P // C + 1)
    row = pl.BlockSpec((T, C), lambda i: (i, 0))
    nxt = pl.BlockSpec((HALO, C), lambda i: (jnp.minimum((i + 1) * per, L // HALO - 1), 0))
    wspec = pl.BlockSpec((CONV_TAPS, C), lambda i: (0, 0))
    return _call(
        body, (dc, dc, proj, proj, w_dw), name="conv_bwd", grid=(n,),
        in_specs=[row, nxt, cur_v, cur_g, wspec],
        out_specs=[row, row, pl.BlockSpec((taps_pad, C), lambda i: (0, 0))],
        out_shape=[_sds((L, C), BF16), _sds((L, C), BF16), _sds((taps_pad, C), F32)],
        scratch_shapes=[pltpu.VMEM((8, T + HALO, C), F32), pltpu.VMEM((taps_pad, 8, C), F32)],
        compiler_params=_params(1))


def _mix_fwd(ya_pre, s, wpo, wco, proj, D):
    L, P = ya_pre.shape
    Q, _, DS = wpo.shape
    bm = _pick(L, 1088)
    gate0 = (proj.shape[1] - 2 * D) // DS
    per = D // DS

    def body(a1, a2, b1, b2, ga, gb, m_ref, ya_ref, yb_ref):
        ya = jnp.dot(a1[...], b1[...], preferred_element_type=F32)
        yb = jnp.dot(a2[...], b2[...], preferred_element_type=F32)
        ya_ref[...] = ya.astype(BF16)
        yb_ref[...] = yb.astype(BF16)
        m_ref[...] = (_sigmoid(ga[...]) * ya + _sigmoid(gb[...]) * yb).astype(BF16)

    act = pl.BlockSpec((bm, P), lambda i, q: (i, 0))
    wsp = pl.BlockSpec((None, P, DS), lambda i, q: (q, 0, 0))
    out = pl.BlockSpec((bm, DS), lambda i, q: (i, q))
    return _call(
        body, (ya_pre, s, wpo, wco, proj, proj), name="mix_fwd", grid=(L // bm, Q),
        in_specs=[act, act, wsp, wsp,
                  pl.BlockSpec((bm, DS), lambda i, q: (i, gate0 + q)),
                  pl.BlockSpec((bm, DS), lambda i, q: (i, gate0 + per + q))],
        out_specs=[out, out, out],
        out_shape=[_sds((L, D), BF16), _sds((L, D), BF16), _sds((L, D), BF16)],
        compiler_params=_params(2))


def _mix_bwd(do, w_o, proj, ya, yb):
    L, D = do.shape
    bm = _pick(L, 544)
    bn = _pick(D // N_CHIPS, 512)
    gate0 = (proj.shape[1] - 2 * D) // bn
    per = D // bn

    def epilogue(dm, extras, outs):
        ga, gb, ya_ref, yb_ref = extras
        sa = _sigmoid(ga[...])
        sb = _sigmoid(gb[...])
        outs[0][...] = (dm * sa).astype(BF16)
        outs[1][...] = (dm * sb).astype(BF16)
        outs[2][...] = (dm * ya_ref[...].astype(F32) * sa * (1.0 - sa)).astype(BF16)
        outs[3][...] = (dm * yb_ref[...].astype(F32) * sb * (1.0 - sb)).astype(BF16)

    blk = pl.BlockSpec((bm, bn), lambda i, j: (i, j))
    return _mm(
        "mix_bwd", (L // bm, D // bn), [do, w_o, proj, proj, ya, yb],
        [pl.BlockSpec((bm, D), lambda i, j: (i, 0)), pl.BlockSpec((bn, D), lambda i, j: (j, 0)),
         pl.BlockSpec((bm, bn), lambda i, j: (i, gate0 + j)),
         pl.BlockSpec((bm, bn), lambda i, j: (i, gate0 + per + j)), blk, blk],
        [_sds((L, D), BF16)] * 4, [blk] * 4, NT, 1, epilogue)


def _mm_act_colw(name, a, wg, bn_pref, epilogue=_store, out_dtypes=(F32,)):
    L, K = a.shape
    Q, _, n = wg.shape
    bn = _pick(n, bn_pref)
    nj = n // bn
    out = pl.BlockSpec((L, bn), lambda q, j: (0, q * nj + j))
    return _mm(name, (Q, nj), [a, wg],
               [pl.BlockSpec((L, K), lambda q, j: (0, 0)), pl.BlockSpec((None, K, bn), lambda q, j: (q, 0, j))],
               [_sds((L, Q * n), dt) for dt in out_dtypes], [out] * len(out_dtypes), NN, 1, epilogue)


def _mm_grad_colw_t(name, g, wg, bm_pref, bn_pref):
    L = g.shape[0]
    Q, K, n = wg.shape
    bm = _pick(L, bm_pref)
    bn = _pick(K, bn_pref)
    return _mm(name, (L // bm, K // bn, Q), [g, wg],
               [pl.BlockSpec((bm, n), lambda i, j, k: (i, k)), pl.BlockSpec((None, bn, n), lambda i, j, k: (k, j, 0))],
               [_sds((L, K), F32)], [pl.BlockSpec((bm, bn), lambda i, j, k: (i, j))], NT, Q,
               acc_shape=(bm, bn))


def _mm_wgrad_colw(name, a, g, Q, bm_pref, bn_pref, rows=None):
    L, K = a.shape
    n = g.shape[1] // Q
    first, count = rows or (0, K)
    bm = _pick(count, bm_pref)
    bn = _pick(n, bn_pref)
    nj = n // bn
    i0 = first // bm
    return _mm(name, (Q, count // bm, nj), [a, g],
               [pl.BlockSpec((L, bm), lambda q, i, j: (0, i0 + i)),
                pl.BlockSpec((L, bn), lambda q, i, j: (0, q * nj + j))],
               [_sds((Q, count, n), F32)], [pl.BlockSpec((None, bm, bn), lambda q, i, j: (q, i, j))], TN, 1)


def _mm_wgrad(name, a, g, bm_pref, bn_pref):
    L, K = a.shape
    N = g.shape[1]
    bm = _pick(K, bm_pref)
    bn = _pick(N, bn_pref)
    return _mm(name, (K // bm, N // bn), [a, g],
               [pl.BlockSpec((L, bm), lambda i, j: (0, i)), pl.BlockSpec((L, bn), lambda i, j: (0, j))],
               [_sds((K, N), F32)], [pl.BlockSpec((bm, bn), lambda i, j: (i, j))], TN, 1)


def _mm_act_roww(name, a, w, bm_pref, bn_pref, bk_pref):
    L, K = a.shape
    N = w.shape[1]
    bm, bn, bk = _pick(L, bm_pref), _pick(N, bn_pref), _pick(K, bk_pref)
    nk = K // bk
    return _mm(name, (L // bm, N // bn, nk), [a, w],
               [pl.BlockSpec((bm, bk), lambda i, j, k: (i, k)), pl.BlockSpec((bk, bn), lambda i, j, k: (k, j))],
               [_sds((L, N), F32)], [pl.BlockSpec((bm, bn), lambda i, j, k: (i, j))], NN, nk,
               acc_shape=(bm, bn))


def _concat_cols(name, parts):
    L = parts[0].shape[0]
    T = _row_tile(L)
    widths = [p.shape[1] for p in parts]

    def body(*refs):
        at = 0
        for ref, wd in zip(refs[:-1], widths):
            refs[-1][:, at:at + wd] = ref[...]
            at += wd

    return _call(
        body, list(parts), name=name, grid=(L // T,),
        in_specs=[pl.BlockSpec((T, wd), lambda i: (i, 0)) for wd in widths],
        out_specs=pl.BlockSpec((T, sum(widths)), lambda i: (i, 0)),
        out_shape=_sds((L, sum(widths)), parts[0].dtype), compiler_params=_params(1))


def _up_epilogue(val, extras, outs):
    outs[0][...] = val.astype(BF16)
    r = jnp.maximum(val, 0.0)
    outs[1][...] = (r * r).astype(BF16)


def _mlp_down_bwd(df, w_down, a_up):
    L, D = df.shape
    F = w_down.shape[0]
    bm = _pick(L, 1088)
    bn = _pick(F, 1024)

    def epilogue(val, extras, outs):
        outs[0][...] = (val * (2.0 * jnp.maximum(extras[0][...].astype(F32), 0.0))).astype(BF16)

    blk = pl.BlockSpec((bm, bn), lambda i, j: (i, j))
    return _mm("mlp_down_bwd", (L // bm, F // bn), [df, w_down, a_up],
               [pl.BlockSpec((bm, D), lambda i, j: (i, 0)), pl.BlockSpec((bn, D), lambda i, j: (j, 0)), blk],
               [_sds((L, F), BF16)], [blk], NT, 1, epilogue)


ANY = pl.BlockSpec(memory_space=pl.ANY)
HBM = pl.BlockSpec(memory_space=pltpu.HBM)
SEM = pl.BlockSpec(memory_space=pltpu.SEMAPHORE)
EFFECT = pltpu.SideEffectType.DATAFLOW_SIDE_EFFECTING
N_CHIPS = 4


def _place():
    x, y, c = lax.axis_index("x"), lax.axis_index("y"), lax.axis_index("c")
    return x, y, c


def _chip_at(x, y, k):
    px = 1 - x if k & 2 else x
    py = 1 - y if k & 1 else y
    return px, py


def _cast_into_slab(w2d, chip, dtype):
    R, C = w2d.shape
    T = _elem_tile(R, C)

    def body(p_ref, w_ref, o_ref):
        o_ref[...] = w_ref[...].astype(dtype)

    return _call(
        body, (w2d,), name="cast_into_slab", grid=(R // T,), scalars=jnp.reshape(chip, (1,)).astype(jnp.int32),
        in_specs=[pl.BlockSpec((T, C), lambda i, p: (i, 0))],
        out_specs=pl.BlockSpec((None, T, C), lambda i, p: (p[0], i, 0)),
        out_shape=_sds((N_CHIPS, R, C), dtype), compiler_params=_params(1))


TOKEN = jax.ShapeDtypeStruct((8, LANES), F32)


class _Sems:
    def __init__(self, items, shape):
        self.items, self.shape = list(items), tuple(shape)

    def pair(self, idx):
        flat = 0
        for i, n in zip(idx, self.shape):
            flat = flat * n + i
        half = len(self.items) // 2
        return self.items[flat], self.items[half + flat]


def _sem_count(shape):
    n = 1
    for s in shape:
        n *= s
    return n


def _remote(src, dst, sems, idx, device):
    send, recv = sems.pair(idx)
    return pltpu.make_async_remote_copy(src_ref=src, dst_ref=dst, send_sem=send, recv_sem=recv,
                                        device_id=device, device_id_type=MESH)


def _thru(arrays):
    return ([pltpu.with_memory_space_constraint(a, pltpu.HBM) for a in arrays],
            [pltpu.HBM(a.shape, a.dtype) for a in arrays])


def _comm_start(name, arrays, sem_shape, plan, follows=False):
    na, ns = len(arrays), 2 * _sem_count(sem_shape)

    def body(*refs):
        sems, token = _Sems(refs[na:na + ns], sem_shape), refs[-1]
        for src, dst, idx, device in plan(refs[:na])[0]:
            _remote(src, dst, sems, idx, device).start()
        token[...] = jnp.zeros(token.shape, F32)

    ins, outs = _thru(arrays)
    res = _call(
        body, ins, name=name, in_specs=[HBM] * na, mark=-1, follows=follows, made_from=arrays,
        out_specs=[SEM] * ns + [HBM] * na + [pl.BlockSpec(memory_space=pltpu.VMEM)],
        out_shape=[pltpu.SemaphoreType.DMA(())] * ns + outs + [TOKEN],
        input_output_aliases={a: ns + a for a in range(na)},
        compiler_params=pltpu.CompilerParams(has_side_effects=EFFECT))
    return _Sems(res[:ns], sem_shape), list(res[ns:ns + na])


def _wait_plans(refs, sem_refs, waits):
    x, y, c = _place()
    at = 0
    for sems, plan in waits:
        here = _Sems(sem_refs[at:at + len(sems.items)], sems.shape)
        at += len(sems.items)
        _, mine, arrivals = plan(refs)
        for dst, idx in arrivals:
            _remote(dst, dst, here, idx, (x, y, c)).wait_recv()
        for src, idx in mine:
            _remote(src, src, here, idx, (x, y, c)).wait_send()


def _comm_wait(name, arrays, waits):
    na = len(arrays)
    sem_items = [s for sems, _ in waits for s in sems.items]
    ns = len(sem_items)

    def body(*refs):
        _wait_plans(refs[:na], refs[na:na + ns], waits)
        refs[-1][...] = jnp.zeros(refs[-1].shape, F32)

    ins, outs = _thru(arrays)
    res = _call(
        body, ins + sem_items, name=name, in_specs=[HBM] * na + [SEM] * ns, mark=-1,
        out_specs=[HBM] * na + [pl.BlockSpec(memory_space=pltpu.VMEM)], out_shape=outs + [TOKEN],
        input_output_aliases={a: a for a in range(na)},
        compiler_params=pltpu.CompilerParams(has_side_effects=EFFECT))
    return list(res[:na])


def _comm_relay(name, arrays, sems, plan, sem_shape, next_plan):
    na, ns_in, ns_out = len(arrays), len(sems.items), 2 * _sem_count(sem_shape)

    def body(*refs):
        bufs = refs[:na]
        sems_in = _Sems(refs[na:na + ns_in], sems.shape)
        sems_out = _Sems(refs[na + ns_in:na + ns_in + ns_out], sem_shape)
        x, y, c = _place()
        _, mine, arrivals = plan(bufs)
        onward = next_plan(bufs)[0]
        for dst, idx in arrivals:
            _remote(dst, dst, sems_in, idx, (x, y, c)).wait_recv()
            for src, to, idx2, device, after_idx in onward:
                if after_idx == idx:
                    _remote(src, to, sems_out, idx2, device).start()
        for src, idx in mine:
            _remote(src, src, sems_in, idx, (x, y, c)).wait_send()
        refs[-1][...] = jnp.zeros(refs[-1].shape, F32)

    ins, outs = _thru(arrays)
    res = _call(
        body, ins + sems.items, name=name, in_specs=[HBM] * na + [SEM] * ns_in, mark=-1,
        out_specs=[SEM] * ns_out + [HBM] * na + [pl.BlockSpec(memory_space=pltpu.VMEM)],
        out_shape=[pltpu.SemaphoreType.DMA(())] * ns_out + outs + [TOKEN],
        input_output_aliases={a: ns_out + a for a in range(na)},
        compiler_params=pltpu.CompilerParams(has_side_effects=EFFECT))
    return _Sems(res[:ns_out], sem_shape), list(res[ns_out:ns_out + na])


def _half(ref, q, which):
    h = ref.shape[1] // 2
    return ref.at[q, pl.ds(which * h, h)]


def _quarter(ref, q, half, which):
    h = ref.shape[1] // 2
    return ref.at[q, pl.ds(half * h + which * (h // 2), h // 2)]


def _gather_plan(n_halved):
    def plan(refs):
        x, y, c = _place()
        p = 2 * x + y
        starts, mine, arrivals = [], [], []
        for n, ref in enumerate(refs):
            for k in range(1, N_CHIPS if n >= n_halved else 3):
                px, py = _chip_at(x, y, k)
                q = 2 * px + py
                out = _half(ref, p, c) if n < n_halved else ref.at[p]
                inc = _half(ref, q, c) if n < n_halved else ref.at[q]
                starts.append((out, out, (n, k - 1), (px, py, c)))
                mine.append((out, (n, k - 1)))
                arrivals.append((inc, (n, k - 1)))
        return starts, mine, arrivals
    return plan


def _spread_plan(n_halved, part):
    def plan(refs):
        x, y, c = _place()
        p = 2 * x + y
        starts, mine, arrivals = [], [], []
        for n in range(n_halved):
            for k in (1, 2):
                px, py = _chip_at(x, y, k)
                q = 2 * px + py
                tx, ty = _chip_at(x, y, 3 - k)
                dx, dy = _chip_at(x, y, 3)
                piece = _quarter(refs[n], q, c, 2 - k)
                landed = _half(refs[n], q, c)
                starts.append((piece, piece, (0, n, k - 1), (tx, ty, c), (n, k - 1)))
                starts.append((landed, landed, (1, n, k - 1), (x, y, 1 - c), (n, k - 1)))
                if part != 1:
                    mine.append((piece, (0, n, k - 1)))
                    arrivals.append((_quarter(refs[n], 2 * dx + dy, c, k - 1), (0, n, 2 - k)))
                if part != 0:
                    mine.append((landed, (1, n, k - 1)))
                    arrivals.append((_half(refs[n], q, 1 - c), (1, n, k - 1)))
        return starts, mine, arrivals
    return plan


def _last_hand_on_plan(n_halved):
    def plan(refs):
        x, y, c = _place()
        dx, dy = _chip_at(x, y, 3)
        d = 2 * dx + dy
        starts, mine, arrivals = [], [], []
        for n in range(n_halved):
            for k in (1, 2):
                piece = _quarter(refs[n], d, c, k - 1)
                starts.append((piece, piece, (n, k - 1), (x, y, 1 - c), (0, n, 2 - k)))
                mine.append((piece, (n, k - 1)))
                arrivals.append((_quarter(refs[n], d, 1 - c, k - 1), (n, k - 1)))
        return starts, mine, arrivals
    return plan


def _swap_plan(n):
    def plan(refs):
        x, y, c = _place()
        starts, mine, arrivals = [], [], []
        for a in range(n):
            h = refs[a].shape[1] // 2
            src = refs[a].at[:, pl.ds((1 - c) * h, h)]
            starts.append((src, refs[n + a], (a,), (x, y, 1 - c)))
            mine.append((src, (a,)))
            arrivals.append((refs[n + a], (a,)))
        return starts, mine, arrivals
    return plan


def _scatter_plan(n):
    def plan(refs):
        x, y, c = _place()
        starts, mine, arrivals = [], [], []
        for a in range(n):
            for k in range(1, N_CHIPS):
                px, py = _chip_at(x, y, k)
                src = refs[a].at[2 * px + py]
                starts.append((src, refs[n + a].at[k - 1], (a, k - 1), (px, py, c)))
                mine.append((src, (a, k - 1)))
                arrivals.append((refs[n + a].at[k - 1], (a, k - 1)))
        return starts, mine, arrivals
    return plan


def _share_plan(n):
    def plan(refs):
        x, y, c = _place()
        starts, mine, arrivals = [], [], []
        for a in range(n):
            h = refs[a].shape[0] // 2
            own = refs[a].at[pl.ds(c * h, h)]
            starts.append((own, own, (a,), (x, y, 1 - c)))
            mine.append((own, (a,)))
            arrivals.append((refs[a].at[pl.ds((1 - c) * h, h)], (a,)))
        return starts, mine, arrivals
    return plan


N_DEVICES = 8


def _packs_plan(refs):
    buf = refs[0]
    x, y, c = _place()
    me = 4 * x + 2 * y + c
    starts, mine, arrivals = [], [], []
    for r in range(1, N_DEVICES):
        peer = (1 - x if r & 4 else x, 1 - y if r & 2 else y, 1 - c if r & 1 else c)
        starts.append((buf.at[me], buf.at[me], (r - 1,), peer))
        mine.append((buf.at[me], (r - 1,)))
        arrivals.append((buf.at[4 * peer[0] + 2 * peer[1] + peer[2]], (r - 1,)))
    return starts, mine, arrivals


class _Reduction:
    def __init__(self, tag, slabs, c_idx, chip):
        self.tag, self.n, self.c_idx, self.chip = tag, len(slabs), c_idx, chip
        lands = [lax.empty((g.shape[0], g.shape[1] // 2, g.shape[2]), g.dtype) for g in slabs]
        self.sems = _comm_start("swap_start_" + tag, list(slabs) + lands, (self.n,), _swap_plan(self.n))

    def partial(self):
        n = self.n
        sems, bufs = self.sems
        bufs = _comm_wait("swap_wait_" + self.tag, bufs, [(sems, _swap_plan(n))])
        both = [_chip_partial(g, r, self.c_idx, self.chip) for g, r in zip(bufs[:n], bufs[n:])]
        self.own = [o for _, o in both]
        parts = [p for p, _ in both]
        lands = [lax.empty((N_CHIPS - 1,) + p.shape[1:], p.dtype) for p in parts]
        self.sems = _comm_start("scatter_start_" + self.tag, parts + lands, (n, N_CHIPS - 1), _scatter_plan(n))

    def total(self):
        n = self.n
        sems, bufs = self.sems
        bufs = _comm_wait("scatter_wait_" + self.tag, bufs, [(sems, _scatter_plan(n))])
        fulls = [_sum_partials(o, r, self.c_idx) for o, r in zip(self.own, bufs[n:])]
        self.sems = _comm_start("share_start_" + self.tag, fulls, (n,), _share_plan(n))

    def finish(self):
        sems, bufs = self.sems
        return _comm_wait("share_wait_" + self.tag, bufs, [(sems, _share_plan(self.n))])


def _elem_tile(rows, cols):
    return _pick(rows, max(8, (1 << 19) // cols // 8 * 8))


def _chip_partial(grad, recv, c_idx, p_idx):
    Q, R, C = grad.shape
    h = R // 2
    T = _elem_tile(h, C)
    nt = h // T

    def body(sc_ref, g_ref, r_ref, sb_ref, own_ref):
        q = pl.program_id(1)
        s = g_ref[...] + r_ref[...]
        sb_ref[...] = s.astype(BF16)

        @pl.when(q == sc_ref[1])
        def _():
            own_ref[...] = s

    return _call(
        body, (grad, recv), name="chip_partial", grid=(nt, Q),
        scalars=jnp.stack([c_idx, p_idx]).astype(jnp.int32),
        in_specs=[pl.BlockSpec((None, T, C), lambda t, q, sc: (q, sc[0] * nt + t, 0)),
                  pl.BlockSpec((None, T, C), lambda t, q, sc: (q, t, 0))],
        out_specs=[pl.BlockSpec((None, T, C), lambda t, q, sc: (q, t, 0)),
                   pl.BlockSpec((T, C), lambda t, q, sc: (t, 0))],
        out_shape=[_sds((Q, h, C), BF16), _sds((h, C), F32)], compiler_params=_params(2))


def _sum_partials(own, parts, c_idx):
    h, C = own.shape
    T = _elem_tile(h, C)
    nt = h // T

    def body(c_ref, o_ref, p_ref, t_ref):
        t = o_ref[...]
        for k in range(N_CHIPS - 1):
            t = t + p_ref[k].astype(F32)
        t_ref[...] = t

    return _call(
        body, (own, parts), name="sum_partials", grid=(nt,), scalars=jnp.reshape(c_idx, (1,)).astype(jnp.int32),
        in_specs=[pl.BlockSpec((T, C), lambda i, c: (i, 0)),
                  pl.BlockSpec((N_CHIPS - 1, T, C), lambda i, c: (0, i, 0))],
        out_specs=pl.BlockSpec((T, C), lambda i, c: (c[0] * nt + i, 0)),
        out_shape=_sds((2 * h, C), F32), compiler_params=_params(1))


def _pack_rows(name, parts, slot=None, n_slots=1):
    width = parts[0].shape[1]
    offsets, at = [], 0
    for p in parts:
        offsets.append(at)
        at += p.shape[0]
    total = -(-at // 8) * 8
    lead = 0 if slot is None else 1

    def body(*refs):
        out = refs[-1]
        out[...] = jnp.zeros(out.shape, F32)
        for ref, o in zip(refs[lead:-1], offsets):
            out[o:o + ref.shape[0], :] = ref[...]

    if slot is None:
        whole = pl.BlockSpec(memory_space=pltpu.VMEM)
        return _call(body, list(parts), name=name, in_specs=[whole] * len(parts), out_specs=whole,
                     out_shape=_sds((total, width), F32))
    return _call(body, list(parts), name=name, grid=(1,), scalars=jnp.reshape(slot, (1,)).astype(jnp.int32),
                 in_specs=[pl.BlockSpec(p.shape, lambda i, s: (0, 0)) for p in parts],
                 out_specs=pl.BlockSpec((None, total, width), lambda i, s: (s[0], 0, 0)),
                 out_shape=_sds((n_slots, total, width), F32))


def _sum_packs(packs):
    n, R, C = packs.shape

    def body(p_ref, o_ref):
        t = p_ref[0]
        for k in range(1, n):
            t = t + p_ref[k]
        o_ref[...] = t

    return _call(
        body, (packs,), name="sum_packs", grid=(1,), in_specs=[pl.BlockSpec((n, R, C), lambda i: (0, 0, 0))],
        out_specs=pl.BlockSpec((R, C), lambda i: (0, 0)), out_shape=_sds((R, C), F32), compiler_params=_params(1))


def _adamw(w, g, m, v):
    R, C = w.shape
    T = _elem_tile(R, C)

    def body(w_ref, g_ref, m_ref, v_ref, d_ref, m2_ref, v2_ref):
        g_ = g_ref[...]
        m2 = ADAM_B1 * m_ref[...] + (1.0 - ADAM_B1) * g_
        v2 = ADAM_B2 * v_ref[...] + (1.0 - ADAM_B2) * (g_ * g_)
        m_hat = m2 / (1.0 - ADAM_B1 ** ADAM_STEP)
        v_hat = v2 / (1.0 - ADAM_B2 ** ADAM_STEP)
        d_ref[...] = -ADAM_LR * (m_hat / (jnp.sqrt(v_hat) + ADAM_EPS) + ADAM_WD * w_ref[...])
        m2_ref[...] = m2
        v2_ref[...] = v2

    blk = pl.BlockSpec((T, C), lambda i: (i, 0))
    return _call(
        body, (w, g, m, v), name="adamw", grid=(R // T,), in_specs=[blk] * 4, out_specs=[blk] * 3,
        out_shape=[_sds((R, C), F32)] * 3, compiler_params=_params(1))


def _adamw_rows(w, g, m, v, row0, prev=None):
    R, C = w.shape
    T = _elem_tile(g.shape[0], C)
    off = row0 // T

    def body(w_ref, g_ref, m_ref, v_ref, *rest):
        d_ref, m2_ref, v2_ref, g2_ref = rest[-4:]
        g_ = g_ref[...]
        m2 = ADAM_B1 * m_ref[...] + (1.0 - ADAM_B1) * g_
        v2 = ADAM_B2 * v_ref[...] + (1.0 - ADAM_B2) * (g_ * g_)
        m_hat = m2 / (1.0 - ADAM_B1 ** ADAM_STEP)
        v_hat = v2 / (1.0 - ADAM_B2 ** ADAM_STEP)
        d_ref[...] = -ADAM_LR * (m_hat / (jnp.sqrt(v_hat) + ADAM_EPS) + ADAM_WD * w_ref[...])
        m2_ref[...] = m2
        v2_ref[...] = v2
        g2_ref[...] = g_

    here = pl.BlockSpec((T, C), lambda i: (off + i, 0))
    piece = pl.BlockSpec((T, C), lambda i: (i, 0))
    done = tuple(prev or ())
    return _call(
        body, (w, g, m, v) + done, name="adamw_rows", grid=(g.shape[0] // T,), follows=prev is None,
        in_specs=[here, piece, here, here] + [ANY] * len(done), out_specs=[here] * 4,
        out_shape=[_sds((R, C), F32)] * 4, input_output_aliases={4 + j: j for j in range(len(done))},
        compiler_params=_params(1))


SC_TILES = 32
SC_ROWS = 8
SC_LANES = 16


def _adamw_sc(w, g, m, v):
    R, C = w.shape
    per_tile = R // SC_TILES
    assert per_tile % SC_ROWS == 0 and C % SC_LANES == 0

    def body(w_hbm, g_hbm, m_hbm, v_hbm, d_out, m_out, v_out, g_out, wb, gb, mb, vb):
        tile = lax.axis_index("sc_tile") * 2 + lax.axis_index("sc_core")

        @pl.loop(0, per_tile // SC_ROWS)
        def _(chunk):
            here = pl.ds(tile * per_tile + chunk * SC_ROWS, SC_ROWS)
            pltpu.sync_copy(w_hbm.at[here], wb)
            pltpu.sync_copy(g_hbm.at[here], gb)
            pltpu.sync_copy(m_hbm.at[here], mb)
            pltpu.sync_copy(v_hbm.at[here], vb)
            for r in range(SC_ROWS):
                @pl.loop(0, C, step=SC_LANES)
                def _(j):
                    at = pl.ds(j, SC_LANES)
                    g_ = gb[r, at]
                    m2 = ADAM_B1 * mb[r, at] + (1.0 - ADAM_B1) * g_
                    v2 = ADAM_B2 * vb[r, at] + (1.0 - ADAM_B2) * (g_ * g_)
                    m_hat = m2 / (1.0 - ADAM_B1 ** ADAM_STEP)
                    v_hat = v2 / (1.0 - ADAM_B2 ** ADAM_STEP)
                    wb[r, at] = -ADAM_LR * (m_hat / (jnp.sqrt(v_hat) + ADAM_EPS) + ADAM_WD * wb[r, at])
                    mb[r, at] = m2
                    vb[r, at] = v2
            pltpu.sync_copy(wb, d_out.at[here])
            pltpu.sync_copy(mb, m_out.at[here])
            pltpu.sync_copy(vb, v_out.at[here])
            pltpu.sync_copy(gb, g_out.at[here])

    buf = pltpu.VMEM((SC_ROWS, C), F32)
    return pl.kernel(
        body, name="adamw_sc", out_type=(_sds((R, C), F32),) * 4,
        mesh=plsc.VectorSubcoreMesh(core_axis_name="sc_core", subcore_axis_name="sc_tile"),
        scratch_types=[buf, buf, buf, buf])(w, g, m, v)


BIG = ("w_in", "w_pool_out", "w_conv_out", "w_o", "w_up", "w_down", "w_pool_grp")
VECTORS = ("g_pre_mix", "pool_scale", "b_dw", "conv_ln_g", "conv_ln_b", "g_post_mix", "g_pre_mlp", "g_post_mlp")
WEIGHTS = ("meta", "g_pre_mix", "w_in", "w_pool_grp", "pool_scale", "w_pool_out", "w_dw", "b_dw", "conv_ln_g",
           "conv_ln_b", "w_conv_out", "w_o", "g_post_mix", "g_pre_mlp", "w_up", "w_down", "g_post_mlp")


def _as_rows(a, width):
    r, cols = a.shape
    return a.reshape(r * (cols // width), width)


def _step(w, m, v, x, tgt):
    S, D = x.shape
    P = D // 2
    xi, yi, ci = _place()
    chip = 2 * xi + yi
    _CHAIN["after"] = []

    C = D // 2
    G = POOL_GROUPS
    GD = P // G
    GS = GD // N_CHIPS
    Q = N_CHIPS
    vecs = {k: w[k] for k in VECTORS}
    shard2d = {k: w[k].reshape(-1, w[k].shape[-1]) for k in BIG}
    grads, delta, new_m, new_v = {}, {}, {}, {}

    def update(names, reduced, on_sparsecore=False):
        for k, g in zip(names, reduced):
            args = (shard2d[k], g, m[k].reshape(shard2d[k].shape), v[k].reshape(shard2d[k].shape))
            delta[k], new_m[k], new_v[k], grads[k] = _adamw_sc(*args) if on_sparsecore else _adamw_rows(*args, 0)

    groups = dict(a=(("w_in", "w_pool_grp"), ("w_dw", "meta")), b=(("w_pool_out", "w_conv_out", "w_o"), ()),
                  c=(("w_up",), ()), d=(("w_down",), ()))
    flying = {}

    def start(tag, follows):
        halved, whole = groups[tag]
        flying[tag] = _comm_start("gather_start_" + tag, flying[tag], (len(halved + whole), N_CHIPS - 1),
                                  _gather_plan(len(halved)), follows=follows)

    def spread(tag):
        nh = len(groups[tag][0])
        sems, bufs = flying[tag]
        flying[tag] = _comm_relay("gather_relay_" + tag, bufs, sems, _gather_plan(nh), (2, nh, 2),
                                  _spread_plan(nh, 0))

    def landed(tag):
        halved, whole = groups[tag]
        nh = len(halved)
        sems, bufs = flying.pop(tag)
        last, tree = _comm_relay("gather_relay2_" + tag, bufs[:nh], sems, _spread_plan(nh, 0), (nh, 2),
                                 _last_hand_on_plan(nh))
        done = _comm_wait("gather_wait_" + tag, tree,
                          [(sems, _spread_plan(nh, 1)), (last, _last_hand_on_plan(nh))])
        return dict(zip(halved + whole, done + bufs[nh:]))

    for tag, (halved, whole) in groups.items():
        flying[tag] = [_cast_into_slab(shard2d[k], chip, BF16) for k in halved]
        flying[tag] += [_cast_into_slab(w[k], chip, F32) for k in whole]
        if tag == "a":
            start(tag, False)

    small_names = [k for k in WEIGHTS if k not in BIG]

    def pack_small(tree):
        parts = []
        for k in small_names:
            flat = tree[k].reshape(-1)
            parts.append(jnp.pad(flat, (0, -flat.shape[0] % P)).reshape(-1, P))
        return _pack_rows("pack_small", parts)

    small_w, small_m, small_v = pack_small(w), pack_small(m), pack_small(v)
    h0, u1 = _pre_norm_tokens(x, vecs["g_pre_mix"])
    spread("a")
    start("b", True)

    got = landed("a")
    win_g = got["w_in"]
    w_grp = got["w_pool_grp"].reshape(N_CHIPS, G, GS, GD).transpose(1, 0, 2, 3).reshape(G, GD, GD)
    w_dw = got["w_dw"].transpose(1, 0, 2).reshape(CONV_TAPS, P)
    meta = got["meta"].transpose(1, 0, 2).reshape(N_META, D)
    h0, u1 = _pre_norm_meta(meta, vecs["g_pre_mix"], h0, u1)
    spread("b")
    start("c", True)
    start("d", True)
    proj = _mm_act_colw("proj", u1, win_g, 256)
    d, ya_pre = _pool_fwd(proj, w_grp, vecs["pool_scale"])
    s, c = _conv_fwd(proj, C, w_dw, vecs["b_dw"], vecs["conv_ln_g"], vecs["conv_ln_b"])
    spread("c")
    got = landed("b")
    wpo_g, wco_g, w_o = got["w_pool_out"], got["w_conv_out"], got["w_o"].reshape(D, D)
    mix, ya, yb = _mix_fwd(ya_pre, s, wpo_g, wco_g, proj, D)
    o = _mm_act_roww("attn_out", mix, w_o, 1088, 1024, 2048)
    spread("d")
    h1, u2 = _mid_norm(o, h0, vecs["g_post_mix"], vecs["g_pre_mlp"])
    wup_g = landed("c")["w_up"]
    a_up, fact = _mm_act_colw("mlp_up", u2, wup_g, 512, _up_epilogue, (BF16, BF16))
    w_down = landed("d")["w_down"].reshape(-1, D)
    f = _mm_act_roww("mlp_down", fact, w_down, 1088, 1024, 2048)
    dy, df, dg_post_mlp, loss = _loss_head(f, h1, tgt, vecs["g_post_mlp"])

    g_w_down = _mm_wgrad("dw_down", fact, df, 1024, 1024).reshape(N_CHIPS, -1, D)
    _CHAIN["after"] = [g_w_down]
    red1 = _Reduction("1", [g_w_down], ci, chip)
    da_up = _mlp_down_bwd(df, w_down, a_up)
    red1.partial()
    g_w_up = _mm_wgrad_colw("dw_up", u2, da_up, Q, 1024, 1024)
    red2 = _Reduction("2", [g_w_up], ci, chip)
    du2 = _mm_grad_colw_t("du2", da_up, wup_g, 1088, 1024)
    red2.partial()
    dh1, do, dg_pre_mlp, dg_post_mix = _mid_norm_bwd(dy, du2, h1, o, vecs["g_pre_mlp"], vecs["g_post_mix"])
    g_w_o = _mm_wgrad("dw_o", mix, do, 1024, 1024)
    red1.total()
    dya, dyb, dga, dgb = _mix_bwd(do, w_o, proj, ya, yb)
    update(("w_down",), red1.finish(), on_sparsecore=True)
    g_wpo = _mm_wgrad_colw("dw_pool_out", ya_pre, dya, Q, 1024, 512)
    g_wco = _mm_wgrad_colw("dw_conv_out", s, dyb, Q, 1024, 512)
    red3 = _Reduction("3", [g_w_o.reshape(N_CHIPS, D // N_CHIPS, D), g_wpo, g_wco], ci, chip)
    dya_pre = _mm_grad_colw_t("dya_pre", dya, wpo_g, 1088, 1024)
    ds = _mm_grad_colw_t("ds", dyb, wco_g, 1088, 1024)
    red3.partial()
    dz, g_w_grp, dscale = _pool_bwd(dya_pre, d, w_grp, vecs["pool_scale"])
    dc, dln_g, dln_b, db_dw = _conv_ln_bwd(ds, c, vecs["conv_ln_g"], vecs["conv_ln_b"])
    red2.total()
    dv, dgc, g_w_dw = _conv_bwd(dc, proj, C, w_dw)
    update(("w_up",), red2.finish(), on_sparsecore=True)
    dproj = _concat_cols("dproj", [dz, dv, dgc, dga, dgb])
    half_k = D // 2
    g_w_grp = g_w_grp.reshape(G, N_CHIPS, GS, GD).transpose(1, 0, 2, 3).reshape(N_CHIPS, G * GS, GD)
    g_in_a = _mm_wgrad_colw("dw_in_a", u1, dproj, Q, 512, 1792, rows=(0, half_k))
    red4a = _Reduction("4a", [g_in_a, g_w_grp], ci, chip)
    g_in_b = _mm_wgrad_colw("dw_in_b", u1, dproj, Q, 512, 1792, rows=(half_k, half_k))
    red4a.partial()
    red4b = _Reduction("4b", [g_in_b], ci, chip)
    du1 = _mm_grad_colw_t("du1", dproj, win_g, 1088, 1024)
    red4b.partial()
    red3.total()
    grad_x, dmeta, dg_pre_mix = _pre_norm_bwd(dh1, du1, h0, vecs["g_pre_mix"])

    g_vec = dict(g_pre_mix=dg_pre_mix, pool_scale=dscale, b_dw=db_dw, conv_ln_g=dln_g, conv_ln_b=dln_b,
                 g_post_mix=dg_post_mix, g_pre_mlp=dg_pre_mlp, g_post_mlp=dg_post_mlp)
    rows = [g_w_dw, _as_rows(dmeta, P)] + [_as_rows(g_vec[k], P) for k in VECTORS]
    rows.append(jnp.broadcast_to(loss[:, :1], (1, P)))
    packs = _comm_start("packs_start", [_pack_rows("pack_grads", rows, 2 * chip + ci, N_DEVICES)],
                        (N_DEVICES - 1,), _packs_plan)
    red4a.total()
    update(("w_o", "w_pool_out", "w_conv_out"), red3.finish())
    red_in_a, red_grp = red4a.finish()
    update(("w_pool_grp",), [red_grp])
    w_in_rows = (shard2d["w_in"], m["w_in"].reshape(shard2d["w_in"].shape), v["w_in"].reshape(shard2d["w_in"].shape))
    first_rows = _adamw_rows(w_in_rows[0], red_in_a, w_in_rows[1], w_in_rows[2], 0)
    total = _sum_packs(_comm_wait("packs_wait", packs[1], [(packs[0], _packs_plan)])[0])
    at = 0
    taps_pad = g_w_dw.shape[0]
    g_dw_full = total[at:at + CONV_TAPS]
    at += taps_pad
    g_meta_full = total[at:at + 2 * N_META].reshape(N_META, D)
    at += 2 * N_META
    for k in VECTORS:
        n = w[k].shape[-1] // P
        grads[k] = total[at:at + n].reshape(1, n * P)
        at += n
    loss_total = total[at, 0]
    grads["w_dw"] = lax.dynamic_slice_in_dim(g_dw_full, chip * (P // N_CHIPS), P // N_CHIPS, axis=1)
    grads["meta"] = lax.dynamic_slice_in_dim(g_meta_full, chip * (D // N_CHIPS), D // N_CHIPS, axis=1)

    sd, sm, sv = _adamw(small_w, pack_small(grads), small_m, small_v)
    at = 0
    for k in small_names:
        a = w[k].reshape(-1, w[k].shape[-1])
        n = -(-a.size // P)
        for tree, packed in ((delta, sd), (new_m, sm), (new_v, sv)):
            tree[k] = packed[at:at + n].reshape(-1)[:a.size].reshape(a.shape)
        at += n

    red4b.total()
    delta["w_in"], new_m["w_in"], new_v["w_in"], grads["w_in"] = _adamw_rows(
        w_in_rows[0], red4b.finish()[0], w_in_rows[1], w_in_rows[2], half_k, prev=first_rows)
    return loss_total, grad_x, grads, delta, new_m, new_v


def kernel(x, meta, g_pre_mix, w_in, w_pool_grp, pool_scale, w_pool_out, w_dw, b_dw, conv_ln_g, conv_ln_b, w_conv_out, w_o, g_post_mix, g_pre_mlp, w_up, w_down, g_post_mlp, loss_target, m_meta, m_g_pre_mix, m_w_in, m_w_pool_grp, m_pool_scale, m_w_pool_out, m_w_dw, m_b_dw, m_conv_ln_g, m_conv_ln_b, m_w_conv_out, m_w_o, m_g_post_mix, m_g_pre_mlp, m_w_up, m_w_down, m_g_post_mlp, v_meta, v_g_pre_mix, v_w_in, v_w_pool_grp, v_pool_scale, v_w_pool_out, v_w_dw, v_b_dw, v_conv_ln_g, v_conv_ln_b, v_w_conv_out, v_w_o, v_g_post_mix, v_g_pre_mlp, v_w_up, v_w_down, v_g_post_mlp):
    args = dict(locals())
    shapes = {k: args[k].shape for k in WEIGHTS}
    w = {k: args[k] for k in WEIGHTS}
    m = {k: args["m_" + k] for k in WEIGHTS}
    v = {k: args["v_" + k] for k in WEIGHTS}
    for tree in (w, m, v):
        tree["w_dw"] = tree["w_dw"].reshape(tree["w_dw"].shape[-2:])
    loss, grad_x, grads, delta, new_m, new_v = _step(w, m, v, x[0], loss_target[0])
    out = [loss, grad_x[None]]
    for tree in (grads, delta, new_m, new_v):
        out += [tree[k].reshape(shapes[k]) for k in WEIGHTS]
    return tuple(out)
```

```python
import jax
import jax.numpy as jnp
from jax import lax
from jax.experimental import pallas as pl
from jax.experimental.pallas import tpu as pltpu
from jax.experimental.pallas import tpu_sc as plsc

F32 = jnp.float32
BF16 = jnp.bfloat16

N_META = 16
PAD_ROWS = 112
TOKEN_ROW0 = PAD_ROWS + N_META
POOL_GROUPS = 4
CONV_TAPS = 31
HALO = 32
CONV_ROWS = 128
LANES = 128
RMS_EPS = 1e-6
LN_EPS = 1e-5
ADAM_LR = 0.001
ADAM_B1 = 0.9
ADAM_B2 = 0.999
ADAM_EPS = 1e-08
ADAM_WD = 0.01
ADAM_STEP = 10
VMEM_LIMIT_MB = 56

MESH = pl.DeviceIdType.MESH
NN = (((1,), (0,)), ((), ()))
NT = (((1,), (1,)), ((), ()))
TN = (((0,), (0,)), ((), ()))


def _pick(n, pref):
    if n <= pref:
        return n
    if n % pref == 0:
        return pref
    for step in (LANES, 8, 1):
        t = (pref // step) * step
        while t >= step:
            if n % t == 0:
                return t
            t -= step
    return n


def _params(n_axes, vmem_mb=VMEM_LIMIT_MB):
    return pltpu.CompilerParams(dimension_semantics=("arbitrary",) * n_axes,
                                vmem_limit_bytes=vmem_mb << 20)


def _sigmoid(x):
    return jax.nn.sigmoid(x)


_CHAIN = {"after": []}


def _call(body, args, *, in_specs, out_specs, out_shape, grid=(), scalars=None, mark=0, follows=True, made_from=None,
          **kw):
    pending = _CHAIN["after"]
    after = pending if follows else []
    n = len(args)
    lead = 0 if scalars is None else 1
    specs = list(in_specs)
    operands = list(args)
    fn = body
    if after:
        def fn(*refs):
            body(*refs[:lead + n], *refs[lead + n + len(after):])
        specs += [pl.BlockSpec(memory_space=pl.ANY)] * len(after)
        operands += after
    if scalars is None:
        if grid:
            kw["grid"] = grid
        res = pl.pallas_call(fn, in_specs=specs, out_specs=out_specs, out_shape=out_shape, **kw)(*operands)
    else:
        grid_spec = pltpu.PrefetchScalarGridSpec(num_scalar_prefetch=1, grid=grid, in_specs=specs, out_specs=out_specs)
        res = pl.pallas_call(fn, grid_spec=grid_spec, out_shape=out_shape, **kw)(scalars, *operands)
    outs = res if isinstance(res, (list, tuple)) else [res]
    consumed = args if made_from is None else made_from
    kept = [] if follows else [m for m in pending if not any(m is a for a in consumed)]
    _CHAIN["after"] = kept + [outs[mark]]
    return res


def _store(val, extras, outs):
    outs[0][...] = val.astype(outs[0].dtype)


def _mm(name, grid, arrays, in_specs, out_shapes, out_specs, dims, nk, epilogue=_store, acc_shape=None):
    n_in, n_out = len(arrays), len(out_shapes)

    def body(*refs):
        extras = refs[2:n_in]
        outs = refs[n_in:n_in + n_out]
        part = lax.dot_general(refs[0][...], refs[1][...], dims, preferred_element_type=F32)
        if nk == 1:
            epilogue(part, extras, outs)
        else:
            acc = refs[n_in + n_out]
            k = pl.program_id(len(grid) - 1)

            @pl.when(k == 0)
            def _():
                acc[...] = part

            @pl.when(k > 0)
            def _():
                acc[...] += part

            @pl.when(k == nk - 1)
            def _():
                epilogue(acc[...], extras, outs)

    scratch = [pltpu.VMEM(acc_shape, F32)] if nk > 1 else []
    single = n_out == 1
    return _call(
        body, arrays, name=name, grid=grid, in_specs=in_specs,
        out_specs=out_specs[0] if single else out_specs,
        out_shape=out_shapes[0] if single else out_shapes,
        scratch_shapes=scratch, compiler_params=_params(len(grid)))


def _sds(shape, dtype):
    return jax.ShapeDtypeStruct(shape, dtype)


def _rms_scale(h):
    return lax.rsqrt(jnp.mean(h * h, axis=-1, keepdims=True) + RMS_EPS)


def _rms_bwd(du, h, g):
    r = _rms_scale(h)
    y = h * r
    dy = du * g
    dh = r * (dy - y * jnp.mean(dy * y, axis=-1, keepdims=True))
    return dh, jnp.sum(du * y, axis=0, keepdims=True)


def _row_tile(L):
    return _pick(L, 272)


def _pre_norm_tokens(x, g):
    S, D = x.shape
    T = TOKEN_ROW0
    L = S + T

    def body(x_ref, g_ref, h_ref, u_ref):
        h = x_ref[...]
        h_ref[...] = h
        u_ref[...] = (h * _rms_scale(h) * g_ref[...]).astype(BF16)

    below = pl.BlockSpec((T, D), lambda i: (i + 1, 0))
    return _call(
        body, (x, g), name="pre_norm_tokens", grid=(S // T,),
        in_specs=[pl.BlockSpec((T, D), lambda i: (i, 0)), pl.BlockSpec((1, D), lambda i: (0, 0))],
        out_specs=[below, below], out_shape=[_sds((L, D), F32), _sds((L, D), BF16)], compiler_params=_params(1))


def _pre_norm_meta(meta, g, h0, u1):
    L, D = h0.shape
    T = TOKEN_ROW0

    def body(m_ref, g_ref, h_in, u_in, h_ref, u_ref):
        h_ref[...] = jnp.zeros(h_ref.shape, F32)
        h_ref[PAD_ROWS:, :] = m_ref[...]
        h = h_ref[...]
        u_ref[...] = (h * _rms_scale(h) * g_ref[...]).astype(BF16)

    first = pl.BlockSpec((T, D), lambda i: (0, 0))
    return _call(
        body, (meta, g, h0, u1), name="pre_norm_meta", grid=(1,),
        in_specs=[pl.BlockSpec((N_META, D), lambda i: (0, 0)), pl.BlockSpec((1, D), lambda i: (0, 0)), ANY, ANY],
        out_specs=[first, first], out_shape=[_sds((L, D), F32), _sds((L, D), BF16)],
        input_output_aliases={2: 0, 3: 1}, compiler_params=_params(1))


def _mid_norm(o, h0, g_post, g_pre):
    L, D = h0.shape
    T = _row_tile(L)

    def body(o_ref, h_ref, gp_ref, gm_ref, h1_ref, u2_ref):
        o_ = o_ref[...]
        h1 = h_ref[...] + o_ * _rms_scale(o_) * gp_ref[...]
        h1_ref[...] = h1
        u2_ref[...] = (h1 * _rms_scale(h1) * gm_ref[...]).astype(BF16)

    row = pl.BlockSpec((T, D), lambda i: (i, 0))
    vec = pl.BlockSpec((1, D), lambda i: (0, 0))
    return _call(
        body, (o, h0, g_post, g_pre), name="mid_norm", grid=(L // T,),
        in_specs=[row, row, vec, vec], out_specs=[row, row],
        out_shape=[_sds((L, D), F32), _sds((L, D), BF16)], compiler_params=_params(1))


def _loss_head(f, h1, tgt, g_post):
    L, D = h1.shape
    T = TOKEN_ROW0
    n = L // T

    def body(f_ref, h_ref, t_ref, g_ref, dy_ref, df_ref, dg_ref, loss_ref):
        i = pl.program_id(0)
        f_ = f_ref[...]
        g = g_ref[...]
        y = h_ref[...] + f_ * _rms_scale(f_) * g
        live = (i > 0).astype(F32)
        diff = (y - t_ref[...]) * live
        part = 0.5 * jnp.sum(jnp.mean(diff * diff, axis=-1, keepdims=True), axis=0, keepdims=True)
        dy = diff * (1.0 / D)
        dy_ref[...] = dy
        df, dg = _rms_bwd(dy, f_, g)
        df_ref[...] = df.astype(BF16)

        @pl.when(i == 0)
        def _():
            dg_ref[...] = dg
            loss_ref[...] = jnp.broadcast_to(part, loss_ref.shape)

        @pl.when(i > 0)
        def _():
            dg_ref[...] += dg
            loss_ref[...] += jnp.broadcast_to(part, loss_ref.shape)

    row = pl.BlockSpec((T, D), lambda i: (i, 0))
    vec = pl.BlockSpec((1, D), lambda i: (0, 0))
    return _call(
        body, (f, h1, tgt, g_post), name="loss_head", grid=(n,),
        in_specs=[row, row, pl.BlockSpec((T, D), lambda i: (jnp.maximum(i - 1, 0), 0)), vec],
        out_specs=[row, row, vec, pl.BlockSpec((1, LANES), lambda i: (0, 0))],
        out_shape=[_sds((L, D), F32), _sds((L, D), BF16), _sds((1, D), F32), _sds((1, LANES), F32)],
        compiler_params=_params(1))


def _mid_norm_bwd(dy, du2, h1, o, g_pre, g_post):
    L, D = h1.shape
    T = _row_tile(L)

    def body(dy_ref, du_ref, h_ref, o_ref, gm_ref, gp_ref, dh1_ref, do_ref, dgm_ref, dgp_ref):
        i = pl.program_id(0)
        dh, dgm = _rms_bwd(du_ref[...], h_ref[...], gm_ref[...])
        dh1 = dy_ref[...] + dh
        dh1_ref[...] = dh1
        do, dgp = _rms_bwd(dh1, o_ref[...], gp_ref[...])
        do_ref[...] = do.astype(BF16)

        @pl.when(i == 0)
        def _():
            dgm_ref[...] = dgm
            dgp_ref[...] = dgp

        @pl.when(i > 0)
        def _():
            dgm_ref[...] += dgm
            dgp_ref[...] += dgp

    row = pl.BlockSpec((T, D), lambda i: (i, 0))
    vec = pl.BlockSpec((1, D), lambda i: (0, 0))
    return _call(
        body, (dy, du2, h1, o, g_pre, g_post), name="mid_norm_bwd", grid=(L // T,),
        in_specs=[row, row, row, row, vec, vec], out_specs=[row, row, vec, vec],
        out_shape=[_sds((L, D), F32), _sds((L, D), BF16), _sds((1, D), F32), _sds((1, D), F32)],
        compiler_params=_params(1))


def _pre_norm_bwd(dh1, du1, h0, g):
    L, D = h0.shape
    T = TOKEN_ROW0
    n = L // T

    def body(dh_ref, du_ref, h_ref, g_ref, gx_ref, dmeta_ref, dg_ref):
        i = pl.program_id(0)
        dh, dg = _rms_bwd(du_ref[...], h_ref[...], g_ref[...])
        dh0 = dh_ref[...] + dh
        gx_ref[...] = dh0

        @pl.when(i == 0)
        def _():
            dmeta_ref[...] = dh0[PAD_ROWS:, :]
            dg_ref[...] = dg

        @pl.when(i > 0)
        def _():
            dg_ref[...] += dg

    row = pl.BlockSpec((T, D), lambda i: (i, 0))
    vec = pl.BlockSpec((1, D), lambda i: (0, 0))
    return _call(
        body, (dh1, du1, h0, g), name="pre_norm_bwd", grid=(n,),
        in_specs=[row, row, row, vec],
        out_specs=[pl.BlockSpec((T, D), lambda i: (jnp.maximum(i - 1, 0), 0)),
                   pl.BlockSpec((N_META, D), lambda i: (0, 0)), vec],
        out_shape=[_sds((L - T, D), F32), _sds((N_META, D), F32), _sds((1, D), F32)],
        compiler_params=_params(1))


def _window_sum(z, g, shift_sign, L):
    s = z
    for j in range(POOL_GROUPS):
        k = 1 << j
        nxt = s + pltpu.roll(s, k if shift_sign > 0 else L - k, 0)
        s = jnp.where(j <= g, nxt, s)
    return s


def _inv_count(g, L):
    t = lax.broadcasted_iota(jnp.int32, (L, 1), 0)
    w = jnp.left_shift(2, g)
    cnt = jnp.clip(t - (PAD_ROWS - 1), 1, w)
    return 1.0 / cnt.astype(F32)


def _pool_fwd(proj, w_grp, scale):
    L = proj.shape[0]
    G, GD, _ = w_grp.shape
    P = G * GD

    def body(z_ref, w_ref, sc_ref, d_ref, ya_ref):
        g = pl.program_id(0)
        z = z_ref[...]
        d = (_window_sum(z, g, +1, L) * _inv_count(g, L) - z).astype(BF16)
        d_ref[...] = d
        y = jnp.dot(d, w_ref[...], preferred_element_type=F32)
        ya_ref[...] = (y * sc_ref[...]).astype(BF16)

    col = pl.BlockSpec((L, GD), lambda g: (0, g))
    return _call(
        body, (proj, w_grp, scale), name="pool_fwd", grid=(G,),
        in_specs=[col, pl.BlockSpec((None, GD, GD), lambda g: (g, 0, 0)), pl.BlockSpec((1, GD), lambda g: (0, g))],
        out_specs=[col, col], out_shape=[_sds((L, P), BF16), _sds((L, P), BF16)],
        compiler_params=_params(1))


def _pool_bwd(dya, d, w_grp, scale):
    L, P = dya.shape
    G, GD, _ = w_grp.shape

    def body(dya_ref, d_ref, w_ref, sc_ref, dz_ref, dw_ref, dsc_ref):
        g = pl.program_id(0)
        dya_ = dya_ref[...]
        d_ = d_ref[...]
        w = w_ref[...]
        y = jnp.dot(d_, w, preferred_element_type=F32)
        dsc_ref[...] = jnp.sum(dya_ * y, axis=0, keepdims=True)
        dy = (dya_ * sc_ref[...]).astype(BF16)
        dw_ref[...] = lax.dot_general(d_, dy, TN, preferred_element_type=F32)
        dd = lax.dot_general(dy, w, NT, preferred_element_type=F32)
        dz = _window_sum(dd * _inv_count(g, L), g, -1, L) - dd
        dz_ref[...] = dz.astype(BF16)

    col = pl.BlockSpec((L, GD), lambda g: (0, g))
    wspec = pl.BlockSpec((None, GD, GD), lambda g: (g, 0, 0))
    vec = pl.BlockSpec((1, GD), lambda g: (0, g))
    return _call(
        body, (dya, d, w_grp, scale), name="pool_bwd", grid=(G,),
        in_specs=[col, col, wspec, vec], out_specs=[col, wspec, vec],
        out_shape=[_sds((L, P), BF16), _sds((G, GD, GD), F32), _sds((1, P), F32)],
        compiler_params=_params(1))


def _fill_rotations(rot_ref, ext):
    n = ext.shape[0]
    rot_ref[0] = ext
    for r in range(1, 8):
        rot_ref[r] = pltpu.roll(ext, n - r, 0)


def _lane_chunks(C):
    step = LANES if C % LANES == 0 else C
    return [(c0, step) for c0 in range(0, C, step)]


def _conv_specs(L, C, col_v, col_g):
    T = CONV_ROWS
    per = T // HALO
    cur_v = pl.BlockSpec((T, C), lambda i: (i, col_v))
    cur_g = pl.BlockSpec((T, C), lambda i: (i, col_g))
    prev_v = pl.BlockSpec((HALO, C), lambda i: (jnp.maximum(i * per - 1, 0), col_v))
    prev_g = pl.BlockSpec((HALO, C), lambda i: (jnp.maximum(i * per - 1, 0), col_g))
    return cur_v, cur_g, prev_v, prev_g


def _glu_ext(vc, gc, vh, gh, i):
    a_cur = vc[...] * _sigmoid(gc[...])
    a_prev = vh[...] * _sigmoid(gh[...]) * (i > 0).astype(F32)
    return jnp.concatenate([a_prev, a_cur], axis=0)


def _conv_fwd(proj, C, w_dw, b_dw, ln_g, ln_b):
    L = proj.shape[0]
    T = CONV_ROWS
    P = C

    def body(vc, gc, vh, gh, w_ref, b_ref, lg_ref, lb_ref, s_ref, c_ref, rot):
        i = pl.program_id(0)
        _fill_rotations(rot, _glu_ext(vc, gc, vh, gh, i))
        for c0, cw in _lane_chunks(C):
            acc = jnp.zeros((T, cw), F32)
            for k in range(CONV_TAPS):
                q, r = divmod(HALO - (CONV_TAPS - 1) + k, 8)
                acc = acc + w_ref[k:k + 1, c0:c0 + cw] * rot[r, 8 * q:8 * q + T, c0:c0 + cw]
            c_ref[:, c0:c0 + cw] = acc + b_ref[:, c0:c0 + cw]
        c = c_ref[...]
        mu = jnp.mean(c, axis=-1, keepdims=True)
        cen = c - mu
        var = jnp.mean(cen * cen, axis=-1, keepdims=True)
        ln = cen * lax.rsqrt(var + LN_EPS) * lg_ref[...] + lb_ref[...]
        s_ref[...] = (ln * _sigmoid(ln)).astype(BF16)

    cur_v, cur_g, prev_v, prev_g = _conv_specs(L, C, P // C, P // C + 1)
    row = pl.BlockSpec((T, C), lambda i: (i, 0))
    vec = pl.BlockSpec((1, C), lambda i: (0, 0))
    return _call(
        body, (proj, proj, proj, proj, w_dw, b_dw, ln_g, ln_b), name="conv_fwd", grid=(L // T,),
        in_specs=[cur_v, cur_g, prev_v, prev_g, pl.BlockSpec((CONV_TAPS, C), lambda i: (0, 0)), vec, vec, vec],
        out_specs=[row, row], out_shape=[_sds((L, C), BF16), _sds((L, C), F32)],
        scratch_shapes=[pltpu.VMEM((8, T + HALO, C), F32)], compiler_params=_params(1))


def _conv_ln_bwd(ds, c, ln_g, ln_b):
    L, C = c.shape
    T = _row_tile(L)

    def body(ds_ref, c_ref, lg_ref, lb_ref, dc_ref, dlg_ref, dlb_ref, db_ref):
        i = pl.program_id(0)
        c_ = c_ref[...]
        g = lg_ref[...]
        mu = jnp.mean(c_, axis=-1, keepdims=True)
        cen = c_ - mu
        rstd = lax.rsqrt(jnp.mean(cen * cen, axis=-1, keepdims=True) + LN_EPS)
        xhat = cen * rstd
        ln = xhat * g + lb_ref[...]
        sg = _sigmoid(ln)
        dln = ds_ref[...] * (sg * (1.0 + ln * (1.0 - sg)))
        dxh = dln * g
        dc = rstd * (dxh - jnp.mean(dxh, axis=-1, keepdims=True)
                     - xhat * jnp.mean(dxh * xhat, axis=-1, keepdims=True))
        dc_ref[...] = dc
        dlg = jnp.sum(dln * xhat, axis=0, keepdims=True)
        dlb = jnp.sum(dln, axis=0, keepdims=True)
        db = jnp.sum(dc, axis=0, keepdims=True)

        @pl.when(i == 0)
        def _():
            dlg_ref[...] = dlg
            dlb_ref[...] = dlb
            db_ref[...] = db

        @pl.when(i > 0)
        def _():
            dlg_ref[...] += dlg
            dlb_ref[...] += dlb
            db_ref[...] += db

    row = pl.BlockSpec((T, C), lambda i: (i, 0))
    vec = pl.BlockSpec((1, C), lambda i: (0, 0))
    return _call(
        body, (ds, c, ln_g, ln_b), name="conv_ln_bwd", grid=(L // T,),
        in_specs=[row, row, vec, vec], out_specs=[row, vec, vec, vec],
        out_shape=[_sds((L, C), F32), _sds((1, C), F32), _sds((1, C), F32), _sds((1, C), F32)],
        compiler_params=_params(1))


def _conv_bwd(dc, proj, C, w_dw):
    L = proj.shape[0]
    T = CONV_ROWS
    per = T // HALO
    n = L // T
    P = C
    taps_pad = 32

    def body(dcc, dcn, vc, gc, w_ref, dv_ref, dg_ref, dw_ref, rot_d, dw_acc):
        i = pl.program_id(0)
        dc_next = dcn[...] * (i < n - 1).astype(F32)
        _fill_rotations(rot_d, jnp.concatenate([dcc[...], dc_next], axis=0))

        @pl.when(i == 0)
        def _():
            dw_acc[...] = jnp.zeros(dw_acc.shape, F32)

        for c0, cw in _lane_chunks(C):
            v = vc[:, c0:c0 + cw]
            sg = _sigmoid(gc[:, c0:c0 + cw])
            a = v * sg
            da = jnp.zeros((T, cw), F32)
            for k in range(CONV_TAPS):
                q, r = divmod(CONV_TAPS - 1 - k, 8)
                slab = rot_d[r, 8 * q:8 * q + T, c0:c0 + cw]
                da = da + w_ref[k:k + 1, c0:c0 + cw] * slab
                dw_acc[k, :, c0:c0 + cw] += jnp.sum((a * slab).reshape(T // 8, 8, cw), axis=0)
            dv_ref[:, c0:c0 + cw] = (da * sg).astype(BF16)
            dg_ref[:, c0:c0 + cw] = (da * v * sg * (1.0 - sg)).astype(BF16)

        @pl.when(i == n - 1)
        def _():
            dw_ref[...] = jnp.sum(dw_acc[...], axis=1)

    cur_v, cur_g, _, _ = _conv_specs(L, C, P // C, P // C + 1)
    row = pl.BlockSpec((T, C), lambda i: (i, 0))
    nxt = pl.BlockSpec((HALO, C), lambda i: (jnp.minimum((i + 1) * per, L // HALO - 1), 0))
    wspec = pl.BlockSpec((CONV_TAPS, C), lambda i: (0, 0))
    return _call(
        body, (dc, dc, proj, proj, w_dw), name="conv_bwd", grid=(n,),
        in_specs=[row, nxt, cur_v, cur_g, wspec],
        out_specs=[row, row, pl.BlockSpec((taps_pad, C), lambda i: (0, 0))],
        out_shape=[_sds((L, C), BF16), _sds((L, C), BF16), _sds((taps_pad, C), F32)],
        scratch_shapes=[pltpu.VMEM((8, T + HALO, C), F32), pltpu.VMEM((taps_pad, 8, C), F32)],
        compiler_params=_params(1))


def _mix_fwd(ya_pre, s, wpo, wco, proj, D):
    L, P = ya_pre.shape
    Q, _, DS = wpo.shape
    bm = _pick(L, 1088)
    gate0 = (proj.shape[1] - 2 * D) // DS
    per = D // DS

    def body(a1, a2, b1, b2, ga, gb, m_ref, ya_ref, yb_ref):
        ya = jnp.dot(a1[...], b1[...], preferred_element_type=F32)
        yb = jnp.dot(a2[...], b2[...], preferred_element_type=F32)
        ya_ref[...] = ya.astype(BF16)
        yb_ref[...] = yb.astype(BF16)
        m_ref[...] = (_sigmoid(ga[...]) * ya + _sigmoid(gb[...]) * yb).astype(BF16)

    act = pl.BlockSpec((bm, P), lambda i, q: (i, 0))
    wsp = pl.BlockSpec((None, P, DS), lambda i, q: (q, 0, 0))
    out = pl.BlockSpec((bm, DS), lambda i, q: (i, q))
    return _call(
        body, (ya_pre, s, wpo, wco, proj, proj), name="mix_fwd", grid=(L // bm, Q),
        in_specs=[act, act, wsp, wsp,
                  pl.BlockSpec((bm, DS), lambda i, q: (i, gate0 + q)),
                  pl.BlockSpec((bm, DS), lambda i, q: (i, gate0 + per + q))],
        out_specs=[out, out, out],
        out_shape=[_sds((L, D), BF16), _sds((L, D), BF16), _sds((L, D), BF16)],
        compiler_params=_params(2))


def _mix_bwd(do, w_o, proj, ya, yb):
    L, D = do.shape
    bm = _pick(L, 544)
    bn = _pick(D // N_CHIPS, 512)
    gate0 = (proj.shape[1] - 2 * D) // bn
    per = D // bn

    def epilogue(dm, extras, outs):
        ga, gb, ya_ref, yb_ref = extras
        sa = _sigmoid(ga[...])
        sb = _sigmoid(gb[...])
        outs[0][...] = (dm * sa).astype(BF16)
        outs[1][...] = (dm * sb).astype(BF16)
        outs[2][...] = (dm * ya_ref[...].astype(F32) * sa * (1.0 - sa)).astype(BF16)
        outs[3][...] = (dm * yb_ref[...].astype(F32) * sb * (1.0 - sb)).astype(BF16)

    blk = pl.BlockSpec((bm, bn), lambda i, j: (i, j))
    return _mm(
        "mix_bwd", (L // bm, D // bn), [do, w_o, proj, proj, ya, yb],
        [pl.BlockSpec((bm, D), lambda i, j: (i, 0)), pl.BlockSpec((bn, D), lambda i, j: (j, 0)),
         pl.BlockSpec((bm, bn), lambda i, j: (i, gate0 + j)),
         pl.BlockSpec((bm, bn), lambda i, j: (i, gate0 + per + j)), blk, blk],
        [_sds((L, D), BF16)] * 4, [blk] * 4, NT, 1, epilogue)


def _mm_act_colw(name, a, wg, bn_pref, epilogue=_store, out_dtypes=(F32,)):
    L, K = a.shape
    Q, _, n = wg.shape
    bn = _pick(n, bn_pref)
    nj = n // bn
    out = pl.BlockSpec((L, bn), lambda q, j: (0, q * nj + j))
    return _mm(name, (Q, nj), [a, wg],
               [pl.BlockSpec((L, K), lambda q, j: (0, 0)), pl.BlockSpec((None, K, bn), lambda q, j: (q, 0, j))],
               [_sds((L, Q * n), dt) for dt in out_dtypes], [out] * len(out_dtypes), NN, 1, epilogue)


def _mm_grad_colw_t(name, g, wg, bm_pref, bn_pref):
    L = g.shape[0]
    Q, K, n = wg.shape
    bm = _pick(L, bm_pref)
    bn = _pick(K, bn_pref)
    return _mm(name, (L // bm, K // bn, Q), [g, wg],
               [pl.BlockSpec((bm, n), lambda i, j, k: (i, k)), pl.BlockSpec((None, bn, n), lambda i, j, k: (k, j, 0))],
               [_sds((L, K), F32)], [pl.BlockSpec((bm, bn), lambda i, j, k: (i, j))], NT, Q,
               acc_shape=(bm, bn))


def _mm_wgrad_colw(name, a, g, Q, bm_pref, bn_pref, rows=None):
    L, K = a.shape
    n = g.shape[1] // Q
    first, count = rows or (0, K)
    bm = _pick(count, bm_pref)
    bn = _pick(n, bn_pref)
    nj = n // bn
    i0 = first // bm
    return _mm(name, (Q, count // bm, nj), [a, g],
               [pl.BlockSpec((L, bm), lambda q, i, j: (0, i0 + i)),
                pl.BlockSpec((L, bn), lambda q, i, j: (0, q * nj + j))],
               [_sds((Q, count, n), F32)], [pl.BlockSpec((None, bm, bn), lambda q, i, j: (q, i, j))], TN, 1)


def _mm_wgrad(name, a, g, bm_pref, bn_pref):
    L, K = a.shape
    N = g.shape[1]
    bm = _pick(K, bm_pref)
    bn = _pick(N, bn_pref)
    return _mm(name, (K // bm, N // bn), [a, g],
               [pl.BlockSpec((L, bm), lambda i, j: (0, i)), pl.BlockSpec((L, bn), lambda i, j: (0, j))],
               [_sds((K, N), F32)], [pl.BlockSpec((bm, bn), lambda i, j: (i, j))], TN, 1)


def _mm_act_roww(name, a, w, bm_pref, bn_pref, bk_pref):
    L, K = a.shape
    N = w.shape[1]
    bm, bn, bk = _pick(L, bm_pref), _pick(N, bn_pref), _pick(K, bk_pref)
    nk = K // bk
    return _mm(name, (L // bm, N // bn, nk), [a, w],
               [pl.BlockSpec((bm, bk), lambda i, j, k: (i, k)), pl.BlockSpec((bk, bn), lambda i, j, k: (k, j))],
               [_sds((L, N), F32)], [pl.BlockSpec((bm, bn), lambda i, j, k: (i, j))], NN, nk,
               acc_shape=(bm, bn))


def _concat_cols(name, parts):
    L = parts[0].shape[0]
    T = _row_tile(L)
    widths = [p.shape[1] for p in parts]

    def body(*refs):
        at = 0
        for ref, wd in zip(refs[:-1], widths):
            refs[-1][:, at:at + wd] = ref[...]
            at += wd

    return _call(
        body, list(parts), name=name, grid=(L // T,),
        in_specs=[pl.BlockSpec((T, wd), lambda i: (i, 0)) for wd in widths],
        out_specs=pl.BlockSpec((T, sum(widths)), lambda i: (i, 0)),
        out_shape=_sds((L, sum(widths)), parts[0].dtype), compiler_params=_params(1))


def _up_epilogue(val, extras, outs):
    outs[0][...] = val.astype(BF16)
    r = jnp.maximum(val, 0.0)
    outs[1][...] = (r * r).astype(BF16)


def _mlp_down_bwd(df, w_down, a_up):
    L, D = df.shape
    F = w_down.shape[0]
    bm = _pick(L, 1088)
    bn = _pick(F, 1024)

    def epilogue(val, extras, outs):
        outs[0][...] = (val * (2.0 * jnp.maximum(extras[0][...].astype(F32), 0.0))).astype(BF16)

    blk = pl.BlockSpec((bm, bn), lambda i, j: (i, j))
    return _mm("mlp_down_bwd", (L // bm, F // bn), [df, w_down, a_up],
               [pl.BlockSpec((bm, D), lambda i, j: (i, 0)), pl.BlockSpec((bn, D), lambda i, j: (j, 0)), blk],
               [_sds((L, F), BF16)], [blk], NT, 1, epilogue)


ANY = pl.BlockSpec(memory_space=pl.ANY)
HBM = pl.BlockSpec(memory_space=pltpu.HBM)
SEM = pl.BlockSpec(memory_space=pltpu.SEMAPHORE)
EFFECT = pltpu.SideEffectType.DATAFLOW_SIDE_EFFECTING
N_CHIPS = 4


def _place():
    x, y, c = lax.axis_index("x"), lax.axis_index("y"), lax.axis_index("c")
    return x, y, c


def _chip_at(x, y, k):
    px = 1 - x if k & 2 else x
    py = 1 - y if k & 1 else y
    return px, py


def _cast_into_slab(w2d, chip, dtype):
    R, C = w2d.shape
    T = _elem_tile(R, C)

    def body(p_ref, w_ref, o_ref):
        o_ref[...] = w_ref[...].astype(dtype)

    return _call(
        body, (w2d,), name="cast_into_slab", grid=(R // T,), scalars=jnp.reshape(chip, (1,)).astype(jnp.int32),
        in_specs=[pl.BlockSpec((T, C), lambda i, p: (i, 0))],
        out_specs=pl.BlockSpec((None, T, C), lambda i, p: (p[0], i, 0)),
        out_shape=_sds((N_CHIPS, R, C), dtype), compiler_params=_params(1))


TOKEN = jax.ShapeDtypeStruct((8, LANES), F32)


class _Sems:
    def __init__(self, items, shape):
        self.items, self.shape = list(items), tuple(shape)

    def pair(self, idx):
        flat = 0
        for i, n in zip(idx, self.shape):
            flat = flat * n + i
        half = len(self.items) // 2
        return self.items[flat], self.items[half + flat]


def _sem_count(shape):
    n = 1
    for s in shape:
        n *= s
    return n


def _remote(src, dst, sems, idx, device):
    send, recv = sems.pair(idx)
    return pltpu.make_async_remote_copy(src_ref=src, dst_ref=dst, send_sem=send, recv_sem=recv,
                                        device_id=device, device_id_type=MESH)


def _thru(arrays):
    return ([pltpu.with_memory_space_constraint(a, pltpu.HBM) for a in arrays],
            [pltpu.HBM(a.shape, a.dtype) for a in arrays])


def _comm_start(name, arrays, sem_shape, plan, follows=False):
    na, ns = len(arrays), 2 * _sem_count(sem_shape)

    def body(*refs):
        sems, token = _Sems(refs[na:na + ns], sem_shape), refs[-1]
        for src, dst, idx, device in plan(refs[:na])[0]:
            _remote(src, dst, sems, idx, device).start()
        token[...] = jnp.zeros(token.shape, F32)

    ins, outs = _thru(arrays)
    res = _call(
        body, ins, name=name, in_specs=[HBM] * na, mark=-1, follows=follows, made_from=arrays,
        out_specs=[SEM] * ns + [HBM] * na + [pl.BlockSpec(memory_space=pltpu.VMEM)],
        out_shape=[pltpu.SemaphoreType.DMA(())] * ns + outs + [TOKEN],
        input_output_aliases={a: ns + a for a in range(na)},
        compiler_params=pltpu.CompilerParams(has_side_effects=EFFECT))
    return _Sems(res[:ns], sem_shape), list(res[ns:ns + na])


def _wait_plans(refs, sem_refs, waits):
    x, y, c = _place()
    at = 0
    for sems, plan in waits:
        here = _Sems(sem_refs[at:at + len(sems.items)], sems.shape)
        at += len(sems.items)
        _, mine, arrivals = plan(refs)
        for dst, idx in arrivals:
            _remote(dst, dst, here, idx, (x, y, c)).wait_recv()
        for src, idx in mine:
            _remote(src, src, here, idx, (x, y, c)).wait_send()


def _comm_wait(name, arrays, waits):
    na = len(arrays)
    sem_items = [s for sems, _ in waits for s in sems.items]
    ns = len(sem_items)

    def body(*refs):
        _wait_plans(refs[:na], refs[na:na + ns], waits)
        refs[-1][...] = jnp.zeros(refs[-1].shape, F32)

    ins, outs = _thru(arrays)
    res = _call(
        body, ins + sem_items, name=name, in_specs=[HBM] * na + [SEM] * ns, mark=-1,
        out_specs=[HBM] * na + [pl.BlockSpec(memory_space=pltpu.VMEM)], out_shape=outs + [TOKEN],
        input_output_aliases={a: a for a in range(na)},
        compiler_params=pltpu.CompilerParams(has_side_effects=EFFECT))
    return list(res[:na])


def _comm_relay(name, arrays, sems, plan, sem_shape, next_plan):
    na, ns_in, ns_out = len(arrays), len(sems.items), 2 * _sem_count(sem_shape)

    def body(*refs):
        bufs = refs[:na]
        sems_in = _Sems(refs[na:na + ns_in], sems.shape)
        sems_out = _Sems(refs[na + ns_in:na + ns_in + ns_out], sem_shape)
        x, y, c = _place()
        _, mine, arrivals = plan(bufs)
        onward = next_plan(bufs)[0]
        for dst, idx in arrivals:
            _remote(dst, dst, sems_in, idx, (x, y, c)).wait_recv()
            for src, to, idx2, device, after_idx in onward:
                if after_idx == idx:
                    _remote(src, to, sems_out, idx2, device).start()
        for src, idx in mine:
            _remote(src, src, sems_in, idx, (x, y, c)).wait_send()
        refs[-1][...] = jnp.zeros(refs[-1].shape, F32)

    ins, outs = _thru(arrays)
    res = _call(
        body, ins + sems.items, name=name, in_specs=[HBM] * na + [SEM] * ns_in, mark=-1,
        out_specs=[SEM] * ns_out + [HBM] * na + [pl.BlockSpec(memory_space=pltpu.VMEM)],
        out_shape=[pltpu.SemaphoreType.DMA(())] * ns_out + outs + [TOKEN],
        input_output_aliases={a: ns_out + a for a in range(na)},
        compiler_params=pltpu.CompilerParams(has_side_effects=EFFECT))
    return _Sems(res[:ns_out], sem_shape), list(res[ns_out:ns_out + na])


def _half(ref, q, which):
    h = ref.shape[1] // 2
    return ref.at[q, pl.ds(which * h, h)]


def _quarter(ref, q, half, which):
    h = ref.shape[1] // 2
    return ref.at[q, pl.ds(half * h + which * (h // 2), h // 2)]


def _gather_plan(n_halved):
    def plan(refs):
        x, y, c = _place()
        p = 2 * x + y
        starts, mine, arrivals = [], [], []
        for n, ref in enumerate(refs):
            for k in range(1, N_CHIPS if n >= n_halved else 3):
                px, py = _chip_at(x, y, k)
                q = 2 * px + py
                out = _half(ref, p, c) if n < n_halved else ref.at[p]
                inc = _half(ref, q, c) if n < n_halved else ref.at[q]
                starts.append((out, out, (n, k - 1), (px, py, c)))
                mine.append((out, (n, k - 1)))
                arrivals.append((inc, (n, k - 1)))
        return starts, mine, arrivals
    return plan


def _spread_plan(n_halved, part):
    def plan(refs):
        x, y, c = _place()
        p = 2 * x + y
        starts, mine, arrivals = [], [], []
        for n in range(n_halved):
            for k in (1, 2):
                px, py = _chip_at(x, y, k)
                q = 2 * px + py
                tx, ty = _chip_at(x, y, 3 - k)
                dx, dy = _chip_at(x, y, 3)
                piece = _quarter(refs[n], q, c, 2 - k)
                landed = _half(refs[n], q, c)
                starts.append((piece, piece, (0, n, k - 1), (tx, ty, c), (n, k - 1)))
                starts.append((landed, landed, (1, n, k - 1), (x, y, 1 - c), (n, k - 1)))
                if part != 1:
                    mine.append((piece, (0, n, k - 1)))
                    arrivals.append((_quarter(refs[n], 2 * dx + dy, c, k - 1), (0, n, 2 - k)))
                if part != 0:
                    mine.append((landed, (1, n, k - 1)))
                    arrivals.append((_half(refs[n], q, 1 - c), (1, n, k - 1)))
        return starts, mine, arrivals
    return plan


def _last_hand_on_plan(n_halved):
    def plan(refs):
        x, y, c = _place()
        dx, dy = _chip_at(x, y, 3)
        d = 2 * dx + dy
        starts, mine, arrivals = [], [], []
        for n in range(n_halved):
            for k in (1, 2):
                piece = _quarter(refs[n], d, c, k - 1)
                starts.append((piece, piece, (n, k - 1), (x, y, 1 - c), (0, n, 2 - k)))
                mine.append((piece, (n, k - 1)))
                arrivals.append((_quarter(refs[n], d, 1 - c, k - 1), (n, k - 1)))
        return starts, mine, arrivals
    return plan


def _swap_plan(n):
    def plan(refs):
        x, y, c = _place()
        starts, mine, arrivals = [], [], []
        for a in range(n):
            h = refs[a].shape[1] // 2
            src = refs[a].at[:, pl.ds((1 - c) * h, h)]
            starts.append((src, refs[n + a], (a,), (x, y, 1 - c)))
            mine.append((src, (a,)))
            arrivals.append((refs[n + a], (a,)))
        return starts, mine, arrivals
    return plan


def _scatter_plan(n):
    def plan(refs):
        x, y, c = _place()
        starts, mine, arrivals = [], [], []
        for a in range(n):
            for k in range(1, N_CHIPS):
                px, py = _chip_at(x, y, k)
                src = refs[a].at[2 * px + py]
                starts.append((src, refs[n + a].at[k - 1], (a, k - 1), (px, py, c)))
                mine.append((src, (a, k - 1)))
                arrivals.append((refs[n + a].at[k - 1], (a, k - 1)))
        return starts, mine, arrivals
    return plan


def _share_plan(n):
    def plan(refs):
        x, y, c = _place()
        starts, mine, arrivals = [], [], []
        for a in range(n):
            h = refs[a].shape[0] // 2
            own = refs[a].at[pl.ds(c * h, h)]
            starts.append((own, own, (a,), (x, y, 1 - c)))
            mine.append((own, (a,)))
            arrivals.append((refs[a].at[pl.ds((1 - c) * h, h)], (a,)))
        return starts, mine, arrivals
    return plan


N_DEVICES = 8


def _packs_plan(refs):
    buf = refs[0]
    x, y, c = _place()
    me = 4 * x + 2 * y + c
    starts, mine, arrivals = [], [], []
    for r in range(1, N_DEVICES):
        peer = (1 - x if r & 4 else x, 1 - y if r & 2 else y, 1 - c if r & 1 else c)
        starts.append((buf.at[me], buf.at[me], (r - 1,), peer))
        mine.append((buf.at[me], (r - 1,)))
        arrivals.append((buf.at[4 * peer[0] + 2 * peer[1] + peer[2]], (r - 1,)))
    return starts, mine, arrivals


class _Reduction:
    def __init__(self, tag, slabs, c_idx, chip):
        self.tag, self.n, self.c_idx, self.chip = tag, len(slabs), c_idx, chip
        lands = [lax.empty((g.shape[0], g.shape[1] // 2, g.shape[2]), g.dtype) for g in slabs]
        self.sems = _comm_start("swap_start_" + tag, list(slabs) + lands, (self.n,), _swap_plan(self.n))

    def partial(self):
        n = self.n
        sems, bufs = self.sems
        bufs = _comm_wait("swap_wait_" + self.tag, bufs, [(sems, _swap_plan(n))])
        both = [_chip_partial(g, r, self.c_idx, self.chip) for g, r in zip(bufs[:n], bufs[n:])]
        self.own = [o for _, o in both]
        parts = [p for p, _ in both]
        lands = [lax.empty((N_CHIPS - 1,) + p.shape[1:], p.dtype) for p in parts]
        self.sems = _comm_start("scatter_start_" + self.tag, parts + lands, (n, N_CHIPS - 1), _scatter_plan(n))

    def total(self):
        n = self.n
        sems, bufs = self.sems
        bufs = _comm_wait("scatter_wait_" + self.tag, bufs, [(sems, _scatter_plan(n))])
        fulls = [_sum_partials(o, r, self.c_idx) for o, r in zip(self.own, bufs[n:])]
        self.sems = _comm_start("share_start_" + self.tag, fulls, (n,), _share_plan(n))

    def finish(self):
        sems, bufs = self.sems
        return _comm_wait("share_wait_" + self.tag, bufs, [(sems, _share_plan(self.n))])


def _elem_tile(rows, cols):
    return _pick(rows, max(8, (1 << 19) // cols // 8 * 8))


def _chip_partial(grad, recv, c_idx, p_idx):
    Q, R, C = grad.shape
    h = R // 2
    T = _elem_tile(h, C)
    nt = h // T

    def body(sc_ref, g_ref, r_ref, sb_ref, own_ref):
        q = pl.program_id(1)
        s = g_ref[...] + r_ref[...]
        sb_ref[...] = s.astype(BF16)

        @pl.when(q == sc_ref[1])
        def _():
            own_ref[...] = s

    return _call(
        body, (grad, recv), name="chip_partial", grid=(nt, Q),
        scalars=jnp.stack([c_idx, p_idx]).astype(jnp.int32),
        in_specs=[pl.BlockSpec((None, T, C), lambda t, q, sc: (q, sc[0] * nt + t, 0)),
                  pl.BlockSpec((None, T, C), lambda t, q, sc: (q, t, 0))],
        out_specs=[pl.BlockSpec((None, T, C), lambda t, q, sc: (q, t, 0)),
                   pl.BlockSpec((T, C), lambda t, q, sc: (t, 0))],
        out_shape=[_sds((Q, h, C), BF16), _sds((h, C), F32)], compiler_params=_params(2))


def _sum_partials(own, parts, c_idx):
    h, C = own.shape
    T = _elem_tile(h, C)
    nt = h // T

    def body(c_ref, o_ref, p_ref, t_ref):
        t = o_ref[...]
        for k in range(N_CHIPS - 1):
            t = t + p_ref[k].astype(F32)
        t_ref[...] = t

    return _call(
        body, (own, parts), name="sum_partials", grid=(nt,), scalars=jnp.reshape(c_idx, (1,)).astype(jnp.int32),
        in_specs=[pl.BlockSpec((T, C), lambda i, c: (i, 0)),
                  pl.BlockSpec((N_CHIPS - 1, T, C), lambda i, c: (0, i, 0))],
        out_specs=pl.BlockSpec((T, C), lambda i, c: (c[0] * nt + i, 0)),
        out_shape=_sds((2 * h, C), F32), compiler_params=_params(1))


def _pack_rows(name, parts, slot=None, n_slots=1):
    width = parts[0].shape[1]
    offsets, at = [], 0
    for p in parts:
        offsets.append(at)
        at += p.shape[0]
    total = -(-at // 8) * 8
    lead = 0 if slot is None else 1

    def body(*refs):
        out = refs[-1]
        out[...] = jnp.zeros(out.shape, F32)
        for ref, o in zip(refs[lead:-1], offsets):
            out[o:o + ref.shape[0], :] = ref[...]

    if slot is None:
        whole = pl.BlockSpec(memory_space=pltpu.VMEM)
        return _call(body, list(parts), name=name, in_specs=[whole] * len(parts), out_specs=whole,
                     out_shape=_sds((total, width), F32))
    return _call(body, list(parts), name=name, grid=(1,), scalars=jnp.reshape(slot, (1,)).astype(jnp.int32),
                 in_specs=[pl.BlockSpec(p.shape, lambda i, s: (0, 0)) for p in parts],
                 out_specs=pl.BlockSpec((None, total, width), lambda i, s: (s[0], 0, 0)),
                 out_shape=_sds((n_slots, total, width), F32))


def _sum_packs(packs):
    n, R, C = packs.shape

    def body(p_ref, o_ref):
        t = p_ref[0]
        for k in range(1, n):
            t = t + p_ref[k]
        o_ref[...] = t

    return _call(
        body, (packs,), name="sum_packs", grid=(1,), in_specs=[pl.BlockSpec((n, R, C), lambda i: (0, 0, 0))],
        out_specs=pl.BlockSpec((R, C), lambda i: (0, 0)), out_shape=_sds((R, C), F32), compiler_params=_params(1))


def _adamw(w, g, m, v):
    R, C = w.shape
    T = _elem_tile(R, C)

    def body(w_ref, g_ref, m_ref, v_ref, d_ref, m2_ref, v2_ref):
        g_ = g_ref[...]
        m2 = ADAM_B1 * m_ref[...] + (1.0 - ADAM_B1) * g_
        v2 = ADAM_B2 * v_ref[...] + (1.0 - ADAM_B2) * (g_ * g_)
        m_hat = m2 / (1.0 - ADAM_B1 ** ADAM_STEP)
        v_hat = v2 / (1.0 - ADAM_B2 ** ADAM_STEP)
        d_ref[...] = -ADAM_LR * (m_hat / (jnp.sqrt(v_hat) + ADAM_EPS) + ADAM_WD * w_ref[...])
        m2_ref[...] = m2
        v2_ref[...] = v2

    blk = pl.BlockSpec((T, C), lambda i: (i, 0))
    return _call(
        body, (w, g, m, v), name="adamw", grid=(R // T,), in_specs=[blk] * 4, out_specs=[blk] * 3,
        out_shape=[_sds((R, C), F32)] * 3, compiler_params=_params(1))


def _adamw_rows(w, g, m, v, row0, prev=None):
    R, C = w.shape
    T = _elem_tile(g.shape[0], C)
    off = row0 // T

    def body(w_ref, g_ref, m_ref, v_ref, *rest):
        d_ref, m2_ref, v2_ref, g2_ref = rest[-4:]
        g_ = g_ref[...]
        m2 = ADAM_B1 * m_ref[...] + (1.0 - ADAM_B1) * g_
        v2 = ADAM_B2 * v_ref[...] + (1.0 - ADAM_B2) * (g_ * g_)
        m_hat = m2 / (1.0 - ADAM_B1 ** ADAM_STEP)
        v_hat = v2 / (1.0 - ADAM_B2 ** ADAM_STEP)
        d_ref[...] = -ADAM_LR * (m_hat / (jnp.sqrt(v_hat) + ADAM_EPS) + ADAM_WD * w_ref[...])
        m2_ref[...] = m2
        v2_ref[...] = v2
        g2_ref[...] = g_

    here = pl.BlockSpec((T, C), lambda i: (off + i, 0))
    piece = pl.BlockSpec((T, C), lambda i: (i, 0))
    done = tuple(prev or ())
    return _call(
        body, (w, g, m, v) + done, name="adamw_rows", grid=(g.shape[0] // T,), follows=prev is None,
        in_specs=[here, piece, here, here] + [ANY] * len(done), out_specs=[here] * 4,
        out_shape=[_sds((R, C), F32)] * 4, input_output_aliases={4 + j: j for j in range(len(done))},
        compiler_params=_params(1))


SC_TILES = 32
SC_ROWS = 8
SC_LANES = 16


def _adamw_sc(w, g, m, v):
    R, C = w.shape
    per_tile = R // SC_TILES
    assert per_tile % SC_ROWS == 0 and C % SC_LANES == 0

    def body(w_hbm, g_hbm, m_hbm, v_hbm, d_out, m_out, v_out, g_out, wb, gb, mb, vb):
        tile = lax.axis_index("sc_tile") * 2 + lax.axis_index("sc_core")

        @pl.loop(0, per_tile // SC_ROWS)
        def _(chunk):
            here = pl.ds(tile * per_tile + chunk * SC_ROWS, SC_ROWS)
            pltpu.sync_copy(w_hbm.at[here], wb)
            pltpu.sync_copy(g_hbm.at[here], gb)
            pltpu.sync_copy(m_hbm.at[here], mb)
            pltpu.sync_copy(v_hbm.at[here], vb)
            for r in range(SC_ROWS):
                @pl.loop(0, C, step=SC_LANES)
                def _(j):
                    at = pl.ds(j, SC_LANES)
                    g_ = gb[r, at]
                    m2 = ADAM_B1 * mb[r, at] + (1.0 - ADAM_B1) * g_
                    v2 = ADAM_B2 * vb[r, at] + (1.0 - ADAM_B2) * (g_ * g_)
                    m_hat = m2 / (1.0 - ADAM_B1 ** ADAM_STEP)
                    v_hat = v2 / (1.0 - ADAM_B2 ** ADAM_STEP)
                    wb[r, at] = -ADAM_LR * (m_hat / (jnp.sqrt(v_hat) + ADAM_EPS) + ADAM_WD * wb[r, at])
                    mb[r, at] = m2
                    vb[r, at] = v2
            pltpu.sync_copy(wb, d_out.at[here])
            pltpu.sync_copy(mb, m_out.at[here])
            pltpu.sync_copy(vb, v_out.at[here])
            pltpu.sync_copy(gb, g_out.at[here])

    buf = pltpu.VMEM((SC_ROWS, C), F32)
    return pl.kernel(
        body, name="adamw_sc", out_type=(_sds((R, C), F32),) * 4,
        mesh=plsc.VectorSubcoreMesh(core_axis_name="sc_core", subcore_axis_name="sc_tile"),
        scratch_types=[buf, buf, buf, buf])(w, g, m, v)


BIG = ("w_in", "w_pool_out", "w_conv_out", "w_o", "w_up", "w_down", "w_pool_grp")
VECTORS = ("g_pre_mix", "pool_scale", "b_dw", "conv_ln_g", "conv_ln_b", "g_post_mix", "g_pre_mlp", "g_post_mlp")
WEIGHTS = ("meta", "g_pre_mix", "w_in", "w_pool_grp", "pool_scale", "w_pool_out", "w_dw", "b_dw", "conv_ln_g",
           "conv_ln_b", "w_conv_out", "w_o", "g_post_mix", "g_pre_mlp", "w_up", "w_down", "g_post_mlp")


def _as_rows(a, width):
    r, cols = a.shape
    return a.reshape(r * (cols // width), width)


def _step(w, m, v, x, tgt):
    S, D = x.shape
    P = D // 2
    xi, yi, ci = _place()
    chip = 2 * xi + yi
    _CHAIN["after"] = []

    C = D // 2
    G = POOL_GROUPS
    GD = P // G
    GS = GD // N_CHIPS
    Q = N_CHIPS
    vecs = {k: w[k] for k in VECTORS}
    shard2d = {k: w[k].reshape(-1, w[k].shape[-1]) for k in BIG}
    grads, delta, new_m, new_v = {}, {}, {}, {}

    def update(names, reduced, on_sparsecore=False):
        for k, g in zip(names, reduced):
            args = (shard2d[k], g, m[k].reshape(shard2d[k].shape), v[k].reshape(shard2d[k].shape))
            delta[k], new_m[k], new_v[k], grads[k] = _adamw_sc(*args) if on_sparsecore else _adamw_rows(*args, 0)

    groups = dict(a=(("w_in", "w_pool_grp"), ("w_dw", "meta")), b=(("w_pool_out", "w_conv_out", "w_o"), ()),
                  c=(("w_up",), ()), d=(("w_down",), ()))
    flying = {}

    def start(tag, follows):
        halved, whole = groups[tag]
        flying[tag] = _comm_start("gather_start_" + tag, flying[tag], (len(halved + whole), N_CHIPS - 1),
                                  _gather_plan(len(halved)), follows=follows)

    def spread(tag):
        nh = len(groups[tag][0])
        sems, bufs = flying[tag]
        flying[tag] = _comm_relay("gather_relay_" + tag, bufs, sems, _gather_plan(nh), (2, nh, 2),
                                  _spread_plan(nh, 0))

    def landed(tag):
        halved, whole = groups[tag]
        nh = len(halved)
        sems, bufs = flying.pop(tag)
        last, tree = _comm_relay("gather_relay2_" + tag, bufs[:nh], sems, _spread_plan(nh, 0), (nh, 2),
                                 _last_hand_on_plan(nh))
        done = _comm_wait("gather_wait_" + tag, tree,
                          [(sems, _spread_plan(nh, 1)), (last, _last_hand_on_plan(nh))])
        return dict(zip(halved + whole, done + bufs[nh:]))

    for tag, (halved, whole) in groups.items():
        flying[tag] = [_cast_into_slab(shard2d[k], chip, BF16) for k in halved]
        flying[tag] += [_cast_into_slab(w[k], chip, F32) for k in whole]
        if tag == "a":
            start(tag, False)

    small_names = [k for k in WEIGHTS if k not in BIG]

    def pack_small(tree):
        parts = []
        for k in small_names:
            flat = tree[k].reshape(-1)
            parts.append(jnp.pad(flat, (0, -flat.shape[0] % P)).reshape(-1, P))
        return _pack_rows("pack_small", parts)

    small_w, small_m, small_v = pack_small(w), pack_small(m), pack_small(v)
    h0, u1 = _pre_norm_tokens(x, vecs["g_pre_mix"])
    spread("a")
    start("b", True)

    got = landed("a")
    win_g = got["w_in"]
    w_grp = got["w_pool_grp"].reshape(N_CHIPS, G, GS, GD).transpose(1, 0, 2, 3).reshape(G, GD, GD)
    w_dw = got["w_dw"].transpose(1, 0, 2).reshape(CONV_TAPS, P)
    meta = got["meta"].transpose(1, 0, 2).reshape(N_META, D)
    h0, u1 = _pre_norm_meta(meta, vecs["g_pre_mix"], h0, u1)
    spread("b")
    start("c", True)
    start("d", True)
    proj = _mm_act_colw("proj", u1, win_g, 256)
    d, ya_pre = _pool_fwd(proj, w_grp, vecs["pool_scale"])
    s, c = _conv_fwd(proj, C, w_dw, vecs["b_dw"], vecs["conv_ln_g"], vecs["conv_ln_b"])
    spread("c")
    got = landed("b")
    wpo_g, wco_g, w_o = got["w_pool_out"], got["w_conv_out"], got["w_o"].reshape(D, D)
    mix, ya, yb = _mix_fwd(ya_pre, s, wpo_g, wco_g, proj, D)
    o = _mm_act_roww("attn_out", mix, w_o, 1088, 1024, 2048)
    spread("d")
    h1, u2 = _mid_norm(o, h0, vecs["g_post_mix"], vecs["g_pre_mlp"])
    wup_g = landed("c")["w_up"]
    a_up, fact = _mm_act_colw("mlp_up", u2, wup_g, 512, _up_epilogue, (BF16, BF16))
    w_down = landed("d")["w_down"].reshape(-1, D)
    f = _mm_act_roww("mlp_down", fact, w_down, 1088, 1024, 2048)
    dy, df, dg_post_mlp, loss = _loss_head(f, h1, tgt, vecs["g_post_mlp"])

    g_w_down = _mm_wgrad("dw_down", fact, df, 1024, 1024).reshape(N_CHIPS, -1, D)
    _CHAIN["after"] = [g_w_down]
    red1 = _Reduction("1", [g_w_down], ci, chip)
    da_up = _mlp_down_bwd(df, w_down, a_up)
    red1.partial()
    g_w_up = _mm_wgrad_colw("dw_up", u2, da_up, Q, 1024, 1024)
    red2 = _Reduction("2", [g_w_up], ci, chip)
    du2 = _mm_grad_colw_t("du2", da_up, wup_g, 1088, 1024)
    red2.partial()
    red1.total()
    dh1, do, dg_pre_mlp, dg_post_mix = _mid_norm_bwd(dy, du2, h1, o, vecs["g_pre_mlp"], vecs["g_post_mix"])
    update(("w_down",), red1.finish(), on_sparsecore=True)
    g_w_o = _mm_wgrad("dw_o", mix, do, 1024, 1024)
    dya, dyb, dga, dgb = _mix_bwd(do, w_o, proj, ya, yb)
    g_wpo = _mm_wgrad_colw("dw_pool_out", ya_pre, dya, Q, 1024, 512)
    g_wco = _mm_wgrad_colw("dw_conv_out", s, dyb, Q, 1024, 512)
    red3 = _Reduction("3", [g_w_o.reshape(N_CHIPS, D // N_CHIPS, D), g_wpo, g_wco], ci, chip)
    dya_pre = _mm_grad_colw_t("dya_pre", dya, wpo_g, 1088, 1024)
    ds = _mm_grad_colw_t("ds", dyb, wco_g, 1088, 1024)
    red3.partial()
    red2.total()
    dz, g_w_grp, dscale = _pool_bwd(dya_pre, d, w_grp, vecs["pool_scale"])
    update(("w_up",), red2.finish(), on_sparsecore=True)
    dc, dln_g, dln_b, db_dw = _conv_ln_bwd(ds, c, vecs["conv_ln_g"], vecs["conv_ln_b"])
    dv, dgc, g_w_dw = _conv_bwd(dc, proj, C, w_dw)
    dproj = _concat_cols("dproj", [dz, dv, dgc, dga, dgb])
    half_k = D // 2
    g_w_grp = g_w_grp.reshape(G, N_CHIPS, GS, GD).transpose(1, 0, 2, 3).reshape(N_CHIPS, G * GS, GD)
    g_in_a = _mm_wgrad_colw("dw_in_a", u1, dproj, Q, 512, 1792, rows=(0, half_k))
    red4a = _Reduction("4a", [g_in_a, g_w_grp], ci, chip)
    g_in_b = _mm_wgrad_colw("dw_in_b", u1, dproj, Q, 512, 1792, rows=(half_k, half_k))
    red4a.partial()
    red4b = _Reduction("4b", [g_in_b], ci, chip)
    du1 = _mm_grad_colw_t("du1", dproj, win_g, 1088, 1024)
    red4b.partial()
    red3.total()
    grad_x, dmeta, dg_pre_mix = _pre_norm_bwd(dh1, du1, h0, vecs["g_pre_mix"])

    g_vec = dict(g_pre_mix=dg_pre_mix, pool_scale=dscale, b_dw=db_dw, conv_ln_g=dln_g, conv_ln_b=dln_b,
                 g_post_mix=dg_post_mix, g_pre_mlp=dg_pre_mlp, g_post_mlp=dg_post_mlp)
    rows = [g_w_dw, _as_rows(dmeta, P)] + [_as_rows(g_vec[k], P) for k in VECTORS]
    rows.append(jnp.broadcast_to(loss[:, :1], (1, P)))
    packs = _comm_start("packs_start", [_pack_rows("pack_grads", rows, 2 * chip + ci, N_DEVICES)],
                        (N_DEVICES - 1,), _packs_plan)
    red4a.total()
    update(("w_o", "w_pool_out", "w_conv_out"), red3.finish())
    red_in_a, red_grp = red4a.finish()
    update(("w_pool_grp",), [red_grp])
    w_in_rows = (shard2d["w_in"], m["w_in"].reshape(shard2d["w_in"].shape), v["w_in"].reshape(shard2d["w_in"].shape))
    first_rows = _adamw_rows(w_in_rows[0], red_in_a, w_in_rows[1], w_in_rows[2], 0)
    total = _sum_packs(_comm_wait("packs_wait", packs[1], [(packs[0], _packs_plan)])[0])
    at = 0
    taps_pad = g_w_dw.shape[0]
    g_dw_full = total[at:at + CONV_TAPS]
    at += taps_pad
    g_meta_full = total[at:at + 2 * N_META].reshape(N_META, D)
    at += 2 * N_META
    for k in VECTORS:
        n = w[k].shape[-1] // P
        grads[k] = total[at:at + n].reshape(1, n * P)
        at += n
    loss_total = total[at, 0]
    grads["w_dw"] = lax.dynamic_slice_in_dim(g_dw_full, chip * (P // N_CHIPS), P // N_CHIPS, axis=1)
    grads["meta"] = lax.dynamic_slice_in_dim(g_meta_full, chip * (D // N_CHIPS), D // N_CHIPS, axis=1)

    sd, sm, sv = _adamw(small_w, pack_small(grads), small_m, small_v)
    at = 0
    for k in small_names:
        a = w[k].reshape(-1, w[k].shape[-1])
        n = -(-a.size // P)
        for tree, packed in ((delta, sd), (new_m, sm), (new_v, sv)):
            tree[k] = packed[at:at + n].reshape(-1)[:a.size].reshape(a.shape)
        at += n

    red4b.total()
    delta["w_in"], new_m["w_in"], new_v["w_in"], grads["w_in"] = _adamw_rows(
        w_in_rows[0], red4b.finish()[0], w_in_rows[1], w_in_rows[2], half_k, prev=first_rows)
    return loss_total, grad_x, grads, delta, new_m, new_v


def kernel(x, meta, g_pre_mix, w_in, w_pool_grp, pool_scale, w_pool_out, w_dw, b_dw, conv_ln_g, conv_ln_b, w_conv_out, w_o, g_post_mix, g_pre_mlp, w_up, w_down, g_post_mlp, loss_target, m_meta, m_g_pre_mix, m_w_in, m_w_pool_grp, m_pool_scale, m_w_pool_out, m_w_dw, m_b_dw, m_conv_ln_g, m_conv_ln_b, m_w_conv_out, m_w_o, m_g_post_mix, m_g_pre_mlp, m_w_up, m_w_down, m_g_post_mlp, v_meta, v_g_pre_mix, v_w_in, v_w_pool_grp, v_pool_scale, v_w_pool_out, v_w_dw, v_b_dw, v_conv_ln_g, v_conv_ln_b, v_w_conv_out, v_w_o, v_g_post_mix, v_g_pre_mlp, v_w_up, v_w_down, v_g_post_mlp):
    args = dict(locals())
    shapes = {k: args[k].shape for k in WEIGHTS}
    w = {k: args[k] for k in WEIGHTS}
    m = {k: args["m_" + k] for k in WEIGHTS}
    v = {k: args["v_" + k] for k in WEIGHTS}
    for tree in (w, m, v):
        tree["w_dw"] = tree["w_dw"].reshape(tree["w_dw"].shape[-2:])
    loss, grad_x, grads, delta, new_m, new_v = _step(w, m, v, x[0], loss_target[0])
    out = [loss, grad_x[None]]
    for tree in (grads, delta, new_m, new_v):
        out += [tree[k].reshape(shapes[k]) for k in WEIGHTS]
    return tuple(out)
```

```python
import math

import jax
import jax.numpy as jnp
from jax import lax
from jax.experimental import pallas as pl
from jax.experimental.pallas import tpu as pltpu
from jax.experimental.pallas import tpu_sc as plsc

F32 = jnp.float32
BF16 = jnp.bfloat16

N_META = 16
PAD_ROWS = 112
TOKEN_ROW0 = PAD_ROWS + N_META
POOL_GROUPS = 4
CONV_TAPS = 31
HALO = 32
CONV_ROWS = 128
LANES = 128
RMS_EPS = 1e-6
LN_EPS = 1e-5
ADAM_LR = 0.001
ADAM_B1 = 0.9
ADAM_B2 = 0.999
ADAM_EPS = 1e-08
ADAM_WD = 0.01
ADAM_STEP = 10
VMEM_LIMIT_MB = 56

MESH = pl.DeviceIdType.MESH
NN = (((1,), (0,)), ((), ()))
NT = (((1,), (1,)), ((), ()))
TN = (((0,), (0,)), ((), ()))


def _pick(n, pref):
    if n <= pref:
        return n
    if n % pref == 0:
        return pref
    for step in (LANES, 8, 1):
        t = (pref // step) * step
        while t >= step:
            if n % t == 0:
                return t
            t -= step
    return n


def _params(n_axes, vmem_mb=VMEM_LIMIT_MB):
    return pltpu.CompilerParams(dimension_semantics=("arbitrary",) * n_axes,
                                vmem_limit_bytes=vmem_mb << 20)


def _sigmoid(x):
    return jax.nn.sigmoid(x)


_CHAIN = {"after": []}


def _call(body, args, *, in_specs, out_specs, out_shape, grid=(), scalars=None, mark=0, follows=True, made_from=None,
          **kw):
    pending = _CHAIN["after"]
    after = pending if follows else []
    n = len(args)
    lead = 0 if scalars is None else 1
    specs = list(in_specs)
    operands = list(args)
    fn = body
    if after:
        def fn(*refs):
            body(*refs[:lead + n], *refs[lead + n + len(after):])
        specs += [pl.BlockSpec(memory_space=pl.ANY)] * len(after)
        operands += after
    if scalars is None:
        if grid:
            kw["grid"] = grid
        res = pl.pallas_call(fn, in_specs=specs, out_specs=out_specs, out_shape=out_shape, **kw)(*operands)
    else:
        grid_spec = pltpu.PrefetchScalarGridSpec(num_scalar_prefetch=1, grid=grid, in_specs=specs, out_specs=out_specs)
        res = pl.pallas_call(fn, grid_spec=grid_spec, out_shape=out_shape, **kw)(scalars, *operands)
    outs = res if isinstance(res, (list, tuple)) else [res]
    consumed = args if made_from is None else made_from
    kept = [] if follows else [m for m in pending if not any(m is a for a in consumed)]
    _CHAIN["after"] = kept + [outs[mark]]
    return res


def _store(val, extras, outs):
    outs[0][...] = val.astype(outs[0].dtype)


def _mm(name, grid, arrays, in_specs, out_shapes, out_specs, dims, nk, epilogue=_store, acc_shape=None):
    n_in, n_out = len(arrays), len(out_shapes)

    def body(*refs):
        extras = refs[2:n_in]
        outs = refs[n_in:n_in + n_out]
        part = lax.dot_general(refs[0][...], refs[1][...], dims, preferred_element_type=F32)
        if nk == 1:
            epilogue(part, extras, outs)
        else:
            acc = refs[n_in + n_out]
            k = pl.program_id(len(grid) - 1)

            @pl.when(k == 0)
            def _():
                acc[...] = part

            @pl.when(k > 0)
            def _():
                acc[...] += part

            @pl.when(k == nk - 1)
            def _():
                epilogue(acc[...], extras, outs)

    scratch = [pltpu.VMEM(acc_shape, F32)] if nk > 1 else []
    single = n_out == 1
    return _call(
        body, arrays, name=name, grid=grid, in_specs=in_specs,
        out_specs=out_specs[0] if single else out_specs,
        out_shape=out_shapes[0] if single else out_shapes,
        scratch_shapes=scratch, compiler_params=_params(len(grid)))


def _sds(shape, dtype):
    return jax.ShapeDtypeStruct(shape, dtype)


def _rms_scale(h):
    return lax.rsqrt(jnp.mean(h * h, axis=-1, keepdims=True) + RMS_EPS)


def _rms_bwd(du, h, g):
    r = _rms_scale(h)
    y = h * r
    dy = du * g
    dh = r * (dy - y * jnp.mean(dy * y, axis=-1, keepdims=True))
    return dh, jnp.sum(du * y, axis=0, keepdims=True)


def _row_tile(L):
    return _pick(L, 272)


def _pre_norm_tokens(x, g):
    S, D = x.shape
    T = TOKEN_ROW0
    L = S + T

    def body(x_ref, g_ref, h_ref, u_ref):
        h = x_ref[...]
        h_ref[...] = h
        u_ref[...] = (h * _rms_scale(h) * g_ref[...]).astype(BF16)

    below = pl.BlockSpec((T, D), lambda i: (i + 1, 0))
    return _call(
        body, (x, g), name="pre_norm_tokens", grid=(S // T,),
        in_specs=[pl.BlockSpec((T, D), lambda i: (i, 0)), pl.BlockSpec((1, D), lambda i: (0, 0))],
        out_specs=[below, below], out_shape=[_sds((L, D), F32), _sds((L, D), BF16)], compiler_params=_params(1))


def _pre_norm_meta(meta, g, h0, u1):
    L, D = h0.shape
    T = TOKEN_ROW0

    def body(m_ref, g_ref, h_in, u_in, h_ref, u_ref):
        h_ref[...] = jnp.zeros(h_ref.shape, F32)
        h_ref[PAD_ROWS:, :] = m_ref[...]
        h = h_ref[...]
        u_ref[...] = (h * _rms_scale(h) * g_ref[...]).astype(BF16)

    first = pl.BlockSpec((T, D), lambda i: (0, 0))
    return _call(
        body, (meta, g, h0, u1), name="pre_norm_meta", grid=(1,),
        in_specs=[pl.BlockSpec((N_META, D), lambda i: (0, 0)), pl.BlockSpec((1, D), lambda i: (0, 0)), ANY, ANY],
        out_specs=[first, first], out_shape=[_sds((L, D), F32), _sds((L, D), BF16)],
        input_output_aliases={2: 0, 3: 1}, compiler_params=_params(1))


def _mid_norm(o, h0, g_post, g_pre):
    L, D = h0.shape
    T = _row_tile(L)

    def body(o_ref, h_ref, gp_ref, gm_ref, h1_ref, u2_ref):
        o_ = o_ref[...]
        h1 = h_ref[...] + o_ * _rms_scale(o_) * gp_ref[...]
        h1_ref[...] = h1
        u2_ref[...] = (h1 * _rms_scale(h1) * gm_ref[...]).astype(BF16)

    row = pl.BlockSpec((T, D), lambda i: (i, 0))
    vec = pl.BlockSpec((1, D), lambda i: (0, 0))
    return _call(
        body, (o, h0, g_post, g_pre), name="mid_norm", grid=(L // T,),
        in_specs=[row, row, vec, vec], out_specs=[row, row],
        out_shape=[_sds((L, D), F32), _sds((L, D), BF16)], compiler_params=_params(1))


def _loss_head(f, h1, tgt, g_post):
    L, D = h1.shape
    T = TOKEN_ROW0
    n = L // T

    def body(f_ref, h_ref, t_ref, g_ref, dy_ref, df_ref, dg_ref, loss_ref):
        i = pl.program_id(0)
        f_ = f_ref[...]
        g = g_ref[...]
        y = h_ref[...] + f_ * _rms_scale(f_) * g
        live = (i > 0).astype(F32)
        diff = (y - t_ref[...]) * live
        part = 0.5 * jnp.sum(jnp.mean(diff * diff, axis=-1, keepdims=True), axis=0, keepdims=True)
        dy = diff * (1.0 / D)
        dy_ref[...] = dy
        df, dg = _rms_bwd(dy, f_, g)
        df_ref[...] = df.astype(BF16)

        @pl.when(i == 0)
        def _():
            dg_ref[...] = dg
            loss_ref[...] = jnp.broadcast_to(part, loss_ref.shape)

        @pl.when(i > 0)
        def _():
            dg_ref[...] += dg
            loss_ref[...] += jnp.broadcast_to(part, loss_ref.shape)

    row = pl.BlockSpec((T, D), lambda i: (i, 0))
    vec = pl.BlockSpec((1, D), lambda i: (0, 0))
    return _call(
        body, (f, h1, tgt, g_post), name="loss_head", grid=(n,),
        in_specs=[row, row, pl.BlockSpec((T, D), lambda i: (jnp.maximum(i - 1, 0), 0)), vec],
        out_specs=[row, row, vec, pl.BlockSpec((1, LANES), lambda i: (0, 0))],
        out_shape=[_sds((L, D), F32), _sds((L, D), BF16), _sds((1, D), F32), _sds((1, LANES), F32)],
        compiler_params=_params(1))


def _mid_norm_bwd(dy, du2, h1, o, g_pre, g_post):
    L, D = h1.shape
    T = _row_tile(L)

    def body(dy_ref, du_ref, h_ref, o_ref, gm_ref, gp_ref, dh1_ref, do_ref, dgm_ref, dgp_ref):
        i = pl.program_id(0)
        dh, dgm = _rms_bwd(du_ref[...], h_ref[...], gm_ref[...])
        dh1 = dy_ref[...] + dh
        dh1_ref[...] = dh1
        do, dgp = _rms_bwd(dh1, o_ref[...], gp_ref[...])
        do_ref[...] = do.astype(BF16)

        @pl.when(i == 0)
        def _():
            dgm_ref[...] = dgm
            dgp_ref[...] = dgp

        @pl.when(i > 0)
        def _():
            dgm_ref[...] += dgm
            dgp_ref[...] += dgp

    row = pl.BlockSpec((T, D), lambda i: (i, 0))
    vec = pl.BlockSpec((1, D), lambda i: (0, 0))
    return _call(
        body, (dy, du2, h1, o, g_pre, g_post), name="mid_norm_bwd", grid=(L // T,),
        in_specs=[row, row, row, row, vec, vec], out_specs=[row, row, vec, vec],
        out_shape=[_sds((L, D), F32), _sds((L, D), BF16), _sds((1, D), F32), _sds((1, D), F32)],
        compiler_params=_params(1))


def _pre_norm_bwd(dh1, du1, h0, g):
    L, D = h0.shape
    T = TOKEN_ROW0
    n = L // T

    def body(dh_ref, du_ref, h_ref, g_ref, gx_ref, dmeta_ref, dg_ref):
        i = pl.program_id(0)
        dh, dg = _rms_bwd(du_ref[...], h_ref[...], g_ref[...])
        dh0 = dh_ref[...] + dh
        gx_ref[...] = dh0

        @pl.when(i == 0)
        def _():
            dmeta_ref[...] = dh0[PAD_ROWS:, :]
            dg_ref[...] = dg

        @pl.when(i > 0)
        def _():
            dg_ref[...] += dg

    row = pl.BlockSpec((T, D), lambda i: (i, 0))
    vec = pl.BlockSpec((1, D), lambda i: (0, 0))
    return _call(
        body, (dh1, du1, h0, g), name="pre_norm_bwd", grid=(n,),
        in_specs=[row, row, row, vec],
        out_specs=[pl.BlockSpec((T, D), lambda i: (jnp.maximum(i - 1, 0), 0)),
                   pl.BlockSpec((N_META, D), lambda i: (0, 0)), vec],
        out_shape=[_sds((L - T, D), F32), _sds((N_META, D), F32), _sds((1, D), F32)],
        compiler_params=_params(1))


def _window_sum(z, g, shift_sign, L):
    s = z
    for j in range(POOL_GROUPS):
        k = 1 << j
        nxt = s + pltpu.roll(s, k if shift_sign > 0 else L - k, 0)
        s = jnp.where(j <= g, nxt, s)
    return s


def _inv_count(g, L):
    t = lax.broadcasted_iota(jnp.int32, (L, 1), 0)
    w = jnp.left_shift(2, g)
    cnt = jnp.clip(t - (PAD_ROWS - 1), 1, w)
    return 1.0 / cnt.astype(F32)


def _pool_fwd(proj, w_grp, scale):
    L = proj.shape[0]
    G, GD, _ = w_grp.shape
    P = G * GD

    def body(z_ref, w_ref, sc_ref, d_ref, ya_ref):
        g = pl.program_id(0)
        z = z_ref[...]
        d = (_window_sum(z, g, +1, L) * _inv_count(g, L) - z).astype(BF16)
        d_ref[...] = d
        y = jnp.dot(d, w_ref[...], preferred_element_type=F32)
        ya_ref[...] = (y * sc_ref[...]).astype(BF16)

    col = pl.BlockSpec((L, GD), lambda g: (0, g))
    return _call(
        body, (proj, w_grp, scale), name="pool_fwd", grid=(G,),
        in_specs=[col, pl.BlockSpec((None, GD, GD), lambda g: (g, 0, 0)), pl.BlockSpec((1, GD), lambda g: (0, g))],
        out_specs=[col, col], out_shape=[_sds((L, P), BF16), _sds((L, P), BF16)],
        compiler_params=_params(1))


def _pool_bwd(dya, d, w_grp, scale):
    L, P = dya.shape
    G, GD, _ = w_grp.shape

    def body(dya_ref, d_ref, w_ref, sc_ref, dz_ref, dw_ref, dsc_ref):
        g = pl.program_id(0)
        dya_ = dya_ref[...]
        d_ = d_ref[...]
        w = w_ref[...]
        y = jnp.dot(d_, w, preferred_element_type=F32)
        dsc_ref[...] = jnp.sum(dya_ * y, axis=0, keepdims=True)
        dy = (dya_ * sc_ref[...]).astype(BF16)
        dw_ref[...] = lax.dot_general(d_, dy, TN, preferred_element_type=F32)
        dd = lax.dot_general(dy, w, NT, preferred_element_type=F32)
        dz = _window_sum(dd * _inv_count(g, L), g, -1, L) - dd
        dz_ref[...] = dz.astype(BF16)

    col = pl.BlockSpec((L, GD), lambda g: (0, g))
    wspec = pl.BlockSpec((None, GD, GD), lambda g: (g, 0, 0))
    vec = pl.BlockSpec((1, GD), lambda g: (0, g))
    return _call(
        body, (dya, d, w_grp, scale), name="pool_bwd", grid=(G,),
        in_specs=[col, col, wspec, vec], out_specs=[col, wspec, vec],
        out_shape=[_sds((L, P), BF16), _sds((G, GD, GD), F32), _sds((1, P), F32)],
        compiler_params=_params(1))


def _fill_rotations(rot_ref, ext):
    n = ext.shape[0]
    rot_ref[0] = ext
    for r in range(1, 8):
        rot_ref[r] = pltpu.roll(ext, n - r, 0)


def _lane_chunks(C):
    step = LANES if C % LANES == 0 else C
    return [(c0, step) for c0 in range(0, C, step)]


def _conv_specs(L, C, col_v, col_g):
    T = CONV_ROWS
    per = T // HALO
    cur_v = pl.BlockSpec((T, C), lambda i: (i, col_v))
    cur_g = pl.BlockSpec((T, C), lambda i: (i, col_g))
    prev_v = pl.BlockSpec((HALO, C), lambda i: (jnp.maximum(i * per - 1, 0), col_v))
    prev_g = pl.BlockSpec((HALO, C), lambda i: (jnp.maximum(i * per - 1, 0), col_g))
    return cur_v, cur_g, prev_v, prev_g


def _glu_ext(vc, gc, vh, gh, i):
    a_cur = vc[...] * _sigmoid(gc[...])
    a_prev = vh[...] * _sigmoid(gh[...]) * (i > 0).astype(F32)
    return jnp.concatenate([a_prev, a_cur], axis=0)


def _conv_fwd(proj, C, w_dw, b_dw, ln_g, ln_b):
    L = proj.shape[0]
    T = CONV_ROWS
    P = C

    def body(vc, gc, vh, gh, w_ref, b_ref, lg_ref, lb_ref, s_ref, c_ref, rot):
        i = pl.program_id(0)
        _fill_rotations(rot, _glu_ext(vc, gc, vh, gh, i))
        for c0, cw in _lane_chunks(C):
            acc = jnp.zeros((T, cw), F32)
            for k in range(CONV_TAPS):
                q, r = divmod(HALO - (CONV_TAPS - 1) + k, 8)
                acc = acc + w_ref[k:k + 1, c0:c0 + cw] * rot[r, 8 * q:8 * q + T, c0:c0 + cw]
            c_ref[:, c0:c0 + cw] = acc + b_ref[:, c0:c0 + cw]
        c = c_ref[...]
        mu = jnp.mean(c, axis=-1, keepdims=True)
        cen = c - mu
        var = jnp.mean(cen * cen, axis=-1, keepdims=True)
        ln = cen * lax.rsqrt(var + LN_EPS) * lg_ref[...] + lb_ref[...]
        s_ref[...] = (ln * _sigmoid(ln)).astype(BF16)

    cur_v, cur_g, prev_v, prev_g = _conv_specs(L, C, P // C, P // C + 1)
    row = pl.BlockSpec((T, C), lambda i: (i, 0))
    vec = pl.BlockSpec((1, C), lambda i: (0, 0))
    return _call(
        body, (proj, proj, proj, proj, w_dw, b_dw, ln_g, ln_b), name="conv_fwd", grid=(L // T,),
        in_specs=[cur_v, cur_g, prev_v, prev_g, pl.BlockSpec((CONV_TAPS, C), lambda i: (0, 0)), vec, vec, vec],
        out_specs=[row, row], out_shape=[_sds((L, C), BF16), _sds((L, C), F32)],
        scratch_shapes=[pltpu.VMEM((8, T + HALO, C), F32)], compiler_params=_params(1))


def _conv_ln_bwd(ds, c, ln_g, ln_b):
    L, C = c.shape
    T = _row_tile(L)

    def body(ds_ref, c_ref, lg_ref, lb_ref, dc_ref, dlg_ref, dlb_ref, db_ref):
        i = pl.program_id(0)
        c_ = c_ref[...]
        g = lg_ref[...]
        mu = jnp.mean(c_, axis=-1, keepdims=True)
        cen = c_ - mu
        rstd = lax.rsqrt(jnp.mean(cen * cen, axis=-1, keepdims=True) + LN_EPS)
        xhat = cen * rstd
        ln = xhat * g + lb_ref[...]
        sg = _sigmoid(ln)
        dln = ds_ref[...] * (sg * (1.0 + ln * (1.0 - sg)))
        dxh = dln * g
        dc = rstd * (dxh - jnp.mean(dxh, axis=-1, keepdims=True)
                     - xhat * jnp.mean(dxh * xhat, axis=-1, keepdims=True))
        dc_ref[...] = dc
        dlg = jnp.sum(dln * xhat, axis=0, keepdims=True)
        dlb = jnp.sum(dln, axis=0, keepdims=True)
        db = jnp.sum(dc, axis=0, keepdims=True)

        @pl.when(i == 0)
        def _():
            dlg_ref[...] = dlg
            dlb_ref[...] = dlb
            db_ref[...] = db

        @pl.when(i > 0)
        def _():
            dlg_ref[...] += dlg
            dlb_ref[...] += dlb
            db_ref[...] += db

    row = pl.BlockSpec((T, C), lambda i: (i, 0))
    vec = pl.BlockSpec((1, C), lambda i: (0, 0))
    return _call(
        body, (ds, c, ln_g, ln_b), name="conv_ln_bwd", grid=(L // T,),
        in_specs=[row, row, vec, vec], out_specs=[row, vec, vec, vec],
        out_shape=[_sds((L, C), F32), _sds((1, C), F32), _sds((1, C), F32), _sds((1, C), F32)],
        compiler_params=_params(1))


def _conv_bwd(dc, proj, C, w_dw):
    L = proj.shape[0]
    T = CONV_ROWS
    per = T // HALO
    n = L // T
    P = C
    taps_pad = 32

    def body(dcc, dcn, vc, gc, w_ref, dv_ref, dg_ref, dw_ref, rot_d, dw_acc):
        i = pl.program_id(0)
        dc_next = dcn[...] * (i < n - 1).astype(F32)
        _fill_rotations(rot_d, jnp.concatenate([dcc[...], dc_next], axis=0))

        @pl.when(i == 0)
        def _():
            dw_acc[...] = jnp.zeros(dw_acc.shape, F32)

        for c0, cw in _lane_chunks(C):
            v = vc[:, c0:c0 + cw]
            sg = _sigmoid(gc[:, c0:c0 + cw])
            a = v * sg
            da = jnp.zeros((T, cw), F32)
            for k in range(CONV_TAPS):
                q, r = divmod(CONV_TAPS - 1 - k, 8)
                slab = rot_d[r, 8 * q:8 * q + T, c0:c0 + cw]
                da = da + w_ref[k:k + 1, c0:c0 + cw] * slab
                dw_acc[k, :, c0:c0 + cw] += jnp.sum((a * slab).reshape(T // 8, 8, cw), axis=0)
            dv_ref[:, c0:c0 + cw] = (da * sg).astype(BF16)
            dg_ref[:, c0:c0 + cw] = (da * v * sg * (1.0 - sg)).astype(BF16)

        @pl.when(i == n - 1)
        def _():
            dw_ref[...] = jnp.sum(dw_acc[...], axis=1)

    cur_v, cur_g, _, _ = _conv_specs(L, C, P // C, P // C + 1)
    row = pl.BlockSpec((T, C), lambda i: (i, 0))
    nxt = pl.BlockSpec((HALO, C), lambda i: (jnp.minimum((i + 1) * per, L // HALO - 1), 0))
    wspec = pl.BlockSpec((CONV_TAPS, C), lambda i: (0, 0))
    return _call(
        body, (dc, dc, proj, proj, w_dw), name="conv_bwd", grid=(n,),
        in_specs=[row, nxt, cur_v, cur_g, wspec],
        out_specs=[row, row, pl.BlockSpec((taps_pad, C), lambda i: (0, 0))],
        out_shape=[_sds((L, C), BF16), _sds((L, C), BF16), _sds((taps_pad, C), F32)],
        scratch_shapes=[pltpu.VMEM((8, T + HALO, C), F32), pltpu.VMEM((taps_pad, 8, C), F32)],
        compiler_params=_params(1))


def _mix_fwd(ya_pre, s, wpo, wco, proj, D):
    L, P = ya_pre.shape
    Q, _, DS = wpo.shape
    bm = _pick(L, 1088)
    gate0 = (proj.shape[1] - 2 * D) // DS
    per = D // DS

    def body(a1, a2, b1, b2, ga, gb, m_ref, ya_ref, yb_ref):
        ya = jnp.dot(a1[...], b1[...], preferred_element_type=F32)
        yb = jnp.dot(a2[...], b2[...], preferred_element_type=F32)
        ya_ref[...] = ya.astype(BF16)
        yb_ref[...] = yb.astype(BF16)
        m_ref[...] = (_sigmoid(ga[...].astype(F32)) * ya + _sigmoid(gb[...].astype(F32)) * yb).astype(BF16)

    act = pl.BlockSpec((bm, P), lambda i, q: (i, 0))
    wsp = pl.BlockSpec((None, P, DS), lambda i, q: (q, 0, 0))
    out = pl.BlockSpec((bm, DS), lambda i, q: (i, q))
    return _call(
        body, (ya_pre, s, wpo, wco, proj, proj), name="mix_fwd", grid=(L // bm, Q),
        in_specs=[act, act, wsp, wsp,
                  pl.BlockSpec((bm, DS), lambda i, q: (i, gate0 + q)),
                  pl.BlockSpec((bm, DS), lambda i, q: (i, gate0 + per + q))],
        out_specs=[out, out, out],
        out_shape=[_sds((L, D), BF16), _sds((L, D), BF16), _sds((L, D), BF16)],
        compiler_params=_params(2))


def _mix_bwd(do, w_o, proj, ya, yb):
    L, D = do.shape
    bm = _pick(L, 544)
    bn = _pick(D // N_CHIPS, 512)
    gate0 = (proj.shape[1] - 2 * D) // bn
    per = D // bn

    def epilogue(dm, extras, outs):
        ga, gb, ya_ref, yb_ref = extras
        sa = _sigmoid(ga[...].astype(F32))
        sb = _sigmoid(gb[...].astype(F32))
        outs[0][...] = (dm * sa).astype(BF16)
        outs[1][...] = (dm * sb).astype(BF16)
        outs[2][...] = (dm * ya_ref[...].astype(F32) * sa * (1.0 - sa)).astype(BF16)
        outs[3][...] = (dm * yb_ref[...].astype(F32) * sb * (1.0 - sb)).astype(BF16)

    blk = pl.BlockSpec((bm, bn), lambda i, j: (i, j))
    return _mm(
        "mix_bwd", (L // bm, D // bn), [do, w_o, proj, proj, ya, yb],
        [pl.BlockSpec((bm, D), lambda i, j: (i, 0)), pl.BlockSpec((bn, D), lambda i, j: (j, 0)),
         pl.BlockSpec((bm, bn), lambda i, j: (i, gate0 + j)),
         pl.BlockSpec((bm, bn), lambda i, j: (i, gate0 + per + j)), blk, blk],
        [_sds((L, D), BF16)] * 4, [blk] * 4, NT, 1, epilogue)


def _mm_act_colw(name, a, wg, bn_pref, epilogue=_store, out_dtypes=(F32,)):
    L, K = a.shape
    Q, _, n = wg.shape
    bn = _pick(n, bn_pref)
    nj = n // bn
    out = pl.BlockSpec((L, bn), lambda q, j: (0, q * nj + j))
    return _mm(name, (Q, nj), [a, wg],
               [pl.BlockSpec((L, K), lambda q, j: (0, 0)), pl.BlockSpec((None, K, bn), lambda q, j: (q, 0, j))],
               [_sds((L, Q * n), dt) for dt in out_dtypes], [out] * len(out_dtypes), NN, 1, epilogue)


def _proj(u1, wg, n_front):
    L, K = u1.shape
    Q, _, n = wg.shape
    bn = _pick(math.gcd(n, n_front), 256)
    nj = n // bn
    nf = n_front // bn

    def epilogue(val, extras, outs):
        col = pl.program_id(0) * nj + pl.program_id(1)

        @pl.when(col < nf)
        def _():
            outs[0][...] = val

        @pl.when(col >= nf)
        def _():
            outs[1][...] = val.astype(BF16)

    front = pl.BlockSpec((L, bn), lambda q, j: (0, jnp.minimum(q * nj + j, nf - 1)))
    gates = pl.BlockSpec((L, bn), lambda q, j: (0, jnp.maximum(q * nj + j - nf, 0)))
    return _mm("proj", (Q, nj), [u1, wg],
               [pl.BlockSpec((L, K), lambda q, j: (0, 0)), pl.BlockSpec((None, K, bn), lambda q, j: (q, 0, j))],
               [_sds((L, n_front), F32), _sds((L, Q * n - n_front), BF16)], [front, gates], NN, 1, epilogue)


def _mm_grad_colw_t(name, g, wg, bm_pref, bn_pref):
    L = g.shape[0]
    Q, K, n = wg.shape
    bm = _pick(L, bm_pref)
    bn = _pick(K, bn_pref)
    return _mm(name, (L // bm, K // bn, Q), [g, wg],
               [pl.BlockSpec((bm, n), lambda i, j, k: (i, k)), pl.BlockSpec((None, bn, n), lambda i, j, k: (k, j, 0))],
               [_sds((L, K), F32)], [pl.BlockSpec((bm, bn), lambda i, j, k: (i, j))], NT, Q,
               acc_shape=(bm, bn))


def _mm_wgrad_colw(name, a, g, Q, bm_pref, bn_pref, rows=None):
    L, K = a.shape
    n = g.shape[1] // Q
    first, count = rows or (0, K)
    bm = _pick(count, bm_pref)
    bn = _pick(n, bn_pref)
    nj = n // bn
    i0 = first // bm
    return _mm(name, (Q, count // bm, nj), [a, g],
               [pl.BlockSpec((L, bm), lambda q, i, j: (0, i0 + i)),
                pl.BlockSpec((L, bn), lambda q, i, j: (0, q * nj + j))],
               [_sds((Q, count, n), F32)], [pl.BlockSpec((None, bm, bn), lambda q, i, j: (q, i, j))], TN, 1)


def _mm_wgrad(name, a, g, bm_pref, bn_pref):
    L, K = a.shape
    N = g.shape[1]
    bm = _pick(K, bm_pref)
    bn = _pick(N, bn_pref)
    return _mm(name, (K // bm, N // bn), [a, g],
               [pl.BlockSpec((L, bm), lambda i, j: (0, i)), pl.BlockSpec((L, bn), lambda i, j: (0, j))],
               [_sds((K, N), F32)], [pl.BlockSpec((bm, bn), lambda i, j: (i, j))], TN, 1)


def _mm_act_roww(name, a, w, bm_pref, bn_pref, bk_pref):
    L, K = a.shape
    N = w.shape[1]
    bm, bn, bk = _pick(L, bm_pref), _pick(N, bn_pref), _pick(K, bk_pref)
    nk = K // bk
    return _mm(name, (L // bm, N // bn, nk), [a, w],
               [pl.BlockSpec((bm, bk), lambda i, j, k: (i, k)), pl.BlockSpec((bk, bn), lambda i, j, k: (k, j))],
               [_sds((L, N), F32)], [pl.BlockSpec((bm, bn), lambda i, j, k: (i, j))], NN, nk,
               acc_shape=(bm, bn))


def _concat_cols(name, parts):
    L = parts[0].shape[0]
    T = _row_tile(L)
    widths = [p.shape[1] for p in parts]

    def body(*refs):
        at = 0
        for ref, wd in zip(refs[:-1], widths):
            refs[-1][:, at:at + wd] = ref[...]
            at += wd

    return _call(
        body, list(parts), name=name, grid=(L // T,),
        in_specs=[pl.BlockSpec((T, wd), lambda i: (i, 0)) for wd in widths],
        out_specs=pl.BlockSpec((T, sum(widths)), lambda i: (i, 0)),
        out_shape=_sds((L, sum(widths)), parts[0].dtype), compiler_params=_params(1))


def _up_epilogue(val, extras, outs):
    outs[0][...] = val.astype(BF16)
    r = jnp.maximum(val, 0.0)
    outs[1][...] = (r * r).astype(BF16)


def _mlp_down_bwd(df, w_down, a_up):
    L, D = df.shape
    F = w_down.shape[0]
    bm = _pick(L, 1088)
    bn = _pick(F, 1024)

    def epilogue(val, extras, outs):
        outs[0][...] = (val * (2.0 * jnp.maximum(extras[0][...].astype(F32), 0.0))).astype(BF16)

    blk = pl.BlockSpec((bm, bn), lambda i, j: (i, j))
    return _mm("mlp_down_bwd", (L // bm, F // bn), [df, w_down, a_up],
               [pl.BlockSpec((bm, D), lambda i, j: (i, 0)), pl.BlockSpec((bn, D), lambda i, j: (j, 0)), blk],
               [_sds((L, F), BF16)], [blk], NT, 1, epilogue)


ANY = pl.BlockSpec(memory_space=pl.ANY)
HBM = pl.BlockSpec(memory_space=pltpu.HBM)
SEM = pl.BlockSpec(memory_space=pltpu.SEMAPHORE)
EFFECT = pltpu.SideEffectType.DATAFLOW_SIDE_EFFECTING
N_CHIPS = 4


def _place():
    x, y, c = lax.axis_index("x"), lax.axis_index("y"), lax.axis_index("c")
    return x, y, c


def _chip_at(x, y, k):
    px = 1 - x if k & 2 else x
    py = 1 - y if k & 1 else y
    return px, py


def _cast_into_slab(w2d, chip, dtype):
    R, C = w2d.shape
    T = _elem_tile(R, C)

    def body(p_ref, w_ref, o_ref):
        o_ref[...] = w_ref[...].astype(dtype)

    return _call(
        body, (w2d,), name="cast_into_slab", grid=(R // T,), scalars=jnp.reshape(chip, (1,)).astype(jnp.int32),
        in_specs=[pl.BlockSpec((T, C), lambda i, p: (i, 0))],
        out_specs=pl.BlockSpec((None, T, C), lambda i, p: (p[0], i, 0)),
        out_shape=_sds((N_CHIPS, R, C), dtype), compiler_params=_params(1))


TOKEN = jax.ShapeDtypeStruct((8, LANES), F32)


class _Sems:
    def __init__(self, items, shape):
        self.items, self.shape = list(items), tuple(shape)

    def pair(self, idx):
        flat = 0
        for i, n in zip(idx, self.shape):
            flat = flat * n + i
        half = len(self.items) // 2
        return self.items[flat], self.items[half + flat]


def _sem_count(shape):
    n = 1
    for s in shape:
        n *= s
    return n


def _remote(src, dst, sems, idx, device):
    send, recv = sems.pair(idx)
    return pltpu.make_async_remote_copy(src_ref=src, dst_ref=dst, send_sem=send, recv_sem=recv,
                                        device_id=device, device_id_type=MESH)


def _thru(arrays):
    return ([pltpu.with_memory_space_constraint(a, pltpu.HBM) for a in arrays],
            [pltpu.HBM(a.shape, a.dtype) for a in arrays])


def _comm_start(name, arrays, sem_shape, plan, follows=False):
    na, ns = len(arrays), 2 * _sem_count(sem_shape)

    def body(*refs):
        sems, token = _Sems(refs[na:na + ns], sem_shape), refs[-1]
        for src, dst, idx, device in plan(refs[:na])[0]:
            _remote(src, dst, sems, idx, device).start()
        token[...] = jnp.zeros(token.shape, F32)

    ins, outs = _thru(arrays)
    res = _call(
        body, ins, name=name, in_specs=[HBM] * na, mark=-1, follows=follows, made_from=arrays,
        out_specs=[SEM] * ns + [HBM] * na + [pl.BlockSpec(memory_space=pltpu.VMEM)],
        out_shape=[pltpu.SemaphoreType.DMA(())] * ns + outs + [TOKEN],
        input_output_aliases={a: ns + a for a in range(na)},
        compiler_params=pltpu.CompilerParams(has_side_effects=EFFECT))
    return _Sems(res[:ns], sem_shape), list(res[ns:ns + na])


def _wait_plans(refs, sem_refs, waits):
    x, y, c = _place()
    at = 0
    for sems, plan in waits:
        here = _Sems(sem_refs[at:at + len(sems.items)], sems.shape)
        at += len(sems.items)
        _, mine, arrivals = plan(refs)
        for dst, idx in arrivals:
            _remote(dst, dst, here, idx, (x, y, c)).wait_recv()
        for src, idx in mine:
            _remote(src, src, here, idx, (x, y, c)).wait_send()


def _comm_wait(name, arrays, waits):
    na = len(arrays)
    sem_items = [s for sems, _ in waits for s in sems.items]
    ns = len(sem_items)

    def body(*refs):
        _wait_plans(refs[:na], refs[na:na + ns], waits)
        refs[-1][...] = jnp.zeros(refs[-1].shape, F32)

    ins, outs = _thru(arrays)
    res = _call(
        body, ins + sem_items, name=name, in_specs=[HBM] * na + [SEM] * ns, mark=-1,
        out_specs=[HBM] * na + [pl.BlockSpec(memory_space=pltpu.VMEM)], out_shape=outs + [TOKEN],
        input_output_aliases={a: a for a in range(na)},
        compiler_params=pltpu.CompilerParams(has_side_effects=EFFECT))
    return list(res[:na])


def _comm_relay(name, arrays, sems, plan, sem_shape, next_plan):
    na, ns_in, ns_out = len(arrays), len(sems.items), 2 * _sem_count(sem_shape)

    def body(*refs):
        bufs = refs[:na]
        sems_in = _Sems(refs[na:na + ns_in], sems.shape)
        sems_out = _Sems(refs[na + ns_in:na + ns_in + ns_out], sem_shape)
        x, y, c = _place()
        _, mine, arrivals = plan(bufs)
        onward = next_plan(bufs)[0]
        for dst, idx in arrivals:
            _remote(dst, dst, sems_in, idx, (x, y, c)).wait_recv()
            for src, to, idx2, device, after_idx in onward:
                if after_idx == idx:
                    _remote(src, to, sems_out, idx2, device).start()
        for src, idx in mine:
            _remote(src, src, sems_in, idx, (x, y, c)).wait_send()
        refs[-1][...] = jnp.zeros(refs[-1].shape, F32)

    ins, outs = _thru(arrays)
    res = _call(
        body, ins + sems.items, name=name, in_specs=[HBM] * na + [SEM] * ns_in, mark=-1,
        out_specs=[SEM] * ns_out + [HBM] * na + [pl.BlockSpec(memory_space=pltpu.VMEM)],
        out_shape=[pltpu.SemaphoreType.DMA(())] * ns_out + outs + [TOKEN],
        input_output_aliases={a: ns_out + a for a in range(na)},
        compiler_params=pltpu.CompilerParams(has_side_effects=EFFECT))
    return _Sems(res[:ns_out], sem_shape), list(res[ns_out:ns_out + na])


def _half(ref, q, which):
    h = ref.shape[1] // 2
    return ref.at[q, pl.ds(which * h, h)]


def _quarter(ref, q, half, which):
    h = ref.shape[1] // 2
    return ref.at[q, pl.ds(half * h + which * (h // 2), h // 2)]


def _gather_plan(n_halved):
    def plan(refs):
        x, y, c = _place()
        p = 2 * x + y
        starts, mine, arrivals = [], [], []
        for n, ref in enumerate(refs):
            for k in range(1, N_CHIPS if n >= n_halved else 3):
                px, py = _chip_at(x, y, k)
                q = 2 * px + py
                out = _half(ref, p, c) if n < n_halved else ref.at[p]
                inc = _half(ref, q, c) if n < n_halved else ref.at[q]
                starts.append((out, out, (n, k - 1), (px, py, c)))
                mine.append((out, (n, k - 1)))
                arrivals.append((inc, (n, k - 1)))
        return starts, mine, arrivals
    return plan


def _spread_plan(n_halved, part):
    def plan(refs):
        x, y, c = _place()
        p = 2 * x + y
        starts, mine, arrivals = [], [], []
        for n in range(n_halved):
            for k in (1, 2):
                px, py = _chip_at(x, y, k)
                q = 2 * px + py
                tx, ty = _chip_at(x, y, 3 - k)
                dx, dy = _chip_at(x, y, 3)
                piece = _quarter(refs[n], q, c, 2 - k)
                landed = _half(refs[n], q, c)
                starts.append((piece, piece, (0, n, k - 1), (tx, ty, c), (n, k - 1)))
                starts.append((landed, landed, (1, n, k - 1), (x, y, 1 - c), (n, k - 1)))
                if part != 1:
                    mine.append((piece, (0, n, k - 1)))
                    arrivals.append((_quarter(refs[n], 2 * dx + dy, c, k - 1), (0, n, 2 - k)))
                if part != 0:
                    mine.append((landed, (1, n, k - 1)))
                    arrivals.append((_half(refs[n], q, 1 - c), (1, n, k - 1)))
        return starts, mine, arrivals
    return plan


def _last_hand_on_plan(n_halved):
    def plan(refs):
        x, y, c = _place()
        dx, dy = _chip_at(x, y, 3)
        d = 2 * dx + dy
        starts, mine, arrivals = [], [], []
        for n in range(n_halved):
            for k in (1, 2):
                piece = _quarter(refs[n], d, c, k - 1)
                starts.append((piece, piece, (n, k - 1), (x, y, 1 - c), (0, n, 2 - k)))
                mine.append((piece, (n, k - 1)))
                arrivals.append((_quarter(refs[n], d, 1 - c, k - 1), (n, k - 1)))
        return starts, mine, arrivals
    return plan


def _swap_plan(n):
    def plan(refs):
        x, y, c = _place()
        starts, mine, arrivals = [], [], []
        for a in range(n):
            h = refs[a].shape[1] // 2
            src = refs[a].at[:, pl.ds((1 - c) * h, h)]
            starts.append((src, refs[n + a], (a,), (x, y, 1 - c)))
            mine.append((src, (a,)))
            arrivals.append((refs[n + a], (a,)))
        return starts, mine, arrivals
    return plan


def _scatter_plan(n):
    def plan(refs):
        x, y, c = _place()
        starts, mine, arrivals = [], [], []
        for a in range(n):
            for k in range(1, N_CHIPS):
                px, py = _chip_at(x, y, k)
                src = refs[a].at[2 * px + py]
                starts.append((src, refs[n + a].at[k - 1], (a, k - 1), (px, py, c)))
                mine.append((src, (a, k - 1)))
                arrivals.append((refs[n + a].at[k - 1], (a, k - 1)))
        return starts, mine, arrivals
    return plan


def _share_plan(n):
    def plan(refs):
        x, y, c = _place()
        starts, mine, arrivals = [], [], []
        for a in range(n):
            h = refs[a].shape[0] // 2
            own = refs[a].at[pl.ds(c * h, h)]
            starts.append((own, own, (a,), (x, y, 1 - c)))
            mine.append((own, (a,)))
            arrivals.append((refs[a].at[pl.ds((1 - c) * h, h)], (a,)))
        return starts, mine, arrivals
    return plan


N_DEVICES = 8


def _packs_plan(refs):
    buf = refs[0]
    x, y, c = _place()
    me = 4 * x + 2 * y + c
    starts, mine, arrivals = [], [], []
    for r in range(1, N_DEVICES):
        peer = (1 - x if r & 4 else x, 1 - y if r & 2 else y, 1 - c if r & 1 else c)
        starts.append((buf.at[me], buf.at[me], (r - 1,), peer))
        mine.append((buf.at[me], (r - 1,)))
        arrivals.append((buf.at[4 * peer[0] + 2 * peer[1] + peer[2]], (r - 1,)))
    return starts, mine, arrivals


class _Reduction:
    def __init__(self, tag, slabs, c_idx, chip):
        self.tag, self.n, self.c_idx, self.chip = tag, len(slabs), c_idx, chip
        lands = [lax.empty((g.shape[0], g.shape[1] // 2, g.shape[2]), g.dtype) for g in slabs]
        self.sems = _comm_start("swap_start_" + tag, list(slabs) + lands, (self.n,), _swap_plan(self.n))

    def partial(self):
        n = self.n
        sems, bufs = self.sems
        bufs = _comm_wait("swap_wait_" + self.tag, bufs, [(sems, _swap_plan(n))])
        both = [_chip_partial(g, r, self.c_idx, self.chip) for g, r in zip(bufs[:n], bufs[n:])]
        self.own = [o for _, o in both]
        parts = [p for p, _ in both]
        lands = [lax.empty((N_CHIPS - 1,) + p.shape[1:], p.dtype) for p in parts]
        self.sems = _comm_start("scatter_start_" + self.tag, parts + lands, (n, N_CHIPS - 1), _scatter_plan(n))

    def total(self):
        n = self.n
        sems, bufs = self.sems
        bufs = _comm_wait("scatter_wait_" + self.tag, bufs, [(sems, _scatter_plan(n))])
        fulls = [_sum_partials(o, r, self.c_idx) for o, r in zip(self.own, bufs[n:])]
        self.sems = _comm_start("share_start_" + self.tag, fulls, (n,), _share_plan(n))

    def finish(self):
        sems, bufs = self.sems
        return _comm_wait("share_wait_" + self.tag, bufs, [(sems, _share_plan(self.n))])


def _elem_tile(rows, cols):
    return _pick(rows, max(8, (1 << 19) // cols // 8 * 8))


def _chip_partial(grad, recv, c_idx, p_idx):
    Q, R, C = grad.shape
    h = R // 2
    T = _elem_tile(h, C)
    nt = h // T

    def body(sc_ref, g_ref, r_ref, sb_ref, own_ref):
        q = pl.program_id(1)
        s = g_ref[...] + r_ref[...]
        sb_ref[...] = s.astype(BF16)

        @pl.when(q == sc_ref[1])
        def _():
            own_ref[...] = s

    return _call(
        body, (grad, recv), name="chip_partial", grid=(nt, Q),
        scalars=jnp.stack([c_idx, p_idx]).astype(jnp.int32),
        in_specs=[pl.BlockSpec((None, T, C), lambda t, q, sc: (q, sc[0] * nt + t, 0)),
                  pl.BlockSpec((None, T, C), lambda t, q, sc: (q, t, 0))],
        out_specs=[pl.BlockSpec((None, T, C), lambda t, q, sc: (q, t, 0)),
                   pl.BlockSpec((T, C), lambda t, q, sc: (t, 0))],
        out_shape=[_sds((Q, h, C), BF16), _sds((h, C), F32)], compiler_params=_params(2))


def _sum_partials(own, parts, c_idx):
    h, C = own.shape
    T = _elem_tile(h, C)
    nt = h // T

    def body(c_ref, o_ref, p_ref, t_ref):
        t = o_ref[...]
        for k in range(N_CHIPS - 1):
            t = t + p_ref[k].astype(F32)
        t_ref[...] = t

    return _call(
        body, (own, parts), name="sum_partials", grid=(nt,), scalars=jnp.reshape(c_idx, (1,)).astype(jnp.int32),
        in_specs=[pl.BlockSpec((T, C), lambda i, c: (i, 0)),
                  pl.BlockSpec((N_CHIPS - 1, T, C), lambda i, c: (0, i, 0))],
        out_specs=pl.BlockSpec((T, C), lambda i, c: (c[0] * nt + i, 0)),
        out_shape=_sds((2 * h, C), F32), compiler_params=_params(1))


def _pack_rows(name, parts, slot=None, n_slots=1):
    width = parts[0].shape[1]
    offsets, at = [], 0
    for p in parts:
        offsets.append(at)
        at += p.shape[0]
    total = -(-at // 8) * 8
    lead = 0 if slot is None else 1

    def body(*refs):
        out = refs[-1]
        out[...] = jnp.zeros(out.shape, F32)
        for ref, o in zip(refs[lead:-1], offsets):
            out[o:o + ref.shape[0], :] = ref[...]

    if slot is None:
        whole = pl.BlockSpec(memory_space=pltpu.VMEM)
        return _call(body, list(parts), name=name, in_specs=[whole] * len(parts), out_specs=whole,
                     out_shape=_sds((total, width), F32))
    return _call(body, list(parts), name=name, grid=(1,), scalars=jnp.reshape(slot, (1,)).astype(jnp.int32),
                 in_specs=[pl.BlockSpec(p.shape, lambda i, s: (0, 0)) for p in parts],
                 out_specs=pl.BlockSpec((None, total, width), lambda i, s: (s[0], 0, 0)),
                 out_shape=_sds((n_slots, total, width), F32))


def _sum_packs(packs):
    n, R, C = packs.shape

    def body(p_ref, o_ref):
        t = p_ref[0]
        for k in range(1, n):
            t = t + p_ref[k]
        o_ref[...] = t

    return _call(
        body, (packs,), name="sum_packs", grid=(1,), in_specs=[pl.BlockSpec((n, R, C), lambda i: (0, 0, 0))],
        out_specs=pl.BlockSpec((R, C), lambda i: (0, 0)), out_shape=_sds((R, C), F32), compiler_params=_params(1))


def _adamw(w, g, m, v):
    R, C = w.shape
    T = _elem_tile(R, C)

    def body(w_ref, g_ref, m_ref, v_ref, d_ref, m2_ref, v2_ref):
        g_ = g_ref[...]
        m2 = ADAM_B1 * m_ref[...] + (1.0 - ADAM_B1) * g_
        v2 = ADAM_B2 * v_ref[...] + (1.0 - ADAM_B2) * (g_ * g_)
        m_hat = m2 / (1.0 - ADAM_B1 ** ADAM_STEP)
        v_hat = v2 / (1.0 - ADAM_B2 ** ADAM_STEP)
        d_ref[...] = -ADAM_LR * (m_hat / (jnp.sqrt(v_hat) + ADAM_EPS) + ADAM_WD * w_ref[...])
        m2_ref[...] = m2
        v2_ref[...] = v2

    blk = pl.BlockSpec((T, C), lambda i: (i, 0))
    return _call(
        body, (w, g, m, v), name="adamw", grid=(R // T,), in_specs=[blk] * 4, out_specs=[blk] * 3,
        out_shape=[_sds((R, C), F32)] * 3, compiler_params=_params(1))


def _adamw_rows(w, g, m, v, row0, prev=None):
    R, C = w.shape
    T = _elem_tile(g.shape[0], C)
    off = row0 // T

    def body(w_ref, g_ref, m_ref, v_ref, *rest):
        d_ref, m2_ref, v2_ref, g2_ref = rest[-4:]
        g_ = g_ref[...]
        m2 = ADAM_B1 * m_ref[...] + (1.0 - ADAM_B1) * g_
        v2 = ADAM_B2 * v_ref[...] + (1.0 - ADAM_B2) * (g_ * g_)
        m_hat = m2 / (1.0 - ADAM_B1 ** ADAM_STEP)
        v_hat = v2 / (1.0 - ADAM_B2 ** ADAM_STEP)
        d_ref[...] = -ADAM_LR * (m_hat / (jnp.sqrt(v_hat) + ADAM_EPS) + ADAM_WD * w_ref[...])
        m2_ref[...] = m2
        v2_ref[...] = v2
        g2_ref[...] = g_

    here = pl.BlockSpec((T, C), lambda i: (off + i, 0))
    piece = pl.BlockSpec((T, C), lambda i: (i, 0))
    done = tuple(prev or ())
    return _call(
        body, (w, g, m, v) + done, name="adamw_rows", grid=(g.shape[0] // T,), follows=prev is None,
        in_specs=[here, piece, here, here] + [ANY] * len(done), out_specs=[here] * 4,
        out_shape=[_sds((R, C), F32)] * 4, input_output_aliases={4 + j: j for j in range(len(done))},
        compiler_params=_params(1))


SC_TILES = 32
SC_ROWS = 8
SC_LANES = 16


def _adamw_sc(w, g, m, v):
    R, C = w.shape
    per_tile = R // SC_TILES
    assert per_tile % SC_ROWS == 0 and C % SC_LANES == 0

    def body(w_hbm, g_hbm, m_hbm, v_hbm, d_out, m_out, v_out, g_out, wb, gb, mb, vb):
        tile = lax.axis_index("sc_tile") * 2 + lax.axis_index("sc_core")

        @pl.loop(0, per_tile // SC_ROWS)
        def _(chunk):
            here = pl.ds(tile * per_tile + chunk * SC_ROWS, SC_ROWS)
            pltpu.sync_copy(w_hbm.at[here], wb)
            pltpu.sync_copy(g_hbm.at[here], gb)
            pltpu.sync_copy(m_hbm.at[here], mb)
            pltpu.sync_copy(v_hbm.at[here], vb)
            for r in range(SC_ROWS):
                @pl.loop(0, C, step=SC_LANES)
                def _(j):
                    at = pl.ds(j, SC_LANES)
                    g_ = gb[r, at]
                    m2 = ADAM_B1 * mb[r, at] + (1.0 - ADAM_B1) * g_
                    v2 = ADAM_B2 * vb[r, at] + (1.0 - ADAM_B2) * (g_ * g_)
                    m_hat = m2 / (1.0 - ADAM_B1 ** ADAM_STEP)
                    v_hat = v2 / (1.0 - ADAM_B2 ** ADAM_STEP)
                    wb[r, at] = -ADAM_LR * (m_hat / (jnp.sqrt(v_hat) + ADAM_EPS) + ADAM_WD * wb[r, at])
                    mb[r, at] = m2
                    vb[r, at] = v2
            pltpu.sync_copy(wb, d_out.at[here])
            pltpu.sync_copy(mb, m_out.at[here])
            pltpu.sync_copy(vb, v_out.at[here])
            pltpu.sync_copy(gb, g_out.at[here])

    buf = pltpu.VMEM((SC_ROWS, C), F32)
    return pl.kernel(
        body, name="adamw_sc", out_type=(_sds((R, C), F32),) * 4,
        mesh=plsc.VectorSubcoreMesh(core_axis_name="sc_core", subcore_axis_name="sc_tile"),
        scratch_types=[buf, buf, buf, buf])(w, g, m, v)


BIG = ("w_in", "w_pool_out", "w_conv_out", "w_o", "w_up", "w_down", "w_pool_grp")
VECTORS = ("g_pre_mix", "pool_scale", "b_dw", "conv_ln_g", "conv_ln_b", "g_post_mix", "g_pre_mlp", "g_post_mlp")
WEIGHTS = ("meta", "g_pre_mix", "w_in", "w_pool_grp", "pool_scale", "w_pool_out", "w_dw", "b_dw", "conv_ln_g",
           "conv_ln_b", "w_conv_out", "w_o", "g_post_mix", "g_pre_mlp", "w_up", "w_down", "g_post_mlp")


def _as_rows(a, width):
    r, cols = a.shape
    return a.reshape(r * (cols // width), width)


def _step(w, m, v, x, tgt):
    S, D = x.shape
    P = D // 2
    xi, yi, ci = _place()
    chip = 2 * xi + yi
    _CHAIN["after"] = []

    C = D // 2
    G = POOL_GROUPS
    GD = P // G
    GS = GD // N_CHIPS
    Q = N_CHIPS
    vecs = {k: w[k] for k in VECTORS}
    shard2d = {k: w[k].reshape(-1, w[k].shape[-1]) for k in BIG}
    grads, delta, new_m, new_v = {}, {}, {}, {}

    def update(names, reduced, on_sparsecore=False):
        for k, g in zip(names, reduced):
            args = (shard2d[k], g, m[k].reshape(shard2d[k].shape), v[k].reshape(shard2d[k].shape))
            delta[k], new_m[k], new_v[k], grads[k] = _adamw_sc(*args) if on_sparsecore else _adamw_rows(*args, 0)

    groups = dict(a=(("w_in", "w_pool_grp"), ("w_dw", "meta")), b=(("w_pool_out", "w_conv_out", "w_o"), ()),
                  c=(("w_up",), ()), d=(("w_down",), ()))
    flying = {}

    def start(tag, follows):
        halved, whole = groups[tag]
        flying[tag] = _comm_start("gather_start_" + tag, flying[tag], (len(halved + whole), N_CHIPS - 1),
                                  _gather_plan(len(halved)), follows=follows)

    def spread(tag):
        nh = len(groups[tag][0])
        sems, bufs = flying[tag]
        flying[tag] = _comm_relay("gather_relay_" + tag, bufs, sems, _gather_plan(nh), (2, nh, 2),
                                  _spread_plan(nh, 0))

    def landed(tag):
        halved, whole = groups[tag]
        nh = len(halved)
        sems, bufs = flying.pop(tag)
        last, tree = _comm_relay("gather_relay2_" + tag, bufs[:nh], sems, _spread_plan(nh, 0), (nh, 2),
                                 _last_hand_on_plan(nh))
        done = _comm_wait("gather_wait_" + tag, tree,
                          [(sems, _spread_plan(nh, 1)), (last, _last_hand_on_plan(nh))])
        return dict(zip(halved + whole, done + bufs[nh:]))

    for tag, (halved, whole) in groups.items():
        flying[tag] = [_cast_into_slab(shard2d[k], chip, BF16) for k in halved]
        flying[tag] += [_cast_into_slab(w[k], chip, F32) for k in whole]
        if tag == "a":
            start(tag, False)

    small_names = [k for k in WEIGHTS if k not in BIG]

    def pack_small(tree):
        parts = []
        for k in small_names:
            flat = tree[k].reshape(-1)
            parts.append(jnp.pad(flat, (0, -flat.shape[0] % P)).reshape(-1, P))
        return _pack_rows("pack_small", parts)

    small_w, small_m, small_v = pack_small(w), pack_small(m), pack_small(v)
    h0, u1 = _pre_norm_tokens(x, vecs["g_pre_mix"])
    spread("a")
    start("b", True)

    got = landed("a")
    win_g = got["w_in"]
    w_grp = got["w_pool_grp"].reshape(N_CHIPS, G, GS, GD).transpose(1, 0, 2, 3).reshape(G, GD, GD)
    w_dw = got["w_dw"].transpose(1, 0, 2).reshape(CONV_TAPS, P)
    meta = got["meta"].transpose(1, 0, 2).reshape(N_META, D)
    h0, u1 = _pre_norm_meta(meta, vecs["g_pre_mix"], h0, u1)
    spread("b")
    start("c", True)
    start("d", True)
    proj, gates = _proj(u1, win_g, P + 2 * C)
    d, ya_pre = _pool_fwd(proj, w_grp, vecs["pool_scale"])
    s, c = _conv_fwd(proj, C, w_dw, vecs["b_dw"], vecs["conv_ln_g"], vecs["conv_ln_b"])
    spread("c")
    got = landed("b")
    wpo_g, wco_g, w_o = got["w_pool_out"], got["w_conv_out"], got["w_o"].reshape(D, D)
    mix, ya, yb = _mix_fwd(ya_pre, s, wpo_g, wco_g, gates, D)
    o = _mm_act_roww("attn_out", mix, w_o, 1088, 1024, 2048)
    spread("d")
    h1, u2 = _mid_norm(o, h0, vecs["g_post_mix"], vecs["g_pre_mlp"])
    wup_g = landed("c")["w_up"]
    a_up, fact = _mm_act_colw("mlp_up", u2, wup_g, 512, _up_epilogue, (BF16, BF16))
    w_down = landed("d")["w_down"].reshape(-1, D)
    f = _mm_act_roww("mlp_down", fact, w_down, 1088, 1024, 2048)
    dy, df, dg_post_mlp, loss = _loss_head(f, h1, tgt, vecs["g_post_mlp"])

    g_w_down = _mm_wgrad("dw_down", fact, df, 1024, 1024).reshape(N_CHIPS, -1, D)
    _CHAIN["after"] = [g_w_down]
    red1 = _Reduction("1", [g_w_down], ci, chip)
    da_up = _mlp_down_bwd(df, w_down, a_up)
    red1.partial()
    g_w_up = _mm_wgrad_colw("dw_up", u2, da_up, Q, 1024, 1024)
    red2 = _Reduction("2", [g_w_up], ci, chip)
    du2 = _mm_grad_colw_t("du2", da_up, wup_g, 1088, 1024)
    red2.partial()
    red1.total()
    dh1, do, dg_pre_mlp, dg_post_mix = _mid_norm_bwd(dy, du2, h1, o, vecs["g_pre_mlp"], vecs["g_post_mix"])
    update(("w_down",), red1.finish(), on_sparsecore=True)
    g_w_o = _mm_wgrad("dw_o", mix, do, 1024, 1024)
    dya, dyb, dga, dgb = _mix_bwd(do, w_o, gates, ya, yb)
    g_wpo = _mm_wgrad_colw("dw_pool_out", ya_pre, dya, Q, 1024, 512)
    g_wco = _mm_wgrad_colw("dw_conv_out", s, dyb, Q, 1024, 512)
    red3 = _Reduction("3", [g_w_o.reshape(N_CHIPS, D // N_CHIPS, D), g_wpo, g_wco], ci, chip)
    dya_pre = _mm_grad_colw_t("dya_pre", dya, wpo_g, 1088, 1024)
    ds = _mm_grad_colw_t("ds", dyb, wco_g, 1088, 1024)
    red3.partial()
    red2.total()
    dz, g_w_grp, dscale = _pool_bwd(dya_pre, d, w_grp, vecs["pool_scale"])
    update(("w_up",), red2.finish(), on_sparsecore=True)
    dc, dln_g, dln_b, db_dw = _conv_ln_bwd(ds, c, vecs["conv_ln_g"], vecs["conv_ln_b"])
    dv, dgc, g_w_dw = _conv_bwd(dc, proj, C, w_dw)
    dproj = _concat_cols("dproj", [dz, dv, dgc, dga, dgb])
    half_k = D // 2
    g_w_grp = g_w_grp.reshape(G, N_CHIPS, GS, GD).transpose(1, 0, 2, 3).reshape(N_CHIPS, G * GS, GD)
    g_in_a = _mm_wgrad_colw("dw_in_a", u1, dproj, Q, 512, 1792, rows=(0, half_k))
    red4a = _Reduction("4a", [g_in_a, g_w_grp], ci, chip)
    g_in_b = _mm_wgrad_colw("dw_in_b", u1, dproj, Q, 512, 1792, rows=(half_k, half_k))
    red4a.partial()
    red4b = _Reduction("4b", [g_in_b], ci, chip)
    du1 = _mm_grad_colw_t("du1", dproj, win_g, 1088, 1024)
    red4b.partial()
    red3.total()
    grad_x, dmeta, dg_pre_mix = _pre_norm_bwd(dh1, du1, h0, vecs["g_pre_mix"])

    g_vec = dict(g_pre_mix=dg_pre_mix, pool_scale=dscale, b_dw=db_dw, conv_ln_g=dln_g, conv_ln_b=dln_b,
                 g_post_mix=dg_post_mix, g_pre_mlp=dg_pre_mlp, g_post_mlp=dg_post_mlp)
    rows = [g_w_dw, _as_rows(dmeta, P)] + [_as_rows(g_vec[k], P) for k in VECTORS]
    rows.append(jnp.broadcast_to(loss[:, :1], (1, P)))
    packs = _comm_start("packs_start", [_pack_rows("pack_grads", rows, 2 * chip + ci, N_DEVICES)],
                        (N_DEVICES - 1,), _packs_plan)
    red4a.total()
    update(("w_o", "w_pool_out", "w_conv_out"), red3.finish())
    red_in_a, red_grp = red4a.finish()
    update(("w_pool_grp",), [red_grp])
    w_in_rows = (shard2d["w_in"], m["w_in"].reshape(shard2d["w_in"].shape), v["w_in"].reshape(shard2d["w_in"].shape))
    first_rows = _adamw_rows(w_in_rows[0], red_in_a, w_in_rows[1], w_in_rows[2], 0)
    total = _sum_packs(_comm_wait("packs_wait", packs[1], [(packs[0], _packs_plan)])[0])
    at = 0
    taps_pad = g_w_dw.shape[0]
    g_dw_full = total[at:at + CONV_TAPS]
    at += taps_pad
    g_meta_full = total[at:at + 2 * N_META].reshape(N_META, D)
    at += 2 * N_META
    for k in VECTORS:
        n = w[k].shape[-1] // P
        grads[k] = total[at:at + n].reshape(1, n * P)
        at += n
    loss_total = total[at, 0]
    grads["w_dw"] = lax.dynamic_slice_in_dim(g_dw_full, chip * (P // N_CHIPS), P // N_CHIPS, axis=1)
    grads["meta"] = lax.dynamic_slice_in_dim(g_meta_full, chip * (D // N_CHIPS), D // N_CHIPS, axis=1)

    sd, sm, sv = _adamw(small_w, pack_small(grads), small_m, small_v)
    at = 0
    for k in small_names:
        a = w[k].reshape(-1, w[k].shape[-1])
        n = -(-a.size // P)
        for tree, packed in ((delta, sd), (new_m, sm), (new_v, sv)):
            tree[k] = packed[at:at + n].reshape(-1)[:a.size].reshape(a.shape)
        at += n

    red4b.total()
    delta["w_in"], new_m["w_in"], new_v["w_in"], grads["w_in"] = _adamw_rows(
        w_in_rows[0], red4b.finish()[0], w_in_rows[1], w_in_rows[2], half_k, prev=first_rows)
    return loss_total, grad_x, grads, delta, new_m, new_v


def kernel(x, meta, g_pre_mix, w_in, w_pool_grp, pool_scale, w_pool_out, w_dw, b_dw, conv_ln_g, conv_ln_b, w_conv_out, w_o, g_post_mix, g_pre_mlp, w_up, w_down, g_post_mlp, loss_target, m_meta, m_g_pre_mix, m_w_in, m_w_pool_grp, m_pool_scale, m_w_pool_out, m_w_dw, m_b_dw, m_conv_ln_g, m_conv_ln_b, m_w_conv_out, m_w_o, m_g_post_mix, m_g_pre_mlp, m_w_up, m_w_down, m_g_post_mlp, v_meta, v_g_pre_mix, v_w_in, v_w_pool_grp, v_pool_scale, v_w_pool_out, v_w_dw, v_b_dw, v_conv_ln_g, v_conv_ln_b, v_w_conv_out, v_w_o, v_g_post_mix, v_g_pre_mlp, v_w_up, v_w_down, v_g_post_mlp):
    args = dict(locals())
    shapes = {k: args[k].shape for k in WEIGHTS}
    w = {k: args[k] for k in WEIGHTS}
    m = {k: args["m_" + k] for k in WEIGHTS}
    v = {k: args["v_" + k] for k in WEIGHTS}
    for tree in (w, m, v):
        tree["w_dw"] = tree["w_dw"].reshape(tree["w_dw"].shape[-2:])
    loss, grad_x, grads, delta, new_m, new_v = _step(w, m, v, x[0], loss_target[0])
    out = [loss, grad_x[None]]
    for tree in (grads, delta, new_m, new_v):
        out += [tree[k].reshape(shapes[k]) for k in WEIGHTS]
    return tuple(out)
```

```python
import math

import jax
import jax.numpy as jnp
from jax import lax
from jax.experimental import pallas as pl
from jax.experimental.pallas import tpu as pltpu
from jax.experimental.pallas import tpu_sc as plsc

F32 = jnp.float32
BF16 = jnp.bfloat16

N_META = 16
PAD_ROWS = 112
TOKEN_ROW0 = PAD_ROWS + N_META
POOL_GROUPS = 4
CONV_TAPS = 31
HALO = 32
CONV_ROWS = 128
LANES = 128
RMS_EPS = 1e-6
LN_EPS = 1e-5
ADAM_LR = 0.001
ADAM_B1 = 0.9
ADAM_B2 = 0.999
ADAM_EPS = 1e-08
ADAM_WD = 0.01
ADAM_STEP = 10
VMEM_LIMIT_MB = 56

MESH = pl.DeviceIdType.MESH
NN = (((1,), (0,)), ((), ()))
NT = (((1,), (1,)), ((), ()))
TN = (((0,), (0,)), ((), ()))


def _pick(n, pref):
    if n <= pref:
        return n
    if n % pref == 0:
        return pref
    for step in (LANES, 8, 1):
        t = (pref // step) * step
        while t >= step:
            if n % t == 0:
                return t
            t -= step
    return n


def _params(n_axes, vmem_mb=VMEM_LIMIT_MB):
    return pltpu.CompilerParams(dimension_semantics=("arbitrary",) * n_axes,
                                vmem_limit_bytes=vmem_mb << 20)


def _sigmoid(x):
    return jax.nn.sigmoid(x)


_CHAIN = {"after": []}


def _call(body, args, *, in_specs, out_specs, out_shape, grid=(), scalars=None, mark=0, follows=True, made_from=None,
          **kw):
    pending = _CHAIN["after"]
    after = pending if follows else []
    n = len(args)
    lead = 0 if scalars is None else 1
    specs = list(in_specs)
    operands = list(args)
    fn = body
    if after:
        def fn(*refs):
            body(*refs[:lead + n], *refs[lead + n + len(after):])
        specs += [pl.BlockSpec(memory_space=pl.ANY)] * len(after)
        operands += after
    if scalars is None:
        if grid:
            kw["grid"] = grid
        res = pl.pallas_call(fn, in_specs=specs, out_specs=out_specs, out_shape=out_shape, **kw)(*operands)
    else:
        grid_spec = pltpu.PrefetchScalarGridSpec(num_scalar_prefetch=1, grid=grid, in_specs=specs, out_specs=out_specs)
        res = pl.pallas_call(fn, grid_spec=grid_spec, out_shape=out_shape, **kw)(scalars, *operands)
    outs = res if isinstance(res, (list, tuple)) else [res]
    consumed = args if made_from is None else made_from
    kept = [] if follows else [m for m in pending if not any(m is a for a in consumed)]
    _CHAIN["after"] = kept + [outs[mark]]
    return res


def _store(val, extras, outs):
    outs[0][...] = val.astype(outs[0].dtype)


def _mm(name, grid, arrays, in_specs, out_shapes, out_specs, dims, nk, epilogue=_store, acc_shape=None):
    n_in, n_out = len(arrays), len(out_shapes)

    def body(*refs):
        extras = refs[2:n_in]
        outs = refs[n_in:n_in + n_out]
        part = lax.dot_general(refs[0][...], refs[1][...], dims, preferred_element_type=F32)
        if nk == 1:
            epilogue(part, extras, outs)
        else:
            acc = refs[n_in + n_out]
            k = pl.program_id(len(grid) - 1)

            @pl.when(k == 0)
            def _():
                acc[...] = part

            @pl.when(k > 0)
            def _():
                acc[...] += part

            @pl.when(k == nk - 1)
            def _():
                epilogue(acc[...], extras, outs)

    scratch = [pltpu.VMEM(acc_shape, F32)] if nk > 1 else []
    single = n_out == 1
    return _call(
        body, arrays, name=name, grid=grid, in_specs=in_specs,
        out_specs=out_specs[0] if single else out_specs,
        out_shape=out_shapes[0] if single else out_shapes,
        scratch_shapes=scratch, compiler_params=_params(len(grid)))


def _sds(shape, dtype):
    return jax.ShapeDtypeStruct(shape, dtype)


def _rms_scale(h):
    return lax.rsqrt(jnp.mean(h * h, axis=-1, keepdims=True) + RMS_EPS)


def _rms_bwd(du, h, g):
    r = _rms_scale(h)
    y = h * r
    dy = du * g
    dh = r * (dy - y * jnp.mean(dy * y, axis=-1, keepdims=True))
    return dh, jnp.sum(du * y, axis=0, keepdims=True)


def _row_tile(L):
    return _pick(L, 272)


def _pre_norm_tokens(x, g):
    S, D = x.shape
    T = TOKEN_ROW0
    L = S + T

    def body(x_ref, g_ref, h_ref, u_ref):
        h = x_ref[...]
        h_ref[...] = h
        u_ref[...] = (h * _rms_scale(h) * g_ref[...]).astype(BF16)

    below = pl.BlockSpec((T, D), lambda i: (i + 1, 0))
    return _call(
        body, (x, g), name="pre_norm_tokens", grid=(S // T,),
        in_specs=[pl.BlockSpec((T, D), lambda i: (i, 0)), pl.BlockSpec((1, D), lambda i: (0, 0))],
        out_specs=[below, below], out_shape=[_sds((L, D), F32), _sds((L, D), BF16)], compiler_params=_params(1))


def _pre_norm_meta(meta, g, h0, u1):
    L, D = h0.shape
    T = TOKEN_ROW0

    def body(m_ref, g_ref, h_in, u_in, h_ref, u_ref):
        h_ref[...] = jnp.zeros(h_ref.shape, F32)
        h_ref[PAD_ROWS:, :] = m_ref[...]
        h = h_ref[...]
        u_ref[...] = (h * _rms_scale(h) * g_ref[...]).astype(BF16)

    first = pl.BlockSpec((T, D), lambda i: (0, 0))
    return _call(
        body, (meta, g, h0, u1), name="pre_norm_meta", grid=(1,),
        in_specs=[pl.BlockSpec((N_META, D), lambda i: (0, 0)), pl.BlockSpec((1, D), lambda i: (0, 0)), ANY, ANY],
        out_specs=[first, first], out_shape=[_sds((L, D), F32), _sds((L, D), BF16)],
        input_output_aliases={2: 0, 3: 1}, compiler_params=_params(1))


def _mid_norm(o, h0, g_post, g_pre):
    L, D = h0.shape
    T = _row_tile(L)

    def body(o_ref, h_ref, gp_ref, gm_ref, h1_ref, u2_ref):
        o_ = o_ref[...]
        h1 = h_ref[...] + o_ * _rms_scale(o_) * gp_ref[...]
        h1_ref[...] = h1
        u2_ref[...] = (h1 * _rms_scale(h1) * gm_ref[...]).astype(BF16)

    row = pl.BlockSpec((T, D), lambda i: (i, 0))
    vec = pl.BlockSpec((1, D), lambda i: (0, 0))
    return _call(
        body, (o, h0, g_post, g_pre), name="mid_norm", grid=(L // T,),
        in_specs=[row, row, vec, vec], out_specs=[row, row],
        out_shape=[_sds((L, D), F32), _sds((L, D), BF16)], compiler_params=_params(1))


def _loss_head(f, h1, tgt, g_post):
    L, D = h1.shape
    T = TOKEN_ROW0
    n = L // T

    def body(f_ref, h_ref, t_ref, g_ref, dy_ref, df_ref, dg_ref, loss_ref):
        i = pl.program_id(0)
        f_ = f_ref[...]
        g = g_ref[...]
        y = h_ref[...] + f_ * _rms_scale(f_) * g
        live = (i > 0).astype(F32)
        diff = (y - t_ref[...]) * live
        part = 0.5 * jnp.sum(jnp.mean(diff * diff, axis=-1, keepdims=True), axis=0, keepdims=True)
        dy = diff * (1.0 / D)
        dy_ref[...] = dy
        df, dg = _rms_bwd(dy, f_, g)
        df_ref[...] = df.astype(BF16)

        @pl.when(i == 0)
        def _():
            dg_ref[...] = dg
            loss_ref[...] = jnp.broadcast_to(part, loss_ref.shape)

        @pl.when(i > 0)
        def _():
            dg_ref[...] += dg
            loss_ref[...] += jnp.broadcast_to(part, loss_ref.shape)

    row = pl.BlockSpec((T, D), lambda i: (i, 0))
    vec = pl.BlockSpec((1, D), lambda i: (0, 0))
    return _call(
        body, (f, h1, tgt, g_post), name="loss_head", grid=(n,),
        in_specs=[row, row, pl.BlockSpec((T, D), lambda i: (jnp.maximum(i - 1, 0), 0)), vec],
        out_specs=[row, row, vec, pl.BlockSpec((1, LANES), lambda i: (0, 0))],
        out_shape=[_sds((L, D), F32), _sds((L, D), BF16), _sds((1, D), F32), _sds((1, LANES), F32)],
        compiler_params=_params(1))


def _mid_norm_bwd(dy, du2, h1, o, g_pre, g_post):
    L, D = h1.shape
    T = _row_tile(L)

    def body(dy_ref, du_ref, h_ref, o_ref, gm_ref, gp_ref, dh1_ref, do_ref, dgm_ref, dgp_ref):
        i = pl.program_id(0)
        dh, dgm = _rms_bwd(du_ref[...], h_ref[...], gm_ref[...])
        dh1 = dy_ref[...] + dh
        dh1_ref[...] = dh1
        do, dgp = _rms_bwd(dh1, o_ref[...], gp_ref[...])
        do_ref[...] = do.astype(BF16)

        @pl.when(i == 0)
        def _():
            dgm_ref[...] = dgm
            dgp_ref[...] = dgp

        @pl.when(i > 0)
        def _():
            dgm_ref[...] += dgm
            dgp_ref[...] += dgp

    row = pl.BlockSpec((T, D), lambda i: (i, 0))
    vec = pl.BlockSpec((1, D), lambda i: (0, 0))
    return _call(
        body, (dy, du2, h1, o, g_pre, g_post), name="mid_norm_bwd", grid=(L // T,),
        in_specs=[row, row, row, row, vec, vec], out_specs=[row, row, vec, vec],
        out_shape=[_sds((L, D), F32), _sds((L, D), BF16), _sds((1, D), F32), _sds((1, D), F32)],
        compiler_params=_params(1))


def _pre_norm_bwd(dh1, du1, h0, g):
    L, D = h0.shape
    T = TOKEN_ROW0
    n = L // T

    def body(dh_ref, du_ref, h_ref, g_ref, gx_ref, dmeta_ref, dg_ref):
        i = pl.program_id(0)
        dh, dg = _rms_bwd(du_ref[...], h_ref[...], g_ref[...])
        dh0 = dh_ref[...] + dh
        gx_ref[...] = dh0

        @pl.when(i == 0)
        def _():
            dmeta_ref[...] = dh0[PAD_ROWS:, :]
            dg_ref[...] = dg

        @pl.when(i > 0)
        def _():
            dg_ref[...] += dg

    row = pl.BlockSpec((T, D), lambda i: (i, 0))
    vec = pl.BlockSpec((1, D), lambda i: (0, 0))
    return _call(
        body, (dh1, du1, h0, g), name="pre_norm_bwd", grid=(n,),
        in_specs=[row, row, row, vec],
        out_specs=[pl.BlockSpec((T, D), lambda i: (jnp.maximum(i - 1, 0), 0)),
                   pl.BlockSpec((N_META, D), lambda i: (0, 0)), vec],
        out_shape=[_sds((L - T, D), F32), _sds((N_META, D), F32), _sds((1, D), F32)],
        compiler_params=_params(1))


def _window_sum(z, g, shift_sign, L):
    s = z
    for j in range(POOL_GROUPS):
        k = 1 << j
        nxt = s + pltpu.roll(s, k if shift_sign > 0 else L - k, 0)
        s = jnp.where(j <= g, nxt, s)
    return s


def _inv_count(g, L):
    t = lax.broadcasted_iota(jnp.int32, (L, 1), 0)
    w = jnp.left_shift(2, g)
    cnt = jnp.clip(t - (PAD_ROWS - 1), 1, w)
    return 1.0 / cnt.astype(F32)


def _pool_fwd(proj, w_grp, scale):
    L = proj.shape[0]
    G, GD, _ = w_grp.shape
    P = G * GD

    def body(z_ref, w_ref, sc_ref, d_ref, ya_ref):
        g = pl.program_id(0)
        z = z_ref[...]
        d = (_window_sum(z, g, +1, L) * _inv_count(g, L) - z).astype(BF16)
        d_ref[...] = d
        y = jnp.dot(d, w_ref[...], preferred_element_type=F32)
        ya_ref[...] = (y * sc_ref[...]).astype(BF16)

    col = pl.BlockSpec((L, GD), lambda g: (0, g))
    return _call(
        body, (proj, w_grp, scale), name="pool_fwd", grid=(G,),
        in_specs=[col, pl.BlockSpec((None, GD, GD), lambda g: (g, 0, 0)), pl.BlockSpec((1, GD), lambda g: (0, g))],
        out_specs=[col, col], out_shape=[_sds((L, P), BF16), _sds((L, P), BF16)],
        compiler_params=_params(1))


def _pool_bwd(dya, d, w_grp, scale):
    L, P = dya.shape
    G, GD, _ = w_grp.shape

    def body(dya_ref, d_ref, w_ref, sc_ref, dz_ref, dw_ref, dsc_ref):
        g = pl.program_id(0)
        dya_ = dya_ref[...]
        d_ = d_ref[...]
        w = w_ref[...]
        y = jnp.dot(d_, w, preferred_element_type=F32)
        dsc_ref[...] = jnp.sum(dya_ * y, axis=0, keepdims=True)
        dy = (dya_ * sc_ref[...]).astype(BF16)
        dw_ref[...] = lax.dot_general(d_, dy, TN, preferred_element_type=F32)
        dd = lax.dot_general(dy, w, NT, preferred_element_type=F32)
        dz = _window_sum(dd * _inv_count(g, L), g, -1, L) - dd
        dz_ref[...] = dz.astype(BF16)

    col = pl.BlockSpec((L, GD), lambda g: (0, g))
    wspec = pl.BlockSpec((None, GD, GD), lambda g: (g, 0, 0))
    vec = pl.BlockSpec((1, GD), lambda g: (0, g))
    return _call(
        body, (dya, d, w_grp, scale), name="pool_bwd", grid=(G,),
        in_specs=[col, col, wspec, vec], out_specs=[col, wspec, vec],
        out_shape=[_sds((L, P), BF16), _sds((G, GD, GD), F32), _sds((1, P), F32)],
        compiler_params=_params(1))


def _fill_rotations(rot_ref, ext):
    n = ext.shape[0]
    rot_ref[0] = ext
    for r in range(1, 8):
        rot_ref[r] = pltpu.roll(ext, n - r, 0)


def _lane_chunks(C):
    step = LANES if C % LANES == 0 else C
    return [(c0, step) for c0 in range(0, C, step)]


def _conv_specs(L, C, col_v, col_g):
    T = CONV_ROWS
    per = T // HALO
    cur_v = pl.BlockSpec((T, C), lambda i: (i, col_v))
    cur_g = pl.BlockSpec((T, C), lambda i: (i, col_g))
    prev_v = pl.BlockSpec((HALO, C), lambda i: (jnp.maximum(i * per - 1, 0), col_v))
    prev_g = pl.BlockSpec((HALO, C), lambda i: (jnp.maximum(i * per - 1, 0), col_g))
    return cur_v, cur_g, prev_v, prev_g


def _glu_ext(vc, gc, vh, gh, i):
    a_cur = vc[...] * _sigmoid(gc[...])
    a_prev = vh[...] * _sigmoid(gh[...]) * (i > 0).astype(F32)
    return jnp.concatenate([a_prev, a_cur], axis=0)


def _conv_fwd(proj, C, w_dw, b_dw, ln_g, ln_b):
    L = proj.shape[0]
    T = CONV_ROWS
    P = C

    def body(vc, gc, vh, gh, w_ref, b_ref, lg_ref, lb_ref, s_ref, c_ref, rot):
        i = pl.program_id(0)
        _fill_rotations(rot, _glu_ext(vc, gc, vh, gh, i))
        for c0, cw in _lane_chunks(C):
            acc = jnp.zeros((T, cw), F32)
            for k in range(CONV_TAPS):
                q, r = divmod(HALO - (CONV_TAPS - 1) + k, 8)
                acc = acc + w_ref[k:k + 1, c0:c0 + cw] * rot[r, 8 * q:8 * q + T, c0:c0 + cw]
            c_ref[:, c0:c0 + cw] = acc + b_ref[:, c0:c0 + cw]
        c = c_ref[...]
        mu = jnp.mean(c, axis=-1, keepdims=True)
        cen = c - mu
        var = jnp.mean(cen * cen, axis=-1, keepdims=True)
        ln = cen * lax.rsqrt(var + LN_EPS) * lg_ref[...] + lb_ref[...]
        s_ref[...] = (ln * _sigmoid(ln)).astype(BF16)

    cur_v, cur_g, prev_v, prev_g = _conv_specs(L, C, P // C, P // C + 1)
    row = pl.BlockSpec((T, C), lambda i: (i, 0))
    vec = pl.BlockSpec((1, C), lambda i: (0, 0))
    return _call(
        body, (proj, proj, proj, proj, w_dw, b_dw, ln_g, ln_b), name="conv_fwd", grid=(L // T,),
        in_specs=[cur_v, cur_g, prev_v, prev_g, pl.BlockSpec((CONV_TAPS, C), lambda i: (0, 0)), vec, vec, vec],
        out_specs=[row, row], out_shape=[_sds((L, C), BF16), _sds((L, C), F32)],
        scratch_shapes=[pltpu.VMEM((8, T + HALO, C), F32)], compiler_params=_params(1))


def _conv_ln_bwd(ds, c, ln_g, ln_b):
    L, C = c.shape
    T = _row_tile(L)

    def body(ds_ref, c_ref, lg_ref, lb_ref, dc_ref, dlg_ref, dlb_ref, db_ref):
        i = pl.program_id(0)
        c_ = c_ref[...]
        g = lg_ref[...]
        mu = jnp.mean(c_, axis=-1, keepdims=True)
        cen = c_ - mu
        rstd = lax.rsqrt(jnp.mean(cen * cen, axis=-1, keepdims=True) + LN_EPS)
        xhat = cen * rstd
        ln = xhat * g + lb_ref[...]
        sg = _sigmoid(ln)
        dln = ds_ref[...] * (sg * (1.0 + ln * (1.0 - sg)))
        dxh = dln * g
        dc = rstd * (dxh - jnp.mean(dxh, axis=-1, keepdims=True)
                     - xhat * jnp.mean(dxh * xhat, axis=-1, keepdims=True))
        dc_ref[...] = dc
        dlg = jnp.sum(dln * xhat, axis=0, keepdims=True)
        dlb = jnp.sum(dln, axis=0, keepdims=True)
        db = jnp.sum(dc, axis=0, keepdims=True)

        @pl.when(i == 0)
        def _():
            dlg_ref[...] = dlg
            dlb_ref[...] = dlb
            db_ref[...] = db

        @pl.when(i > 0)
        def _():
            dlg_ref[...] += dlg
            dlb_ref[...] += dlb
            db_ref[...] += db

    row = pl.BlockSpec((T, C), lambda i: (i, 0))
    vec = pl.BlockSpec((1, C), lambda i: (0, 0))
    return _call(
        body, (ds, c, ln_g, ln_b), name="conv_ln_bwd", grid=(L // T,),
        in_specs=[row, row, vec, vec], out_specs=[row, vec, vec, vec],
        out_shape=[_sds((L, C), F32), _sds((1, C), F32), _sds((1, C), F32), _sds((1, C), F32)],
        compiler_params=_params(1))


def _conv_bwd(dc, proj, C, w_dw):
    L = proj.shape[0]
    T = CONV_ROWS
    per = T // HALO
    n = L // T
    P = C
    taps_pad = 32

    def body(dcc, dcn, vc, gc, w_ref, dv_ref, dg_ref, dw_ref, rot_d, dw_acc):
        i = pl.program_id(0)
        dc_next = dcn[...] * (i < n - 1).astype(F32)
        _fill_rotations(rot_d, jnp.concatenate([dcc[...], dc_next], axis=0))

        @pl.when(i == 0)
        def _():
            dw_acc[...] = jnp.zeros(dw_acc.shape, F32)

        for c0, cw in _lane_chunks(C):
            v = vc[:, c0:c0 + cw]
            sg = _sigmoid(gc[:, c0:c0 + cw])
            a = v * sg
            da = jnp.zeros((T, cw), F32)
            for k in range(CONV_TAPS):
                q, r = divmod(CONV_TAPS - 1 - k, 8)
                slab = rot_d[r, 8 * q:8 * q + T, c0:c0 + cw]
                da = da + w_ref[k:k + 1, c0:c0 + cw] * slab
                dw_acc[k, :, c0:c0 + cw] += jnp.sum((a * slab).reshape(T // 8, 8, cw), axis=0)
            dv_ref[:, c0:c0 + cw] = (da * sg).astype(BF16)
            dg_ref[:, c0:c0 + cw] = (da * v * sg * (1.0 - sg)).astype(BF16)

        @pl.when(i == n - 1)
        def _():
            dw_ref[...] = jnp.sum(dw_acc[...], axis=1)

    cur_v, cur_g, _, _ = _conv_specs(L, C, P // C, P // C + 1)
    row = pl.BlockSpec((T, C), lambda i: (i, 0))
    nxt = pl.BlockSpec((HALO, C), lambda i: (jnp.minimum((i + 1) * per, L // HALO - 1), 0))
    wspec = pl.BlockSpec((CONV_TAPS, C), lambda i: (0, 0))
    return _call(
        body, (dc, dc, proj, proj, w_dw), name="conv_bwd", grid=(n,),
        in_specs=[row, nxt, cur_v, cur_g, wspec],
        out_specs=[row, row, pl.BlockSpec((taps_pad, C), lambda i: (0, 0))],
        out_shape=[_sds((L, C), BF16), _sds((L, C), BF16), _sds((taps_pad, C), F32)],
        scratch_shapes=[pltpu.VMEM((8, T + HALO, C), F32), pltpu.VMEM((taps_pad, 8, C), F32)],
        compiler_params=_params(1))


def _mix_fwd(ya_pre, s, wpo, wco, gates, D):
    L, P = ya_pre.shape
    Q, _, DS = wpo.shape
    bm = _pick(L, 1088)
    per = D // DS

    def body(a1, a2, b1, b2, ga, gb, m_ref, ya_ref, yb_ref):
        ya = jnp.dot(a1[...], b1[...], preferred_element_type=F32)
        yb = jnp.dot(a2[...], b2[...], preferred_element_type=F32)
        ya_ref[...] = ya.astype(BF16)
        yb_ref[...] = yb.astype(BF16)
        m_ref[...] = (_sigmoid(ga[...].astype(F32)) * ya + _sigmoid(gb[...].astype(F32)) * yb).astype(BF16)

    act = pl.BlockSpec((bm, P), lambda i, q: (i, 0))
    wsp = pl.BlockSpec((None, P, DS), lambda i, q: (q, 0, 0))
    out = pl.BlockSpec((bm, DS), lambda i, q: (i, q))
    return _call(
        body, (ya_pre, s, wpo, wco, gates, gates), name="mix_fwd", grid=(L // bm, Q),
        in_specs=[act, act, wsp, wsp,
                  pl.BlockSpec((bm, DS), lambda i, q: (i, q)),
                  pl.BlockSpec((bm, DS), lambda i, q: (i, per + q))],
        out_specs=[out, out, out],
        out_shape=[_sds((L, D), BF16), _sds((L, D), BF16), _sds((L, D), BF16)],
        compiler_params=_params(2))


def _mix_bwd(do, w_o, gates, ya, yb):
    L, D = do.shape
    bm = _pick(L, 544)
    bn = _pick(D, 512)
    per = D // bn

    def epilogue(dm, extras, outs):
        ga, gb, ya_ref, yb_ref = extras
        sa = _sigmoid(ga[...].astype(F32))
        sb = _sigmoid(gb[...].astype(F32))
        outs[0][...] = (dm * sa).astype(BF16)
        outs[1][...] = (dm * sb).astype(BF16)
        outs[2][...] = (dm * ya_ref[...].astype(F32) * sa * (1.0 - sa)).astype(BF16)
        outs[3][...] = (dm * yb_ref[...].astype(F32) * sb * (1.0 - sb)).astype(BF16)

    blk = pl.BlockSpec((bm, bn), lambda i, j: (i, j))
    return _mm(
        "mix_bwd", (L // bm, D // bn), [do, w_o, gates, gates, ya, yb],
        [pl.BlockSpec((bm, D), lambda i, j: (i, 0)), pl.BlockSpec((bn, D), lambda i, j: (j, 0)),
         blk, pl.BlockSpec((bm, bn), lambda i, j: (i, per + j)), blk, blk],
        [_sds((L, D), BF16)] * 4, [blk] * 4, NT, 1, epilogue)


def _mm_act_colw(name, a, wg, bn_pref, epilogue=_store, out_dtypes=(F32,)):
    L, K = a.shape
    Q, _, n = wg.shape
    bn = _pick(n, bn_pref)
    nj = n // bn
    out = pl.BlockSpec((L, bn), lambda q, j: (0, q * nj + j))
    return _mm(name, (Q, nj), [a, wg],
               [pl.BlockSpec((L, K), lambda q, j: (0, 0)), pl.BlockSpec((None, K, bn), lambda q, j: (q, 0, j))],
               [_sds((L, Q * n), dt) for dt in out_dtypes], [out] * len(out_dtypes), NN, 1, epilogue)


def _proj(u1, wg, n_front):
    L, K = u1.shape
    Q, _, n = wg.shape
    bn = _pick(math.gcd(n, n_front), 256)
    nj = n // bn
    nf = n_front // bn

    def epilogue(val, extras, outs):
        col = pl.program_id(0) * nj + pl.program_id(1)

        @pl.when(col < nf)
        def _():
            outs[0][...] = val

        @pl.when(col >= nf)
        def _():
            outs[1][...] = val.astype(BF16)

    front = pl.BlockSpec((L, bn), lambda q, j: (0, jnp.minimum(q * nj + j, nf - 1)))
    gates = pl.BlockSpec((L, bn), lambda q, j: (0, jnp.maximum(q * nj + j - nf, 0)))
    return _mm("proj", (Q, nj), [u1, wg],
               [pl.BlockSpec((L, K), lambda q, j: (0, 0)), pl.BlockSpec((None, K, bn), lambda q, j: (q, 0, j))],
               [_sds((L, n_front), F32), _sds((L, Q * n - n_front), BF16)], [front, gates], NN, 1, epilogue)


def _mm_grad_colw_t(name, g, wg, bm_pref, bn_pref):
    L = g.shape[0]
    Q, K, n = wg.shape
    bm = _pick(L, bm_pref)
    bn = _pick(K, bn_pref)
    return _mm(name, (L // bm, K // bn, Q), [g, wg],
               [pl.BlockSpec((bm, n), lambda i, j, k: (i, k)), pl.BlockSpec((None, bn, n), lambda i, j, k: (k, j, 0))],
               [_sds((L, K), F32)], [pl.BlockSpec((bm, bn), lambda i, j, k: (i, j))], NT, Q,
               acc_shape=(bm, bn))


def _mm_wgrad_colw(name, a, g, Q, bm_pref, bn_pref, rows=None):
    L, K = a.shape
    n = g.shape[1] // Q
    first, count = rows or (0, K)
    bm = _pick(count, bm_pref)
    bn = _pick(n, bn_pref)
    nj = n // bn
    i0 = first // bm
    return _mm(name, (Q, count // bm, nj), [a, g],
               [pl.BlockSpec((L, bm), lambda q, i, j: (0, i0 + i)),
                pl.BlockSpec((L, bn), lambda q, i, j: (0, q * nj + j))],
               [_sds((Q, count, n), F32)], [pl.BlockSpec((None, bm, bn), lambda q, i, j: (q, i, j))], TN, 1)


def _mm_wgrad(name, a, g, bm_pref, bn_pref):
    L, K = a.shape
    N = g.shape[1]
    bm = _pick(K, bm_pref)
    bn = _pick(N, bn_pref)
    return _mm(name, (K // bm, N // bn), [a, g],
               [pl.BlockSpec((L, bm), lambda i, j: (0, i)), pl.BlockSpec((L, bn), lambda i, j: (0, j))],
               [_sds((K, N), F32)], [pl.BlockSpec((bm, bn), lambda i, j: (i, j))], TN, 1)


def _mm_act_roww(name, a, w, bm_pref, bn_pref, bk_pref):
    L, K = a.shape
    N = w.shape[1]
    bm, bn, bk = _pick(L, bm_pref), _pick(N, bn_pref), _pick(K, bk_pref)
    nk = K // bk
    return _mm(name, (L // bm, N // bn, nk), [a, w],
               [pl.BlockSpec((bm, bk), lambda i, j, k: (i, k)), pl.BlockSpec((bk, bn), lambda i, j, k: (k, j))],
               [_sds((L, N), F32)], [pl.BlockSpec((bm, bn), lambda i, j, k: (i, j))], NN, nk,
               acc_shape=(bm, bn))


def _concat_cols(name, parts):
    L = parts[0].shape[0]
    T = _row_tile(L)
    widths = [p.shape[1] for p in parts]

    def body(*refs):
        at = 0
        for ref, wd in zip(refs[:-1], widths):
            refs[-1][:, at:at + wd] = ref[...]
            at += wd

    return _call(
        body, list(parts), name=name, grid=(L // T,),
        in_specs=[pl.BlockSpec((T, wd), lambda i: (i, 0)) for wd in widths],
        out_specs=pl.BlockSpec((T, sum(widths)), lambda i: (i, 0)),
        out_shape=_sds((L, sum(widths)), parts[0].dtype), compiler_params=_params(1))


def _up_epilogue(val, extras, outs):
    outs[0][...] = val.astype(BF16)
    r = jnp.maximum(val, 0.0)
    outs[1][...] = (r * r).astype(BF16)


def _mlp_down_bwd(df, w_down, a_up):
    L, D = df.shape
    F = w_down.shape[0]
    bm = _pick(L, 1088)
    bn = _pick(F, 1024)

    def epilogue(val, extras, outs):
        outs[0][...] = (val * (2.0 * jnp.maximum(extras[0][...].astype(F32), 0.0))).astype(BF16)

    blk = pl.BlockSpec((bm, bn), lambda i, j: (i, j))
    return _mm("mlp_down_bwd", (L // bm, F // bn), [df, w_down, a_up],
               [pl.BlockSpec((bm, D), lambda i, j: (i, 0)), pl.BlockSpec((bn, D), lambda i, j: (j, 0)), blk],
               [_sds((L, F), BF16)], [blk], NT, 1, epilogue)


ANY = pl.BlockSpec(memory_space=pl.ANY)
HBM = pl.BlockSpec(memory_space=pltpu.HBM)
SEM = pl.BlockSpec(memory_space=pltpu.SEMAPHORE)
EFFECT = pltpu.SideEffectType.DATAFLOW_SIDE_EFFECTING
N_CHIPS = 4


def _place():
    x, y, c = lax.axis_index("x"), lax.axis_index("y"), lax.axis_index("c")
    return x, y, c


def _chip_at(x, y, k):
    px = 1 - x if k & 2 else x
    py = 1 - y if k & 1 else y
    return px, py


def _cast_into_slab(w2d, chip, dtype):
    R, C = w2d.shape
    T = _elem_tile(R, C)

    def body(p_ref, w_ref, o_ref):
        o_ref[...] = w_ref[...].astype(dtype)

    return _call(
        body, (w2d,), name="cast_into_slab", grid=(R // T,), scalars=jnp.reshape(chip, (1,)).astype(jnp.int32),
        in_specs=[pl.BlockSpec((T, C), lambda i, p: (i, 0))],
        out_specs=pl.BlockSpec((None, T, C), lambda i, p: (p[0], i, 0)),
        out_shape=_sds((N_CHIPS, R, C), dtype), compiler_params=_params(1))


TOKEN = jax.ShapeDtypeStruct((8, LANES), F32)


class _Sems:
    def __init__(self, items, shape):
        self.items, self.shape = list(items), tuple(shape)

    def pair(self, idx):
        flat = 0
        for i, n in zip(idx, self.shape):
            flat = flat * n + i
        half = len(self.items) // 2
        return self.items[flat], self.items[half + flat]


def _sem_count(shape):
    n = 1
    for s in shape:
        n *= s
    return n


def _remote(src, dst, sems, idx, device):
    send, recv = sems.pair(idx)
    return pltpu.make_async_remote_copy(src_ref=src, dst_ref=dst, send_sem=send, recv_sem=recv,
                                        device_id=device, device_id_type=MESH)


def _thru(arrays):
    return ([pltpu.with_memory_space_constraint(a, pltpu.HBM) for a in arrays],
            [pltpu.HBM(a.shape, a.dtype) for a in arrays])


def _comm_start(name, arrays, sem_shape, plan, follows=False):
    na, ns = len(arrays), 2 * _sem_count(sem_shape)

    def body(*refs):
        sems, token = _Sems(refs[na:na + ns], sem_shape), refs[-1]
        for src, dst, idx, device in plan(refs[:na])[0]:
            _remote(src, dst, sems, idx, device).start()
        token[...] = jnp.zeros(token.shape, F32)

    ins, outs = _thru(arrays)
    res = _call(
        body, ins, name=name, in_specs=[HBM] * na, mark=-1, follows=follows, made_from=arrays,
        out_specs=[SEM] * ns + [HBM] * na + [pl.BlockSpec(memory_space=pltpu.VMEM)],
        out_shape=[pltpu.SemaphoreType.DMA(())] * ns + outs + [TOKEN],
        input_output_aliases={a: ns + a for a in range(na)},
        compiler_params=pltpu.CompilerParams(has_side_effects=EFFECT))
    return _Sems(res[:ns], sem_shape), list(res[ns:ns + na])


def _wait_plans(refs, sem_refs, waits):
    x, y, c = _place()
    at = 0
    for sems, plan in waits:
        here = _Sems(sem_refs[at:at + len(sems.items)], sems.shape)
        at += len(sems.items)
        _, mine, arrivals = plan(refs)
        for dst, idx in arrivals:
            _remote(dst, dst, here, idx, (x, y, c)).wait_recv()
        for src, idx in mine:
            _remote(src, src, here, idx, (x, y, c)).wait_send()


def _comm_wait(name, arrays, waits):
    na = len(arrays)
    sem_items = [s for sems, _ in waits for s in sems.items]
    ns = len(sem_items)

    def body(*refs):
        _wait_plans(refs[:na], refs[na:na + ns], waits)
        refs[-1][...] = jnp.zeros(refs[-1].shape, F32)

    ins, outs = _thru(arrays)
    res = _call(
        body, ins + sem_items, name=name, in_specs=[HBM] * na + [SEM] * ns, mark=-1,
        out_specs=[HBM] * na + [pl.BlockSpec(memory_space=pltpu.VMEM)], out_shape=outs + [TOKEN],
        input_output_aliases={a: a for a in range(na)},
        compiler_params=pltpu.CompilerParams(has_side_effects=EFFECT))
    return list(res[:na])


def _comm_relay(name, arrays, sems, plan, sem_shape, next_plan):
    na, ns_in, ns_out = len(arrays), len(sems.items), 2 * _sem_count(sem_shape)

    def body(*refs):
        bufs = refs[:na]
        sems_in = _Sems(refs[na:na + ns_in], sems.shape)
        sems_out = _Sems(refs[na + ns_in:na + ns_in + ns_out], sem_shape)
        x, y, c = _place()
        _, mine, arrivals = plan(bufs)
        onward = next_plan(bufs)[0]
        for dst, idx in arrivals:
            _remote(dst, dst, sems_in, idx, (x, y, c)).wait_recv()
            for src, to, idx2, device, after_idx in onward:
                if after_idx == idx:
                    _remote(src, to, sems_out, idx2, device).start()
        for src, idx in mine:
            _remote(src, src, sems_in, idx, (x, y, c)).wait_send()
        refs[-1][...] = jnp.zeros(refs[-1].shape, F32)

    ins, outs = _thru(arrays)
    res = _call(
        body, ins + sems.items, name=name, in_specs=[HBM] * na + [SEM] * ns_in, mark=-1,
        out_specs=[SEM] * ns_out + [HBM] * na + [pl.BlockSpec(memory_space=pltpu.VMEM)],
        out_shape=[pltpu.SemaphoreType.DMA(())] * ns_out + outs + [TOKEN],
        input_output_aliases={a: ns_out + a for a in range(na)},
        compiler_params=pltpu.CompilerParams(has_side_effects=EFFECT))
    return _Sems(res[:ns_out], sem_shape), list(res[ns_out:ns_out + na])


def _half(ref, q, which):
    h = ref.shape[1] // 2
    return ref.at[q, pl.ds(which * h, h)]


def _quarter(ref, q, half, which):
    h = ref.shape[1] // 2
    return ref.at[q, pl.ds(half * h + which * (h // 2), h // 2)]


def _gather_plan(n_halved):
    def plan(refs):
        x, y, c = _place()
        p = 2 * x + y
        starts, mine, arrivals = [], [], []
        for n, ref in enumerate(refs):
            for k in range(1, N_CHIPS if n >= n_halved else 3):
                px, py = _chip_at(x, y, k)
                q = 2 * px + py
                out = _half(ref, p, c) if n < n_halved else ref.at[p]
                inc = _half(ref, q, c) if n < n_halved else ref.at[q]
                starts.append((out, out, (n, k - 1), (px, py, c)))
                mine.append((out, (n, k - 1)))
                arrivals.append((inc, (n, k - 1)))
        return starts, mine, arrivals
    return plan


def _spread_plan(n_halved, part):
    def plan(refs):
        x, y, c = _place()
        p = 2 * x + y
        starts, mine, arrivals = [], [], []
        for n in range(n_halved):
            for k in (1, 2):
                px, py = _chip_at(x, y, k)
                q = 2 * px + py
                tx, ty = _chip_at(x, y, 3 - k)
                dx, dy = _chip_at(x, y, 3)
                piece = _quarter(refs[n], q, c, 2 - k)
                landed = _half(refs[n], q, c)
                starts.append((piece, piece, (0, n, k - 1), (tx, ty, c), (n, k - 1)))
                starts.append((landed, landed, (1, n, k - 1), (x, y, 1 - c), (n, k - 1)))
                if part != 1:
                    mine.append((piece, (0, n, k - 1)))
                    arrivals.append((_quarter(refs[n], 2 * dx + dy, c, k - 1), (0, n, 2 - k)))
                if part != 0:
                    mine.append((landed, (1, n, k - 1)))
                    arrivals.append((_half(refs[n], q, 1 - c), (1, n, k - 1)))
        return starts, mine, arrivals
    return plan


def _last_hand_on_plan(n_halved):
    def plan(refs):
        x, y, c = _place()
        dx, dy = _chip_at(x, y, 3)
        d = 2 * dx + dy
        starts, mine, arrivals = [], [], []
        for n in range(n_halved):
            for k in (1, 2):
                piece = _quarter(refs[n], d, c, k - 1)
                starts.append((piece, piece, (n, k - 1), (x, y, 1 - c), (0, n, 2 - k)))
                mine.append((piece, (n, k - 1)))
                arrivals.append((_quarter(refs[n], d, 1 - c, k - 1), (n, k - 1)))
        return starts, mine, arrivals
    return plan


def _swap_plan(n):
    def plan(refs):
        x, y, c = _place()
        starts, mine, arrivals = [], [], []
        for a in range(n):
            h = refs[a].shape[1] // 2
            src = refs[a].at[:, pl.ds((1 - c) * h, h)]
            starts.append((src, refs[n + a], (a,), (x, y, 1 - c)))
            mine.append((src, (a,)))
            arrivals.append((refs[n + a], (a,)))
        return starts, mine, arrivals
    return plan


def _scatter_plan(n):
    def plan(refs):
        x, y, c = _place()
        starts, mine, arrivals = [], [], []
        for a in range(n):
            for k in range(1, N_CHIPS):
                px, py = _chip_at(x, y, k)
                src = refs[a].at[2 * px + py]
                starts.append((src, refs[n + a].at[k - 1], (a, k - 1), (px, py, c)))
                mine.append((src, (a, k - 1)))
                arrivals.append((refs[n + a].at[k - 1], (a, k - 1)))
        return starts, mine, arrivals
    return plan


def _share_plan(n):
    def plan(refs):
        x, y, c = _place()
        starts, mine, arrivals = [], [], []
        for a in range(n):
            h = refs[a].shape[0] // 2
            own = refs[a].at[pl.ds(c * h, h)]
            starts.append((own, own, (a,), (x, y, 1 - c)))
            mine.append((own, (a,)))
            arrivals.append((refs[a].at[pl.ds((1 - c) * h, h)], (a,)))
        return starts, mine, arrivals
    return plan


N_DEVICES = 8


def _packs_plan(refs):
    buf = refs[0]
    x, y, c = _place()
    me = 4 * x + 2 * y + c
    starts, mine, arrivals = [], [], []
    for r in range(1, N_DEVICES):
        peer = (1 - x if r & 4 else x, 1 - y if r & 2 else y, 1 - c if r & 1 else c)
        starts.append((buf.at[me], buf.at[me], (r - 1,), peer))
        mine.append((buf.at[me], (r - 1,)))
        arrivals.append((buf.at[4 * peer[0] + 2 * peer[1] + peer[2]], (r - 1,)))
    return starts, mine, arrivals


class _Reduction:
    def __init__(self, tag, slabs, c_idx, chip):
        self.tag, self.n, self.c_idx, self.chip = tag, len(slabs), c_idx, chip
        lands = [lax.empty((g.shape[0], g.shape[1] // 2, g.shape[2]), g.dtype) for g in slabs]
        self.sems = _comm_start("swap_start_" + tag, list(slabs) + lands, (self.n,), _swap_plan(self.n))

    def partial(self):
        n = self.n
        sems, bufs = self.sems
        bufs = _comm_wait("swap_wait_" + self.tag, bufs, [(sems, _swap_plan(n))])
        both = [_chip_partial(g, r, self.c_idx, self.chip) for g, r in zip(bufs[:n], bufs[n:])]
        self.own = [o for _, o in both]
        parts = [p for p, _ in both]
        lands = [lax.empty((N_CHIPS - 1,) + p.shape[1:], p.dtype) for p in parts]
        self.sems = _comm_start("scatter_start_" + self.tag, parts + lands, (n, N_CHIPS - 1), _scatter_plan(n))

    def total(self):
        n = self.n
        sems, bufs = self.sems
        bufs = _comm_wait("scatter_wait_" + self.tag, bufs, [(sems, _scatter_plan(n))])
        fulls = [_sum_partials(o, r, self.c_idx) for o, r in zip(self.own, bufs[n:])]
        self.sems = _comm_start("share_start_" + self.tag, fulls, (n,), _share_plan(n))

    def finish(self):
        sems, bufs = self.sems
        return _comm_wait("share_wait_" + self.tag, bufs, [(sems, _share_plan(self.n))])


def _elem_tile(rows, cols):
    return _pick(rows, max(8, (1 << 19) // cols // 8 * 8))


def _chip_partial(grad, recv, c_idx, p_idx):
    Q, R, C = grad.shape
    h = R // 2
    T = _elem_tile(h, C)
    nt = h // T

    def body(sc_ref, g_ref, r_ref, sb_ref, own_ref):
        q = pl.program_id(1)
        s = g_ref[...] + r_ref[...]
        sb_ref[...] = s.astype(BF16)

        @pl.when(q == sc_ref[1])
        def _():
            own_ref[...] = s

    return _call(
        body, (grad, recv), name="chip_partial", grid=(nt, Q),
        scalars=jnp.stack([c_idx, p_idx]).astype(jnp.int32),
        in_specs=[pl.BlockSpec((None, T, C), lambda t, q, sc: (q, sc[0] * nt + t, 0)),
                  pl.BlockSpec((None, T, C), lambda t, q, sc: (q, t, 0))],
        out_specs=[pl.BlockSpec((None, T, C), lambda t, q, sc: (q, t, 0)),
                   pl.BlockSpec((T, C), lambda t, q, sc: (t, 0))],
        out_shape=[_sds((Q, h, C), BF16), _sds((h, C), F32)], compiler_params=_params(2))


def _sum_partials(own, parts, c_idx):
    h, C = own.shape
    T = _elem_tile(h, C)
    nt = h // T

    def body(c_ref, o_ref, p_ref, t_ref):
        t = o_ref[...]
        for k in range(N_CHIPS - 1):
            t = t + p_ref[k].astype(F32)
        t_ref[...] = t

    return _call(
        body, (own, parts), name="sum_partials", grid=(nt,), scalars=jnp.reshape(c_idx, (1,)).astype(jnp.int32),
        in_specs=[pl.BlockSpec((T, C), lambda i, c: (i, 0)),
                  pl.BlockSpec((N_CHIPS - 1, T, C), lambda i, c: (0, i, 0))],
        out_specs=pl.BlockSpec((T, C), lambda i, c: (c[0] * nt + i, 0)),
        out_shape=_sds((2 * h, C), F32), compiler_params=_params(1))


def _pack_rows(name, parts, slot=None, n_slots=1):
    width = parts[0].shape[1]
    offsets, at = [], 0
    for p in parts:
        offsets.append(at)
        at += p.shape[0]
    total = -(-at // 8) * 8
    lead = 0 if slot is None else 1

    def body(*refs):
        out = refs[-1]
        out[...] = jnp.zeros(out.shape, F32)
        for ref, o in zip(refs[lead:-1], offsets):
            out[o:o + ref.shape[0], :] = ref[...]

    if slot is None:
        whole = pl.BlockSpec(memory_space=pltpu.VMEM)
        return _call(body, list(parts), name=name, in_specs=[whole] * len(parts), out_specs=whole,
                     out_shape=_sds((total, width), F32))
    return _call(body, list(parts), name=name, grid=(1,), scalars=jnp.reshape(slot, (1,)).astype(jnp.int32),
                 in_specs=[pl.BlockSpec(p.shape, lambda i, s: (0, 0)) for p in parts],
                 out_specs=pl.BlockSpec((None, total, width), lambda i, s: (s[0], 0, 0)),
                 out_shape=_sds((n_slots, total, width), F32))


def _sum_packs(packs):
    n, R, C = packs.shape

    def body(p_ref, o_ref):
        t = p_ref[0]
        for k in range(1, n):
            t = t + p_ref[k]
        o_ref[...] = t

    return _call(
        body, (packs,), name="sum_packs", grid=(1,), in_specs=[pl.BlockSpec((n, R, C), lambda i: (0, 0, 0))],
        out_specs=pl.BlockSpec((R, C), lambda i: (0, 0)), out_shape=_sds((R, C), F32), compiler_params=_params(1))


def _adamw(w, g, m, v):
    R, C = w.shape
    T = _elem_tile(R, C)

    def body(w_ref, g_ref, m_ref, v_ref, d_ref, m2_ref, v2_ref):
        g_ = g_ref[...]
        m2 = ADAM_B1 * m_ref[...] + (1.0 - ADAM_B1) * g_
        v2 = ADAM_B2 * v_ref[...] + (1.0 - ADAM_B2) * (g_ * g_)
        m_hat = m2 / (1.0 - ADAM_B1 ** ADAM_STEP)
        v_hat = v2 / (1.0 - ADAM_B2 ** ADAM_STEP)
        d_ref[...] = -ADAM_LR * (m_hat / (jnp.sqrt(v_hat) + ADAM_EPS) + ADAM_WD * w_ref[...])
        m2_ref[...] = m2
        v2_ref[...] = v2

    blk = pl.BlockSpec((T, C), lambda i: (i, 0))
    return _call(
        body, (w, g, m, v), name="adamw", grid=(R // T,), in_specs=[blk] * 4, out_specs=[blk] * 3,
        out_shape=[_sds((R, C), F32)] * 3, compiler_params=_params(1))


def _adamw_rows(w, g, m, v, row0, prev=None):
    R, C = w.shape
    T = _elem_tile(g.shape[0], C)
    off = row0 // T

    def body(w_ref, g_ref, m_ref, v_ref, *rest):
        d_ref, m2_ref, v2_ref, g2_ref = rest[-4:]
        g_ = g_ref[...]
        m2 = ADAM_B1 * m_ref[...] + (1.0 - ADAM_B1) * g_
        v2 = ADAM_B2 * v_ref[...] + (1.0 - ADAM_B2) * (g_ * g_)
        m_hat = m2 / (1.0 - ADAM_B1 ** ADAM_STEP)
        v_hat = v2 / (1.0 - ADAM_B2 ** ADAM_STEP)
        d_ref[...] = -ADAM_LR * (m_hat / (jnp.sqrt(v_hat) + ADAM_EPS) + ADAM_WD * w_ref[...])
        m2_ref[...] = m2
        v2_ref[...] = v2
        g2_ref[...] = g_

    here = pl.BlockSpec((T, C), lambda i: (off + i, 0))
    piece = pl.BlockSpec((T, C), lambda i: (i, 0))
    done = tuple(prev or ())
    return _call(
        body, (w, g, m, v) + done, name="adamw_rows", grid=(g.shape[0] // T,), follows=prev is None,
        in_specs=[here, piece, here, here] + [ANY] * len(done), out_specs=[here] * 4,
        out_shape=[_sds((R, C), F32)] * 4, input_output_aliases={4 + j: j for j in range(len(done))},
        compiler_params=_params(1))


SC_TILES = 32
SC_ROWS = 8
SC_LANES = 16


def _adamw_sc(w, g, m, v):
    R, C = w.shape
    per_tile = R // SC_TILES
    assert per_tile % SC_ROWS == 0 and C % SC_LANES == 0

    def body(w_hbm, g_hbm, m_hbm, v_hbm, d_out, m_out, v_out, g_out, wb, gb, mb, vb):
        tile = lax.axis_index("sc_tile") * 2 + lax.axis_index("sc_core")

        @pl.loop(0, per_tile // SC_ROWS)
        def _(chunk):
            here = pl.ds(tile * per_tile + chunk * SC_ROWS, SC_ROWS)
            pltpu.sync_copy(w_hbm.at[here], wb)
            pltpu.sync_copy(g_hbm.at[here], gb)
            pltpu.sync_copy(m_hbm.at[here], mb)
            pltpu.sync_copy(v_hbm.at[here], vb)
            for r in range(SC_ROWS):
                @pl.loop(0, C, step=SC_LANES)
                def _(j):
                    at = pl.ds(j, SC_LANES)
                    g_ = gb[r, at]
                    m2 = ADAM_B1 * mb[r, at] + (1.0 - ADAM_B1) * g_
                    v2 = ADAM_B2 * vb[r, at] + (1.0 - ADAM_B2) * (g_ * g_)
                    m_hat = m2 / (1.0 - ADAM_B1 ** ADAM_STEP)
                    v_hat = v2 / (1.0 - ADAM_B2 ** ADAM_STEP)
                    wb[r, at] = -ADAM_LR * (m_hat / (jnp.sqrt(v_hat) + ADAM_EPS) + ADAM_WD * wb[r, at])
                    mb[r, at] = m2
                    vb[r, at] = v2
            pltpu.sync_copy(wb, d_out.at[here])
            pltpu.sync_copy(mb, m_out.at[here])
            pltpu.sync_copy(vb, v_out.at[here])
            pltpu.sync_copy(gb, g_out.at[here])

    buf = pltpu.VMEM((SC_ROWS, C), F32)
    return pl.kernel(
        body, name="adamw_sc", out_type=(_sds((R, C), F32),) * 4,
        mesh=plsc.VectorSubcoreMesh(core_axis_name="sc_core", subcore_axis_name="sc_tile"),
        scratch_types=[buf, buf, buf, buf])(w, g, m, v)


BIG = ("w_in", "w_pool_out", "w_conv_out", "w_o", "w_up", "w_down", "w_pool_grp")
VECTORS = ("g_pre_mix", "pool_scale", "b_dw", "conv_ln_g", "conv_ln_b", "g_post_mix", "g_pre_mlp", "g_post_mlp")
WEIGHTS = ("meta", "g_pre_mix", "w_in", "w_pool_grp", "pool_scale", "w_pool_out", "w_dw", "b_dw", "conv_ln_g",
           "conv_ln_b", "w_conv_out", "w_o", "g_post_mix", "g_pre_mlp", "w_up", "w_down", "g_post_mlp")


def _as_rows(a, width):
    r, cols = a.shape
    return a.reshape(r * (cols // width), width)


def _step(w, m, v, x, tgt):
    S, D = x.shape
    P = D // 2
    xi, yi, ci = _place()
    chip = 2 * xi + yi
    _CHAIN["after"] = []

    C = D // 2
    G = POOL_GROUPS
    GD = P // G
    GS = GD // N_CHIPS
    Q = N_CHIPS
    vecs = {k: w[k] for k in VECTORS}
    shard2d = {k: w[k].reshape(-1, w[k].shape[-1]) for k in BIG}
    grads, delta, new_m, new_v = {}, {}, {}, {}

    def update(names, reduced, on_sparsecore=False):
        for k, g in zip(names, reduced):
            args = (shard2d[k], g, m[k].reshape(shard2d[k].shape), v[k].reshape(shard2d[k].shape))
            delta[k], new_m[k], new_v[k], grads[k] = _adamw_sc(*args) if on_sparsecore else _adamw_rows(*args, 0)

    groups = dict(a=(("w_in", "w_pool_grp"), ("w_dw", "meta")), b=(("w_pool_out", "w_conv_out", "w_o"), ()),
                  c=(("w_up",), ()), d=(("w_down",), ()))
    flying = {}

    def start(tag, follows):
        halved, whole = groups[tag]
        flying[tag] = _comm_start("gather_start_" + tag, flying[tag], (len(halved + whole), N_CHIPS - 1),
                                  _gather_plan(len(halved)), follows=follows)

    def spread(tag):
        nh = len(groups[tag][0])
        sems, bufs = flying[tag]
        flying[tag] = _comm_relay("gather_relay_" + tag, bufs, sems, _gather_plan(nh), (2, nh, 2),
                                  _spread_plan(nh, 0))

    def landed(tag):
        halved, whole = groups[tag]
        nh = len(halved)
        sems, bufs = flying.pop(tag)
        last, tree = _comm_relay("gather_relay2_" + tag, bufs[:nh], sems, _spread_plan(nh, 0), (nh, 2),
                                 _last_hand_on_plan(nh))
        done = _comm_wait("gather_wait_" + tag, tree,
                          [(sems, _spread_plan(nh, 1)), (last, _last_hand_on_plan(nh))])
        return dict(zip(halved + whole, done + bufs[nh:]))

    for tag, (halved, whole) in groups.items():
        flying[tag] = [_cast_into_slab(shard2d[k], chip, BF16) for k in halved]
        flying[tag] += [_cast_into_slab(w[k], chip, F32) for k in whole]
        if tag == "a":
            start(tag, False)

    small_names = [k for k in WEIGHTS if k not in BIG]

    def pack_small(tree):
        parts = []
        for k in small_names:
            flat = tree[k].reshape(-1)
            parts.append(jnp.pad(flat, (0, -flat.shape[0] % P)).reshape(-1, P))
        return _pack_rows("pack_small", parts)

    small_w, small_m, small_v = pack_small(w), pack_small(m), pack_small(v)
    h0, u1 = _pre_norm_tokens(x, vecs["g_pre_mix"])
    spread("a")
    start("b", True)

    got = landed("a")
    win_g = got["w_in"]
    w_grp = got["w_pool_grp"].reshape(N_CHIPS, G, GS, GD).transpose(1, 0, 2, 3).reshape(G, GD, GD)
    w_dw = got["w_dw"].transpose(1, 0, 2).reshape(CONV_TAPS, P)
    meta = got["meta"].transpose(1, 0, 2).reshape(N_META, D)
    h0, u1 = _pre_norm_meta(meta, vecs["g_pre_mix"], h0, u1)
    spread("b")
    start("c", True)
    start("d", True)
    proj, gates = _proj(u1, win_g, P + 2 * C)
    d, ya_pre = _pool_fwd(proj, w_grp, vecs["pool_scale"])
    s, c = _conv_fwd(proj, C, w_dw, vecs["b_dw"], vecs["conv_ln_g"], vecs["conv_ln_b"])
    spread("c")
    got = landed("b")
    wpo_g, wco_g, w_o = got["w_pool_out"], got["w_conv_out"], got["w_o"].reshape(D, D)
    mix, ya, yb = _mix_fwd(ya_pre, s, wpo_g, wco_g, gates, D)
    o = _mm_act_roww("attn_out", mix, w_o, 1088, 1024, 2048)
    spread("d")
    h1, u2 = _mid_norm(o, h0, vecs["g_post_mix"], vecs["g_pre_mlp"])
    wup_g = landed("c")["w_up"]
    a_up, fact = _mm_act_colw("mlp_up", u2, wup_g, 512, _up_epilogue, (BF16, BF16))
    w_down = landed("d")["w_down"].reshape(-1, D)
    f = _mm_act_roww("mlp_down", fact, w_down, 1088, 1024, 2048)
    dy, df, dg_post_mlp, loss = _loss_head(f, h1, tgt, vecs["g_post_mlp"])

    g_w_down = _mm_wgrad("dw_down", fact, df, 1024, 1024).reshape(N_CHIPS, -1, D)
    _CHAIN["after"] = [g_w_down]
    red1 = _Reduction("1", [g_w_down], ci, chip)
    da_up = _mlp_down_bwd(df, w_down, a_up)
    red1.partial()
    g_w_up = _mm_wgrad_colw("dw_up", u2, da_up, Q, 1024, 1024)
    red2 = _Reduction("2", [g_w_up], ci, chip)
    du2 = _mm_grad_colw_t("du2", da_up, wup_g, 1088, 1024)
    red2.partial()
    red1.total()
    dh1, do, dg_pre_mlp, dg_post_mix = _mid_norm_bwd(dy, du2, h1, o, vecs["g_pre_mlp"], vecs["g_post_mix"])
    update(("w_down",), red1.finish(), on_sparsecore=True)
    g_w_o = _mm_wgrad("dw_o", mix, do, 1024, 1024)
    dya, dyb, dga, dgb = _mix_bwd(do, w_o, gates, ya, yb)
    g_wpo = _mm_wgrad_colw("dw_pool_out", ya_pre, dya, Q, 1024, 512)
    g_wco = _mm_wgrad_colw("dw_conv_out", s, dyb, Q, 1024, 512)
    red3 = _Reduction("3", [g_w_o.reshape(N_CHIPS, D // N_CHIPS, D), g_wpo, g_wco], ci, chip)
    dya_pre = _mm_grad_colw_t("dya_pre", dya, wpo_g, 1088, 1024)
    ds = _mm_grad_colw_t("ds", dyb, wco_g, 1088, 1024)
    red3.partial()
    red2.total()
    dz, g_w_grp, dscale = _pool_bwd(dya_pre, d, w_grp, vecs["pool_scale"])
    update(("w_up",), red2.finish(), on_sparsecore=True)
    dc, dln_g, dln_b, db_dw = _conv_ln_bwd(ds, c, vecs["conv_ln_g"], vecs["conv_ln_b"])
    dv, dgc, g_w_dw = _conv_bwd(dc, proj, C, w_dw)
    dproj = _concat_cols("dproj", [dz, dv, dgc, dga, dgb])
    half_k = D // 2
    g_w_grp = g_w_grp.reshape(G, N_CHIPS, GS, GD).transpose(1, 0, 2, 3).reshape(N_CHIPS, G * GS, GD)
    g_in_a = _mm_wgrad_colw("dw_in_a", u1, dproj, Q, 512, 1792, rows=(0, half_k))
    red4a = _Reduction("4a", [g_in_a, g_w_grp], ci, chip)
    g_in_b = _mm_wgrad_colw("dw_in_b", u1, dproj, Q, 512, 1792, rows=(half_k, half_k))
    red4a.partial()
    red4b = _Reduction("4b", [g_in_b], ci, chip)
    du1 = _mm_grad_colw_t("du1", dproj, win_g, 1088, 1024)
    red4b.partial()
    red3.total()
    grad_x, dmeta, dg_pre_mix = _pre_norm_bwd(dh1, du1, h0, vecs["g_pre_mix"])

    g_vec = dict(g_pre_mix=dg_pre_mix, pool_scale=dscale, b_dw=db_dw, conv_ln_g=dln_g, conv_ln_b=dln_b,
                 g_post_mix=dg_post_mix, g_pre_mlp=dg_pre_mlp, g_post_mlp=dg_post_mlp)
    rows = [g_w_dw, _as_rows(dmeta, P)] + [_as_rows(g_vec[k], P) for k in VECTORS]
    rows.append(jnp.broadcast_to(loss[:, :1], (1, P)))
    packs = _comm_start("packs_start", [_pack_rows("pack_grads", rows, 2 * chip + ci, N_DEVICES)],
                        (N_DEVICES - 1,), _packs_plan)
    red4a.total()
    update(("w_o", "w_pool_out", "w_conv_out"), red3.finish())
    red_in_a, red_grp = red4a.finish()
    update(("w_pool_grp",), [red_grp])
    w_in_rows = (shard2d["w_in"], m["w_in"].reshape(shard2d["w_in"].shape), v["w_in"].reshape(shard2d["w_in"].shape))
    first_rows = _adamw_rows(w_in_rows[0], red_in_a, w_in_rows[1], w_in_rows[2], 0)
    total = _sum_packs(_comm_wait("packs_wait", packs[1], [(packs[0], _packs_plan)])[0])
    at = 0
    taps_pad = g_w_dw.shape[0]
    g_dw_full = total[at:at + CONV_TAPS]
    at += taps_pad
    g_meta_full = total[at:at + 2 * N_META].reshape(N_META, D)
    at += 2 * N_META
    for k in VECTORS:
        n = w[k].shape[-1] // P
        grads[k] = total[at:at + n].reshape(1, n * P)
        at += n
    loss_total = total[at, 0]
    grads["w_dw"] = lax.dynamic_slice_in_dim(g_dw_full, chip * (P // N_CHIPS), P // N_CHIPS, axis=1)
    grads["meta"] = lax.dynamic_slice_in_dim(g_meta_full, chip * (D // N_CHIPS), D // N_CHIPS, axis=1)

    sd, sm, sv = _adamw(small_w, pack_small(grads), small_m, small_v)
    at = 0
    for k in small_names:
        a = w[k].reshape(-1, w[k].shape[-1])
        n = -(-a.size // P)
        for tree, packed in ((delta, sd), (new_m, sm), (new_v, sv)):
            tree[k] = packed[at:at + n].reshape(-1)[:a.size].reshape(a.shape)
        at += n

    red4b.total()
    delta["w_in"], new_m["w_in"], new_v["w_in"], grads["w_in"] = _adamw_rows(
        w_in_rows[0], red4b.finish()[0], w_in_rows[1], w_in_rows[2], half_k, prev=first_rows)
    return loss_total, grad_x, grads, delta, new_m, new_v


def kernel(x, meta, g_pre_mix, w_in, w_pool_grp, pool_scale, w_pool_out, w_dw, b_dw, conv_ln_g, conv_ln_b, w_conv_out, w_o, g_post_mix, g_pre_mlp, w_up, w_down, g_post_mlp, loss_target, m_meta, m_g_pre_mix, m_w_in, m_w_pool_grp, m_pool_scale, m_w_pool_out, m_w_dw, m_b_dw, m_conv_ln_g, m_conv_ln_b, m_w_conv_out, m_w_o, m_g_post_mix, m_g_pre_mlp, m_w_up, m_w_down, m_g_post_mlp, v_meta, v_g_pre_mix, v_w_in, v_w_pool_grp, v_pool_scale, v_w_pool_out, v_w_dw, v_b_dw, v_conv_ln_g, v_conv_ln_b, v_w_conv_out, v_w_o, v_g_post_mix, v_g_pre_mlp, v_w_up, v_w_down, v_g_post_mlp):
    args = dict(locals())
    shapes = {k: args[k].shape for k in WEIGHTS}
    w = {k: args[k] for k in WEIGHTS}
    m = {k: args["m_" + k] for k in WEIGHTS}
    v = {k: args["v_" + k] for k in WEIGHTS}
    for tree in (w, m, v):
        tree["w_dw"] = tree["w_dw"].reshape(tree["w_dw"].shape[-2:])
    loss, grad_x, grads, delta, new_m, new_v = _step(w, m, v, x[0], loss_target[0])
    out = [loss, grad_x[None]]
    for tree in (grads, delta, new_m, new_v):
        out += [tree[k].reshape(shapes[k]) for k in WEIGHTS]
    return tuple(out)
```

```python
import math

import jax
import jax.numpy as jnp
from jax import lax
from jax.experimental import pallas as pl
from jax.experimental.pallas import tpu as pltpu
from jax.experimental.pallas import tpu_sc as plsc

F32 = jnp.float32
BF16 = jnp.bfloat16

N_META = 16
PAD_ROWS = 112
TOKEN_ROW0 = PAD_ROWS + N_META
POOL_GROUPS = 4
CONV_TAPS = 31
HALO = 32
CONV_ROWS = 128
LANES = 128
RMS_EPS = 1e-6
LN_EPS = 1e-5
ADAM_LR = 0.001
ADAM_B1 = 0.9
ADAM_B2 = 0.999
ADAM_EPS = 1e-08
ADAM_WD = 0.01
ADAM_STEP = 10
VMEM_LIMIT_MB = 56

MESH = pl.DeviceIdType.MESH
NN = (((1,), (0,)), ((), ()))
NT = (((1,), (1,)), ((), ()))
TN = (((0,), (0,)), ((), ()))


def _pick(n, pref):
    if n <= pref:
        return n
    if n % pref == 0:
        return pref
    for step in (LANES, 8, 1):
        t = (pref // step) * step
        while t >= step:
            if n % t == 0:
                return t
            t -= step
    return n


def _params(n_axes, vmem_mb=VMEM_LIMIT_MB):
    return pltpu.CompilerParams(dimension_semantics=("arbitrary",) * n_axes,
                                vmem_limit_bytes=vmem_mb << 20)


def _sigmoid(x):
    return jax.nn.sigmoid(x)


_CHAIN = {"after": []}


def _call(body, args, *, in_specs, out_specs, out_shape, grid=(), scalars=None, mark=0, follows=True, made_from=None,
          **kw):
    pending = _CHAIN["after"]
    after = pending if follows else []
    n = len(args)
    lead = 0 if scalars is None else 1
    specs = list(in_specs)
    operands = list(args)
    fn = body
    if after:
        def fn(*refs):
            body(*refs[:lead + n], *refs[lead + n + len(after):])
        specs += [pl.BlockSpec(memory_space=pl.ANY)] * len(after)
        operands += after
    if scalars is None:
        if grid:
            kw["grid"] = grid
        res = pl.pallas_call(fn, in_specs=specs, out_specs=out_specs, out_shape=out_shape, **kw)(*operands)
    else:
        grid_spec = pltpu.PrefetchScalarGridSpec(num_scalar_prefetch=1, grid=grid, in_specs=specs, out_specs=out_specs)
        res = pl.pallas_call(fn, grid_spec=grid_spec, out_shape=out_shape, **kw)(scalars, *operands)
    outs = res if isinstance(res, (list, tuple)) else [res]
    consumed = args if made_from is None else made_from
    kept = [] if follows else [m for m in pending if not any(m is a for a in consumed)]
    _CHAIN["after"] = kept + [outs[mark]]
    return res


def _store(val, extras, outs):
    outs[0][...] = val.astype(outs[0].dtype)


def _mm(name, grid, arrays, in_specs, out_shapes, out_specs, dims, nk, epilogue=_store, acc_shape=None, aliases=None):
    n_in, n_out = len(arrays), len(out_shapes)

    def body(*refs):
        extras = refs[2:n_in]
        outs = refs[n_in:n_in + n_out]
        part = lax.dot_general(refs[0][...], refs[1][...], dims, preferred_element_type=F32)
        if nk == 1:
            epilogue(part, extras, outs)
        else:
            acc = refs[n_in + n_out]
            k = pl.program_id(len(grid) - 1)

            @pl.when(k == 0)
            def _():
                acc[...] = part

            @pl.when(k > 0)
            def _():
                acc[...] += part

            @pl.when(k == nk - 1)
            def _():
                epilogue(acc[...], extras, outs)

    scratch = [pltpu.VMEM(acc_shape, F32)] if nk > 1 else []
    single = n_out == 1
    return _call(
        body, arrays, name=name, grid=grid, in_specs=in_specs,
        out_specs=out_specs[0] if single else out_specs,
        out_shape=out_shapes[0] if single else out_shapes,
        scratch_shapes=scratch, compiler_params=_params(len(grid)), input_output_aliases=aliases or {})


def _sds(shape, dtype):
    return jax.ShapeDtypeStruct(shape, dtype)


def _rms_scale(h):
    return lax.rsqrt(jnp.mean(h * h, axis=-1, keepdims=True) + RMS_EPS)


def _rms_bwd(du, h, g):
    r = _rms_scale(h)
    y = h * r
    dy = du * g
    dh = r * (dy - y * jnp.mean(dy * y, axis=-1, keepdims=True))
    return dh, jnp.sum(du * y, axis=0, keepdims=True)


def _row_tile(L):
    return _pick(L, 272)


def _pre_norm_tokens(x, g):
    S, D = x.shape
    T = TOKEN_ROW0
    L = S + T

    def body(x_ref, g_ref, h_ref, u_ref):
        h = x_ref[...]
        h_ref[...] = h
        u_ref[...] = (h * _rms_scale(h) * g_ref[...]).astype(BF16)

    below = pl.BlockSpec((T, D), lambda i: (i + 1, 0))
    return _call(
        body, (x, g), name="pre_norm_tokens", grid=(S // T,),
        in_specs=[pl.BlockSpec((T, D), lambda i: (i, 0)), pl.BlockSpec((1, D), lambda i: (0, 0))],
        out_specs=[below, below], out_shape=[_sds((L, D), F32), _sds((L, D), BF16)], compiler_params=_params(1))


def _pre_norm_meta(meta, g, h0, u1):
    L, D = h0.shape
    T = TOKEN_ROW0

    def body(m_ref, g_ref, h_in, u_in, h_ref, u_ref):
        h_ref[...] = jnp.zeros(h_ref.shape, F32)
        h_ref[PAD_ROWS:, :] = m_ref[...]
        h = h_ref[...]
        u_ref[...] = (h * _rms_scale(h) * g_ref[...]).astype(BF16)

    first = pl.BlockSpec((T, D), lambda i: (0, 0))
    return _call(
        body, (meta, g, h0, u1), name="pre_norm_meta", grid=(1,),
        in_specs=[pl.BlockSpec((N_META, D), lambda i: (0, 0)), pl.BlockSpec((1, D), lambda i: (0, 0)), ANY, ANY],
        out_specs=[first, first], out_shape=[_sds((L, D), F32), _sds((L, D), BF16)],
        input_output_aliases={2: 0, 3: 1}, compiler_params=_params(1))


def _mid_norm(o, h0, g_post, g_pre):
    L, D = h0.shape
    T = _row_tile(L)

    def body(o_ref, h_ref, gp_ref, gm_ref, h1_ref, u2_ref):
        o_ = o_ref[...]
        h1 = h_ref[...] + o_ * _rms_scale(o_) * gp_ref[...]
        h1_ref[...] = h1
        u2_ref[...] = (h1 * _rms_scale(h1) * gm_ref[...]).astype(BF16)

    row = pl.BlockSpec((T, D), lambda i: (i, 0))
    vec = pl.BlockSpec((1, D), lambda i: (0, 0))
    return _call(
        body, (o, h0, g_post, g_pre), name="mid_norm", grid=(L // T,),
        in_specs=[row, row, vec, vec], out_specs=[row, row],
        out_shape=[_sds((L, D), F32), _sds((L, D), BF16)], compiler_params=_params(1))


def _loss_head(f, h1, tgt, g_post):
    L, D = h1.shape
    T = TOKEN_ROW0
    n = L // T

    def body(f_ref, h_ref, t_ref, g_ref, dy_ref, df_ref, dg_ref, loss_ref):
        i = pl.program_id(0)
        f_ = f_ref[...]
        g = g_ref[...]
        y = h_ref[...] + f_ * _rms_scale(f_) * g
        live = (i > 0).astype(F32)
        diff = (y - t_ref[...]) * live
        part = 0.5 * jnp.sum(jnp.mean(diff * diff, axis=-1, keepdims=True), axis=0, keepdims=True)
        dy = diff * (1.0 / D)
        dy_ref[...] = dy
        df, dg = _rms_bwd(dy, f_, g)
        df_ref[...] = df.astype(BF16)

        @pl.when(i == 0)
        def _():
            dg_ref[...] = dg
            loss_ref[...] = jnp.broadcast_to(part, loss_ref.shape)

        @pl.when(i > 0)
        def _():
            dg_ref[...] += dg
            loss_ref[...] += jnp.broadcast_to(part, loss_ref.shape)

    row = pl.BlockSpec((T, D), lambda i: (i, 0))
    vec = pl.BlockSpec((1, D), lambda i: (0, 0))
    return _call(
        body, (f, h1, tgt, g_post), name="loss_head", grid=(n,),
        in_specs=[row, row, pl.BlockSpec((T, D), lambda i: (jnp.maximum(i - 1, 0), 0)), vec],
        out_specs=[row, row, vec, pl.BlockSpec((1, LANES), lambda i: (0, 0))],
        out_shape=[_sds((L, D), F32), _sds((L, D), BF16), _sds((1, D), F32), _sds((1, LANES), F32)],
        compiler_params=_params(1))


def _mid_norm_bwd(dy, du2, h1, o, g_pre, g_post):
    L, D = h1.shape
    T = _row_tile(L)

    def body(dy_ref, du_ref, h_ref, o_ref, gm_ref, gp_ref, dh1_ref, do_ref, dgm_ref, dgp_ref):
        i = pl.program_id(0)
        dh, dgm = _rms_bwd(du_ref[...], h_ref[...], gm_ref[...])
        dh1 = dy_ref[...] + dh
        dh1_ref[...] = dh1
        do, dgp = _rms_bwd(dh1, o_ref[...], gp_ref[...])
        do_ref[...] = do.astype(BF16)

        @pl.when(i == 0)
        def _():
            dgm_ref[...] = dgm
            dgp_ref[...] = dgp

        @pl.when(i > 0)
        def _():
            dgm_ref[...] += dgm
            dgp_ref[...] += dgp

    row = pl.BlockSpec((T, D), lambda i: (i, 0))
    vec = pl.BlockSpec((1, D), lambda i: (0, 0))
    return _call(
        body, (dy, du2, h1, o, g_pre, g_post), name="mid_norm_bwd", grid=(L // T,),
        in_specs=[row, row, row, row, vec, vec], out_specs=[row, row, vec, vec],
        out_shape=[_sds((L, D), F32), _sds((L, D), BF16), _sds((1, D), F32), _sds((1, D), F32)],
        compiler_params=_params(1))


def _pre_norm_bwd(dh1, du1, h0, g):
    L, D = h0.shape
    T = TOKEN_ROW0
    n = L // T

    def body(dh_ref, du_ref, h_ref, g_ref, gx_ref, dmeta_ref, dg_ref):
        i = pl.program_id(0)
        dh, dg = _rms_bwd(du_ref[...], h_ref[...], g_ref[...])
        dh0 = dh_ref[...] + dh
        gx_ref[...] = dh0

        @pl.when(i == 0)
        def _():
            dmeta_ref[...] = dh0[PAD_ROWS:, :]
            dg_ref[...] = dg

        @pl.when(i > 0)
        def _():
            dg_ref[...] += dg

    row = pl.BlockSpec((T, D), lambda i: (i, 0))
    vec = pl.BlockSpec((1, D), lambda i: (0, 0))
    return _call(
        body, (dh1, du1, h0, g), name="pre_norm_bwd", grid=(n,),
        in_specs=[row, row, row, vec],
        out_specs=[pl.BlockSpec((T, D), lambda i: (jnp.maximum(i - 1, 0), 0)),
                   pl.BlockSpec((N_META, D), lambda i: (0, 0)), vec],
        out_shape=[_sds((L - T, D), F32), _sds((N_META, D), F32), _sds((1, D), F32)],
        compiler_params=_params(1))


def _window_sum(z, g, shift_sign, L):
    s = z
    for j in range(POOL_GROUPS):
        k = 1 << j
        nxt = s + pltpu.roll(s, k if shift_sign > 0 else L - k, 0)
        s = jnp.where(j <= g, nxt, s)
    return s


def _inv_count(g, L):
    t = lax.broadcasted_iota(jnp.int32, (L, 1), 0)
    w = jnp.left_shift(2, g)
    cnt = jnp.clip(t - (PAD_ROWS - 1), 1, w)
    return 1.0 / cnt.astype(F32)


def _pool_fwd(proj, w_grp, scale):
    L = proj.shape[0]
    G, GD, _ = w_grp.shape
    P = G * GD

    def body(z_ref, w_ref, sc_ref, d_ref, ya_ref):
        g = pl.program_id(0)
        z = z_ref[...]
        d = (_window_sum(z, g, +1, L) * _inv_count(g, L) - z).astype(BF16)
        d_ref[...] = d
        y = jnp.dot(d, w_ref[...], preferred_element_type=F32)
        ya_ref[...] = (y * sc_ref[...]).astype(BF16)

    col = pl.BlockSpec((L, GD), lambda g: (0, g))
    return _call(
        body, (proj, w_grp, scale), name="pool_fwd", grid=(G,),
        in_specs=[col, pl.BlockSpec((None, GD, GD), lambda g: (g, 0, 0)), pl.BlockSpec((1, GD), lambda g: (0, g))],
        out_specs=[col, col], out_shape=[_sds((L, P), BF16), _sds((L, P), BF16)],
        compiler_params=_params(1))


def _pool_bwd(dya, d, w_grp, scale):
    L, P = dya.shape
    G, GD, _ = w_grp.shape

    def body(dya_ref, d_ref, w_ref, sc_ref, dz_ref, dw_ref, dsc_ref):
        g = pl.program_id(0)
        dya_ = dya_ref[...]
        d_ = d_ref[...]
        w = w_ref[...]
        y = jnp.dot(d_, w, preferred_element_type=F32)
        dsc_ref[...] = jnp.sum(dya_ * y, axis=0, keepdims=True)
        dy = (dya_ * sc_ref[...]).astype(BF16)
        dw_ref[...] = lax.dot_general(d_, dy, TN, preferred_element_type=F32)
        dd = lax.dot_general(dy, w, NT, preferred_element_type=F32)
        dz = _window_sum(dd * _inv_count(g, L), g, -1, L) - dd
        dz_ref[...] = dz.astype(BF16)

    col = pl.BlockSpec((L, GD), lambda g: (0, g))
    wspec = pl.BlockSpec((None, GD, GD), lambda g: (g, 0, 0))
    vec = pl.BlockSpec((1, GD), lambda g: (0, g))
    return _call(
        body, (dya, d, w_grp, scale), name="pool_bwd", grid=(G,),
        in_specs=[col, col, wspec, vec], out_specs=[col, wspec, vec],
        out_shape=[_sds((L, P), BF16), _sds((G, GD, GD), F32), _sds((1, P), F32)],
        compiler_params=_params(1))


def _fill_rotations(rot_ref, ext):
    n = ext.shape[0]
    rot_ref[0] = ext
    for r in range(1, 8):
        rot_ref[r] = pltpu.roll(ext, n - r, 0)


def _lane_chunks(C):
    step = LANES if C % LANES == 0 else C
    return [(c0, step) for c0 in range(0, C, step)]


def _conv_specs(L, C, col_v, col_g):
    T = CONV_ROWS
    per = T // HALO
    cur_v = pl.BlockSpec((T, C), lambda i: (i, col_v))
    cur_g = pl.BlockSpec((T, C), lambda i: (i, col_g))
    prev_v = pl.BlockSpec((HALO, C), lambda i: (jnp.maximum(i * per - 1, 0), col_v))
    prev_g = pl.BlockSpec((HALO, C), lambda i: (jnp.maximum(i * per - 1, 0), col_g))
    return cur_v, cur_g, prev_v, prev_g


def _glu_ext(vc, gc, vh, gh, i):
    a_cur = vc[...] * _sigmoid(gc[...])
    a_prev = vh[...] * _sigmoid(gh[...]) * (i > 0).astype(F32)
    return jnp.concatenate([a_prev, a_cur], axis=0)


def _conv_fwd(proj, C, w_dw, b_dw, ln_g, ln_b):
    L = proj.shape[0]
    T = CONV_ROWS
    P = C

    def body(vc, gc, vh, gh, w_ref, b_ref, lg_ref, lb_ref, s_ref, c_ref, rot):
        i = pl.program_id(0)
        _fill_rotations(rot, _glu_ext(vc, gc, vh, gh, i))
        for c0, cw in _lane_chunks(C):
            acc = jnp.zeros((T, cw), F32)
            for k in range(CONV_TAPS):
                q, r = divmod(HALO - (CONV_TAPS - 1) + k, 8)
                acc = acc + w_ref[k:k + 1, c0:c0 + cw] * rot[r, 8 * q:8 * q + T, c0:c0 + cw]
            c_ref[:, c0:c0 + cw] = acc + b_ref[:, c0:c0 + cw]
        c = c_ref[...]
        mu = jnp.mean(c, axis=-1, keepdims=True)
        cen = c - mu
        var = jnp.mean(cen * cen, axis=-1, keepdims=True)
        ln = cen * lax.rsqrt(var + LN_EPS) * lg_ref[...] + lb_ref[...]
        s_ref[...] = (ln * _sigmoid(ln)).astype(BF16)

    cur_v, cur_g, prev_v, prev_g = _conv_specs(L, C, P // C, P // C + 1)
    row = pl.BlockSpec((T, C), lambda i: (i, 0))
    vec = pl.BlockSpec((1, C), lambda i: (0, 0))
    return _call(
        body, (proj, proj, proj, proj, w_dw, b_dw, ln_g, ln_b), name="conv_fwd", grid=(L // T,),
        in_specs=[cur_v, cur_g, prev_v, prev_g, pl.BlockSpec((CONV_TAPS, C), lambda i: (0, 0)), vec, vec, vec],
        out_specs=[row, row], out_shape=[_sds((L, C), BF16), _sds((L, C), F32)],
        scratch_shapes=[pltpu.VMEM((8, T + HALO, C), F32)], compiler_params=_params(1))


def _conv_ln_bwd(ds, c, ln_g, ln_b):
    L, C = c.shape
    T = _row_tile(L)

    def body(ds_ref, c_ref, lg_ref, lb_ref, dc_ref, dlg_ref, dlb_ref, db_ref):
        i = pl.program_id(0)
        c_ = c_ref[...]
        g = lg_ref[...]
        mu = jnp.mean(c_, axis=-1, keepdims=True)
        cen = c_ - mu
        rstd = lax.rsqrt(jnp.mean(cen * cen, axis=-1, keepdims=True) + LN_EPS)
        xhat = cen * rstd
        ln = xhat * g + lb_ref[...]
        sg = _sigmoid(ln)
        dln = ds_ref[...] * (sg * (1.0 + ln * (1.0 - sg)))
        dxh = dln * g
        dc = rstd * (dxh - jnp.mean(dxh, axis=-1, keepdims=True)
                     - xhat * jnp.mean(dxh * xhat, axis=-1, keepdims=True))
        dc_ref[...] = dc
        dlg = jnp.sum(dln * xhat, axis=0, keepdims=True)
        dlb = jnp.sum(dln, axis=0, keepdims=True)
        db = jnp.sum(dc, axis=0, keepdims=True)

        @pl.when(i == 0)
        def _():
            dlg_ref[...] = dlg
            dlb_ref[...] = dlb
            db_ref[...] = db

        @pl.when(i > 0)
        def _():
            dlg_ref[...] += dlg
            dlb_ref[...] += dlb
            db_ref[...] += db

    row = pl.BlockSpec((T, C), lambda i: (i, 0))
    vec = pl.BlockSpec((1, C), lambda i: (0, 0))
    return _call(
        body, (ds, c, ln_g, ln_b), name="conv_ln_bwd", grid=(L // T,),
        in_specs=[row, row, vec, vec], out_specs=[row, vec, vec, vec],
        out_shape=[_sds((L, C), F32), _sds((1, C), F32), _sds((1, C), F32), _sds((1, C), F32)],
        compiler_params=_params(1))


def _conv_bwd(dc, proj, C, w_dw):
    L = proj.shape[0]
    T = CONV_ROWS
    per = T // HALO
    n = L // T
    P = C
    taps_pad = 32

    def body(dcc, dcn, vc, gc, w_ref, dv_ref, dg_ref, dw_ref, rot_d, dw_acc):
        i = pl.program_id(0)
        dc_next = dcn[...] * (i < n - 1).astype(F32)
        _fill_rotations(rot_d, jnp.concatenate([dcc[...], dc_next], axis=0))

        @pl.when(i == 0)
        def _():
            dw_acc[...] = jnp.zeros(dw_acc.shape, F32)

        for c0, cw in _lane_chunks(C):
            v = vc[:, c0:c0 + cw]
            sg = _sigmoid(gc[:, c0:c0 + cw])
            a = v * sg
            da = jnp.zeros((T, cw), F32)
            for k in range(CONV_TAPS):
                q, r = divmod(CONV_TAPS - 1 - k, 8)
                slab = rot_d[r, 8 * q:8 * q + T, c0:c0 + cw]
                da = da + w_ref[k:k + 1, c0:c0 + cw] * slab
                dw_acc[k, :, c0:c0 + cw] += jnp.sum((a * slab).reshape(T // 8, 8, cw), axis=0)
            dv_ref[:, c0:c0 + cw] = (da * sg).astype(BF16)
            dg_ref[:, c0:c0 + cw] = (da * v * sg * (1.0 - sg)).astype(BF16)

        @pl.when(i == n - 1)
        def _():
            dw_ref[...] = jnp.sum(dw_acc[...], axis=1)

    cur_v, cur_g, _, _ = _conv_specs(L, C, P // C, P // C + 1)
    row = pl.BlockSpec((T, C), lambda i: (i, 0))
    nxt = pl.BlockSpec((HALO, C), lambda i: (jnp.minimum((i + 1) * per, L // HALO - 1), 0))
    wspec = pl.BlockSpec((CONV_TAPS, C), lambda i: (0, 0))
    return _call(
        body, (dc, dc, proj, proj, w_dw), name="conv_bwd", grid=(n,),
        in_specs=[row, nxt, cur_v, cur_g, wspec],
        out_specs=[row, row, pl.BlockSpec((taps_pad, C), lambda i: (0, 0))],
        out_shape=[_sds((L, C), BF16), _sds((L, C), BF16), _sds((taps_pad, C), F32)],
        scratch_shapes=[pltpu.VMEM((8, T + HALO, C), F32), pltpu.VMEM((taps_pad, 8, C), F32)],
        compiler_params=_params(1))


def _mix_fwd(ya_pre, s, wpo, wco, gates, D):
    L, P = ya_pre.shape
    Q, _, DS = wpo.shape
    bm = _pick(L, 1088)
    per = D // DS

    def body(a1, a2, b1, b2, ga, gb, m_ref, ya_ref, yb_ref):
        ya = jnp.dot(a1[...], b1[...], preferred_element_type=F32)
        yb = jnp.dot(a2[...], b2[...], preferred_element_type=F32)
        ya_ref[...] = ya.astype(BF16)
        yb_ref[...] = yb.astype(BF16)
        m_ref[...] = (_sigmoid(ga[...].astype(F32)) * ya + _sigmoid(gb[...].astype(F32)) * yb).astype(BF16)

    act = pl.BlockSpec((bm, P), lambda i, q: (i, 0))
    wsp = pl.BlockSpec((None, P, DS), lambda i, q: (q, 0, 0))
    out = pl.BlockSpec((bm, DS), lambda i, q: (i, q))
    return _call(
        body, (ya_pre, s, wpo, wco, gates, gates), name="mix_fwd", grid=(L // bm, Q),
        in_specs=[act, act, wsp, wsp,
                  pl.BlockSpec((bm, DS), lambda i, q: (i, q)),
                  pl.BlockSpec((bm, DS), lambda i, q: (i, per + q))],
        out_specs=[out, out, out],
        out_shape=[_sds((L, D), BF16), _sds((L, D), BF16), _sds((L, D), BF16)],
        compiler_params=_params(2))


def _mix_bwd(do, w_o, gates, ya, yb):
    L, D = do.shape
    bm = _pick(L, 544)
    bn = _pick(D, 512)
    per = D // bn

    def epilogue(dm, extras, outs):
        ga, gb, ya_ref, yb_ref = extras
        sa = _sigmoid(ga[...].astype(F32))
        sb = _sigmoid(gb[...].astype(F32))
        outs[0][...] = (dm * sa).astype(BF16)
        outs[1][...] = (dm * sb).astype(BF16)
        outs[2][...] = (dm * ya_ref[...].astype(F32) * sa * (1.0 - sa)).astype(BF16)
        outs[3][...] = (dm * yb_ref[...].astype(F32) * sb * (1.0 - sb)).astype(BF16)

    blk = pl.BlockSpec((bm, bn), lambda i, j: (i, j))
    return _mm(
        "mix_bwd", (L // bm, D // bn), [do, w_o, gates, gates, ya, yb],
        [pl.BlockSpec((bm, D), lambda i, j: (i, 0)), pl.BlockSpec((bn, D), lambda i, j: (j, 0)),
         blk, pl.BlockSpec((bm, bn), lambda i, j: (i, per + j)), blk, blk],
        [_sds((L, D), BF16)] * 4, [blk] * 4, NT, 1, epilogue)


def _mm_act_colw(name, a, wg, bn_pref, epilogue=_store, out_dtypes=(F32,)):
    L, K = a.shape
    Q, _, n = wg.shape
    bn = _pick(n, bn_pref)
    nj = n // bn
    out = pl.BlockSpec((L, bn), lambda q, j: (0, q * nj + j))
    return _mm(name, (Q, nj), [a, wg],
               [pl.BlockSpec((L, K), lambda q, j: (0, 0)), pl.BlockSpec((None, K, bn), lambda q, j: (q, 0, j))],
               [_sds((L, Q * n), dt) for dt in out_dtypes], [out] * len(out_dtypes), NN, 1, epilogue)


def _proj_rest(u1, wg, n_front, first, gates):
    L, K = u1.shape
    Q, _, n = wg.shape
    bn = _pick(math.gcd(n, n_front), 256)
    nj = n // bn
    nf = n_front // bn
    assert first * nj >= nf
    return _mm("proj_rest", (Q - first, nj), [u1, wg, gates],
               [pl.BlockSpec((L, K), lambda q, j: (0, 0)),
                pl.BlockSpec((None, K, bn), lambda q, j: (first + q, 0, j)), ANY],
               [_sds(gates.shape, gates.dtype)],
               [pl.BlockSpec((L, bn), lambda q, j: (0, (first + q) * nj + j - nf))], NN, 1, aliases={2: 0})


def _proj(u1, wg, n_front, Q):
    L, K = u1.shape
    _, _, n = wg.shape
    bn = _pick(math.gcd(n, n_front), 256)
    nj = n // bn
    nf = n_front // bn
    assert Q * nj >= nf

    def epilogue(val, extras, outs):
        col = pl.program_id(0) * nj + pl.program_id(1)

        @pl.when(col < nf)
        def _():
            outs[0][...] = val

        @pl.when(col >= nf)
        def _():
            outs[1][...] = val.astype(BF16)

    front = pl.BlockSpec((L, bn), lambda q, j: (0, jnp.minimum(q * nj + j, nf - 1)))
    gates = pl.BlockSpec((L, bn), lambda q, j: (0, jnp.maximum(q * nj + j - nf, 0)))
    return _mm("proj", (Q, nj), [u1, wg],
               [pl.BlockSpec((L, K), lambda q, j: (0, 0)), pl.BlockSpec((None, K, bn), lambda q, j: (q, 0, j))],
               [_sds((L, n_front), F32), _sds((L, wg.shape[0] * n - n_front), BF16)], [front, gates], NN, 1, epilogue)


def _mm_grad_colw_t(name, g, wg, bm_pref, bn_pref):
    L = g.shape[0]
    Q, K, n = wg.shape
    bm = _pick(L, bm_pref)
    bn = _pick(K, bn_pref)
    return _mm(name, (L // bm, K // bn, Q), [g, wg],
               [pl.BlockSpec((bm, n), lambda i, j, k: (i, k)), pl.BlockSpec((None, bn, n), lambda i, j, k: (k, j, 0))],
               [_sds((L, K), F32)], [pl.BlockSpec((bm, bn), lambda i, j, k: (i, j))], NT, Q,
               acc_shape=(bm, bn))


def _mm_wgrad_colw(name, a, g, Q, bm_pref, bn_pref, rows=None):
    L, K = a.shape
    n = g.shape[1] // Q
    first, count = rows or (0, K)
    bm = _pick(count, bm_pref)
    bn = _pick(n, bn_pref)
    nj = n // bn
    i0 = first // bm
    return _mm(name, (Q, count // bm, nj), [a, g],
               [pl.BlockSpec((L, bm), lambda q, i, j: (0, i0 + i)),
                pl.BlockSpec((L, bn), lambda q, i, j: (0, q * nj + j))],
               [_sds((Q, count, n), F32)], [pl.BlockSpec((None, bm, bn), lambda q, i, j: (q, i, j))], TN, 1)


def _mm_wgrad(name, a, g, bm_pref, bn_pref):
    L, K = a.shape
    N = g.shape[1]
    bm = _pick(K, bm_pref)
    bn = _pick(N, bn_pref)
    return _mm(name, (K // bm, N // bn), [a, g],
               [pl.BlockSpec((L, bm), lambda i, j: (0, i)), pl.BlockSpec((L, bn), lambda i, j: (0, j))],
               [_sds((K, N), F32)], [pl.BlockSpec((bm, bn), lambda i, j: (i, j))], TN, 1)


def _mm_act_roww(name, a, w, bm_pref, bn_pref, bk_pref):
    L, K = a.shape
    N = w.shape[1]
    bm, bn, bk = _pick(L, bm_pref), _pick(N, bn_pref), _pick(K, bk_pref)
    nk = K // bk
    return _mm(name, (L // bm, N // bn, nk), [a, w],
               [pl.BlockSpec((bm, bk), lambda i, j, k: (i, k)), pl.BlockSpec((bk, bn), lambda i, j, k: (k, j))],
               [_sds((L, N), F32)], [pl.BlockSpec((bm, bn), lambda i, j, k: (i, j))], NN, nk,
               acc_shape=(bm, bn))


def _concat_cols(name, parts):
    L = parts[0].shape[0]
    T = _row_tile(L)
    widths = [p.shape[1] for p in parts]

    def body(*refs):
        at = 0
        for ref, wd in zip(refs[:-1], widths):
            refs[-1][:, at:at + wd] = ref[...]
            at += wd

    return _call(
        body, list(parts), name=name, grid=(L // T,),
        in_specs=[pl.BlockSpec((T, wd), lambda i: (i, 0)) for wd in widths],
        out_specs=pl.BlockSpec((T, sum(widths)), lambda i: (i, 0)),
        out_shape=_sds((L, sum(widths)), parts[0].dtype), compiler_params=_params(1))


def _up_epilogue(val, extras, outs):
    outs[0][...] = val.astype(BF16)
    r = jnp.maximum(val, 0.0)
    outs[1][...] = (r * r).astype(BF16)


def _mlp_down_bwd(df, w_down, a_up):
    L, D = df.shape
    F = w_down.shape[0]
    bm = _pick(L, 1088)
    bn = _pick(F, 1024)

    def epilogue(val, extras, outs):
        outs[0][...] = (val * (2.0 * jnp.maximum(extras[0][...].astype(F32), 0.0))).astype(BF16)

    blk = pl.BlockSpec((bm, bn), lambda i, j: (i, j))
    return _mm("mlp_down_bwd", (L // bm, F // bn), [df, w_down, a_up],
               [pl.BlockSpec((bm, D), lambda i, j: (i, 0)), pl.BlockSpec((bn, D), lambda i, j: (j, 0)), blk],
               [_sds((L, F), BF16)], [blk], NT, 1, epilogue)


ANY = pl.BlockSpec(memory_space=pl.ANY)
HBM = pl.BlockSpec(memory_space=pltpu.HBM)
SEM = pl.BlockSpec(memory_space=pltpu.SEMAPHORE)
EFFECT = pltpu.SideEffectType.DATAFLOW_SIDE_EFFECTING
N_CHIPS = 4


def _place():
    x, y, c = lax.axis_index("x"), lax.axis_index("y"), lax.axis_index("c")
    return x, y, c


def _chip_at(x, y, k):
    px = 1 - x if k & 2 else x
    py = 1 - y if k & 1 else y
    return px, py


def _cast_into_slab(w2d, chip, dtype):
    R, C = w2d.shape
    T = _elem_tile(R, C)

    def body(p_ref, w_ref, o_ref):
        o_ref[...] = w_ref[...].astype(dtype)

    return _call(
        body, (w2d,), name="cast_into_slab", grid=(R // T,), scalars=jnp.reshape(chip, (1,)).astype(jnp.int32),
        in_specs=[pl.BlockSpec((T, C), lambda i, p: (i, 0))],
        out_specs=pl.BlockSpec((None, T, C), lambda i, p: (p[0], i, 0)),
        out_shape=_sds((N_CHIPS, R, C), dtype), compiler_params=_params(1))


TOKEN = jax.ShapeDtypeStruct((8, LANES), F32)


class _Sems:
    def __init__(self, items, shape):
        self.items, self.shape = list(items), tuple(shape)

    def pair(self, idx):
        flat = 0
        for i, n in zip(idx, self.shape):
            flat = flat * n + i
        half = len(self.items) // 2
        return self.items[flat], self.items[half + flat]


def _sem_count(shape):
    n = 1
    for s in shape:
        n *= s
    return n


def _remote(src, dst, sems, idx, device):
    send, recv = sems.pair(idx)
    return pltpu.make_async_remote_copy(src_ref=src, dst_ref=dst, send_sem=send, recv_sem=recv,
                                        device_id=device, device_id_type=MESH)


def _thru(arrays):
    return ([pltpu.with_memory_space_constraint(a, pltpu.HBM) for a in arrays],
            [pltpu.HBM(a.shape, a.dtype) for a in arrays])


def _comm_start(name, arrays, sem_shape, plan, follows=False):
    na, ns = len(arrays), 2 * _sem_count(sem_shape)

    def body(*refs):
        sems, token = _Sems(refs[na:na + ns], sem_shape), refs[-1]
        for src, dst, idx, device in plan(refs[:na])[0]:
            _remote(src, dst, sems, idx, device).start()
        token[...] = jnp.zeros(token.shape, F32)

    ins, outs = _thru(arrays)
    res = _call(
        body, ins, name=name, in_specs=[HBM] * na, mark=-1, follows=follows, made_from=arrays,
        out_specs=[SEM] * ns + [HBM] * na + [pl.BlockSpec(memory_space=pltpu.VMEM)],
        out_shape=[pltpu.SemaphoreType.DMA(())] * ns + outs + [TOKEN],
        input_output_aliases={a: ns + a for a in range(na)},
        compiler_params=pltpu.CompilerParams(has_side_effects=EFFECT))
    return _Sems(res[:ns], sem_shape), list(res[ns:ns + na])


def _wait_plans(refs, sem_refs, waits):
    x, y, c = _place()
    at = 0
    for sems, plan in waits:
        here = _Sems(sem_refs[at:at + len(sems.items)], sems.shape)
        at += len(sems.items)
        _, mine, arrivals = plan(refs)
        for dst, idx in arrivals:
            _remote(dst, dst, here, idx, (x, y, c)).wait_recv()
        for src, idx in mine:
            _remote(src, src, here, idx, (x, y, c)).wait_send()


def _comm_wait(name, arrays, waits):
    na = len(arrays)
    sem_items = [s for sems, _ in waits for s in sems.items]
    ns = len(sem_items)

    def body(*refs):
        _wait_plans(refs[:na], refs[na:na + ns], waits)
        refs[-1][...] = jnp.zeros(refs[-1].shape, F32)

    ins, outs = _thru(arrays)
    res = _call(
        body, ins + sem_items, name=name, in_specs=[HBM] * na + [SEM] * ns, mark=-1,
        out_specs=[HBM] * na + [pl.BlockSpec(memory_space=pltpu.VMEM)], out_shape=outs + [TOKEN],
        input_output_aliases={a: a for a in range(na)},
        compiler_params=pltpu.CompilerParams(has_side_effects=EFFECT))
    return list(res[:na])


def _comm_relay(name, arrays, sems, plan, sem_shape, next_plan):
    na, ns_in, ns_out = len(arrays), len(sems.items), 2 * _sem_count(sem_shape)

    def body(*refs):
        bufs = refs[:na]
        sems_in = _Sems(refs[na:na + ns_in], sems.shape)
        sems_out = _Sems(refs[na + ns_in:na + ns_in + ns_out], sem_shape)
        x, y, c = _place()
        _, mine, arrivals = plan(bufs)
        onward = next_plan(bufs)[0]
        for dst, idx in arrivals:
            _remote(dst, dst, sems_in, idx, (x, y, c)).wait_recv()
            for src, to, idx2, device, after_idx in onward:
                if after_idx == idx:
                    _remote(src, to, sems_out, idx2, device).start()
        for src, idx in mine:
            _remote(src, src, sems_in, idx, (x, y, c)).wait_send()
        refs[-1][...] = jnp.zeros(refs[-1].shape, F32)

    ins, outs = _thru(arrays)
    res = _call(
        body, ins + sems.items, name=name, in_specs=[HBM] * na + [SEM] * ns_in, mark=-1,
        out_specs=[SEM] * ns_out + [HBM] * na + [pl.BlockSpec(memory_space=pltpu.VMEM)],
        out_shape=[pltpu.SemaphoreType.DMA(())] * ns_out + outs + [TOKEN],
        input_output_aliases={a: ns_out + a for a in range(na)},
        compiler_params=pltpu.CompilerParams(has_side_effects=EFFECT))
    return _Sems(res[:ns_out], sem_shape), list(res[ns_out:ns_out + na])


def _half(ref, q, which):
    h = ref.shape[1] // 2
    return ref.at[q, pl.ds(which * h, h)]


def _quarter(ref, q, half, which):
    h = ref.shape[1] // 2
    return ref.at[q, pl.ds(half * h + which * (h // 2), h // 2)]


def _gather_plan(n_halved):
    def plan(refs):
        x, y, c = _place()
        p = 2 * x + y
        starts, mine, arrivals = [], [], []
        for n, ref in enumerate(refs):
            for k in range(1, N_CHIPS if n >= n_halved else 3):
                px, py = _chip_at(x, y, k)
                q = 2 * px + py
                out = _half(ref, p, c) if n < n_halved else ref.at[p]
                inc = _half(ref, q, c) if n < n_halved else ref.at[q]
                starts.append((out, out, (n, k - 1), (px, py, c)))
                mine.append((out, (n, k - 1)))
                arrivals.append((inc, (n, k - 1)))
        return starts, mine, arrivals
    return plan


def _spread_plan(n_halved, part):
    def plan(refs):
        x, y, c = _place()
        p = 2 * x + y
        starts, mine, arrivals = [], [], []
        for n in range(n_halved):
            for k in (1, 2):
                px, py = _chip_at(x, y, k)
                q = 2 * px + py
                tx, ty = _chip_at(x, y, 3 - k)
                dx, dy = _chip_at(x, y, 3)
                piece = _quarter(refs[n], q, c, 2 - k)
                landed = _half(refs[n], q, c)
                starts.append((piece, piece, (0, n, k - 1), (tx, ty, c), (n, k - 1)))
                starts.append((landed, landed, (1, n, k - 1), (x, y, 1 - c), (n, k - 1)))
                if part != 1:
                    mine.append((piece, (0, n, k - 1)))
                    arrivals.append((_quarter(refs[n], 2 * dx + dy, c, k - 1), (0, n, 2 - k)))
                if part != 0:
                    mine.append((landed, (1, n, k - 1)))
                    arrivals.append((_half(refs[n], q, 1 - c), (1, n, k - 1)))
        return starts, mine, arrivals
    return plan


def _last_hand_on_plan(n_halved):
    def plan(refs):
        x, y, c = _place()
        dx, dy = _chip_at(x, y, 3)
        d = 2 * dx + dy
        starts, mine, arrivals = [], [], []
        for n in range(n_halved):
            for k in (1, 2):
                piece = _quarter(refs[n], d, c, k - 1)
                starts.append((piece, piece, (n, k - 1), (x, y, 1 - c), (0, n, 2 - k)))
                mine.append((piece, (n, k - 1)))
                arrivals.append((_quarter(refs[n], d, 1 - c, k - 1), (n, k - 1)))
        return starts, mine, arrivals
    return plan


def _swap_plan(n):
    def plan(refs):
        x, y, c = _place()
        starts, mine, arrivals = [], [], []
        for a in range(n):
            h = refs[a].shape[1] // 2
            src = refs[a].at[:, pl.ds((1 - c) * h, h)]
            starts.append((src, refs[n + a], (a,), (x, y, 1 - c)))
            mine.append((src, (a,)))
            arrivals.append((refs[n + a], (a,)))
        return starts, mine, arrivals
    return plan


def _scatter_plan(n):
    def plan(refs):
        x, y, c = _place()
        starts, mine, arrivals = [], [], []
        for a in range(n):
            for k in range(1, N_CHIPS):
                px, py = _chip_at(x, y, k)
                src = refs[a].at[2 * px + py]
                starts.append((src, refs[n + a].at[k - 1], (a, k - 1), (px, py, c)))
                mine.append((src, (a, k - 1)))
                arrivals.append((refs[n + a].at[k - 1], (a, k - 1)))
        return starts, mine, arrivals
    return plan


def _share_plan(n):
    def plan(refs):
        x, y, c = _place()
        starts, mine, arrivals = [], [], []
        for a in range(n):
            h = refs[a].shape[0] // 2
            own = refs[a].at[pl.ds(c * h, h)]
            starts.append((own, own, (a,), (x, y, 1 - c)))
            mine.append((own, (a,)))
            arrivals.append((refs[a].at[pl.ds((1 - c) * h, h)], (a,)))
        return starts, mine, arrivals
    return plan


N_DEVICES = 8


def _packs_plan(refs):
    buf = refs[0]
    x, y, c = _place()
    me = 4 * x + 2 * y + c
    starts, mine, arrivals = [], [], []
    for r in range(1, N_DEVICES):
        peer = (1 - x if r & 4 else x, 1 - y if r & 2 else y, 1 - c if r & 1 else c)
        starts.append((buf.at[me], buf.at[me], (r - 1,), peer))
        mine.append((buf.at[me], (r - 1,)))
        arrivals.append((buf.at[4 * peer[0] + 2 * peer[1] + peer[2]], (r - 1,)))
    return starts, mine, arrivals


class _Reduction:
    def __init__(self, tag, slabs, c_idx, chip):
        self.tag, self.n, self.c_idx, self.chip = tag, len(slabs), c_idx, chip
        lands = [lax.empty((g.shape[0], g.shape[1] // 2, g.shape[2]), g.dtype) for g in slabs]
        self.sems = _comm_start("swap_start_" + tag, list(slabs) + lands, (self.n,), _swap_plan(self.n))

    def partial(self):
        n = self.n
        sems, bufs = self.sems
        bufs = _comm_wait("swap_wait_" + self.tag, bufs, [(sems, _swap_plan(n))])
        both = [_chip_partial(g, r, self.c_idx, self.chip) for g, r in zip(bufs[:n], bufs[n:])]
        self.own = [o for _, o in both]
        parts = [p for p, _ in both]
        lands = [lax.empty((N_CHIPS - 1,) + p.shape[1:], p.dtype) for p in parts]
        self.sems = _comm_start("scatter_start_" + self.tag, parts + lands, (n, N_CHIPS - 1), _scatter_plan(n))

    def total(self):
        n = self.n
        sems, bufs = self.sems
        bufs = _comm_wait("scatter_wait_" + self.tag, bufs, [(sems, _scatter_plan(n))])
        fulls = [_sum_partials(o, r, self.c_idx) for o, r in zip(self.own, bufs[n:])]
        self.sems = _comm_start("share_start_" + self.tag, fulls, (n,), _share_plan(n))

    def finish(self):
        sems, bufs = self.sems
        return _comm_wait("share_wait_" + self.tag, bufs, [(sems, _share_plan(self.n))])


def _elem_tile(rows, cols):
    return _pick(rows, max(8, (1 << 19) // cols // 8 * 8))


def _chip_partial(grad, recv, c_idx, p_idx):
    Q, R, C = grad.shape
    h = R // 2
    T = _elem_tile(h, C)
    nt = h // T

    def body(sc_ref, g_ref, r_ref, sb_ref, own_ref):
        q = pl.program_id(1)
        s = g_ref[...] + r_ref[...]
        sb_ref[...] = s.astype(BF16)

        @pl.when(q == sc_ref[1])
        def _():
            own_ref[...] = s

    return _call(
        body, (grad, recv), name="chip_partial", grid=(nt, Q),
        scalars=jnp.stack([c_idx, p_idx]).astype(jnp.int32),
        in_specs=[pl.BlockSpec((None, T, C), lambda t, q, sc: (q, sc[0] * nt + t, 0)),
                  pl.BlockSpec((None, T, C), lambda t, q, sc: (q, t, 0))],
        out_specs=[pl.BlockSpec((None, T, C), lambda t, q, sc: (q, t, 0)),
                   pl.BlockSpec((T, C), lambda t, q, sc: (t, 0))],
        out_shape=[_sds((Q, h, C), BF16), _sds((h, C), F32)], compiler_params=_params(2))


def _sum_partials(own, parts, c_idx):
    h, C = own.shape
    T = _elem_tile(h, C)
    nt = h // T

    def body(c_ref, o_ref, p_ref, t_ref):
        t = o_ref[...]
        for k in range(N_CHIPS - 1):
            t = t + p_ref[k].astype(F32)
        t_ref[...] = t

    return _call(
        body, (own, parts), name="sum_partials", grid=(nt,), scalars=jnp.reshape(c_idx, (1,)).astype(jnp.int32),
        in_specs=[pl.BlockSpec((T, C), lambda i, c: (i, 0)),
                  pl.BlockSpec((N_CHIPS - 1, T, C), lambda i, c: (0, i, 0))],
        out_specs=pl.BlockSpec((T, C), lambda i, c: (c[0] * nt + i, 0)),
        out_shape=_sds((2 * h, C), F32), compiler_params=_params(1))


def _pack_rows(name, parts, slot=None, n_slots=1):
    width = parts[0].shape[1]
    offsets, at = [], 0
    for p in parts:
        offsets.append(at)
        at += p.shape[0]
    total = -(-at // 8) * 8
    lead = 0 if slot is None else 1

    def body(*refs):
        out = refs[-1]
        out[...] = jnp.zeros(out.shape, F32)
        for ref, o in zip(refs[lead:-1], offsets):
            out[o:o + ref.shape[0], :] = ref[...]

    if slot is None:
        whole = pl.BlockSpec(memory_space=pltpu.VMEM)
        return _call(body, list(parts), name=name, in_specs=[whole] * len(parts), out_specs=whole,
                     out_shape=_sds((total, width), F32))
    return _call(body, list(parts), name=name, grid=(1,), scalars=jnp.reshape(slot, (1,)).astype(jnp.int32),
                 in_specs=[pl.BlockSpec(p.shape, lambda i, s: (0, 0)) for p in parts],
                 out_specs=pl.BlockSpec((None, total, width), lambda i, s: (s[0], 0, 0)),
                 out_shape=_sds((n_slots, total, width), F32))


def _sum_packs(packs):
    n, R, C = packs.shape

    def body(p_ref, o_ref):
        t = p_ref[0]
        for k in range(1, n):
            t = t + p_ref[k]
        o_ref[...] = t

    return _call(
        body, (packs,), name="sum_packs", grid=(1,), in_specs=[pl.BlockSpec((n, R, C), lambda i: (0, 0, 0))],
        out_specs=pl.BlockSpec((R, C), lambda i: (0, 0)), out_shape=_sds((R, C), F32), compiler_params=_params(1))


def _adamw(w, g, m, v):
    R, C = w.shape
    T = _elem_tile(R, C)

    def body(w_ref, g_ref, m_ref, v_ref, d_ref, m2_ref, v2_ref):
        g_ = g_ref[...]
        m2 = ADAM_B1 * m_ref[...] + (1.0 - ADAM_B1) * g_
        v2 = ADAM_B2 * v_ref[...] + (1.0 - ADAM_B2) * (g_ * g_)
        m_hat = m2 / (1.0 - ADAM_B1 ** ADAM_STEP)
        v_hat = v2 / (1.0 - ADAM_B2 ** ADAM_STEP)
        d_ref[...] = -ADAM_LR * (m_hat / (jnp.sqrt(v_hat) + ADAM_EPS) + ADAM_WD * w_ref[...])
        m2_ref[...] = m2
        v2_ref[...] = v2

    blk = pl.BlockSpec((T, C), lambda i: (i, 0))
    return _call(
        body, (w, g, m, v), name="adamw", grid=(R // T,), in_specs=[blk] * 4, out_specs=[blk] * 3,
        out_shape=[_sds((R, C), F32)] * 3, compiler_params=_params(1))


def _adamw_rows(w, g, m, v, row0, prev=None):
    R, C = w.shape
    T = _elem_tile(g.shape[0], C)
    off = row0 // T

    def body(w_ref, g_ref, m_ref, v_ref, *rest):
        d_ref, m2_ref, v2_ref, g2_ref = rest[-4:]
        g_ = g_ref[...]
        m2 = ADAM_B1 * m_ref[...] + (1.0 - ADAM_B1) * g_
        v2 = ADAM_B2 * v_ref[...] + (1.0 - ADAM_B2) * (g_ * g_)
        m_hat = m2 / (1.0 - ADAM_B1 ** ADAM_STEP)
        v_hat = v2 / (1.0 - ADAM_B2 ** ADAM_STEP)
        d_ref[...] = -ADAM_LR * (m_hat / (jnp.sqrt(v_hat) + ADAM_EPS) + ADAM_WD * w_ref[...])
        m2_ref[...] = m2
        v2_ref[...] = v2
        g2_ref[...] = g_

    here = pl.BlockSpec((T, C), lambda i: (off + i, 0))
    piece = pl.BlockSpec((T, C), lambda i: (i, 0))
    done = tuple(prev or ())
    return _call(
        body, (w, g, m, v) + done, name="adamw_rows", grid=(g.shape[0] // T,), follows=prev is None,
        in_specs=[here, piece, here, here] + [ANY] * len(done), out_specs=[here] * 4,
        out_shape=[_sds((R, C), F32)] * 4, input_output_aliases={4 + j: j for j in range(len(done))},
        compiler_params=_params(1))


SC_TILES = 32
SC_ROWS = 8
SC_LANES = 16


def _adamw_sc(w, g, m, v):
    R, C = w.shape
    per_tile = R // SC_TILES
    assert per_tile % SC_ROWS == 0 and C % SC_LANES == 0

    def body(w_hbm, g_hbm, m_hbm, v_hbm, d_out, m_out, v_out, g_out, wb, gb, mb, vb):
        tile = lax.axis_index("sc_tile") * 2 + lax.axis_index("sc_core")

        @pl.loop(0, per_tile // SC_ROWS)
        def _(chunk):
            here = pl.ds(tile * per_tile + chunk * SC_ROWS, SC_ROWS)
            pltpu.sync_copy(w_hbm.at[here], wb)
            pltpu.sync_copy(g_hbm.at[here], gb)
            pltpu.sync_copy(m_hbm.at[here], mb)
            pltpu.sync_copy(v_hbm.at[here], vb)
            for r in range(SC_ROWS):
                @pl.loop(0, C, step=SC_LANES)
                def _(j):
                    at = pl.ds(j, SC_LANES)
                    g_ = gb[r, at]
                    m2 = ADAM_B1 * mb[r, at] + (1.0 - ADAM_B1) * g_
                    v2 = ADAM_B2 * vb[r, at] + (1.0 - ADAM_B2) * (g_ * g_)
                    m_hat = m2 / (1.0 - ADAM_B1 ** ADAM_STEP)
                    v_hat = v2 / (1.0 - ADAM_B2 ** ADAM_STEP)
                    wb[r, at] = -ADAM_LR * (m_hat / (jnp.sqrt(v_hat) + ADAM_EPS) + ADAM_WD * wb[r, at])
                    mb[r, at] = m2
                    vb[r, at] = v2
            pltpu.sync_copy(wb, d_out.at[here])
            pltpu.sync_copy(mb, m_out.at[here])
            pltpu.sync_copy(vb, v_out.at[here])
            pltpu.sync_copy(gb, g_out.at[here])

    buf = pltpu.VMEM((SC_ROWS, C), F32)
    return pl.kernel(
        body, name="adamw_sc", out_type=(_sds((R, C), F32),) * 4,
        mesh=plsc.VectorSubcoreMesh(core_axis_name="sc_core", subcore_axis_name="sc_tile"),
        scratch_types=[buf, buf, buf, buf])(w, g, m, v)


BIG = ("w_in", "w_pool_out", "w_conv_out", "w_o", "w_up", "w_down", "w_pool_grp")
VECTORS = ("g_pre_mix", "pool_scale", "b_dw", "conv_ln_g", "conv_ln_b", "g_post_mix", "g_pre_mlp", "g_post_mlp")
WEIGHTS = ("meta", "g_pre_mix", "w_in", "w_pool_grp", "pool_scale", "w_pool_out", "w_dw", "b_dw", "conv_ln_g",
           "conv_ln_b", "w_conv_out", "w_o", "g_post_mix", "g_pre_mlp", "w_up", "w_down", "g_post_mlp")


def _as_rows(a, width):
    r, cols = a.shape
    return a.reshape(r * (cols // width), width)


def _step(w, m, v, x, tgt):
    S, D = x.shape
    P = D // 2
    xi, yi, ci = _place()
    chip = 2 * xi + yi
    _CHAIN["after"] = []

    C = D // 2
    G = POOL_GROUPS
    GD = P // G
    GS = GD // N_CHIPS
    Q = N_CHIPS
    vecs = {k: w[k] for k in VECTORS}
    shard2d = {k: w[k].reshape(-1, w[k].shape[-1]) for k in BIG}
    grads, delta, new_m, new_v = {}, {}, {}, {}

    def update(names, reduced, on_sparsecore=False):
        for k, g in zip(names, reduced):
            args = (shard2d[k], g, m[k].reshape(shard2d[k].shape), v[k].reshape(shard2d[k].shape))
            delta[k], new_m[k], new_v[k], grads[k] = _adamw_sc(*args) if on_sparsecore else _adamw_rows(*args, 0)

    groups = dict(a=(("w_in", "w_pool_grp"), ("w_dw", "meta")), b=(("w_pool_out", "w_conv_out", "w_o"), ()),
                  c=(("w_up",), ()), d=(("w_down",), ()))
    flying = {}

    def start(tag, follows):
        halved, whole = groups[tag]
        flying[tag] = _comm_start("gather_start_" + tag, flying[tag], (len(halved + whole), N_CHIPS - 1),
                                  _gather_plan(len(halved)), follows=follows)

    def spread(tag):
        nh = len(groups[tag][0])
        sems, bufs = flying[tag]
        flying[tag] = _comm_relay("gather_relay_" + tag, bufs, sems, _gather_plan(nh), (2, nh, 2),
                                  _spread_plan(nh, 0))

    def landed(tag):
        halved, whole = groups[tag]
        nh = len(halved)
        sems, bufs = flying.pop(tag)
        last, tree = _comm_relay("gather_relay2_" + tag, bufs[:nh], sems, _spread_plan(nh, 0), (nh, 2),
                                 _last_hand_on_plan(nh))
        done = _comm_wait("gather_wait_" + tag, tree,
                          [(sems, _spread_plan(nh, 1)), (last, _last_hand_on_plan(nh))])
        return dict(zip(halved + whole, done + bufs[nh:]))

    for tag, (halved, whole) in groups.items():
        flying[tag] = [_cast_into_slab(shard2d[k], chip, BF16) for k in halved]
        flying[tag] += [_cast_into_slab(w[k], chip, F32) for k in whole]
        if tag == "a":
            start(tag, False)

    small_names = [k for k in WEIGHTS if k not in BIG]

    def pack_small(tree):
        parts = []
        for k in small_names:
            flat = tree[k].reshape(-1)
            parts.append(jnp.pad(flat, (0, -flat.shape[0] % P)).reshape(-1, P))
        return _pack_rows("pack_small", parts)

    small_w, small_m, small_v = pack_small(w), pack_small(m), pack_small(v)
    h0, u1 = _pre_norm_tokens(x, vecs["g_pre_mix"])
    spread("a")
    start("b", True)

    got = landed("a")
    win_g = got["w_in"]
    w_grp = got["w_pool_grp"].reshape(N_CHIPS, G, GS, GD).transpose(1, 0, 2, 3).reshape(G, GD, GD)
    w_dw = got["w_dw"].transpose(1, 0, 2).reshape(CONV_TAPS, P)
    meta = got["meta"].transpose(1, 0, 2).reshape(N_META, D)
    h0, u1 = _pre_norm_meta(meta, vecs["g_pre_mix"], h0, u1)
    proj, gates = _proj(u1, win_g, P + 2 * C, 2)
    spread("b")
    start("c", True)
    start("d", True)
    gates = _proj_rest(u1, win_g, P + 2 * C, 2, gates)
    d, ya_pre = _pool_fwd(proj, w_grp, vecs["pool_scale"])
    s, c = _conv_fwd(proj, C, w_dw, vecs["b_dw"], vecs["conv_ln_g"], vecs["conv_ln_b"])
    spread("c")
    got = landed("b")
    wpo_g, wco_g, w_o = got["w_pool_out"], got["w_conv_out"], got["w_o"].reshape(D, D)
    mix, ya, yb = _mix_fwd(ya_pre, s, wpo_g, wco_g, gates, D)
    o = _mm_act_roww("attn_out", mix, w_o, 1088, 1024, 2048)
    spread("d")
    h1, u2 = _mid_norm(o, h0, vecs["g_post_mix"], vecs["g_pre_mlp"])
    wup_g = landed("c")["w_up"]
    a_up, fact = _mm_act_colw("mlp_up", u2, wup_g, 512, _up_epilogue, (BF16, BF16))
    w_down = landed("d")["w_down"].reshape(-1, D)
    f = _mm_act_roww("mlp_down", fact, w_down, 1088, 1024, 2048)
    dy, df, dg_post_mlp, loss = _loss_head(f, h1, tgt, vecs["g_post_mlp"])

    g_w_down = _mm_wgrad("dw_down", fact, df, 1024, 1024).reshape(N_CHIPS, -1, D)
    _CHAIN["after"] = [g_w_down]
    red1 = _Reduction("1", [g_w_down], ci, chip)
    da_up = _mlp_down_bwd(df, w_down, a_up)
    red1.partial()
    g_w_up = _mm_wgrad_colw("dw_up", u2, da_up, Q, 1024, 1024)
    red2 = _Reduction("2", [g_w_up], ci, chip)
    du2 = _mm_grad_colw_t("du2", da_up, wup_g, 1088, 1024)
    red2.partial()
    red1.total()
    dh1, do, dg_pre_mlp, dg_post_mix = _mid_norm_bwd(dy, du2, h1, o, vecs["g_pre_mlp"], vecs["g_post_mix"])
    update(("w_down",), red1.finish(), on_sparsecore=True)
    g_w_o = _mm_wgrad("dw_o", mix, do, 1024, 1024)
    dya, dyb, dga, dgb = _mix_bwd(do, w_o, gates, ya, yb)
    g_wpo = _mm_wgrad_colw("dw_pool_out", ya_pre, dya, Q, 1024, 512)
    g_wco = _mm_wgrad_colw("dw_conv_out", s, dyb, Q, 1024, 512)
    red3 = _Reduction("3", [g_w_o.reshape(N_CHIPS, D // N_CHIPS, D), g_wpo, g_wco], ci, chip)
    dya_pre = _mm_grad_colw_t("dya_pre", dya, wpo_g, 1088, 1024)
    ds = _mm_grad_colw_t("ds", dyb, wco_g, 1088, 1024)
    red3.partial()
    red2.total()
    dz, g_w_grp, dscale = _pool_bwd(dya_pre, d, w_grp, vecs["pool_scale"])
    update(("w_up",), red2.finish(), on_sparsecore=True)
    dc, dln_g, dln_b, db_dw = _conv_ln_bwd(ds, c, vecs["conv_ln_g"], vecs["conv_ln_b"])
    dv, dgc, g_w_dw = _conv_bwd(dc, proj, C, w_dw)
    dproj = _concat_cols("dproj", [dz, dv, dgc, dga, dgb])
    half_k = D // 2
    g_w_grp = g_w_grp.reshape(G, N_CHIPS, GS, GD).transpose(1, 0, 2, 3).reshape(N_CHIPS, G * GS, GD)
    g_in_a = _mm_wgrad_colw("dw_in_a", u1, dproj, Q, 512, 1792, rows=(0, half_k))
    red4a = _Reduction("4a", [g_in_a, g_w_grp], ci, chip)
    g_in_b = _mm_wgrad_colw("dw_in_b", u1, dproj, Q, 512, 1792, rows=(half_k, half_k))
    red4a.partial()
    red4b = _Reduction("4b", [g_in_b], ci, chip)
    du1 = _mm_grad_colw_t("du1", dproj, win_g, 1088, 1024)
    red4b.partial()
    red3.total()
    grad_x, dmeta, dg_pre_mix = _pre_norm_bwd(dh1, du1, h0, vecs["g_pre_mix"])

    g_vec = dict(g_pre_mix=dg_pre_mix, pool_scale=dscale, b_dw=db_dw, conv_ln_g=dln_g, conv_ln_b=dln_b,
                 g_post_mix=dg_post_mix, g_pre_mlp=dg_pre_mlp, g_post_mlp=dg_post_mlp)
    rows = [g_w_dw, _as_rows(dmeta, P)] + [_as_rows(g_vec[k], P) for k in VECTORS]
    rows.append(jnp.broadcast_to(loss[:, :1], (1, P)))
    packs = _comm_start("packs_start", [_pack_rows("pack_grads", rows, 2 * chip + ci, N_DEVICES)],
                        (N_DEVICES - 1,), _packs_plan)
    red4a.total()
    update(("w_o", "w_pool_out", "w_conv_out"), red3.finish())
    red_in_a, red_grp = red4a.finish()
    update(("w_pool_grp",), [red_grp])
    w_in_rows = (shard2d["w_in"], m["w_in"].reshape(shard2d["w_in"].shape), v["w_in"].reshape(shard2d["w_in"].shape))
    first_rows = _adamw_rows(w_in_rows[0], red_in_a, w_in_rows[1], w_in_rows[2], 0)
    total = _sum_packs(_comm_wait("packs_wait", packs[1], [(packs[0], _packs_plan)])[0])
    at = 0
    taps_pad = g_w_dw.shape[0]
    g_dw_full = total[at:at + CONV_TAPS]
    at += taps_pad
    g_meta_full = total[at:at + 2 * N_META].reshape(N_META, D)
    at += 2 * N_META
    for k in VECTORS:
        n = w[k].shape[-1] // P
        grads[k] = total[at:at + n].reshape(1, n * P)
        at += n
    loss_total = total[at, 0]
    grads["w_dw"] = lax.dynamic_slice_in_dim(g_dw_full, chip * (P // N_CHIPS), P // N_CHIPS, axis=1)
    grads["meta"] = lax.dynamic_slice_in_dim(g_meta_full, chip * (D // N_CHIPS), D // N_CHIPS, axis=1)

    sd, sm, sv = _adamw(small_w, pack_small(grads), small_m, small_v)
    at = 0
    for k in small_names:
        a = w[k].reshape(-1, w[k].shape[-1])
        n = -(-a.size // P)
        for tree, packed in ((delta, sd), (new_m, sm), (new_v, sv)):
            tree[k] = packed[at:at + n].reshape(-1)[:a.size].reshape(a.shape)
        at += n

    red4b.total()
    delta["w_in"], new_m["w_in"], new_v["w_in"], grads["w_in"] = _adamw_rows(
        w_in_rows[0], red4b.finish()[0], w_in_rows[1], w_in_rows[2], half_k, prev=first_rows)
    return loss_total, grad_x, grads, delta, new_m, new_v


def kernel(x, meta, g_pre_mix, w_in, w_pool_grp, pool_scale, w_pool_out, w_dw, b_dw, conv_ln_g, conv_ln_b, w_conv_out, w_o, g_post_mix, g_pre_mlp, w_up, w_down, g_post_mlp, loss_target, m_meta, m_g_pre_mix, m_w_in, m_w_pool_grp, m_pool_scale, m_w_pool_out, m_w_dw, m_b_dw, m_conv_ln_g, m_conv_ln_b, m_w_conv_out, m_w_o, m_g_post_mix, m_g_pre_mlp, m_w_up, m_w_down, m_g_post_mlp, v_meta, v_g_pre_mix, v_w_in, v_w_pool_grp, v_pool_scale, v_w_pool_out, v_w_dw, v_b_dw, v_conv_ln_g, v_conv_ln_b, v_w_conv_out, v_w_o, v_g_post_mix, v_g_pre_mlp, v_w_up, v_w_down, v_g_post_mlp):
    args = dict(locals())
    shapes = {k: args[k].shape for k in WEIGHTS}
    w = {k: args[k] for k in WEIGHTS}
    m = {k: args["m_" + k] for k in WEIGHTS}
    v = {k: args["v_" + k] for k in WEIGHTS}
    for tree in (w, m, v):
        tree["w_dw"] = tree["w_dw"].reshape(tree["w_dw"].shape[-2:])
    loss, grad_x, grads, delta, new_m, new_v = _step(w, m, v, x[0], loss_target[0])
    out = [loss, grad_x[None]]
    for tree in (grads, delta, new_m, new_v):
        out += [tree[k].reshape(shapes[k]) for k in WEIGHTS]
    return tuple(out)
```
